```python
import math
import jax
import jax.numpy as jnp
from jax import lax
import numpy as np

D_MODEL = 1024
BATCH = 8
SEQ = 4096
DEPTH = 2

CTX_LEN = 256
GRID_W = 64

N_BRANCH = 4
N_MOD = 9
FF_DIM = 2816
EPS = 1e-6
ROPE_BASE = 10000.0
NEG_INF = -1e30
BRANCH_DIM = 512

MLA_HEADS = 8
MLA_NOPE = 64
MLA_ROPE = 32
MLA_V = 64
MLA_Q_RANK = 256
MLA_KV_RANK = 128

RWKV_HEADS = 8
RWKV_HEAD = 64
RWKV_DIM = RWKV_HEADS * RWKV_HEAD
DECAY_LORA = 64
AAA_LORA = 64
GATE_LORA = 128
RWKV_LN_EPS = 64e-5

HYENA_DIM = 512
HYENA_ORDER = 2
HYENA_EMB = 33
HYENA_BANDS = (HYENA_EMB - 1) // 2
HYENA_FW = 64
HYENA_TARGET = 1e-2
HYENA_FAST = 0.3
HYENA_SLOW = 1.5
SHORT_CONV = 3

SWA_HEADS = 8
SWA_KV_HEADS = 2
SWA_HEAD = 64
SWA_GROUP = SWA_HEADS // SWA_KV_HEADS
WINDOW = 128
BLOCK = 128

GATE_COLS = N_BRANCH * D_MODEL
MLA_COLS = MLA_Q_RANK + MLA_KV_RANK + MLA_ROPE
RWKV_COLS = 3 * RWKV_DIM + DECAY_LORA + AAA_LORA + GATE_LORA
HYENA_COLS = 3 * HYENA_DIM
SWA_COLS = (SWA_HEADS + 2 * SWA_KV_HEADS) * SWA_HEAD
IN_COLS = GATE_COLS + MLA_COLS + RWKV_COLS + HYENA_COLS + SWA_COLS
IN_SPLITS = (GATE_COLS, GATE_COLS + MLA_COLS, GATE_COLS + MLA_COLS + RWKV_COLS,
             GATE_COLS + MLA_COLS + RWKV_COLS + HYENA_COLS)
RWKV_SPLITS = (RWKV_DIM, 2 * RWKV_DIM, 3 * RWKV_DIM, 3 * RWKV_DIM + DECAY_LORA,
               3 * RWKV_DIM + DECAY_LORA + AAA_LORA)

kernel_name = "hybrid_mla_rwkv7_hyena_swa_diffusion_block"


def rmsnorm(x, g):
    xf = x.astype(jnp.float32)
    y = xf * lax.rsqrt(jnp.mean(xf * xf, axis=-1, keepdims=True) + EPS)
    return (y * g.astype(jnp.float32)).astype(x.dtype)


def modulate(x, shift, scale):
    return x * (1 + scale) + shift


def swiglu(x, w13, w2):
    a, b = jnp.split(x @ w13, 2, axis=-1)
    return (jax.nn.silu(a) * b) @ w2


def ffn_half_step(x, shift, scale, gate, g_pre, g_post, w13, w2):
    h = swiglu(modulate(rmsnorm(x, g_pre), shift, scale), w13, w2)
    return x + 0.5 * gate * rmsnorm(h, g_post)


def centred_shift(p):
    zero = jnp.zeros_like(p[:, :1])
    prev = jnp.concatenate([zero, p[:, :-1]], axis=1)
    nxt = jnp.concatenate([p[:, 1:], zero], axis=1)
    return prev, nxt


def axial_rope_tables(rows, rot_dim):
    row = jnp.repeat(jnp.arange(rows, dtype=jnp.float32), GRID_W)
    col = jnp.tile(jnp.arange(GRID_W, dtype=jnp.float32), rows)
    axis_dim = rot_dim // 2
    inv_freq = ROPE_BASE ** (-jnp.arange(0, axis_dim, 2, dtype=jnp.float32) / axis_dim)
    ang_r = row[:, None] * inv_freq
    ang_c = col[:, None] * inv_freq
    return (jnp.cos(ang_r), jnp.sin(ang_r), jnp.cos(ang_c), jnp.sin(ang_c))


def rope_rotate(x, cos, sin):
    x1, x2 = jnp.split(x, 2, axis=-1)
    cos = cos[None, :, None, :].astype(x.dtype)
    sin = sin[None, :, None, :].astype(x.dtype)
    return jnp.concatenate([x1 * cos - x2 * sin, x1 * sin + x2 * cos], axis=-1)


def axial_rope(x, tabs):
    cos_r, sin_r, cos_c, sin_c = tabs
    x_row, x_col = jnp.split(x, 2, axis=-1)
    return jnp.concatenate([rope_rotate(x_row, cos_r, sin_r), rope_rotate(x_col, cos_c, sin_c)], axis=-1)


def softmax_attend(q, k, v, scale):
    s = jnp.einsum("bqhd,bkhd->bhqk", q, k).astype(jnp.float32) * scale
    p = jax.nn.softmax(s, axis=-1).astype(v.dtype)
    return jnp.einsum("bhqk,bkhd->bqhd", p, v)


def blocked_attend(q, k, v, scale):
    b, n, h, d = q.shape
    nb = n // BLOCK
    qb = jnp.moveaxis(q.reshape(b, nb, BLOCK, h, d), 1, 0)
    out = lax.map(lambda qi: softmax_attend(qi, k, v, scale), qb)
    return jnp.moveaxis(out, 0, 1).reshape(b, n, h, v.shape[-1])


def mla_project(p, lp, tabs):
    b, n, _ = p.shape
    c_q, c_kv, k_r = jnp.split(p, [MLA_Q_RANK, MLA_Q_RANK + MLA_KV_RANK], axis=-1)
    q = (rmsnorm(c_q, lp["mla_norm_q"]) @ lp["mla_w_uq"]).reshape(b, n, MLA_HEADS, MLA_NOPE + MLA_ROPE)
    kv = (rmsnorm(c_kv, lp["mla_norm_kv"]) @ lp["mla_w_ukv"]).reshape(b, n, MLA_HEADS, MLA_NOPE + MLA_V)
    q_nope, q_rope = jnp.split(q, [MLA_NOPE], axis=-1)
    k_nope, v = jnp.split(kv, [MLA_NOPE], axis=-1)
    k_r = k_r[:, :, None, :]
    if tabs is not None:
        q_rope = axial_rope(q_rope, tabs)
        k_r = axial_rope(k_r, tabs)
    q = jnp.concatenate([q_nope, q_rope], axis=-1)
    k = jnp.concatenate([k_nope, jnp.broadcast_to(k_r, (b, n, MLA_HEADS, MLA_ROPE))], axis=-1)
    return q, k, v


def mla_mixer(p_lat, p_ctx, lp, tabs, with_ctx):
    scale = (MLA_NOPE + MLA_ROPE) ** -0.5
    q, k, v = mla_project(p_lat, lp, tabs)
    qc, kc, vc = mla_project(p_ctx, lp, None)
    k_all = jnp.concatenate([k, kc], axis=1)
    v_all = jnp.concatenate([v, vc], axis=1)
    b, n = p_lat.shape[:2]
    y = blocked_attend(q, k_all, v_all, scale).reshape(b, n, MLA_HEADS * MLA_V)
    yc = None
    if with_ctx:
        yc = softmax_attend(qc, kc, vc, scale).reshape(b, p_ctx.shape[1], MLA_HEADS * MLA_V)
    return y, yc


def rwkv_prepare(p, lp):
    f32 = jnp.float32
    p = p.astype(f32)
    prev, nxt = centred_shift(p)
    mu = lp["rwkv_mu"].astype(f32)
    p = p + mu[0] * (prev - p) + mu[1] * (nxt - p)
    r, k, v, w_lo, a_lo, g_lo = jnp.split(p, RWKV_SPLITS, axis=-1)
    b, n, _ = p.shape

    def heads(t):
        return t.reshape(b, n, RWKV_HEADS, RWKV_HEAD)

    kk = heads(k * lp["rwkv_kvec"][0])
    kk = kk * lax.rsqrt(jnp.sum(kk * kk, axis=-1, keepdims=True) + 1e-12)
    g = jax.nn.sigmoid(g_lo) @ lp["rwkv_g_up"].astype(f32)
    per_dir = []
    for d in range(2):
        w_log = -jax.nn.softplus(-(lp["rwkv_w0"][d] + jnp.tanh(w_lo) @ lp["rwkv_w_up"][d])) - 0.5
        decay = jnp.exp(-jnp.exp(w_log))
        a = jax.nn.sigmoid(lp["rwkv_a0"][d] + a_lo @ lp["rwkv_a_up"][d])
        k_d = k * (1 + (a - 1) * lp["rwkv_kvec"][1])
        per_dir.append((heads(decay), heads(k_d), -kk, kk * heads(a)))
    return heads(r), heads(v), g, per_dir


def wkv_scan(s0, r, w, k, v, z, bb, reverse):
    def step(s, inp):
        r_t, w_t, k_t, v_t, z_t, b_t = inp
        sz = jnp.einsum("bhvk,bhk->bhv", s, z_t)
        s = s * w_t[:, :, None, :] + sz[..., None] * b_t[:, :, None, :] + v_t[..., None] * k_t[:, :, None, :]
        return s, jnp.einsum("bhvk,bhk->bhv", s, r_t)

    xs = tuple(jnp.moveaxis(t, 1, 0) for t in (r, w, k, v, z, bb))
    s_final, ys = lax.scan(step, s0, xs, reverse=reverse)
    return jnp.moveaxis(ys, 0, 1), s_final


def rwkv_readout(y, r, v, k_bonus, g, lp, dtype):
    b, n = y.shape[:2]
    mean = jnp.mean(y, axis=-1, keepdims=True)
    var = jnp.mean(jnp.square(y - mean), axis=-1, keepdims=True)
    yn = ((y - mean) * lax.rsqrt(var + RWKV_LN_EPS)).reshape(b, n, RWKV_DIM)
    yn = yn * lp["rwkv_ln_g"] + lp["rwkv_ln_b"]
    r_k = lp["rwkv_r_k"].astype(jnp.float32).reshape(RWKV_HEADS, RWKV_HEAD)
    bonus = (jnp.sum(r * k_bonus * r_k, axis=-1, keepdims=True) * v).reshape(b, n, RWKV_DIM)
    return ((yn + bonus) * g).astype(dtype)


def rwkv_mixer(p_lat, p_ctx, lp, with_ctx):
    r, v, g, dirs = rwkv_prepare(p_lat, lp)
    rc, vc, gc, dirs_c = rwkv_prepare(p_ctx, lp)
    s0 = jnp.zeros((p_lat.shape[0], RWKV_HEADS, RWKV_HEAD, RWKV_HEAD), jnp.float32)
    y = 0.0
    yc = 0.0
    for d, rev in enumerate((False, True)):
        dec_c, k_c, z_c, b_c = dirs_c[d]
        yc_d, s_ctx = wkv_scan(s0, rc, dec_c, k_c, vc, z_c, b_c, rev)
        dec, k_d, z_d, b_d = dirs[d]
        y_d, _ = wkv_scan(s_ctx, r, dec, k_d, v, z_d, b_d, rev)
        y = y + y_d
        yc = yc + yc_d
    out = rwkv_readout(y, r, v, 0.5 * (dirs[0][1] + dirs[1][1]), g, lp, p_lat.dtype)
    out_c = None
    if with_ctx:
        out_c = rwkv_readout(yc, rc, vc, 0.5 * (dirs_c[0][1] + dirs_c[1][1]), gc, lp, p_ctx.dtype)
    return out, out_c


def hyena_filters(n, lp):
    f32 = jnp.float32
    t = jnp.linspace(0.0, 1.0, n, dtype=f32)[:, None]
    bands = jnp.linspace(1e-4, HYENA_BANDS - 1, HYENA_BANDS, dtype=f32)
    ang = (2.0 * math.pi / n) * jnp.arange(n, dtype=f32)[:, None] * bands[None, :]
    feats = jnp.concatenate([t, jnp.cos(ang), -jnp.sin(ang)], axis=-1)
    freq = lp["hyena_freq"].astype(f32)
    h = jnp.sin(freq[0] * (feats @ lp["hyena_w1"].astype(f32) + lp["hyena_b1"].astype(f32)))
    h = jnp.sin(freq[1] * (h @ lp["hyena_w2"].astype(f32) + lp["hyena_b2"].astype(f32)))
    h = (h @ lp["hyena_w3"].astype(f32)).reshape(n, HYENA_ORDER, 2, HYENA_DIM)
    deltas = jnp.abs(jnp.linspace(math.log(HYENA_TARGET) / HYENA_SLOW,
                                  math.log(HYENA_TARGET) / HYENA_FAST, HYENA_DIM, dtype=f32))
    h = h * jnp.exp(-t * deltas)[:, None, None, :]
    return h / jnp.sum(jnp.abs(h), axis=(0, 2), keepdims=True)


def bidir_fftconv(u, h_fwd, h_bwd, bias):
    n = u.shape[1]
    kbuf = jnp.concatenate([h_fwd, jnp.zeros_like(h_fwd[:1]), h_bwd[1:][::-1]], axis=0)
    uf = jnp.fft.rfft(u.astype(jnp.float32), n=2 * n, axis=1)
    kf = jnp.fft.rfft(kbuf, n=2 * n, axis=0)
    y = jnp.fft.irfft(uf * kf[None], n=2 * n, axis=1)[:, :n]
    return (y + u.astype(jnp.float32) * bias.astype(jnp.float32)).astype(u.dtype)


def hyena_operator(p, lp):
    prev, nxt = centred_shift(p)
    ck = lp["hyena_conv"]
    p = ck[0] * prev + ck[1] * p + ck[2] * nxt + lp["hyena_conv_b"]
    v, x1, x2 = jnp.split(p, 3, axis=-1)
    h = hyena_filters(p.shape[1], lp)
    z = v
    for o, gate in enumerate((x1, x2)):
        z = gate * bidir_fftconv(z, h[:, o, 0], h[:, o, 1], lp["hyena_bias"][o])
    return z


def swa_project(p, tabs):
    b, n, _ = p.shape
    q, k, v = jnp.split(p, [SWA_HEADS * SWA_HEAD, (SWA_HEADS + SWA_KV_HEADS) * SWA_HEAD], axis=-1)
    q = q.reshape(b, n, SWA_HEADS, SWA_HEAD)
    k = k.reshape(b, n, SWA_KV_HEADS, SWA_HEAD)
    v = v.reshape(b, n, SWA_KV_HEADS, SWA_HEAD)
    if tabs is not None:
        q = axial_rope(q, tabs)
        k = axial_rope(k, tabs)
    return q.reshape(b, n, SWA_KV_HEADS, SWA_GROUP, SWA_HEAD), k, v


def sink_softmax(parts, sink):
    lead = parts[0].shape[:-1]
    s = jnp.concatenate(parts + [jnp.broadcast_to(sink[None, :, :, None, None], lead + (1,))], axis=-1)
    return jax.nn.softmax(s, axis=-1)[..., :-1]


def swa_mixer(p_lat, p_ctx, lp, tabs, with_ctx):
    f32 = jnp.float32
    scale = SWA_HEAD ** -0.5
    q, k, v = swa_project(p_lat, tabs)
    qc, kc, vc = swa_project(p_ctx, None)
    sink = lp["swa_sink"].astype(f32).reshape(SWA_KV_HEADS, SWA_GROUP)
    b, n = p_lat.shape[:2]
    nb = n // BLOCK
    pad = jnp.zeros((b, BLOCK, SWA_KV_HEADS, SWA_HEAD), k.dtype)
    kp = jnp.concatenate([pad, k, pad], axis=1)
    vp = jnp.concatenate([pad, v, pad], axis=1)
    qb = jnp.moveaxis(q.reshape(b, nb, BLOCK, SWA_KV_HEADS, SWA_GROUP, SWA_HEAD), 1, 0)
    offs = jnp.arange(3 * BLOCK)
    in_window = jnp.abs(offs[None, :] - BLOCK - jnp.arange(BLOCK)[:, None]) <= WINDOW

    def block(args):
        i, qi = args
        start = i * BLOCK
        ki = lax.dynamic_slice_in_dim(kp, start, 3 * BLOCK, axis=1)
        vi = lax.dynamic_slice_in_dim(vp, start, 3 * BLOCK, axis=1)
        key_pos = start - BLOCK + offs
        mask = in_window & ((key_pos >= 0) & (key_pos < n))[None, :]
        s_loc = jnp.einsum("bqkgd,bskd->bkgqs", qi, ki).astype(f32) * scale
        s_loc = jnp.where(mask, s_loc, NEG_INF)
        s_ctx = jnp.einsum("bqkgd,bckd->bkgqc", qi, kc).astype(f32) * scale
        pr = sink_softmax([s_loc, s_ctx], sink).astype(vi.dtype)
        return (jnp.einsum("bkgqs,bskd->bqkgd", pr[..., :3 * BLOCK], vi)
                + jnp.einsum("bkgqc,bckd->bqkgd", pr[..., 3 * BLOCK:], vc))

    out = lax.map(block, (jnp.arange(nb), qb))
    y = jnp.moveaxis(out, 0, 1).reshape(b, n, SWA_HEADS * SWA_HEAD)
    yc = None
    if with_ctx:
        s_cc = jnp.einsum("bqkgd,bckd->bkgqc", qc, kc).astype(f32) * scale
        pr = sink_softmax([s_cc], sink).astype(vc.dtype)
        yc = jnp.einsum("bkgqc,bckd->bqkgd", pr, vc).reshape(b, p_ctx.shape[1], SWA_HEADS * SWA_HEAD)
    return y, yc


def merge_branches(gates, branches, lp):
    d = gates.shape[-1] // N_BRANCH
    merged = None
    for i, y in enumerate(branches):
        gi = jax.nn.sigmoid(gates[..., i * d:(i + 1) * d] + lp["b_gate"][i])
        term = gi * (y @ lp["w_branch"][i])
        merged = term if merged is None else merged + term
    return merged @ lp["w_out"]


def token_mixing(u, uc, lp, tabs_mla, tabs_swa, with_ctx):
    gates, pa, pb, ph, pd = jnp.split(u @ lp["w_in"], IN_SPLITS, axis=-1)
    gates_c, pa_c, pb_c, ph_c, pd_c = jnp.split(uc @ lp["w_in"], IN_SPLITS, axis=-1)
    ya, ya_c = mla_mixer(pa, pa_c, lp, tabs_mla, with_ctx)
    yb, yb_c = rwkv_mixer(pb, pb_c, lp, with_ctx)
    yh = hyena_operator(ph, lp)
    yd, yd_c = swa_mixer(pd, pd_c, lp, tabs_swa, with_ctx)
    y = merge_branches(gates, (ya, yb, yh, yd), lp)
    y_ctx = None
    if with_ctx:
        yh_c = hyena_operator(ph_c, lp)
        y_ctx = merge_branches(gates_c, (ya_c, yb_c, yh_c, yd_c), lp)
    return y, y_ctx


def trunk_layer(x, xc, c, c_ctx, lp, tabs_mla, tabs_swa, with_ctx):
    b, d = c.shape
    m = (jax.nn.silu(c) @ lp["w_mod"] + lp["b_mod"]).reshape(b, N_MOD, 1, d)
    mc = (jax.nn.silu(c_ctx) @ lp["w_mod"] + lp["b_mod"]).reshape(N_MOD, d)
    g = lp["norm_g"]
    w13 = lp["ffn_w13"]
    w2 = lp["ffn_w2"]
    x = ffn_half_step(x, m[:, 0], m[:, 1], m[:, 2], g[0], g[1], w13[0], w2[0])
    xc = ffn_half_step(xc, mc[0], mc[1], mc[2], g[0], g[1], w13[0], w2[0])
    u = modulate(rmsnorm(x, g[2]), m[:, 3], m[:, 4])
    uc = modulate(rmsnorm(xc, g[2]), mc[3], mc[4])
    y, y_ctx = token_mixing(u, uc, lp, tabs_mla, tabs_swa, with_ctx)
    x = x + m[:, 5] * rmsnorm(y, g[3])
    x = ffn_half_step(x, m[:, 6], m[:, 7], m[:, 8], g[4], g[5], w13[1], w2[1])
    if with_ctx:
        xc = xc + mc[5] * rmsnorm(y_ctx, g[3])
        xc = ffn_half_step(xc, mc[6], mc[7], mc[8], g[4], g[5], w13[1], w2[1])
    return x, xc


def setup_inputs(seed: int = 0) -> dict:
    key = jax.random.key(seed)
    ks = iter(jax.random.split(key, 48))
    f32 = jnp.float32

    def nrm(shape, scale):
        return scale * jax.random.normal(next(ks), shape, f32)

    def gain(shape):
        return 1.0 + nrm(shape, 0.02)

    L, D = DEPTH, D_MODEL
    return {
        "x": nrm((BATCH, SEQ, D), 1.0),
        "c": nrm((BATCH, D), 1.0),
        "ctx": nrm((BATCH, CTX_LEN, D), 1.0),
        "c_ctx": nrm((D,), 1.0),
        "w_mod": nrm((L, D, N_MOD * D), 0.5 * D ** -0.5),
        "b_mod": nrm((L, N_MOD * D), 0.02),
        "norm_g": gain((L, 6, D)),
        "ffn_w13": nrm((L, 2, D, 2 * FF_DIM), D ** -0.5),
        "ffn_w2": nrm((L, 2, FF_DIM, D), FF_DIM ** -0.5),
        "w_in": nrm((L, D, IN_COLS), D ** -0.5),
        "b_gate": nrm((L, N_BRANCH, D), 0.02),
        "mla_norm_q": gain((L, MLA_Q_RANK)),
        "mla_norm_kv": gain((L, MLA_KV_RANK)),
        "mla_w_uq": nrm((L, MLA_Q_RANK, MLA_HEADS * (MLA_NOPE + MLA_ROPE)), MLA_Q_RANK ** -0.5),
        "mla_w_ukv": nrm((L, MLA_KV_RANK, MLA_HEADS * (MLA_NOPE + MLA_V)), MLA_KV_RANK ** -0.5),
        "rwkv_mu": jax.random.uniform(next(ks), (L, 2, RWKV_COLS), f32, 0.0, 0.5),
        "rwkv_w0": nrm((L, 2, RWKV_DIM), 0.5),
        "rwkv_w_up": nrm((L, 2, DECAY_LORA, RWKV_DIM), DECAY_LORA ** -0.5),
        "rwkv_a0": nrm((L, 2, RWKV_DIM), 0.5),
        "rwkv_a_up": nrm((L, 2, AAA_LORA, RWKV_DIM), AAA_LORA ** -0.5),
        "rwkv_g_up": nrm((L, GATE_LORA, RWKV_DIM), GATE_LORA ** -0.5),
        "rwkv_kvec": gain((L, 2, RWKV_DIM)),
        "rwkv_r_k": nrm((L, RWKV_DIM), 0.1),
        "rwkv_ln_g": gain((L, RWKV_DIM)),
        "rwkv_ln_b": nrm((L, RWKV_DIM), 0.02),
        "hyena_conv": nrm((L, SHORT_CONV, HYENA_COLS), SHORT_CONV ** -0.5),
        "hyena_conv_b": nrm((L, HYENA_COLS), 0.02),
        "hyena_w1": nrm((L, HYENA_EMB, HYENA_FW), HYENA_EMB ** -0.5),
        "hyena_b1": nrm((L, HYENA_FW), 0.02),
        "hyena_w2": nrm((L, HYENA_FW, HYENA_FW), HYENA_FW ** -0.5),
        "hyena_b2": nrm((L, HYENA_FW), 0.02),
        "hyena_w3": nrm((L, HYENA_FW, HYENA_ORDER * 2 * HYENA_DIM), HYENA_FW ** -0.5),
        "hyena_freq": gain((L, 2, HYENA_FW)),
        "hyena_bias": nrm((L, HYENA_ORDER, HYENA_DIM), 1.0),
        "swa_sink": nrm((L, SWA_HEADS), 0.5),
        "w_branch": nrm((L, N_BRANCH, BRANCH_DIM, D), BRANCH_DIM ** -0.5),
        "w_out": nrm((L, D, D), D ** -0.5),
    }


def reference(x, c, ctx, c_ctx, w_mod, b_mod, norm_g, ffn_w13, ffn_w2, w_in, b_gate,
              mla_norm_q, mla_norm_kv, mla_w_uq, mla_w_ukv,
              rwkv_mu, rwkv_w0, rwkv_w_up, rwkv_a0, rwkv_a_up, rwkv_g_up, rwkv_kvec, rwkv_r_k,
              rwkv_ln_g, rwkv_ln_b,
              hyena_conv, hyena_conv_b, hyena_w1, hyena_b1, hyena_w2, hyena_b2, hyena_w3,
              hyena_freq, hyena_bias,
              swa_sink, w_branch, w_out):
    ROWS = x.shape[1] // GRID_W
    tabs_mla = axial_rope_tables(ROWS, MLA_ROPE)
    tabs_swa = axial_rope_tables(ROWS, SWA_HEAD)
    xc = ctx
    for l in range(DEPTH):
        lp = {
            "w_mod": w_mod[l], "b_mod": b_mod[l], "norm_g": norm_g[l],
            "ffn_w13": ffn_w13[l], "ffn_w2": ffn_w2[l], "w_in": w_in[l], "b_gate": b_gate[l],
            "mla_norm_q": mla_norm_q[l], "mla_norm_kv": mla_norm_kv[l],
            "mla_w_uq": mla_w_uq[l], "mla_w_ukv": mla_w_ukv[l],
            "rwkv_mu": rwkv_mu[l], "rwkv_w0": rwkv_w0[l], "rwkv_w_up": rwkv_w_up[l],
            "rwkv_a0": rwkv_a0[l], "rwkv_a_up": rwkv_a_up[l], "rwkv_g_up": rwkv_g_up[l],
            "rwkv_kvec": rwkv_kvec[l], "rwkv_r_k": rwkv_r_k[l],
            "rwkv_ln_g": rwkv_ln_g[l], "rwkv_ln_b": rwkv_ln_b[l],
            "hyena_conv": hyena_conv[l], "hyena_conv_b": hyena_conv_b[l],
            "hyena_w1": hyena_w1[l], "hyena_b1": hyena_b1[l], "hyena_w2": hyena_w2[l],
            "hyena_b2": hyena_b2[l], "hyena_w3": hyena_w3[l], "hyena_freq": hyena_freq[l],
            "hyena_bias": hyena_bias[l],
            "swa_sink": swa_sink[l], "w_branch": w_branch[l], "w_out": w_out[l],
        }
        x, xc = trunk_layer(x, xc, c, c_ctx, lp, tabs_mla, tabs_swa, with_ctx=(l < DEPTH - 1))
    return x
```

```python
import functools
import math

import numpy as np
import jax
import jax.numpy as jnp
from jax import lax
from jax.experimental import pallas as pl
from jax.experimental.pallas import tpu as pltpu

F32 = jnp.float32
BF16 = jnp.bfloat16

D_MODEL = 1024
GRID_W = 64
N_BRANCH = 4
N_MOD = 9
FF_DIM = 2816
EPS = 1e-6
ROPE_BASE = 10000.0
NEG_INF = -1e30
BRANCH_DIM = 512
MLA_HEADS = 8
MLA_NOPE = 64
MLA_ROPE = 32
MLA_V = 64
MLA_Q_RANK = 256
MLA_KV_RANK = 128
RWKV_HEADS = 8
RWKV_HEAD = 64
RWKV_DIM = RWKV_HEADS * RWKV_HEAD
DECAY_LORA = 64
AAA_LORA = 64
GATE_LORA = 128
RWKV_LN_EPS = 64e-5
HYENA_DIM = 512
HYENA_ORDER = 2
HYENA_EMB = 33
HYENA_BANDS = (HYENA_EMB - 1) // 2
HYENA_FW = 64
HYENA_TARGET = 1e-2
HYENA_FAST = 0.3
HYENA_SLOW = 1.5
SWA_HEADS = 8
SWA_KV_HEADS = 2
SWA_HEAD = 64
SWA_GROUP = SWA_HEADS // SWA_KV_HEADS
WINDOW = 128
GATE_COLS = N_BRANCH * D_MODEL
MLA_COLS = MLA_Q_RANK + MLA_KV_RANK + MLA_ROPE
RWKV_COLS = 3 * RWKV_DIM + DECAY_LORA + AAA_LORA + GATE_LORA
HYENA_COLS = 3 * HYENA_DIM
SWA_COLS = (SWA_HEADS + 2 * SWA_KV_HEADS) * SWA_HEAD

LANES = 128
V7X_VMEM_LIMIT = 56 * 1024 * 1024

TM = 256
FF_CHUNK = 256
IN_CHUNK = 512
CHUNK = 64

P_GATE = 0
P_HY = 4096
P_RKV = 5632
P_SWAQ = 7168
P_LORA = 7680
P_CQ = 7936
P_CKV = 8192
P_KR = 8320
P_SWAK = 8448
P_SWAV = 8576
P_COLS = 8704
_SHIFT_COLS = ((P_HY, P_SWAQ), (P_LORA, P_CQ))
_SHIFT_CHUNKS = [any(lo < (j + 1) * IN_CHUNK and j * IN_CHUNK < hi for lo, hi in _SHIFT_COLS)
                 for j in range(P_COLS // IN_CHUNK)]


def _cparams(sem, vmem=V7X_VMEM_LIMIT):
    return pltpu.CompilerParams(dimension_semantics=sem, vmem_limit_bytes=vmem)


def _mm(a, b):
    return jnp.dot(a.astype(BF16), b.astype(BF16), preferred_element_type=F32)


def _mm_nt(a, b):
    return lax.dot_general(a.astype(BF16), b.astype(BF16), (((1,), (1,)), ((), ())),
                           preferred_element_type=F32)


def _mm_tn(a, b):
    return lax.dot_general(a.astype(BF16), b.astype(BF16), (((0,), (0,)), ((), ())),
                           preferred_element_type=F32)


def _mm_f32(a, b):
    return jnp.dot(a, b, preferred_element_type=F32, precision=lax.Precision.HIGHEST)


def _rms(x, g):
    return x * lax.rsqrt(jnp.mean(x * x, axis=-1, keepdims=True) + EPS) * g


def _sigmoid(x):
    return 1.0 / (1.0 + jnp.exp(-x))


def _mod_kernel(c_ref, w_ref, b_ref, o_ref):
    c = c_ref[...]
    o_ref[...] = _mm(c * _sigmoid(c), w_ref[...]) + b_ref[...]


def _modulation(c_all, w_mod, b_mod):
    r = c_all.shape[0]
    rp = -(-r // 8) * 8
    c_pad = jnp.zeros((rp, D_MODEL), F32).at[:r].set(c_all)
    tn = 1024
    out = pl.pallas_call(
        _mod_kernel,
        grid=(N_MOD * D_MODEL // tn,),
        in_specs=[pl.BlockSpec((rp, D_MODEL), lambda j: (0, 0)),
                  pl.BlockSpec((D_MODEL, tn), lambda j: (0, j)),
                  pl.BlockSpec((1, tn), lambda j: (0, j))],
        out_specs=pl.BlockSpec((rp, tn), lambda j: (0, j)),
        out_shape=jax.ShapeDtypeStruct((rp, N_MOD * D_MODEL), F32),
        compiler_params=_cparams(("arbitrary",)),
        name="modulation",
    )(c_pad, w_mod, b_mod.reshape(1, -1))
    return out[:r].reshape(r, N_MOD, D_MODEL)


def _ffn_kernel(x_ref, mod_ref, g_ref, w13_ref, w2_ref, o_ref, *, mod0, g0):
    x = x_ref[0]
    shift = mod_ref[0, mod0:mod0 + 1]
    scale = mod_ref[0, mod0 + 1:mod0 + 2]
    gate = mod_ref[0, mod0 + 2:mod0 + 3]
    u = (_rms(x, g_ref[g0:g0 + 1]) * (1.0 + scale) + shift).astype(BF16)
    acc = jnp.zeros(x.shape, F32)
    for f in range(FF_DIM // FF_CHUNK):
        lo = f * FF_CHUNK
        a = jnp.dot(u, w13_ref[:, lo:lo + FF_CHUNK], preferred_element_type=F32)
        b = jnp.dot(u, w13_ref[:, FF_DIM + lo:FF_DIM + lo + FF_CHUNK], preferred_element_type=F32)
        h = (a * _sigmoid(a) * b).astype(BF16)
        acc = acc + jnp.dot(h, w2_ref[lo:lo + FF_CHUNK, :], preferred_element_type=F32)
    o_ref[0] = x + 0.5 * gate * _rms(acc, g_ref[g0 + 1:g0 + 2])


def _ffn(x, mods, norm_g, w13, w2, *, mod0, g0, n_ctx_tiles, row_off):
    b, s, _ = x.shape
    nt = s // TM - row_off
    n_lat = mods.shape[0] - 1

    def mod_idx(bi, i):
        return (jnp.where(i + row_off < n_ctx_tiles, n_lat, bi), 0, 0)

    return pl.pallas_call(
        functools.partial(_ffn_kernel, mod0=mod0, g0=g0),
        grid=(b, nt),
        in_specs=[pl.BlockSpec((1, TM, D_MODEL), lambda bi, i: (bi, i + row_off, 0)),
                  pl.BlockSpec((1, N_MOD, D_MODEL), mod_idx),
                  pl.BlockSpec((6, D_MODEL), lambda bi, i: (0, 0)),
                  pl.BlockSpec(memory_space=pltpu.VMEM),
                  pl.BlockSpec(memory_space=pltpu.VMEM)],
        out_specs=pl.BlockSpec((1, TM, D_MODEL), lambda bi, i: (bi, i, 0)),
        out_shape=jax.ShapeDtypeStruct((b, nt * TM, D_MODEL), F32),
        compiler_params=_cparams(("parallel", "parallel")),
        name="ffn_half_step",
    )(x, mods, norm_g, w13, w2)


def _shift_rows(p, first_row, last_row):
    rows = lax.broadcasted_iota(jnp.int32, p.shape, 0)
    prev = jnp.where(rows == 0, first_row, pltpu.roll(p, 1, axis=0))
    nxt = jnp.where(rows == p.shape[0] - 1, last_row, pltpu.roll(p, p.shape[0] - 1, axis=0))
    return prev, nxt


def _inproj_kernel(x_ref, xp_ref, xn_ref, mod_ref, g_ref, w_ref, coef_ref, o_ref, *, n_ctx_tiles):
    i = pl.program_id(1)
    n_tiles = pl.num_programs(1)
    has_prev = jnp.where((i == 0) | (i == n_ctx_tiles), 0.0, 1.0)
    has_next = jnp.where((i == n_ctx_tiles - 1) | (i == n_tiles - 1), 0.0, 1.0)
    shift = mod_ref[0, 3:4]
    scale = 1.0 + mod_ref[0, 4:5]
    g = g_ref[2:3]
    u = (_rms(x_ref[0], g) * scale + shift).astype(BF16)
    u_prev = (_rms(xp_ref[0], g) * scale + shift).astype(BF16)
    u_next = (_rms(xn_ref[0], g) * scale + shift).astype(BF16)
    for j in range(P_COLS // IN_CHUNK):
        cols = slice(j * IN_CHUNK, (j + 1) * IN_CHUNK)
        p = jnp.dot(u, w_ref[:, cols], preferred_element_type=F32)
        if _SHIFT_CHUNKS[j]:
            p_first = jnp.dot(u_prev, w_ref[:, cols], preferred_element_type=F32)[7:8] * has_prev
            p_last = jnp.dot(u_next, w_ref[:, cols], preferred_element_type=F32)[0:1] * has_next
            prev, nxt = _shift_rows(p, p_first, p_last)
            p = (coef_ref[0:1, cols] * p + coef_ref[1:2, cols] * prev + coef_ref[2:3, cols] * nxt
                 + coef_ref[3:4, cols])
        o_ref[0, :, cols] = p


def _inproj(x, mods, norm_g, w_in_p, coef, *, n_ctx_tiles):
    b, s, _ = x.shape
    n_lat = mods.shape[0] - 1
    r8 = TM // 8
    return pl.pallas_call(
        functools.partial(_inproj_kernel, n_ctx_tiles=n_ctx_tiles),
        grid=(b, s // TM),
        in_specs=[pl.BlockSpec((1, TM, D_MODEL), lambda bi, i: (bi, i, 0)),
                  pl.BlockSpec((1, 8, D_MODEL), lambda bi, i: (bi, jnp.maximum(i * r8 - 1, 0), 0)),
                  pl.BlockSpec((1, 8, D_MODEL),
                               lambda bi, i: (bi, jnp.minimum((i + 1) * r8, s // 8 - 1), 0)),
                  pl.BlockSpec((1, N_MOD, D_MODEL),
                               lambda bi, i: (jnp.where(i < n_ctx_tiles, n_lat, bi), 0, 0)),
                  pl.BlockSpec((6, D_MODEL), lambda bi, i: (0, 0)),
                  pl.BlockSpec(memory_space=pltpu.VMEM),
                  pl.BlockSpec((4, P_COLS), lambda bi, i: (0, 0))],
        out_specs=pl.BlockSpec((1, TM, P_COLS), lambda bi, i: (bi, i, 0)),
        out_shape=jax.ShapeDtypeStruct((b, s, P_COLS), F32),
        compiler_params=_cparams(("parallel", "parallel")),
        name="in_projection",
    )(x, x, x, mods, norm_g, w_in_p, coef)


def _shift_coefficients(rwkv_mu, hyena_conv, hyena_conv_b):
    mu = rwkv_mu.astype(F32)
    coef = jnp.zeros((4, P_COLS), F32).at[0].set(1.0)
    for off, sl in ((P_RKV, slice(0, 3 * RWKV_DIM)), (P_LORA, slice(3 * RWKV_DIM, RWKV_COLS))):
        width = sl.stop - sl.start
        coef = coef.at[0, off:off + width].set(1.0 - mu[0, sl] - mu[1, sl])
        coef = coef.at[1, off:off + width].set(mu[0, sl])
        coef = coef.at[2, off:off + width].set(mu[1, sl])
    hy = slice(P_HY, P_HY + HYENA_COLS)
    coef = coef.at[0, hy].set(hyena_conv[1]).at[1, hy].set(hyena_conv[0]).at[2, hy].set(hyena_conv[2])
    return coef.at[3, hy].set(hyena_conv_b)


def _permute_w_in(w_in):
    o_mla = GATE_COLS
    o_rwkv = o_mla + MLA_COLS
    o_hy = o_rwkv + RWKV_COLS
    o_swa = o_hy + HYENA_COLS
    z = lambda n: jnp.zeros((D_MODEL, n), w_in.dtype)
    parts = [
        w_in[:, :GATE_COLS],
        w_in[:, o_hy:o_hy + HYENA_COLS],
        w_in[:, o_rwkv:o_rwkv + 3 * RWKV_DIM],
        w_in[:, o_swa:o_swa + SWA_HEADS * SWA_HEAD],
        w_in[:, o_rwkv + 3 * RWKV_DIM:o_rwkv + RWKV_COLS],
        w_in[:, o_mla:o_mla + MLA_Q_RANK],
        w_in[:, o_mla + MLA_Q_RANK:o_mla + MLA_Q_RANK + MLA_KV_RANK],
        z(MLA_NOPE), w_in[:, o_mla + MLA_Q_RANK + MLA_KV_RANK:o_mla + MLA_COLS],
        z(LANES - MLA_NOPE - MLA_ROPE),
        w_in[:, o_swa + SWA_HEADS * SWA_HEAD:o_swa + SWA_COLS],
    ]
    out = jnp.concatenate(parts, axis=1)
    assert out.shape[1] == P_COLS
    return out


def _rope_tables(n_lat, n_ctx, rot_dim, lane0, period):
    rows = n_lat // GRID_W
    row = jnp.repeat(jnp.arange(rows, dtype=F32), GRID_W)
    col = jnp.tile(jnp.arange(GRID_W, dtype=F32), rows)
    axis_dim = rot_dim // 2
    h = axis_dim // 2
    inv_freq = ROPE_BASE ** (-jnp.arange(0, axis_dim, 2, dtype=F32) / axis_dim)
    ang_r = row[:, None] * inv_freq
    ang_c = col[:, None] * inv_freq
    cos_rot = jnp.concatenate([jnp.cos(ang_r)] * 2 + [jnp.cos(ang_c)] * 2, axis=1)
    zeros = jnp.zeros_like(ang_r)
    sin_a = jnp.concatenate([-jnp.sin(ang_r), zeros, -jnp.sin(ang_c), zeros], axis=1)
    sin_b = jnp.concatenate([zeros, jnp.sin(ang_r), zeros, jnp.sin(ang_c)], axis=1)

    def widen(t, fill):
        g = jnp.full((n_lat, period), fill, F32).at[:, lane0:lane0 + rot_dim].set(t)
        g = jnp.tile(g, (1, LANES // period))
        ctx = jnp.full((n_ctx, LANES), fill, F32)
        return jnp.concatenate([ctx, g], axis=0)

    return widen(cos_rot, 1.0), widen(sin_a, 0.0), widen(sin_b, 0.0), h


def _rope128(x, cos, sin_a, sin_b, h):
    return x * cos + pltpu.roll(x, LANES - h, axis=1) * sin_a + pltpu.roll(x, h, axis=1) * sin_b


MLA_SCALE = (MLA_NOPE + MLA_ROPE) ** -0.5


def _mla_prep_kernel(cq_ref, ckv_ref, kr_ref, gq_ref, gkv_ref, wq_ref, wk_ref, wv_ref,
                     cos_ref, sa_ref, sb_ref, q_ref, k_ref, v_ref, *, h):
    cos, sa, sb = cos_ref[...], sa_ref[...], sb_ref[...]
    cq = _rms(cq_ref[0], gq_ref[...]).astype(BF16)
    ckv = _rms(ckv_ref[0], gkv_ref[...]).astype(BF16)
    q = jnp.dot(cq, wq_ref[...], preferred_element_type=F32)
    k = jnp.dot(ckv, wk_ref[...], preferred_element_type=F32)
    kr = _rope128(kr_ref[0], cos, sa, sb, h)
    for hd in range(MLA_HEADS):
        sl = slice(hd * LANES, (hd + 1) * LANES)
        q_ref[0, :, sl] = (_rope128(q[:, sl], cos, sa, sb, h) * MLA_SCALE).astype(BF16)
        k_ref[0, :, sl] = (k[:, sl] + kr).astype(BF16)
    v_ref[0] = jnp.dot(ckv, wv_ref[...], preferred_element_type=F32).astype(BF16)


def _mla_prep(p, norm_q, norm_kv, w_uq, w_ukv, tabs):
    b, s, _ = p.shape
    cos, sa, sb, h = tabs
    hq = MLA_NOPE + MLA_ROPE
    wq = jnp.zeros((MLA_Q_RANK, MLA_HEADS, LANES), F32).at[:, :, :hq].set(
        w_uq.reshape(MLA_Q_RANK, MLA_HEADS, hq)).reshape(MLA_Q_RANK, MLA_HEADS * LANES).astype(BF16)
    wkv = w_ukv.reshape(MLA_KV_RANK, MLA_HEADS, MLA_NOPE + MLA_V)
    wk = jnp.zeros((MLA_KV_RANK, MLA_HEADS, LANES), F32).at[:, :, :MLA_NOPE].set(
        wkv[:, :, :MLA_NOPE]).reshape(MLA_KV_RANK, MLA_HEADS * LANES).astype(BF16)
    wv = wkv[:, :, MLA_NOPE:].reshape(MLA_KV_RANK, MLA_HEADS * MLA_V).astype(BF16)
    full = lambda shape: pl.BlockSpec(shape, lambda bi, i: (0,) * len(shape))
    tab = pl.BlockSpec((TM, LANES), lambda bi, i: (i, 0))
    return pl.pallas_call(
        functools.partial(_mla_prep_kernel, h=h),
        grid=(b, s // TM),
        in_specs=[pl.BlockSpec((1, TM, MLA_Q_RANK), lambda bi, i: (bi, i, P_CQ // MLA_Q_RANK)),
                  pl.BlockSpec((1, TM, LANES), lambda bi, i: (bi, i, P_CKV // LANES)),
                  pl.BlockSpec((1, TM, LANES), lambda bi, i: (bi, i, P_KR // LANES)),
                  full((1, MLA_Q_RANK)), full((1, MLA_KV_RANK)),
                  full(wq.shape), full(wk.shape), full(wv.shape), tab, tab, tab],
        out_specs=[pl.BlockSpec((1, TM, MLA_HEADS * LANES), lambda bi, i: (bi, i, 0)),
                   pl.BlockSpec((1, TM, MLA_HEADS * LANES), lambda bi, i: (bi, i, 0)),
                   pl.BlockSpec((1, TM, MLA_HEADS * MLA_V), lambda bi, i: (bi, i, 0))],
        out_shape=[jax.ShapeDtypeStruct((b, s, MLA_HEADS * LANES), BF16),
                   jax.ShapeDtypeStruct((b, s, MLA_HEADS * LANES), BF16),
                   jax.ShapeDtypeStruct((b, s, MLA_HEADS * MLA_V), BF16)],
        compiler_params=_cparams(("parallel", "parallel")),
        name="mla_prep",
    )(p, p, p, norm_q.reshape(1, -1), norm_kv.reshape(1, -1), wq, wk, wv, cos, sa, sb)


def _mla_attn_kernel(q_ref, k_ref, v_ref, o_ref, *, n_ctx, q_off):
    i = pl.program_id(2) + q_off
    v = v_ref[0]
    s_len = v.shape[0]
    tq = q_ref.shape[1]
    key_ok = (lax.broadcasted_iota(jnp.int32, (tq, s_len), 1) < n_ctx) | (i * tq >= n_ctx)
    outs = []
    for hd in range(2):
        sl = slice(hd * LANES, (hd + 1) * LANES)
        s = lax.dot_general(q_ref[0, :, sl], k_ref[0, :, sl], (((1,), (1,)), ((), ())),
                            preferred_element_type=F32)
        s = jnp.where(key_ok, s, NEG_INF)
        m = jnp.max(s, axis=-1, keepdims=True)
        e = jnp.exp(s - m)
        pr = (e / jnp.sum(e, axis=-1, keepdims=True)).astype(BF16)
        outs.append(jnp.dot(pr, v, preferred_element_type=F32))
    lane = lax.broadcasted_iota(jnp.int32, outs[0].shape, 1)
    o_ref[0] = jnp.where(lane < MLA_V, outs[0], outs[1]).astype(BF16)


def _mla_attention(q, k, v, *, n_ctx, q_off):
    b, s, _ = q.shape
    nq = s // TM - q_off
    return pl.pallas_call(
        functools.partial(_mla_attn_kernel, n_ctx=n_ctx, q_off=q_off),
        grid=(b, MLA_HEADS // 2, nq),
        in_specs=[pl.BlockSpec((1, TM, 2 * LANES), lambda bi, hp, i: (bi, i + q_off, hp)),
                  pl.BlockSpec((1, s, 2 * LANES), lambda bi, hp, i: (bi, 0, hp)),
                  pl.BlockSpec((1, s, LANES), lambda bi, hp, i: (bi, 0, hp))],
        out_specs=pl.BlockSpec((1, TM, LANES), lambda bi, hp, i: (bi, i, hp)),
        out_shape=jax.ShapeDtypeStruct((b, nq * TM, MLA_HEADS * MLA_V), BF16),
        compiler_params=_cparams(("parallel", "parallel", "parallel")),
        name="mla_attention",
    )(q, k, v)


SWA_SCALE = SWA_HEAD ** -0.5
SWA_TQ = 128


def _swa_prep_kernel(q_ref, k_ref, v_ref, cos_ref, sa_ref, sb_ref, qo_ref, ko_ref, vo_ref, *, h):
    cos, sa, sb = cos_ref[...], sa_ref[...], sb_ref[...]
    lane = lax.broadcasted_iota(jnp.int32, cos.shape, 1)
    low = lane < SWA_HEAD
    for j in range(SWA_HEADS // 2):
        blk = _rope128(q_ref[0, :, j * LANES:(j + 1) * LANES], cos, sa, sb, h) * SWA_SCALE
        qo_ref[0, :, (2 * j) * LANES:(2 * j + 1) * LANES] = jnp.where(low, blk, 0.0).astype(BF16)
        qo_ref[0, :, (2 * j + 1) * LANES:(2 * j + 2) * LANES] = jnp.where(
            low, pltpu.roll(blk, SWA_HEAD, axis=1), 0.0).astype(BF16)
    kb = _rope128(k_ref[0], cos, sa, sb, h)
    ko_ref[0, :, :LANES] = jnp.where(low, kb, 0.0).astype(BF16)
    ko_ref[0, :, LANES:] = jnp.where(low, pltpu.roll(kb, SWA_HEAD, axis=1), 0.0).astype(BF16)
    vb = v_ref[0]
    vr = pltpu.roll(vb, SWA_HEAD, axis=1)
    vo_ref[0, :, 0 * LANES:1 * LANES] = jnp.where(low, vb, 0.0).astype(BF16)
    vo_ref[0, :, 1 * LANES:2 * LANES] = jnp.where(low, 0.0, vr).astype(BF16)
    vo_ref[0, :, 2 * LANES:3 * LANES] = jnp.where(low, vr, 0.0).astype(BF16)
    vo_ref[0, :, 3 * LANES:4 * LANES] = jnp.where(low, 0.0, vb).astype(BF16)


def _swa_prep(p, tabs):
    b, s, _ = p.shape
    cos, sa, sb, h = tabs
    tab = pl.BlockSpec((TM, LANES), lambda bi, i: (i, 0))
    nq = SWA_HEADS * SWA_HEAD
    return pl.pallas_call(
        functools.partial(_swa_prep_kernel, h=h),
        grid=(b, s // TM),
        in_specs=[pl.BlockSpec((1, TM, nq), lambda bi, i: (bi, i, P_SWAQ // nq)),
                  pl.BlockSpec((1, TM, LANES), lambda bi, i: (bi, i, P_SWAK // LANES)),
                  pl.BlockSpec((1, TM, LANES), lambda bi, i: (bi, i, P_SWAV // LANES)),
                  tab, tab, tab],
        out_specs=[pl.BlockSpec((1, TM, SWA_HEADS * LANES), lambda bi, i: (bi, i, 0)),
                   pl.BlockSpec((1, TM, SWA_KV_HEADS * LANES), lambda bi, i: (bi, i, 0)),
                   pl.BlockSpec((1, TM, 4 * LANES), lambda bi, i: (bi, i, 0))],
        out_shape=[jax.ShapeDtypeStruct((b, s, SWA_HEADS * LANES), BF16),
                   jax.ShapeDtypeStruct((b, s, SWA_KV_HEADS * LANES), BF16),
                   jax.ShapeDtypeStruct((b, s, 4 * LANES), BF16)],
        compiler_params=_cparams(("parallel", "parallel")),
        name="swa_prep",
    )(p, p, p, cos, sa, sb)


def _swa_attn_kernel(sink_ref, q_ref, k_ref, v_ref, o_ref, *, n_ctx, q_off):
    i = pl.program_id(1) + q_off
    s_len = k_ref.shape[1]
    tq = SWA_TQ
    n_loc = tq + 2 * WINDOW
    r0 = i * tq
    is_lat = r0 >= n_ctx
    start = pl.multiple_of(jnp.clip(r0 - WINDOW, 0, s_len - n_loc), LANES)
    qpos = r0 - n_ctx + lax.broadcasted_iota(jnp.int32, (tq, n_loc), 0)
    kpos = start - n_ctx + lax.broadcasted_iota(jnp.int32, (tq, n_loc), 1)
    loc_ok = (jnp.abs(kpos - qpos) <= WINDOW) & (kpos >= 0) & is_lat
    k_loc = k_ref[0, pl.ds(start, n_loc), :]
    v_loc = v_ref[0, pl.ds(start, n_loc), :]
    k_ctx = k_ref[0, 0:n_ctx, :]
    v_ctx = v_ref[0, 0:n_ctx, :]
    for j in range(SWA_HEADS // 2):
        acc = None
        for par in range(2):
            hd = 2 * j + par
            g = hd // SWA_GROUP
            q = q_ref[0, :, hd * LANES:(hd + 1) * LANES]
            kg = slice(g * LANES, (g + 1) * LANES)
            vg = slice((2 * g + par) * LANES, (2 * g + par + 1) * LANES)
            s_loc = lax.dot_general(q, k_loc[:, kg], (((1,), (1,)), ((), ())),
                                    preferred_element_type=F32)
            s_loc = jnp.where(loc_ok, s_loc, NEG_INF)
            s_ctx = lax.dot_general(q, k_ctx[:, kg], (((1,), (1,)), ((), ())),
                                    preferred_element_type=F32)
            sink = sink_ref[hd]
            m = jnp.maximum(jnp.maximum(jnp.max(s_loc, axis=-1, keepdims=True),
                                        jnp.max(s_ctx, axis=-1, keepdims=True)), sink)
            e_loc = jnp.exp(s_loc - m)
            e_ctx = jnp.exp(s_ctx - m)
            den = (jnp.sum(e_loc, axis=-1, keepdims=True) + jnp.sum(e_ctx, axis=-1, keepdims=True)
                   + jnp.exp(sink - m))
            inv = 1.0 / den
            o = (jnp.dot((e_loc * inv).astype(BF16), v_loc[:, vg], preferred_element_type=F32)
                 + jnp.dot((e_ctx * inv).astype(BF16), v_ctx[:, vg], preferred_element_type=F32))
            acc = o if acc is None else acc + o
        o_ref[0, :, j * LANES:(j + 1) * LANES] = acc.astype(BF16)


def _swa_attention(q, k, v, sink, *, n_ctx, q_off):
    b, s, _ = q.shape
    nq = s // SWA_TQ - q_off
    return pl.pallas_call(
        functools.partial(_swa_attn_kernel, n_ctx=n_ctx, q_off=q_off),
        grid=(b, nq),
        in_specs=[pl.BlockSpec(memory_space=pltpu.SMEM),
                  pl.BlockSpec((1, SWA_TQ, SWA_HEADS * LANES), lambda bi, i: (bi, i + q_off, 0)),
                  pl.BlockSpec((1, s, SWA_KV_HEADS * LANES), lambda bi, i: (bi, 0, 0)),
                  pl.BlockSpec((1, s, 4 * LANES), lambda bi, i: (bi, 0, 0))],
        out_specs=pl.BlockSpec((1, SWA_TQ, SWA_HEADS * SWA_HEAD), lambda bi, i: (bi, i, 0)),
        out_shape=jax.ShapeDtypeStruct((b, nq * SWA_TQ, SWA_HEADS * SWA_HEAD), BF16),
        compiler_params=_cparams(("parallel", "parallel")),
        name="swa_attention",
    )(sink, q, k, v)


N_PAIR = RWKV_HEADS // 2
N_DOUBLINGS = int(math.log2(CHUNK))


def _softplus(x):
    return jnp.maximum(x, 0.0) + jnp.log(1.0 + jnp.exp(-jnp.abs(x)))


def _headsum(x, bd):
    hi = x.astype(BF16)
    lo = (x - hi.astype(F32)).astype(BF16)
    return (jnp.dot(hi, bd, preferred_element_type=F32) + jnp.dot(lo, bd, preferred_element_type=F32))


def _chunk_cumsum(x, reverse):
    rows = lax.broadcasted_iota(jnp.int32, x.shape, 0)
    s = 1
    while s < CHUNK:
        if reverse:
            x = x + jnp.where(rows < CHUNK - s, pltpu.roll(x, CHUNK - s, axis=0), 0.0)
        else:
            x = x + jnp.where(rows >= s, pltpu.roll(x, s, axis=0), 0.0)
        s *= 2
    return x


def _stack_zero(top):
    return jnp.concatenate([top, jnp.zeros_like(top)], axis=0)


def _rwkv_chunk_kernel(r_ref, k_ref, v_ref, lo_ref, kvec_ref, w0_ref, a0_ref, wup_ref, aup_ref, gup_ref,
                       rk_ref, bd_ref, rbar_ref, y0_ref, a_ref, g_ref, bonus_ref, gate_ref):
    r = r_ref[0]
    k = k_ref[0]
    v = v_ref[0]
    lora = lo_ref[0]
    bd = bd_ref[...]
    kk = k * kvec_ref[0:1]
    kk = kk * lax.rsqrt(_headsum(kk * kk, bd) + 1e-12)
    gate_ref[0] = _mm(_sigmoid(lora), gup_ref[...])
    tanh_lo = jnp.tanh(lora)

    lane = lax.broadcasted_iota(jnp.int32, (CHUNK, LANES), 1)
    first = lane < RWKV_HEAD
    lane2 = lax.broadcasted_iota(jnp.int32, (CHUNK, 2 * LANES), 1)
    first2 = (lane2 % LANES) < RWKV_HEAD
    trow = lax.broadcasted_iota(jnp.int32, (CHUNK, 2 * CHUNK), 0)
    tcol = lax.broadcasted_iota(jnp.int32, (CHUNK, 2 * CHUNK), 1) % CHUNK
    sq_r = lax.broadcasted_iota(jnp.int32, (LANES, LANES), 0)
    sq_c = lax.broadcasted_iota(jnp.int32, (LANES, LANES), 1)
    same_head = (sq_r // RWKV_HEAD) == (sq_c // RWKV_HEAD)
    eye = sq_r == sq_c

    k_sum = None
    for d in range(2):
        reverse = d == 1
        w_log = -_softplus(-(w0_ref[d] + _mm(tanh_lo, wup_ref[d]))) - 0.5
        ld = -jnp.exp(w_log)
        a = _sigmoid(a0_ref[d] + _mm(lora, aup_ref[d]))
        k_d = k * (1.0 + (a - 1.0) * kvec_ref[1:2])
        k_sum = k_d if k_sum is None else k_sum + k_d
        b_d = kk * a
        lg = _chunk_cumsum(ld, reverse)
        last = 0 if reverse else CHUNK - 1
        tot = lg[last:last + 1]
        e_neg = jnp.exp(-lg)
        e_end = jnp.exp(tot - lg)
        z_t = -kk * jnp.exp(lg - ld)
        r_t = r * jnp.exp(lg)
        b_t = b_d * e_neg
        k_t = k_d * e_neg
        b_e = b_d * e_end
        k_e = k_d * e_end
        e_tot = jnp.exp(tot)
        before = (tcol > trow) if reverse else (tcol < trow)
        before_eq = (tcol >= trow) if reverse else (tcol <= trow)
        for pr in range(N_PAIR):
            sl = slice(pr * LANES, (pr + 1) * LANES)
            zp, rp, vp = z_t[:, sl], r_t[:, sl], v[:, sl]
            bk = jnp.concatenate([b_t[:, sl], k_t[:, sl]], axis=0)
            zero_v = jnp.concatenate([jnp.zeros_like(vp), vp], axis=0)
            xs, lrs = [], []
            for hd in range(2):
                sel = first if hd == 0 else ~first
                lz = jnp.where(before, _mm_nt(jnp.where(sel, zp, 0.0), bk), 0.0)
                lr = jnp.where(before_eq, _mm_nt(jnp.where(sel, rp, 0.0), bk), 0.0)
                x = jnp.concatenate([zp, _mm(lz, zero_v)], axis=1)
                pw = lz
                for it in range(N_DOUBLINGS):
                    x = x + _mm(pw, _stack_zero(x))
                    if it + 1 < N_DOUBLINGS:
                        pw = _mm(pw, _stack_zero(pw))
                xs.append(x)
                lrs.append(lr)
            xp = jnp.where(first2, xs[0], xs[1])
            rhs = jnp.concatenate([xp, jnp.concatenate([jnp.zeros_like(vp), vp], axis=1)], axis=0)
            op = jnp.where(first2, _mm(lrs[0], rhs), _mm(lrs[1], rhs))
            rbar_ref[d, 0, :, sl] = rp + op[:, :LANES]
            y0_ref[d, 0, :, sl] = op[:, LANES:]
            ag = _mm_tn(jnp.concatenate([b_e[:, sl], k_e[:, sl]], axis=0), rhs)
            a_full = ag[:, :LANES] + jnp.where(eye, jnp.broadcast_to(e_tot[:, sl], (LANES, LANES)), 0.0)
            a_ref[d, 0, :, sl] = jnp.where(same_head, a_full, 0.0)
            g_ref[d, 0, :, sl] = jnp.where(same_head, ag[:, LANES:], 0.0)
    bonus_ref[0] = _headsum(r * (0.5 * k_sum) * rk_ref[...], bd) * v


def _head_block_diag():
    idx = np.arange(RWKV_DIM) // RWKV_HEAD
    return jnp.asarray(idx[:, None] == idx[None, :], BF16)


def _rwkv_chunks(p, kvec, w0, a0, w_up, a_up, g_up, r_k):
    b, s, _ = p.shape
    nc = s // CHUNK
    lora_w = DECAY_LORA + AAA_LORA + GATE_LORA
    wup = jnp.zeros((2, lora_w, RWKV_DIM), F32).at[:, :DECAY_LORA].set(w_up).astype(BF16)
    aup = jnp.zeros((2, lora_w, RWKV_DIM), F32).at[:, DECAY_LORA:DECAY_LORA + AAA_LORA].set(a_up).astype(BF16)
    gup = jnp.zeros((lora_w, RWKV_DIM), F32).at[DECAY_LORA + AAA_LORA:].set(g_up).astype(BF16)
    full = lambda shape: pl.BlockSpec(shape, lambda bi, c: (0,) * len(shape))
    col = lambda off: pl.BlockSpec((1, CHUNK, RWKV_DIM), lambda bi, c: (bi, c, off // RWKV_DIM))
    per_tok = pl.BlockSpec((2, 1, CHUNK, RWKV_DIM), lambda bi, c: (0, bi, c, 0))
    per_chunk = pl.BlockSpec((2, 1, 2 * CHUNK, RWKV_DIM), lambda bi, c: (0, bi, c, 0))
    tok = pl.BlockSpec((1, CHUNK, RWKV_DIM), lambda bi, c: (bi, c, 0))
    f32 = lambda *shape: jax.ShapeDtypeStruct(shape, F32)
    return pl.pallas_call(
        _rwkv_chunk_kernel,
        grid=(b, nc),
        in_specs=[col(P_RKV), col(P_RKV + RWKV_DIM), col(P_RKV + 2 * RWKV_DIM),
                  pl.BlockSpec((1, CHUNK, lora_w), lambda bi, c: (bi, c, P_LORA // lora_w)),
                  full((2, RWKV_DIM)), full((2, 1, RWKV_DIM)), full((2, 1, RWKV_DIM)),
                  full(wup.shape), full(aup.shape), full(gup.shape), full((1, RWKV_DIM)),
                  full((RWKV_DIM, RWKV_DIM))],
        out_specs=[per_tok, per_tok, per_chunk, per_chunk, tok, tok],
        out_shape=[f32(2, b, s, RWKV_DIM), f32(2, b, s, RWKV_DIM), f32(2, b, 2 * s, RWKV_DIM),
                   f32(2, b, 2 * s, RWKV_DIM), f32(b, s, RWKV_DIM), f32(b, s, RWKV_DIM)],
        compiler_params=_cparams(("parallel", "parallel")),
        name="rwkv_chunks",
    )(p, p, p, p, kvec, w0.reshape(2, 1, -1), a0.reshape(2, 1, -1), wup, aup, gup, r_k.reshape(1, -1),
      _head_block_diag())


RWKV_TS = 256


def _rwkv_scan_kernel(rbar_ref, y0_ref, a_ref, g_ref, y_ref, s_ref):
    d = pl.program_id(1)

    @pl.when(pl.program_id(2) == 0)
    def _():
        s_ref[...] = jnp.zeros_like(s_ref)

    n_sub = RWKV_TS // CHUNK
    for cc in range(n_sub):
        ci = jnp.where(d == 0, cc, n_sub - 1 - cc)
        rows = pl.ds(pl.multiple_of(ci * CHUNK, CHUNK), CHUNK)
        rows2 = pl.ds(pl.multiple_of(ci * 2 * CHUNK, 2 * CHUNK), 2 * CHUNK)
        for pr in range(N_PAIR):
            sl = slice(pr * LANES, (pr + 1) * LANES)
            st = s_ref[pr]
            y_ref[0, 0, rows, sl] = _mm_f32(rbar_ref[0, 0, rows, sl], st) + y0_ref[0, 0, rows, sl]
            s_ref[pr] = _mm_f32(a_ref[0, 0, rows2, sl], st) + g_ref[0, 0, rows2, sl]


def _rwkv_scan(rbar, y0, a, g, *, n_ctx):
    _, b, s, _ = rbar.shape
    nt = s // RWKV_TS
    nct = n_ctx // RWKV_TS

    def tile(d, j):
        back = jnp.where(j < nct, nct - 1 - j, nt - 1 - (j - nct))
        return jnp.where(d == 0, j, back)

    tok = pl.BlockSpec((1, 1, RWKV_TS, RWKV_DIM), lambda bi, d, j: (d, bi, tile(d, j), 0))
    chk = pl.BlockSpec((1, 1, 2 * RWKV_TS, RWKV_DIM), lambda bi, d, j: (d, bi, tile(d, j), 0))
    return pl.pallas_call(
        _rwkv_scan_kernel,
        grid=(b, 2, nt),
        in_specs=[tok, tok, chk, chk],
        out_specs=tok,
        out_shape=jax.ShapeDtypeStruct((2, b, s, RWKV_DIM), F32),
        scratch_shapes=[pltpu.VMEM((N_PAIR, LANES, LANES), F32)],
        compiler_params=_cparams(("parallel", "parallel", "arbitrary")),
        name="rwkv_scan",
    )(rbar, y0, a, g)


def _rwkv_readout_kernel(yf_ref, yb_ref, bonus_ref, gate_ref, lng_ref, lnb_ref, bd_ref, o_ref):
    bd = bd_ref[...]
    y = yf_ref[0, 0] + yb_ref[0, 0]
    inv_n = 1.0 / RWKV_HEAD
    dev = y - _headsum(y, bd) * inv_n
    var = _headsum(dev * dev, bd) * inv_n
    yn = dev * lax.rsqrt(var + RWKV_LN_EPS) * lng_ref[...] + lnb_ref[...]
    o_ref[0] = ((yn + bonus_ref[0]) * gate_ref[0]).astype(BF16)


def _rwkv_readout(y, bonus, gate, ln_g, ln_b, *, row_off):
    _, b, s, _ = y.shape
    nt = s // TM - row_off
    full = lambda shape: pl.BlockSpec(shape, lambda bi, i: (0,) * len(shape))
    tok = pl.BlockSpec((1, TM, RWKV_DIM), lambda bi, i: (bi, i + row_off, 0))
    return pl.pallas_call(
        _rwkv_readout_kernel,
        grid=(b, nt),
        in_specs=[pl.BlockSpec((1, 1, TM, RWKV_DIM), lambda bi, i: (0, bi, i + row_off, 0)),
                  pl.BlockSpec((1, 1, TM, RWKV_DIM), lambda bi, i: (1, bi, i + row_off, 0)),
                  tok, tok, full((1, RWKV_DIM)), full((1, RWKV_DIM)), full((RWKV_DIM, RWKV_DIM))],
        out_specs=pl.BlockSpec((1, TM, RWKV_DIM), lambda bi, i: (bi, i, 0)),
        out_shape=jax.ShapeDtypeStruct((b, nt * TM, RWKV_DIM), BF16),
        compiler_params=_cparams(("parallel", "parallel")),
        name="rwkv_readout",
    )(y, y, bonus, gate, ln_g.reshape(1, -1), ln_b.reshape(1, -1), _head_block_diag())


DFT_N2 = LANES
HY_MIN_LEN = 1024
HY_TT = 4


def _hyena_filter_kernel(feats_ref, w1_ref, b1_ref, w2_ref, b2_ref, freq_ref, w3f_ref, w3b_ref,
                         t_ref, delta_ref, hf_ref, hb_ref):
    h = jnp.sin(freq_ref[0:1] * (_mm_f32(feats_ref[...], w1_ref[...]) + b1_ref[...]))
    h = jnp.sin(freq_ref[1:2] * (_mm_f32(h, w2_ref[...]) + b2_ref[...]))
    window = jnp.exp(-t_ref[...] * delta_ref[...])
    hf = _mm_f32(h, w3f_ref[...]) * window
    hb = _mm_f32(h, w3b_ref[...]) * window
    norm = (jnp.sum(jnp.abs(hf), axis=0, keepdims=True) + jnp.sum(jnp.abs(hb), axis=0, keepdims=True))
    hf_ref[0] = hf / norm
    hb_ref[0] = hb / norm


def _hyena_filters(n, w1, b1, w2, b2, w3, freq):
    t = jnp.linspace(0.0, 1.0, n, dtype=F32)[:, None]
    bands = jnp.linspace(1e-4, HYENA_BANDS - 1, HYENA_BANDS, dtype=F32)
    ang = (2.0 * math.pi / n) * jnp.arange(n, dtype=F32)[:, None] * bands[None, :]
    feats = jnp.concatenate([t, jnp.cos(ang), -jnp.sin(ang),
                             jnp.zeros((n, HYENA_FW - HYENA_EMB), F32)], axis=-1)
    w1p = jnp.zeros((HYENA_FW, HYENA_FW), F32).at[:HYENA_EMB].set(w1)
    deltas = jnp.abs(jnp.linspace(math.log(HYENA_TARGET) / HYENA_SLOW,
                                  math.log(HYENA_TARGET) / HYENA_FAST, HYENA_DIM, dtype=F32))[None, :]
    tc = 256
    nj = HYENA_DIM // tc
    full = lambda shape: pl.BlockSpec(shape, lambda o, j: (0,) * len(shape))
    out = pl.BlockSpec((1, n, tc), lambda o, j: (o, 0, j))
    return pl.pallas_call(
        _hyena_filter_kernel,
        grid=(HYENA_ORDER, nj),
        in_specs=[full((n, HYENA_FW)), full((HYENA_FW, HYENA_FW)), full((1, HYENA_FW)),
                  full((HYENA_FW, HYENA_FW)), full((1, HYENA_FW)), full((2, HYENA_FW)),
                  pl.BlockSpec((HYENA_FW, tc), lambda o, j: (0, o * 2 * nj + j)),
                  pl.BlockSpec((HYENA_FW, tc), lambda o, j: (0, o * 2 * nj + nj + j)),
                  full((n, 1)), pl.BlockSpec((1, tc), lambda o, j: (0, j))],
        out_specs=[out, out],
        out_shape=[jax.ShapeDtypeStruct((HYENA_ORDER, n, HYENA_DIM), F32)] * 2,
        compiler_params=_cparams(("parallel", "parallel")),
        name="hyena_filters",
    )(feats, w1p, b1.reshape(1, -1), w2, b2.reshape(1, -1), freq, w3, w3, t, deltas)


def _dft_tables(n1):
    nc = n1 * DFT_N2
    t2 = np.arange(DFT_N2)[:, None, None]
    f1 = np.arange(n1)[None, :, None]
    t1 = np.arange(n1)[None, None, :]
    theta = 2.0 * np.pi * ((f1 * (DFT_N2 * t1 + t2)) % nc) / nc
    g_fwd = np.concatenate([np.cos(theta), -np.sin(theta)], axis=1)
    g_inv = np.concatenate([np.cos(theta), -np.sin(theta)], axis=1).transpose(0, 2, 1) / nc
    k = np.arange(DFT_N2)
    phi = 2.0 * np.pi * ((k[:, None] * k[None, :]) % DFT_N2) / DFT_N2
    c, s = np.cos(phi), np.sin(phi)
    f_fwd = np.block([[c, s], [-s, c]])
    f_inv = np.block([[c, -s], [s, c]])
    return (jnp.asarray(g_fwd, F32), jnp.asarray(g_inv, F32), jnp.asarray(f_fwd, BF16),
            jnp.asarray(f_inv, BF16))


def _dft1_kernel(x_ref, g_ref, o_ref, *, rows_valid):
    rows = lax.broadcasted_iota(jnp.int32, (x_ref.shape[1], HYENA_DIM), 0)
    for tt in range(HY_TT):
        cols = slice(tt * HYENA_DIM, (tt + 1) * HYENA_DIM)
        x = x_ref[0, :, cols]
        if rows_valid < x_ref.shape[1]:
            x = jnp.where(rows < rows_valid, x, 0.0)
        o_ref[0, :, cols] = _mm(g_ref[tt], x).astype(o_ref.dtype)


def _dft_stage1(x, g_fwd, n1, rows_valid):
    bx, rows, _ = x.shape
    t1 = rows // DFT_N2
    xv = x.reshape(bx, t1, DFT_N2 * HYENA_DIM)
    w = HY_TT * HYENA_DIM
    return pl.pallas_call(
        functools.partial(_dft1_kernel, rows_valid=rows_valid),
        grid=(bx, DFT_N2 // HY_TT),
        in_specs=[pl.BlockSpec((1, t1, w), lambda bi, j: (bi, 0, j)),
                  pl.BlockSpec((HY_TT, 2 * n1, t1), lambda bi, j: (j, 0, 0))],
        out_specs=pl.BlockSpec((1, 2 * n1, w), lambda bi, j: (bi, 0, j)),
        out_shape=jax.ShapeDtypeStruct((bx, 2 * n1, DFT_N2 * HYENA_DIM), BF16),
        compiler_params=_cparams(("parallel", "parallel")),
        name="hyena_dft_stage1",
    )(xv, g_fwd[:, :, :t1])


def _spectrum_kernel(a_ref, f_ref, o_ref):
    a = jnp.concatenate([a_ref[0, 0, 0], a_ref[0, 1, 0]], axis=0)
    o_ref[0, 0] = jnp.dot(f_ref[...], a, preferred_element_type=F32)


def _filter_spectrum(a, f_fwd, n1):
    no = a.shape[0]
    a5 = a.reshape(no, 2, n1, DFT_N2, HYENA_DIM)
    return pl.pallas_call(
        _spectrum_kernel,
        grid=(no, n1),
        in_specs=[pl.BlockSpec((1, 2, 1, DFT_N2, HYENA_DIM), lambda o, f: (o, 0, f, 0, 0)),
                  pl.BlockSpec((2 * DFT_N2, 2 * DFT_N2), lambda o, f: (0, 0))],
        out_specs=pl.BlockSpec((1, 1, 2 * DFT_N2, HYENA_DIM), lambda o, f: (o, f, 0, 0)),
        out_shape=jax.ShapeDtypeStruct((no, n1, 2 * DFT_N2, HYENA_DIM), F32),
        compiler_params=_cparams(("parallel", "parallel")),
        name="hyena_filter_spectrum",
    )(a5, f_fwd)


def _dft2_kernel(a_ref, k_ref, ff_ref, fi_ref, o_ref):
    a = jnp.concatenate([a_ref[0, 0, 0], a_ref[0, 1, 0]], axis=0)
    x = jnp.dot(ff_ref[...], a, preferred_element_type=F32)
    xre, xim = x[:DFT_N2], x[DFT_N2:]
    kre, kim = k_ref[0, 0, :DFT_N2], k_ref[0, 0, DFT_N2:]
    y = jnp.concatenate([xre * kre - xim * kim, xre * kim + xim * kre], axis=0).astype(BF16)
    bm = jnp.dot(fi_ref[...], y, preferred_element_type=F32).astype(BF16)
    o_ref[0, 0, 0] = bm[:DFT_N2]
    o_ref[0, 1, 0] = bm[DFT_N2:]


def _dft_stage2(a, kspec, order, f_fwd, f_inv, n1):
    bx = a.shape[0]
    a5 = a.reshape(bx, 2, n1, DFT_N2, HYENA_DIM)
    blk = pl.BlockSpec((1, 2, 1, DFT_N2, HYENA_DIM), lambda bi, f: (bi, 0, f, 0, 0))
    mat = pl.BlockSpec((2 * DFT_N2, 2 * DFT_N2), lambda bi, f: (0, 0))
    out = pl.pallas_call(
        _dft2_kernel,
        grid=(bx, n1),
        in_specs=[blk, pl.BlockSpec((1, 1, 2 * DFT_N2, HYENA_DIM), lambda bi, f: (order, f, 0, 0)),
                  mat, mat],
        out_specs=blk,
        out_shape=jax.ShapeDtypeStruct(a5.shape, BF16),
        compiler_params=_cparams(("parallel", "parallel")),
        name="hyena_dft_stage2",
    )(a5, kspec, f_fwd, f_inv)
    return out.reshape(bx, 2 * n1, DFT_N2 * HYENA_DIM)


def _dft3_kernel(b_ref, g_ref, u_ref, gate_ref, bias_ref, o_ref):
    for tt in range(HY_TT):
        cols = slice(tt * HYENA_DIM, (tt + 1) * HYENA_DIM)
        y = _mm(g_ref[tt], b_ref[0, :, cols])
        o_ref[0, :, cols] = (gate_ref[0, :, cols] * (y + bias_ref[...] * u_ref[0, :, cols])).astype(o_ref.dtype)


def _dft_stage3(bm, g_inv, u, gate, bias, n1, out_dtype):
    bx, rows, _ = u.shape
    t1 = rows // DFT_N2
    w = HY_TT * HYENA_DIM
    tok = pl.BlockSpec((1, t1, w), lambda bi, j: (bi, 0, j))
    out = pl.pallas_call(
        _dft3_kernel,
        grid=(bx, DFT_N2 // HY_TT),
        in_specs=[pl.BlockSpec((1, 2 * n1, w), lambda bi, j: (bi, 0, j)),
                  pl.BlockSpec((HY_TT, t1, 2 * n1), lambda bi, j: (j, 0, 0)),
                  tok, tok, pl.BlockSpec((1, HYENA_DIM), lambda bi, j: (0, 0))],
        out_specs=tok,
        out_shape=jax.ShapeDtypeStruct((bx, t1, DFT_N2 * HYENA_DIM), out_dtype),
        compiler_params=_cparams(("parallel", "parallel")),
        name="hyena_dft_stage3",
    )(bm, g_inv[:, :t1, :], u.reshape(bx, t1, -1), gate.reshape(bx, t1, -1), bias.reshape(1, -1))
    return out.reshape(bx, rows, HYENA_DIM)


def _hyena_operator(v, x1, x2, n, filt_params, bias):
    n_pad = v.shape[1]
    nc = 2 * n_pad
    n1 = nc // DFT_N2
    g_fwd, g_inv, f_fwd, f_inv = _dft_tables(n1)
    hf, hb = _hyena_filters(n, *filt_params)
    kbuf = jnp.concatenate([hf, jnp.zeros((HYENA_ORDER, nc - 2 * n + 1, HYENA_DIM), F32),
                            hb[:, 1:][:, ::-1]], axis=1)
    kspec = _filter_spectrum(_dft_stage1(kbuf, g_fwd, n1, nc // DFT_N2), f_fwd, n1)
    z = v
    rows_valid = -(-n // DFT_N2)
    for o, gate in enumerate((x1, x2)):
        a = _dft_stage1(z, g_fwd, n1, rows_valid)
        bm = _dft_stage2(a, kspec, o, f_fwd, f_inv, n1)
        z = _dft_stage3(bm, g_inv, z, gate, bias[o], n1, F32 if o + 1 < HYENA_ORDER else BF16)
    return z


def _merge_kernel(x_ref, mod_ref, g_ref, gates_ref, bg_ref, ya_ref, yb_ref, yh_ref, yd_ref,
                  wb_ref, wo_ref, o_ref):
    merged = None
    for br, y_ref in enumerate((ya_ref, yb_ref, yh_ref, yd_ref)):
        gate = _sigmoid(gates_ref[0, :, br * D_MODEL:(br + 1) * D_MODEL] + bg_ref[br:br + 1])
        term = gate * jnp.dot(y_ref[0], wb_ref[br], preferred_element_type=F32)
        merged = term if merged is None else merged + term
    y = jnp.dot(merged.astype(BF16), wo_ref[...], preferred_element_type=F32)
    o_ref[0] = x_ref[0] + mod_ref[0, 5:6] * _rms(y, g_ref[3:4])


def _merge(x, mods, norm_g, p, b_gate, ya, yb, yh, yd, w_branch, w_out, *, n_ctx_tiles, row_off):
    b, s, _ = x.shape
    nt = s // TM - row_off
    n_lat = mods.shape[0] - 1
    br = lambda: pl.BlockSpec((1, TM, BRANCH_DIM), lambda bi, i: (bi, i, 0))
    return pl.pallas_call(
        _merge_kernel,
        grid=(b, nt),
        in_specs=[pl.BlockSpec((1, TM, D_MODEL), lambda bi, i: (bi, i + row_off, 0)),
                  pl.BlockSpec((1, N_MOD, D_MODEL),
                               lambda bi, i: (jnp.where(i + row_off < n_ctx_tiles, n_lat, bi), 0, 0)),
                  pl.BlockSpec((6, D_MODEL), lambda bi, i: (0, 0)),
                  pl.BlockSpec((1, TM, GATE_COLS), lambda bi, i: (bi, i + row_off, 0)),
                  pl.BlockSpec((N_BRANCH, D_MODEL), lambda bi, i: (0, 0)),
                  br(), br(), br(), br(),
                  pl.BlockSpec((N_BRANCH, BRANCH_DIM, D_MODEL), lambda bi, i: (0, 0, 0)),
                  pl.BlockSpec((D_MODEL, D_MODEL), lambda bi, i: (0, 0))],
        out_specs=pl.BlockSpec((1, TM, D_MODEL), lambda bi, i: (bi, i, 0)),
        out_shape=jax.ShapeDtypeStruct((b, nt * TM, D_MODEL), F32),
        compiler_params=_cparams(("parallel", "parallel")),
        name="merge_branches",
    )(x, mods, norm_g, p, b_gate, ya, yb, yh, yd, w_branch, w_out)


def kernel(x, c, ctx, c_ctx, w_mod, b_mod, norm_g, ffn_w13, ffn_w2, w_in, b_gate, mla_norm_q, mla_norm_kv, mla_w_uq, mla_w_ukv, rwkv_mu, rwkv_w0, rwkv_w_up, rwkv_a0, rwkv_a_up, rwkv_g_up, rwkv_kvec, rwkv_r_k, rwkv_ln_g, rwkv_ln_b, hyena_conv, hyena_conv_b, hyena_w1, hyena_b1, hyena_w2, hyena_b2, hyena_w3, hyena_freq, hyena_bias, swa_sink, w_branch, w_out):
    b, n, _ = x.shape
    n_ctx = ctx.shape[1]
    nct = n_ctx // TM
    xall = jnp.concatenate([ctx, x], axis=1)
    c_all = jnp.concatenate([c, c_ctx[None]], axis=0)
    tabs_mla = _rope_tables(n, n_ctx, MLA_ROPE, MLA_NOPE, LANES)
    tabs_swa = _rope_tables(n, n_ctx, SWA_HEAD, 0, SWA_HEAD)
    depth = w_mod.shape[0]
    for l in range(depth):
        with_ctx = l + 1 < depth
        row_off = 0 if with_ctx else nct
        mods = _modulation(c_all, w_mod[l], b_mod[l])
        xall = _ffn(xall, mods, norm_g[l], ffn_w13[l, 0].astype(BF16), ffn_w2[l, 0].astype(BF16),
                    mod0=0, g0=0, n_ctx_tiles=nct, row_off=0)
        coef = _shift_coefficients(rwkv_mu[l], hyena_conv[l], hyena_conv_b[l])
        p = _inproj(xall, mods, norm_g[l], _permute_w_in(w_in[l]).astype(BF16), coef, n_ctx_tiles=nct)
        q, k, v = _mla_prep(p, mla_norm_q[l], mla_norm_kv[l], mla_w_uq[l], mla_w_ukv[l], tabs_mla)
        ya = _mla_attention(q, k, v, n_ctx=n_ctx, q_off=row_off)
        rbar, y0, a, g, bonus, gate = _rwkv_chunks(p, rwkv_kvec[l], rwkv_w0[l], rwkv_a0[l], rwkv_w_up[l],
                                                   rwkv_a_up[l], rwkv_g_up[l], rwkv_r_k[l])
        y = _rwkv_scan(rbar, y0, a, g, n_ctx=n_ctx)
        yb = _rwkv_readout(y, bonus, gate, rwkv_ln_g[l], rwkv_ln_b[l], row_off=row_off)
        filt = (hyena_w1[l], hyena_b1[l], hyena_w2[l], hyena_b2[l], hyena_w3[l], hyena_freq[l])
        def hyena_segment(rows, n_rows):
            ins = []
            for j in range(3):
                t = p[:, rows, P_HY + j * HYENA_DIM:P_HY + (j + 1) * HYENA_DIM]
                ins.append(jnp.pad(t, ((0, 0), (0, max(n_rows, HY_MIN_LEN) - n_rows), (0, 0))))
            return _hyena_operator(*ins, n_rows, filt, hyena_bias[l])[:, :n_rows]

        yh = hyena_segment(slice(n_ctx, None), n)
        if with_ctx:
            yh = jnp.concatenate([hyena_segment(slice(0, n_ctx), n_ctx), yh], axis=1)
        q, k, v = _swa_prep(p, tabs_swa)
        yd = _swa_attention(q, k, v, swa_sink[l], n_ctx=n_ctx, q_off=row_off * TM // SWA_TQ)
        xall = _merge(xall, mods, norm_g[l], p, b_gate[l], ya, yb, yh, yd, w_branch[l].astype(BF16),
                      w_out[l].astype(BF16), n_ctx_tiles=nct, row_off=row_off)
        xall = _ffn(xall, mods, norm_g[l], ffn_w13[l, 1].astype(BF16), ffn_w2[l, 1].astype(BF16),
                    mod0=6, g0=4, n_ctx_tiles=nct - row_off, row_off=0)
    return xall
```

```python
import functools
import math

import numpy as np
import jax
import jax.numpy as jnp
from jax import lax
from jax.experimental import pallas as pl
from jax.experimental.pallas import tpu as pltpu

F32 = jnp.float32
BF16 = jnp.bfloat16

D_MODEL = 1024
GRID_W = 64
N_BRANCH = 4
N_MOD = 9
FF_DIM = 2816
EPS = 1e-6
ROPE_BASE = 10000.0
NEG_INF = -1e30
BRANCH_DIM = 512
MLA_HEADS = 8
MLA_NOPE = 64
MLA_ROPE = 32
MLA_V = 64
MLA_Q_RANK = 256
MLA_KV_RANK = 128
RWKV_HEADS = 8
RWKV_HEAD = 64
RWKV_DIM = RWKV_HEADS * RWKV_HEAD
DECAY_LORA = 64
AAA_LORA = 64
GATE_LORA = 128
RWKV_LN_EPS = 64e-5
HYENA_DIM = 512
HYENA_ORDER = 2
HYENA_EMB = 33
HYENA_BANDS = (HYENA_EMB - 1) // 2
HYENA_FW = 64
HYENA_TARGET = 1e-2
HYENA_FAST = 0.3
HYENA_SLOW = 1.5
SWA_HEADS = 8
SWA_KV_HEADS = 2
SWA_HEAD = 64
SWA_GROUP = SWA_HEADS // SWA_KV_HEADS
WINDOW = 128
GATE_COLS = N_BRANCH * D_MODEL
MLA_COLS = MLA_Q_RANK + MLA_KV_RANK + MLA_ROPE
RWKV_COLS = 3 * RWKV_DIM + DECAY_LORA + AAA_LORA + GATE_LORA
HYENA_COLS = 3 * HYENA_DIM
SWA_COLS = (SWA_HEADS + 2 * SWA_KV_HEADS) * SWA_HEAD

LANES = 128
V7X_VMEM_LIMIT = 56 * 1024 * 1024

TM = 256
FF_CHUNK = 256
IN_CHUNK = 512
CHUNK = 64

P_GATE = 0
P_HY = 4096
P_RKV = 5632
P_SWAQ = 7168
P_LORA = 7680
P_CQ = 7936
P_CKV = 8192
P_KR = 8320
P_SWAK = 8448
P_SWAV = 8576
P_COLS = 8704
_SHIFT_COLS = ((P_HY, P_SWAQ), (P_LORA, P_CQ))
_SHIFT_CHUNKS = [any(lo < (j + 1) * IN_CHUNK and j * IN_CHUNK < hi for lo, hi in _SHIFT_COLS)
                 for j in range(P_COLS // IN_CHUNK)]


def _cparams(sem, vmem=V7X_VMEM_LIMIT):
    return pltpu.CompilerParams(dimension_semantics=sem, vmem_limit_bytes=vmem)


def _mm(a, b):
    return jnp.dot(a.astype(BF16), b.astype(BF16), preferred_element_type=F32)


def _mm_nt(a, b):
    return lax.dot_general(a.astype(BF16), b.astype(BF16), (((1,), (1,)), ((), ())),
                           preferred_element_type=F32)


def _mm_tn(a, b):
    return lax.dot_general(a.astype(BF16), b.astype(BF16), (((0,), (0,)), ((), ())),
                           preferred_element_type=F32)


def _mm_f32(a, b):
    return jnp.dot(a, b, preferred_element_type=F32, precision=lax.Precision.HIGHEST)


def _rms(x, g):
    return x * lax.rsqrt(jnp.mean(x * x, axis=-1, keepdims=True) + EPS) * g


def _sigmoid(x):
    return 1.0 / (1.0 + jnp.exp(-x))


def _mod_kernel(c_ref, w_ref, b_ref, o_ref):
    c = c_ref[...]
    o_ref[...] = _mm(c * _sigmoid(c), w_ref[...]) + b_ref[...]


def _modulation(c_all, w_mod, b_mod):
    r = c_all.shape[0]
    rp = -(-r // 8) * 8
    c_pad = jnp.zeros((rp, D_MODEL), F32).at[:r].set(c_all)
    tn = 1024
    out = pl.pallas_call(
        _mod_kernel,
        grid=(N_MOD * D_MODEL // tn,),
        in_specs=[pl.BlockSpec((rp, D_MODEL), lambda j: (0, 0)),
                  pl.BlockSpec((D_MODEL, tn), lambda j: (0, j)),
                  pl.BlockSpec((1, tn), lambda j: (0, j))],
        out_specs=pl.BlockSpec((rp, tn), lambda j: (0, j)),
        out_shape=jax.ShapeDtypeStruct((rp, N_MOD * D_MODEL), F32),
        compiler_params=_cparams(("arbitrary",)),
        name="modulation",
    )(c_pad, w_mod, b_mod.reshape(1, -1))
    return out[:r].reshape(r, N_MOD, D_MODEL)


def _ffn_kernel(x_ref, mod_ref, g_ref, w13_ref, w2_ref, o_ref, *, mod0, g0):
    x = x_ref[0]
    shift = mod_ref[0, mod0:mod0 + 1]
    scale = mod_ref[0, mod0 + 1:mod0 + 2]
    gate = mod_ref[0, mod0 + 2:mod0 + 3]
    u = (_rms(x, g_ref[g0:g0 + 1]) * (1.0 + scale) + shift).astype(BF16)
    acc = jnp.zeros(x.shape, F32)
    for f in range(FF_DIM // FF_CHUNK):
        lo = f * FF_CHUNK
        a = jnp.dot(u, w13_ref[:, lo:lo + FF_CHUNK], preferred_element_type=F32)
        b = jnp.dot(u, w13_ref[:, FF_DIM + lo:FF_DIM + lo + FF_CHUNK], preferred_element_type=F32)
        h = (a * _sigmoid(a) * b).astype(BF16)
        acc = acc + jnp.dot(h, w2_ref[lo:lo + FF_CHUNK, :], preferred_element_type=F32)
    o_ref[0] = x + 0.5 * gate * _rms(acc, g_ref[g0 + 1:g0 + 2])


def _ffn(x, mods, norm_g, w13, w2, *, mod0, g0, n_ctx_tiles, row_off):
    b, s, _ = x.shape
    nt = s // TM - row_off
    n_lat = mods.shape[0] - 1

    def mod_idx(bi, i):
        return (jnp.where(i + row_off < n_ctx_tiles, n_lat, bi), 0, 0)

    return pl.pallas_call(
        functools.partial(_ffn_kernel, mod0=mod0, g0=g0),
        grid=(b, nt),
        in_specs=[pl.BlockSpec((1, TM, D_MODEL), lambda bi, i: (bi, i + row_off, 0)),
                  pl.BlockSpec((1, N_MOD, D_MODEL), mod_idx),
                  pl.BlockSpec((6, D_MODEL), lambda bi, i: (0, 0)),
                  pl.BlockSpec(memory_space=pltpu.VMEM),
                  pl.BlockSpec(memory_space=pltpu.VMEM)],
        out_specs=pl.BlockSpec((1, TM, D_MODEL), lambda bi, i: (bi, i, 0)),
        out_shape=jax.ShapeDtypeStruct((b, nt * TM, D_MODEL), F32),
        compiler_params=_cparams(("parallel", "parallel")),
        name="ffn_half_step",
    )(x, mods, norm_g, w13, w2)


def _shift_rows(p, first_row, last_row):
    rows = lax.broadcasted_iota(jnp.int32, p.shape, 0)
    prev = jnp.where(rows == 0, first_row, pltpu.roll(p, 1, axis=0))
    nxt = jnp.where(rows == p.shape[0] - 1, last_row, pltpu.roll(p, p.shape[0] - 1, axis=0))
    return prev, nxt


def _inproj_kernel(x_ref, xp_ref, xn_ref, mod_ref, g_ref, w_ref, coef_ref, o_ref, *, n_ctx_tiles):
    i = pl.program_id(1)
    n_tiles = pl.num_programs(1)
    has_prev = jnp.where((i == 0) | (i == n_ctx_tiles), 0.0, 1.0)
    has_next = jnp.where((i == n_ctx_tiles - 1) | (i == n_tiles - 1), 0.0, 1.0)
    shift = mod_ref[0, 3:4]
    scale = 1.0 + mod_ref[0, 4:5]
    g = g_ref[2:3]
    u = (_rms(x_ref[0], g) * scale + shift).astype(BF16)
    u_prev = (_rms(xp_ref[0], g) * scale + shift).astype(BF16)
    u_next = (_rms(xn_ref[0], g) * scale + shift).astype(BF16)
    for j in range(P_COLS // IN_CHUNK):
        cols = slice(j * IN_CHUNK, (j + 1) * IN_CHUNK)
        p = jnp.dot(u, w_ref[:, cols], preferred_element_type=F32)
        if _SHIFT_CHUNKS[j]:
            p_first = jnp.dot(u_prev, w_ref[:, cols], preferred_element_type=F32)[7:8] * has_prev
            p_last = jnp.dot(u_next, w_ref[:, cols], preferred_element_type=F32)[0:1] * has_next
            prev, nxt = _shift_rows(p, p_first, p_last)
            p = (coef_ref[0:1, cols] * p + coef_ref[1:2, cols] * prev + coef_ref[2:3, cols] * nxt
                 + coef_ref[3:4, cols])
        o_ref[0, :, cols] = p


def _inproj(x, mods, norm_g, w_in_p, coef, *, n_ctx_tiles):
    b, s, _ = x.shape
    n_lat = mods.shape[0] - 1
    r8 = TM // 8
    return pl.pallas_call(
        functools.partial(_inproj_kernel, n_ctx_tiles=n_ctx_tiles),
        grid=(b, s // TM),
        in_specs=[pl.BlockSpec((1, TM, D_MODEL), lambda bi, i: (bi, i, 0)),
                  pl.BlockSpec((1, 8, D_MODEL), lambda bi, i: (bi, jnp.maximum(i * r8 - 1, 0), 0)),
                  pl.BlockSpec((1, 8, D_MODEL),
                               lambda bi, i: (bi, jnp.minimum((i + 1) * r8, s // 8 - 1), 0)),
                  pl.BlockSpec((1, N_MOD, D_MODEL),
                               lambda bi, i: (jnp.where(i < n_ctx_tiles, n_lat, bi), 0, 0)),
                  pl.BlockSpec((6, D_MODEL), lambda bi, i: (0, 0)),
                  pl.BlockSpec(memory_space=pltpu.VMEM),
                  pl.BlockSpec((4, P_COLS), lambda bi, i: (0, 0))],
        out_specs=pl.BlockSpec((1, TM, P_COLS), lambda bi, i: (bi, i, 0)),
        out_shape=jax.ShapeDtypeStruct((b, s, P_COLS), F32),
        compiler_params=_cparams(("parallel", "parallel")),
        name="in_projection",
    )(x, x, x, mods, norm_g, w_in_p, coef)


def _shift_coefficients(rwkv_mu, hyena_conv, hyena_conv_b):
    mu = rwkv_mu.astype(F32)
    coef = jnp.zeros((4, P_COLS), F32).at[0].set(1.0)
    for off, sl in ((P_RKV, slice(0, 3 * RWKV_DIM)), (P_LORA, slice(3 * RWKV_DIM, RWKV_COLS))):
        width = sl.stop - sl.start
        coef = coef.at[0, off:off + width].set(1.0 - mu[0, sl] - mu[1, sl])
        coef = coef.at[1, off:off + width].set(mu[0, sl])
        coef = coef.at[2, off:off + width].set(mu[1, sl])
    hy = slice(P_HY, P_HY + HYENA_COLS)
    coef = coef.at[0, hy].set(hyena_conv[1]).at[1, hy].set(hyena_conv[0]).at[2, hy].set(hyena_conv[2])
    return coef.at[3, hy].set(hyena_conv_b)


def _permute_w_in(w_in):
    o_mla = GATE_COLS
    o_rwkv = o_mla + MLA_COLS
    o_hy = o_rwkv + RWKV_COLS
    o_swa = o_hy + HYENA_COLS
    z = lambda n: jnp.zeros((D_MODEL, n), w_in.dtype)
    parts = [
        w_in[:, :GATE_COLS],
        w_in[:, o_hy:o_hy + HYENA_COLS],
        w_in[:, o_rwkv:o_rwkv + 3 * RWKV_DIM],
        w_in[:, o_swa:o_swa + SWA_HEADS * SWA_HEAD],
        w_in[:, o_rwkv + 3 * RWKV_DIM:o_rwkv + RWKV_COLS],
        w_in[:, o_mla:o_mla + MLA_Q_RANK],
        w_in[:, o_mla + MLA_Q_RANK:o_mla + MLA_Q_RANK + MLA_KV_RANK],
        z(MLA_NOPE), w_in[:, o_mla + MLA_Q_RANK + MLA_KV_RANK:o_mla + MLA_COLS],
        z(LANES - MLA_NOPE - MLA_ROPE),
        w_in[:, o_swa + SWA_HEADS * SWA_HEAD:o_swa + SWA_COLS],
    ]
    out = jnp.concatenate(parts, axis=1)
    assert out.shape[1] == P_COLS
    return out


def _rope_tables(n_lat, n_ctx, rot_dim, lane0, period):
    rows = n_lat // GRID_W
    row = jnp.repeat(jnp.arange(rows, dtype=F32), GRID_W)
    col = jnp.tile(jnp.arange(GRID_W, dtype=F32), rows)
    axis_dim = rot_dim // 2
    h = axis_dim // 2
    inv_freq = ROPE_BASE ** (-jnp.arange(0, axis_dim, 2, dtype=F32) / axis_dim)
    ang_r = row[:, None] * inv_freq
    ang_c = col[:, None] * inv_freq
    cos_rot = jnp.concatenate([jnp.cos(ang_r)] * 2 + [jnp.cos(ang_c)] * 2, axis=1)
    zeros = jnp.zeros_like(ang_r)
    sin_a = jnp.concatenate([-jnp.sin(ang_r), zeros, -jnp.sin(ang_c), zeros], axis=1)
    sin_b = jnp.concatenate([zeros, jnp.sin(ang_r), zeros, jnp.sin(ang_c)], axis=1)

    def widen(t, fill):
        g = jnp.full((n_lat, period), fill, F32).at[:, lane0:lane0 + rot_dim].set(t)
        g = jnp.tile(g, (1, LANES // period))
        ctx = jnp.full((n_ctx, LANES), fill, F32)
        return jnp.concatenate([ctx, g], axis=0)

    return widen(cos_rot, 1.0), widen(sin_a, 0.0), widen(sin_b, 0.0), h


def _rope128(x, cos, sin_a, sin_b, h):
    return x * cos + pltpu.roll(x, LANES - h, axis=1) * sin_a + pltpu.roll(x, h, axis=1) * sin_b


LOG2E = math.log2(math.e)
MLA_SCALE = (MLA_NOPE + MLA_ROPE) ** -0.5 * LOG2E
MLA_ONE_LANE = (MLA_V, 0)


def _mla_prep_kernel(cq_ref, ckv_ref, kr_ref, gq_ref, gkv_ref, wq_ref, wk_ref, wv_ref, vone_ref,
                     cos_ref, sa_ref, sb_ref, q_ref, k_ref, v_ref, *, h):
    cos, sa, sb = cos_ref[...], sa_ref[...], sb_ref[...]
    cq = _rms(cq_ref[0], gq_ref[...]).astype(BF16)
    ckv = _rms(ckv_ref[0], gkv_ref[...]).astype(BF16)
    q = jnp.dot(cq, wq_ref[...], preferred_element_type=F32)
    k = jnp.dot(ckv, wk_ref[...], preferred_element_type=F32)
    kr = _rope128(kr_ref[0], cos, sa, sb, h)
    for hd in range(MLA_HEADS):
        sl = slice(hd * LANES, (hd + 1) * LANES)
        q_ref[0, :, sl] = (_rope128(q[:, sl], cos, sa, sb, h) * MLA_SCALE).astype(BF16)
        k_ref[0, :, sl] = (k[:, sl] + kr).astype(BF16)
    v_ref[0] = (jnp.dot(ckv, wv_ref[...], preferred_element_type=F32) + vone_ref[...]).astype(BF16)


def _mla_prep(p, norm_q, norm_kv, w_uq, w_ukv, tabs):
    b, s, _ = p.shape
    cos, sa, sb, h = tabs
    hq = MLA_NOPE + MLA_ROPE
    wq = jnp.zeros((MLA_Q_RANK, MLA_HEADS, LANES), F32).at[:, :, :hq].set(
        w_uq.reshape(MLA_Q_RANK, MLA_HEADS, hq)).reshape(MLA_Q_RANK, MLA_HEADS * LANES).astype(BF16)
    wkv = w_ukv.reshape(MLA_KV_RANK, MLA_HEADS, MLA_NOPE + MLA_V)
    wk = jnp.zeros((MLA_KV_RANK, MLA_HEADS, LANES), F32).at[:, :, :MLA_NOPE].set(
        wkv[:, :, :MLA_NOPE]).reshape(MLA_KV_RANK, MLA_HEADS * LANES).astype(BF16)
    wv = jnp.zeros((MLA_KV_RANK, MLA_HEADS, LANES), F32)
    wv = wv.at[:, 0::2, :MLA_V].set(wkv[:, 0::2, MLA_NOPE:]).at[:, 1::2, LANES - MLA_V:].set(wkv[:, 1::2, MLA_NOPE:])
    wv = wv.reshape(MLA_KV_RANK, MLA_HEADS * LANES).astype(BF16)
    vone = jnp.zeros((MLA_HEADS, LANES), F32)
    vone = vone.at[0::2, MLA_ONE_LANE[0]].set(1.0).at[1::2, MLA_ONE_LANE[1]].set(1.0).reshape(1, -1)
    full = lambda shape: pl.BlockSpec(shape, lambda bi, i: (0,) * len(shape))
    tab = pl.BlockSpec((TM, LANES), lambda bi, i: (i, 0))
    return pl.pallas_call(
        functools.partial(_mla_prep_kernel, h=h),
        grid=(b, s // TM),
        in_specs=[pl.BlockSpec((1, TM, MLA_Q_RANK), lambda bi, i: (bi, i, P_CQ // MLA_Q_RANK)),
                  pl.BlockSpec((1, TM, LANES), lambda bi, i: (bi, i, P_CKV // LANES)),
                  pl.BlockSpec((1, TM, LANES), lambda bi, i: (bi, i, P_KR // LANES)),
                  full((1, MLA_Q_RANK)), full((1, MLA_KV_RANK)),
                  full(wq.shape), full(wk.shape), full(wv.shape), full(vone.shape), tab, tab, tab],
        out_specs=[pl.BlockSpec((1, TM, MLA_HEADS * LANES), lambda bi, i: (bi, i, 0))] * 3,
        out_shape=[jax.ShapeDtypeStruct((b, s, MLA_HEADS * LANES), BF16)] * 3,
        compiler_params=_cparams(("parallel", "parallel")),
        name="mla_prep",
    )(p, p, p, norm_q.reshape(1, -1), norm_kv.reshape(1, -1), wq, wk, wv, vone, cos, sa, sb)


def _mla_attn_kernel(q_ref, k_ref, v_ref, o_ref, *, n_ctx, q_off):
    i = pl.program_id(2) + q_off
    tq = q_ref.shape[1]

    def attend(n_keys):
        outs = []
        for hd in range(2):
            sl = slice(hd * LANES, (hd + 1) * LANES)
            s = lax.dot_general(q_ref[0, :, sl], k_ref[0, :n_keys, sl], (((1,), (1,)), ((), ())),
                                preferred_element_type=F32)
            e = jnp.exp2(s - jnp.max(s, axis=-1, keepdims=True)).astype(BF16)
            o = jnp.dot(e, v_ref[0, :n_keys, sl], preferred_element_type=F32)
            one = MLA_ONE_LANE[hd]
            outs.append(o / o[:, one:one + 1])
        lane = lax.broadcasted_iota(jnp.int32, outs[0].shape, 1)
        o_ref[0] = jnp.where(lane < MLA_V, outs[0], outs[1]).astype(BF16)

    @pl.when(i * tq < n_ctx)
    def _():
        attend(n_ctx)

    @pl.when(i * tq >= n_ctx)
    def _():
        attend(k_ref.shape[1])


def _mla_attention(q, k, v, *, n_ctx, q_off):
    b, s, _ = q.shape
    nq = s // TM - q_off
    return pl.pallas_call(
        functools.partial(_mla_attn_kernel, n_ctx=n_ctx, q_off=q_off),
        grid=(b, MLA_HEADS // 2, nq),
        in_specs=[pl.BlockSpec((1, TM, 2 * LANES), lambda bi, hp, i: (bi, i + q_off, hp)),
                  pl.BlockSpec((1, s, 2 * LANES), lambda bi, hp, i: (bi, 0, hp)),
                  pl.BlockSpec((1, s, 2 * LANES), lambda bi, hp, i: (bi, 0, hp))],
        out_specs=pl.BlockSpec((1, TM, LANES), lambda bi, hp, i: (bi, i, hp)),
        out_shape=jax.ShapeDtypeStruct((b, nq * TM, MLA_HEADS * MLA_V), BF16),
        compiler_params=_cparams(("parallel", "parallel", "parallel")),
        name="mla_attention",
    )(q, k, v)


SWA_SCALE = SWA_HEAD ** -0.5 * LOG2E
SWA_TQ = 128
SWA_ONE_LANE = (SWA_HEAD, 0)


def _swa_prep_kernel(q_ref, k_ref, v_ref, cos_ref, sa_ref, sb_ref, qo_ref, ko_ref, vo_ref, *, h):
    cos, sa, sb = cos_ref[...], sa_ref[...], sb_ref[...]
    lane = lax.broadcasted_iota(jnp.int32, cos.shape, 1)
    low = lane < SWA_HEAD
    for j in range(SWA_HEADS // 2):
        blk = _rope128(q_ref[0, :, j * LANES:(j + 1) * LANES], cos, sa, sb, h) * SWA_SCALE
        qo_ref[0, :, (2 * j) * LANES:(2 * j + 1) * LANES] = jnp.where(low, blk, 0.0).astype(BF16)
        qo_ref[0, :, (2 * j + 1) * LANES:(2 * j + 2) * LANES] = jnp.where(
            low, pltpu.roll(blk, SWA_HEAD, axis=1), 0.0).astype(BF16)
    kb = _rope128(k_ref[0], cos, sa, sb, h)
    ko_ref[0, :, :LANES] = jnp.where(low, kb, 0.0).astype(BF16)
    ko_ref[0, :, LANES:] = jnp.where(low, pltpu.roll(kb, SWA_HEAD, axis=1), 0.0).astype(BF16)
    vb = v_ref[0]
    vr = pltpu.roll(vb, SWA_HEAD, axis=1)
    one_lo = jnp.where(lane == SWA_ONE_LANE[0], 1.0, 0.0)
    one_hi = jnp.where(lane == SWA_ONE_LANE[1], 1.0, 0.0)
    vo_ref[0, :, 0 * LANES:1 * LANES] = jnp.where(low, vb, one_lo).astype(BF16)
    vo_ref[0, :, 1 * LANES:2 * LANES] = jnp.where(low, one_hi, vr).astype(BF16)
    vo_ref[0, :, 2 * LANES:3 * LANES] = jnp.where(low, vr, one_lo).astype(BF16)
    vo_ref[0, :, 3 * LANES:4 * LANES] = jnp.where(low, one_hi, vb).astype(BF16)


def _swa_prep(p, tabs):
    b, s, _ = p.shape
    cos, sa, sb, h = tabs
    tab = pl.BlockSpec((TM, LANES), lambda bi, i: (i, 0))
    nq = SWA_HEADS * SWA_HEAD
    return pl.pallas_call(
        functools.partial(_swa_prep_kernel, h=h),
        grid=(b, s // TM),
        in_specs=[pl.BlockSpec((1, TM, nq), lambda bi, i: (bi, i, P_SWAQ // nq)),
                  pl.BlockSpec((1, TM, LANES), lambda bi, i: (bi, i, P_SWAK // LANES)),
                  pl.BlockSpec((1, TM, LANES), lambda bi, i: (bi, i, P_SWAV // LANES)),
                  tab, tab, tab],
        out_specs=[pl.BlockSpec((1, TM, SWA_HEADS * LANES), lambda bi, i: (bi, i, 0)),
                   pl.BlockSpec((1, TM, SWA_KV_HEADS * LANES), lambda bi, i: (bi, i, 0)),
                   pl.BlockSpec((1, TM, 4 * LANES), lambda bi, i: (bi, i, 0))],
        out_shape=[jax.ShapeDtypeStruct((b, s, SWA_HEADS * LANES), BF16),
                   jax.ShapeDtypeStruct((b, s, SWA_KV_HEADS * LANES), BF16),
                   jax.ShapeDtypeStruct((b, s, 4 * LANES), BF16)],
        compiler_params=_cparams(("parallel", "parallel")),
        name="swa_prep",
    )(p, p, p, cos, sa, sb)


def _swa_attn_kernel(sink_ref, q_ref, k_ref, v_ref, o_ref, *, n_ctx, q_off):
    i = pl.program_id(1) + q_off
    s_len = k_ref.shape[1]
    tq = SWA_TQ
    n_loc = tq + 2 * WINDOW
    r0 = i * tq
    is_lat = r0 >= n_ctx
    start = pl.multiple_of(jnp.clip(r0 - WINDOW, 0, s_len - n_loc), LANES)
    rows_g = SWA_GROUP * tq
    row = lax.broadcasted_iota(jnp.int32, (rows_g, n_loc), 0)
    qpos = r0 - n_ctx + row % tq
    kpos = start - n_ctx + lax.broadcasted_iota(jnp.int32, (rows_g, n_loc), 1)
    loc_ok = (jnp.abs(kpos - qpos) <= WINDOW) & (kpos >= 0) & is_lat
    k_loc = k_ref[0, pl.ds(start, n_loc), :]
    v_loc = v_ref[0, pl.ds(start, n_loc), :]
    k_ctx = k_ref[0, 0:n_ctx, :]
    v_ctx = v_ref[0, 0:n_ctx, :]
    head_row = lax.broadcasted_iota(jnp.int32, (rows_g, 1), 0) // tq
    lane = lax.broadcasted_iota(jnp.int32, (tq, LANES), 1)
    stages = []
    for g in range(SWA_KV_HEADS):
        q = jnp.concatenate([q_ref[0, :, hd * LANES:(hd + 1) * LANES]
                             for hd in range(g * SWA_GROUP, (g + 1) * SWA_GROUP)], axis=0)
        kg = slice(g * LANES, (g + 1) * LANES)
        s_loc = lax.dot_general(q, k_loc[:, kg], (((1,), (1,)), ((), ())), preferred_element_type=F32)
        s_ctx = lax.dot_general(q, k_ctx[:, kg], (((1,), (1,)), ((), ())), preferred_element_type=F32)
        sink = jnp.zeros((rows_g, 1), F32)
        for hh in range(SWA_GROUP):
            sink = jnp.where(head_row == hh, sink_ref[g * SWA_GROUP + hh] * LOG2E, sink)
        stages.append((jnp.where(loc_ok, s_loc, NEG_INF), s_ctx, sink))
    for g, (s_loc, s_ctx, sink) in enumerate(stages):
        m = jnp.maximum(jnp.maximum(jnp.max(s_loc, axis=-1, keepdims=True),
                                    jnp.max(s_ctx, axis=-1, keepdims=True)), sink)
        e = jnp.concatenate([jnp.exp2(s_loc - m), jnp.exp2(s_ctx - m)], axis=1).astype(BF16)
        e_sink = jnp.exp2(sink - m)
        outs = []
        for par in range(2):
            vg = slice((2 * g + par) * LANES, (2 * g + par + 1) * LANES)
            o = jnp.dot(e, jnp.concatenate([v_loc[:, vg], v_ctx[:, vg]], axis=0), preferred_element_type=F32)
            one = SWA_ONE_LANE[par]
            outs.append(o / (o[:, one:one + 1] + e_sink))
        for pi in range(SWA_GROUP // 2):
            even = outs[0][(2 * pi) * tq:(2 * pi + 1) * tq]
            odd = outs[1][(2 * pi + 1) * tq:(2 * pi + 2) * tq]
            blk = g * (SWA_GROUP // 2) + pi
            o_ref[0, :, blk * LANES:(blk + 1) * LANES] = jnp.where(lane < SWA_HEAD, even, odd).astype(BF16)


def _swa_attention(q, k, v, sink, *, n_ctx, q_off):
    b, s, _ = q.shape
    nq = s // SWA_TQ - q_off
    return pl.pallas_call(
        functools.partial(_swa_attn_kernel, n_ctx=n_ctx, q_off=q_off),
        grid=(b, nq),
        in_specs=[pl.BlockSpec(memory_space=pltpu.SMEM),
                  pl.BlockSpec((1, SWA_TQ, SWA_HEADS * LANES), lambda bi, i: (bi, i + q_off, 0)),
                  pl.BlockSpec((1, s, SWA_KV_HEADS * LANES), lambda bi, i: (bi, 0, 0)),
                  pl.BlockSpec((1, s, 4 * LANES), lambda bi, i: (bi, 0, 0))],
        out_specs=pl.BlockSpec((1, SWA_TQ, SWA_HEADS * SWA_HEAD), lambda bi, i: (bi, i, 0)),
        out_shape=jax.ShapeDtypeStruct((b, nq * SWA_TQ, SWA_HEADS * SWA_HEAD), BF16),
        compiler_params=_cparams(("parallel", "parallel")),
        name="swa_attention",
    )(sink, q, k, v)


N_PAIR = RWKV_HEADS // 2
N_DOUBLINGS = int(math.log2(CHUNK))


def _softplus(x):
    return jnp.maximum(x, 0.0) + jnp.log(1.0 + jnp.exp(-jnp.abs(x)))


def _headsum(x, bd):
    hi = x.astype(BF16)
    lo = (x - hi.astype(F32)).astype(BF16)
    return (jnp.dot(hi, bd, preferred_element_type=F32) + jnp.dot(lo, bd, preferred_element_type=F32))


def _chunk_cumsum(x, reverse):
    rows = lax.broadcasted_iota(jnp.int32, x.shape, 0)
    s = 1
    while s < CHUNK:
        if reverse:
            x = x + jnp.where(rows < CHUNK - s, pltpu.roll(x, CHUNK - s, axis=0), 0.0)
        else:
            x = x + jnp.where(rows >= s, pltpu.roll(x, s, axis=0), 0.0)
        s *= 2
    return x


def _head_rows(x):
    first = lax.broadcasted_iota(jnp.int32, x.shape, 1) < RWKV_HEAD
    return jnp.concatenate([jnp.where(first, x, 0.0), jnp.where(first, 0.0, x)], axis=0)


def _rwkv_chunk_kernel(r_ref, k_ref, v_ref, lo_ref, kvec_ref, w0_ref, a0_ref, wup_ref, aup_ref, gup_ref,
                       rk_ref, bd_ref, rbar_ref, y0_ref, a_ref, g_ref, bonus_ref, gate_ref):
    r = r_ref[0]
    k = k_ref[0]
    v = v_ref[0]
    lora = lo_ref[0]
    bd = bd_ref[...]
    kk = k * kvec_ref[0:1]
    kk = kk * lax.rsqrt(_headsum(kk * kk, bd) + 1e-12)
    gate_ref[0] = _mm(_sigmoid(lora), gup_ref[...])
    tanh_lo = jnp.tanh(lora)

    trow = lax.broadcasted_iota(jnp.int32, (2 * CHUNK, 4 * CHUNK), 0) % CHUNK
    tcol = lax.broadcasted_iota(jnp.int32, (2 * CHUNK, 4 * CHUNK), 1) % CHUNK
    sq_r = lax.broadcasted_iota(jnp.int32, (LANES, LANES), 0)
    sq_c = lax.broadcasted_iota(jnp.int32, (LANES, LANES), 1)
    same_head = (sq_r // RWKV_HEAD) == (sq_c // RWKV_HEAD)
    eye = sq_r == sq_c

    k_sum = None
    chains = []
    for d in range(2):
        reverse = d == 1
        w_log = -_softplus(-(w0_ref[d] + _mm(tanh_lo, wup_ref[d]))) - 0.5
        ld = -jnp.exp(w_log)
        a = _sigmoid(a0_ref[d] + _mm(lora, aup_ref[d]))
        k_d = k * (1.0 + (a - 1.0) * kvec_ref[1:2])
        k_sum = k_d if k_sum is None else k_sum + k_d
        b_d = kk * a
        lg = _chunk_cumsum(ld, reverse)
        last = 0 if reverse else CHUNK - 1
        tot = lg[last:last + 1]
        e_neg = jnp.exp(-lg)
        e_end = jnp.exp(tot - lg)
        z_t = -kk * jnp.exp(lg - ld)
        r_t = r * jnp.exp(lg)
        b_t = b_d * e_neg
        k_t = k_d * e_neg
        b_e = b_d * e_end
        k_e = k_d * e_end
        e_tot = jnp.exp(tot)
        before = (tcol > trow) if reverse else (tcol < trow)
        before_eq = (tcol >= trow) if reverse else (tcol <= trow)
        for pr in range(N_PAIR):
            sl = slice(pr * LANES, (pr + 1) * LANES)
            ch = {"d": d, "sl": sl, "rp": r_t[:, sl], "vp": v[:, sl], "e_tot": e_tot[:, sl],
                  "be_ke": jnp.concatenate([b_e[:, sl], k_e[:, sl]], axis=0)}
            zst, rst, vst = _head_rows(z_t[:, sl]), _head_rows(r_t[:, sl]), _head_rows(v[:, sl])
            bkst = jnp.concatenate([_head_rows(b_t[:, sl]), _head_rows(k_t[:, sl])], axis=0)
            ch["lz"] = jnp.where(before, _mm_nt(zst, bkst), 0.0)
            ch["lr"] = jnp.where(before_eq, _mm_nt(rst, bkst), 0.0)
            ch["zst"], ch["vst"] = zst, vst
            chains.append(ch)
    bonus_ref[0] = _headsum(r * (0.5 * k_sum) * rk_ref[...], bd) * v

    for ch in chains:
        ch["pw"] = ch["lz"][:, :LANES]
        ch["x"] = jnp.concatenate([ch["zst"], _mm(ch["lz"][:, LANES:], ch["vst"])], axis=1)
    for it in range(N_DOUBLINGS):
        for ch in chains:
            ch["x"] = ch["x"] + _mm(ch["pw"], ch["x"])
        if it + 1 < N_DOUBLINGS:
            for ch in chains:
                ch["pw"] = _mm(ch["pw"], ch["pw"])
    for ch in chains:
        low = jnp.concatenate([jnp.zeros_like(ch["vst"]), ch["vst"]], axis=1)
        op = _mm(ch["lr"], jnp.concatenate([ch["x"], low], axis=0))
        ch["op"] = op[:CHUNK] + op[CHUNK:]
        ch["xp"] = ch["x"][:CHUNK] + ch["x"][CHUNK:]
    for ch in chains:
        d, sl = ch["d"], ch["sl"]
        rbar_ref[d, 0, :, sl] = ch["rp"] + ch["op"][:, :LANES]
        y0_ref[d, 0, :, sl] = ch["op"][:, LANES:]
        rhs = jnp.concatenate([ch["xp"], jnp.concatenate([jnp.zeros_like(ch["vp"]), ch["vp"]], axis=1)], axis=0)
        ag = _mm_tn(ch["be_ke"], rhs)
        a_full = ag[:, :LANES] + jnp.where(eye, jnp.broadcast_to(ch["e_tot"], (LANES, LANES)), 0.0)
        a_ref[d, 0, :, sl] = jnp.where(same_head, a_full, 0.0)
        g_ref[d, 0, :, sl] = jnp.where(same_head, ag[:, LANES:], 0.0)


def _head_block_diag():
    idx = np.arange(RWKV_DIM) // RWKV_HEAD
    return jnp.asarray(idx[:, None] == idx[None, :], BF16)


def _rwkv_chunks(p, kvec, w0, a0, w_up, a_up, g_up, r_k):
    b, s, _ = p.shape
    nc = s // CHUNK
    lora_w = DECAY_LORA + AAA_LORA + GATE_LORA
    wup = jnp.zeros((2, lora_w, RWKV_DIM), F32).at[:, :DECAY_LORA].set(w_up).astype(BF16)
    aup = jnp.zeros((2, lora_w, RWKV_DIM), F32).at[:, DECAY_LORA:DECAY_LORA + AAA_LORA].set(a_up).astype(BF16)
    gup = jnp.zeros((lora_w, RWKV_DIM), F32).at[DECAY_LORA + AAA_LORA:].set(g_up).astype(BF16)
    full = lambda shape: pl.BlockSpec(shape, lambda bi, c: (0,) * len(shape))
    col = lambda off: pl.BlockSpec((1, CHUNK, RWKV_DIM), lambda bi, c: (bi, c, off // RWKV_DIM))
    per_tok = pl.BlockSpec((2, 1, CHUNK, RWKV_DIM), lambda bi, c: (0, bi, c, 0))
    per_chunk = pl.BlockSpec((2, 1, 2 * CHUNK, RWKV_DIM), lambda bi, c: (0, bi, c, 0))
    tok = pl.BlockSpec((1, CHUNK, RWKV_DIM), lambda bi, c: (bi, c, 0))
    f32 = lambda *shape: jax.ShapeDtypeStruct(shape, F32)
    return pl.pallas_call(
        _rwkv_chunk_kernel,
        grid=(b, nc),
        in_specs=[col(P_RKV), col(P_RKV + RWKV_DIM), col(P_RKV + 2 * RWKV_DIM),
                  pl.BlockSpec((1, CHUNK, lora_w), lambda bi, c: (bi, c, P_LORA // lora_w)),
                  full((2, RWKV_DIM)), full((2, 1, RWKV_DIM)), full((2, 1, RWKV_DIM)),
                  full(wup.shape), full(aup.shape), full(gup.shape), full((1, RWKV_DIM)),
                  full((RWKV_DIM, RWKV_DIM))],
        out_specs=[per_tok, per_tok, per_chunk, per_chunk, tok, tok],
        out_shape=[f32(2, b, s, RWKV_DIM), f32(2, b, s, RWKV_DIM), f32(2, b, 2 * s, RWKV_DIM),
                   f32(2, b, 2 * s, RWKV_DIM), f32(b, s, RWKV_DIM), f32(b, s, RWKV_DIM)],
        compiler_params=_cparams(("parallel", "parallel")),
        name="rwkv_chunks",
    )(p, p, p, p, kvec, w0.reshape(2, 1, -1), a0.reshape(2, 1, -1), wup, aup, gup, r_k.reshape(1, -1),
      _head_block_diag())


RWKV_TS = 256


def _rwkv_scan_kernel(rbar_ref, y0_ref, a_ref, g_ref, y_ref, s_ref):
    d = pl.program_id(1)

    @pl.when(pl.program_id(2) == 0)
    def _():
        s_ref[...] = jnp.zeros_like(s_ref)

    n_sub = RWKV_TS // CHUNK
    for cc in range(n_sub):
        ci = jnp.where(d == 0, cc, n_sub - 1 - cc)
        rows = pl.ds(pl.multiple_of(ci * CHUNK, CHUNK), CHUNK)
        rows2 = pl.ds(pl.multiple_of(ci * 2 * CHUNK, 2 * CHUNK), 2 * CHUNK)
        for pr in range(N_PAIR):
            sl = slice(pr * LANES, (pr + 1) * LANES)
            st = s_ref[pr]
            y_ref[0, 0, rows, sl] = _mm_f32(rbar_ref[0, 0, rows, sl], st) + y0_ref[0, 0, rows, sl]
            s_ref[pr] = _mm_f32(a_ref[0, 0, rows2, sl], st) + g_ref[0, 0, rows2, sl]


def _rwkv_scan(rbar, y0, a, g, *, n_ctx):
    _, b, s, _ = rbar.shape
    nt = s // RWKV_TS
    nct = n_ctx // RWKV_TS

    def tile(d, j):
        back = jnp.where(j < nct, nct - 1 - j, nt - 1 - (j - nct))
        return jnp.where(d == 0, j, back)

    tok = pl.BlockSpec((1, 1, RWKV_TS, RWKV_DIM), lambda bi, d, j: (d, bi, tile(d, j), 0))
    chk = pl.BlockSpec((1, 1, 2 * RWKV_TS, RWKV_DIM), lambda bi, d, j: (d, bi, tile(d, j), 0))
    return pl.pallas_call(
        _rwkv_scan_kernel,
        grid=(b, 2, nt),
        in_specs=[tok, tok, chk, chk],
        out_specs=tok,
        out_shape=jax.ShapeDtypeStruct((2, b, s, RWKV_DIM), F32),
        scratch_shapes=[pltpu.VMEM((N_PAIR, LANES, LANES), F32)],
        compiler_params=_cparams(("parallel", "parallel", "arbitrary")),
        name="rwkv_scan",
    )(rbar, y0, a, g)


def _rwkv_readout_kernel(yf_ref, yb_ref, bonus_ref, gate_ref, lng_ref, lnb_ref, bd_ref, o_ref):
    bd = bd_ref[...]
    y = yf_ref[0, 0] + yb_ref[0, 0]
    inv_n = 1.0 / RWKV_HEAD
    dev = y - _headsum(y, bd) * inv_n
    var = _headsum(dev * dev, bd) * inv_n
    yn = dev * lax.rsqrt(var + RWKV_LN_EPS) * lng_ref[...] + lnb_ref[...]
    o_ref[0] = ((yn + bonus_ref[0]) * gate_ref[0]).astype(BF16)


def _rwkv_readout(y, bonus, gate, ln_g, ln_b, *, row_off):
    _, b, s, _ = y.shape
    nt = s // TM - row_off
    full = lambda shape: pl.BlockSpec(shape, lambda bi, i: (0,) * len(shape))
    tok = pl.BlockSpec((1, TM, RWKV_DIM), lambda bi, i: (bi, i + row_off, 0))
    return pl.pallas_call(
        _rwkv_readout_kernel,
        grid=(b, nt),
        in_specs=[pl.BlockSpec((1, 1, TM, RWKV_DIM), lambda bi, i: (0, bi, i + row_off, 0)),
                  pl.BlockSpec((1, 1, TM, RWKV_DIM), lambda bi, i: (1, bi, i + row_off, 0)),
                  tok, tok, full((1, RWKV_DIM)), full((1, RWKV_DIM)), full((RWKV_DIM, RWKV_DIM))],
        out_specs=pl.BlockSpec((1, TM, RWKV_DIM), lambda bi, i: (bi, i, 0)),
        out_shape=jax.ShapeDtypeStruct((b, nt * TM, RWKV_DIM), BF16),
        compiler_params=_cparams(("parallel", "parallel")),
        name="rwkv_readout",
    )(y, y, bonus, gate, ln_g.reshape(1, -1), ln_b.reshape(1, -1), _head_block_diag())


DFT_N2 = LANES
HY_MIN_LEN = 1024
HY_TT = 4


def _hyena_filter_kernel(feats_ref, w1_ref, b1_ref, w2_ref, b2_ref, freq_ref, w3f_ref, w3b_ref,
                         t_ref, delta_ref, hf_ref, hb_ref):
    h = jnp.sin(freq_ref[0:1] * (_mm_f32(feats_ref[...], w1_ref[...]) + b1_ref[...]))
    h = jnp.sin(freq_ref[1:2] * (_mm_f32(h, w2_ref[...]) + b2_ref[...]))
    window = jnp.exp(-t_ref[...] * delta_ref[...])
    hf = _mm_f32(h, w3f_ref[...]) * window
    hb = _mm_f32(h, w3b_ref[...]) * window
    norm = (jnp.sum(jnp.abs(hf), axis=0, keepdims=True) + jnp.sum(jnp.abs(hb), axis=0, keepdims=True))
    hf_ref[0] = hf / norm
    hb_ref[0] = hb / norm


def _hyena_filters(n, w1, b1, w2, b2, w3, freq):
    t = jnp.linspace(0.0, 1.0, n, dtype=F32)[:, None]
    bands = jnp.linspace(1e-4, HYENA_BANDS - 1, HYENA_BANDS, dtype=F32)
    ang = (2.0 * math.pi / n) * jnp.arange(n, dtype=F32)[:, None] * bands[None, :]
    feats = jnp.concatenate([t, jnp.cos(ang), -jnp.sin(ang),
                             jnp.zeros((n, HYENA_FW - HYENA_EMB), F32)], axis=-1)
    w1p = jnp.zeros((HYENA_FW, HYENA_FW), F32).at[:HYENA_EMB].set(w1)
    deltas = jnp.abs(jnp.linspace(math.log(HYENA_TARGET) / HYENA_SLOW,
                                  math.log(HYENA_TARGET) / HYENA_FAST, HYENA_DIM, dtype=F32))[None, :]
    tc = 256
    nj = HYENA_DIM // tc
    full = lambda shape: pl.BlockSpec(shape, lambda o, j: (0,) * len(shape))
    out = pl.BlockSpec((1, n, tc), lambda o, j: (o, 0, j))
    return pl.pallas_call(
        _hyena_filter_kernel,
        grid=(HYENA_ORDER, nj),
        in_specs=[full((n, HYENA_FW)), full((HYENA_FW, HYENA_FW)), full((1, HYENA_FW)),
                  full((HYENA_FW, HYENA_FW)), full((1, HYENA_FW)), full((2, HYENA_FW)),
                  pl.BlockSpec((HYENA_FW, tc), lambda o, j: (0, o * 2 * nj + j)),
                  pl.BlockSpec((HYENA_FW, tc), lambda o, j: (0, o * 2 * nj + nj + j)),
                  full((n, 1)), pl.BlockSpec((1, tc), lambda o, j: (0, j))],
        out_specs=[out, out],
        out_shape=[jax.ShapeDtypeStruct((HYENA_ORDER, n, HYENA_DIM), F32)] * 2,
        compiler_params=_cparams(("parallel", "parallel")),
        name="hyena_filters",
    )(feats, w1p, b1.reshape(1, -1), w2, b2.reshape(1, -1), freq, w3, w3, t, deltas)


def _dft_tables(n1):
    nc = n1 * DFT_N2
    t2 = np.arange(DFT_N2)[:, None, None]
    f1 = np.arange(n1)[None, :, None]
    t1 = np.arange(n1)[None, None, :]
    theta = 2.0 * np.pi * ((f1 * (DFT_N2 * t1 + t2)) % nc) / nc
    g_fwd = np.concatenate([np.cos(theta), -np.sin(theta)], axis=1)
    g_inv = np.concatenate([np.cos(theta), -np.sin(theta)], axis=1).transpose(0, 2, 1) / nc
    k = np.arange(DFT_N2)
    phi = 2.0 * np.pi * ((k[:, None] * k[None, :]) % DFT_N2) / DFT_N2
    c, s = np.cos(phi), np.sin(phi)
    f_fwd = np.block([[c, s], [-s, c]])
    f_inv = np.block([[c, -s], [s, c]])
    return (jnp.asarray(g_fwd, F32), jnp.asarray(g_inv, F32), jnp.asarray(f_fwd, BF16),
            jnp.asarray(f_inv, BF16))


def _dft1_kernel(x_ref, g_ref, o_ref, *, rows_valid):
    rows = lax.broadcasted_iota(jnp.int32, (x_ref.shape[1], HYENA_DIM), 0)
    for tt in range(HY_TT):
        cols = slice(tt * HYENA_DIM, (tt + 1) * HYENA_DIM)
        x = x_ref[0, :, cols]
        if rows_valid < x_ref.shape[1]:
            x = jnp.where(rows < rows_valid, x, 0.0)
        o_ref[0, :, cols] = _mm(g_ref[tt], x).astype(o_ref.dtype)


def _dft_stage1(x, g_fwd, n1, rows_valid):
    bx, rows, _ = x.shape
    t1 = rows // DFT_N2
    xv = x.reshape(bx, t1, DFT_N2 * HYENA_DIM)
    w = HY_TT * HYENA_DIM
    return pl.pallas_call(
        functools.partial(_dft1_kernel, rows_valid=rows_valid),
        grid=(bx, DFT_N2 // HY_TT),
        in_specs=[pl.BlockSpec((1, t1, w), lambda bi, j: (bi, 0, j)),
                  pl.BlockSpec((HY_TT, 2 * n1, t1), lambda bi, j: (j, 0, 0))],
        out_specs=pl.BlockSpec((1, 2 * n1, w), lambda bi, j: (bi, 0, j)),
        out_shape=jax.ShapeDtypeStruct((bx, 2 * n1, DFT_N2 * HYENA_DIM), BF16),
        compiler_params=_cparams(("parallel", "parallel")),
        name="hyena_dft_stage1",
    )(xv, g_fwd[:, :, :t1])


def _spectrum_kernel(a_ref, f_ref, o_ref):
    a = jnp.concatenate([a_ref[0, 0, 0], a_ref[0, 1, 0]], axis=0)
    o_ref[0, 0] = jnp.dot(f_ref[...], a, preferred_element_type=F32)


def _filter_spectrum(a, f_fwd, n1):
    no = a.shape[0]
    a5 = a.reshape(no, 2, n1, DFT_N2, HYENA_DIM)
    return pl.pallas_call(
        _spectrum_kernel,
        grid=(no, n1),
        in_specs=[pl.BlockSpec((1, 2, 1, DFT_N2, HYENA_DIM), lambda o, f: (o, 0, f, 0, 0)),
                  pl.BlockSpec((2 * DFT_N2, 2 * DFT_N2), lambda o, f: (0, 0))],
        out_specs=pl.BlockSpec((1, 1, 2 * DFT_N2, HYENA_DIM), lambda o, f: (o, f, 0, 0)),
        out_shape=jax.ShapeDtypeStruct((no, n1, 2 * DFT_N2, HYENA_DIM), F32),
        compiler_params=_cparams(("parallel", "parallel")),
        name="hyena_filter_spectrum",
    )(a5, f_fwd)


def _dft2_kernel(a_ref, k_ref, ff_ref, fi_ref, o_ref):
    a = jnp.concatenate([a_ref[0, 0, 0], a_ref[0, 1, 0]], axis=0)
    x = jnp.dot(ff_ref[...], a, preferred_element_type=F32)
    xre, xim = x[:DFT_N2], x[DFT_N2:]
    kre, kim = k_ref[0, 0, :DFT_N2], k_ref[0, 0, DFT_N2:]
    y = jnp.concatenate([xre * kre - xim * kim, xre * kim + xim * kre], axis=0).astype(BF16)
    bm = jnp.dot(fi_ref[...], y, preferred_element_type=F32).astype(BF16)
    o_ref[0, 0, 0] = bm[:DFT_N2]
    o_ref[0, 1, 0] = bm[DFT_N2:]


def _dft_stage2(a, kspec, order, f_fwd, f_inv, n1):
    bx = a.shape[0]
    a5 = a.reshape(bx, 2, n1, DFT_N2, HYENA_DIM)
    blk = pl.BlockSpec((1, 2, 1, DFT_N2, HYENA_DIM), lambda f, bi: (bi, 0, f, 0, 0))
    mat = pl.BlockSpec((2 * DFT_N2, 2 * DFT_N2), lambda f, bi: (0, 0))
    out = pl.pallas_call(
        _dft2_kernel,
        grid=(n1, bx),
        in_specs=[blk, pl.BlockSpec((1, 1, 2 * DFT_N2, HYENA_DIM), lambda f, bi: (order, f, 0, 0)),
                  mat, mat],
        out_specs=blk,
        out_shape=jax.ShapeDtypeStruct(a5.shape, BF16),
        compiler_params=_cparams(("parallel", "parallel")),
        name="hyena_dft_stage2",
    )(a5, kspec, f_fwd, f_inv)
    return out.reshape(bx, 2 * n1, DFT_N2 * HYENA_DIM)


def _dft3_kernel(b_ref, g_ref, u_ref, gate_ref, bias_ref, o_ref):
    for tt in range(HY_TT):
        cols = slice(tt * HYENA_DIM, (tt + 1) * HYENA_DIM)
        y = _mm(g_ref[tt], b_ref[0, :, cols])
        o_ref[0, :, cols] = (gate_ref[0, :, cols] * (y + bias_ref[...] * u_ref[0, :, cols])).astype(o_ref.dtype)


def _dft_stage3(bm, g_inv, u, gate, bias, n1, out_dtype):
    bx, rows, _ = u.shape
    t1 = rows // DFT_N2
    w = HY_TT * HYENA_DIM
    tok = pl.BlockSpec((1, t1, w), lambda bi, j: (bi, 0, j))
    out = pl.pallas_call(
        _dft3_kernel,
        grid=(bx, DFT_N2 // HY_TT),
        in_specs=[pl.BlockSpec((1, 2 * n1, w), lambda bi, j: (bi, 0, j)),
                  pl.BlockSpec((HY_TT, t1, 2 * n1), lambda bi, j: (j, 0, 0)),
                  tok, tok, pl.BlockSpec((1, HYENA_DIM), lambda bi, j: (0, 0))],
        out_specs=tok,
        out_shape=jax.ShapeDtypeStruct((bx, t1, DFT_N2 * HYENA_DIM), out_dtype),
        compiler_params=_cparams(("parallel", "parallel")),
        name="hyena_dft_stage3",
    )(bm, g_inv[:, :t1, :], u.reshape(bx, t1, -1), gate.reshape(bx, t1, -1), bias.reshape(1, -1))
    return out.reshape(bx, rows, HYENA_DIM)


def _hyena_operator(v, x1, x2, n, filt_params, bias):
    n_pad = v.shape[1]
    nc = 2 * n_pad
    n1 = nc // DFT_N2
    g_fwd, g_inv, f_fwd, f_inv = _dft_tables(n1)
    hf, hb = _hyena_filters(n, *filt_params)
    kbuf = jnp.concatenate([hf, jnp.zeros((HYENA_ORDER, nc - 2 * n + 1, HYENA_DIM), F32),
                            hb[:, 1:][:, ::-1]], axis=1)
    kspec = _filter_spectrum(_dft_stage1(kbuf, g_fwd, n1, nc // DFT_N2), f_fwd, n1)
    z = v
    rows_valid = -(-n // DFT_N2)
    for o, gate in enumerate((x1, x2)):
        a = _dft_stage1(z, g_fwd, n1, rows_valid)
        bm = _dft_stage2(a, kspec, o, f_fwd, f_inv, n1)
        z = _dft_stage3(bm, g_inv, z, gate, bias[o], n1, F32 if o + 1 < HYENA_ORDER else BF16)
    return z


def _merge_kernel(x_ref, mod_ref, g_ref, gates_ref, bg_ref, ya_ref, yb_ref, yh_ref, yd_ref,
                  wb_ref, wo_ref, o_ref):
    merged = None
    for br, y_ref in enumerate((ya_ref, yb_ref, yh_ref, yd_ref)):
        gate = _sigmoid(gates_ref[0, :, br * D_MODEL:(br + 1) * D_MODEL] + bg_ref[br:br + 1])
        term = gate * jnp.dot(y_ref[0], wb_ref[br], preferred_element_type=F32)
        merged = term if merged is None else merged + term
    y = jnp.dot(merged.astype(BF16), wo_ref[...], preferred_element_type=F32)
    o_ref[0] = x_ref[0] + mod_ref[0, 5:6] * _rms(y, g_ref[3:4])


def _merge(x, mods, norm_g, p, b_gate, ya, yb, yh, yd, w_branch, w_out, *, n_ctx_tiles, row_off):
    b, s, _ = x.shape
    nt = s // TM - row_off
    n_lat = mods.shape[0] - 1
    br = lambda: pl.BlockSpec((1, TM, BRANCH_DIM), lambda bi, i: (bi, i, 0))
    return pl.pallas_call(
        _merge_kernel,
        grid=(b, nt),
        in_specs=[pl.BlockSpec((1, TM, D_MODEL), lambda bi, i: (bi, i + row_off, 0)),
                  pl.BlockSpec((1, N_MOD, D_MODEL),
                               lambda bi, i: (jnp.where(i + row_off < n_ctx_tiles, n_lat, bi), 0, 0)),
                  pl.BlockSpec((6, D_MODEL), lambda bi, i: (0, 0)),
                  pl.BlockSpec((1, TM, GATE_COLS), lambda bi, i: (bi, i + row_off, 0)),
                  pl.BlockSpec((N_BRANCH, D_MODEL), lambda bi, i: (0, 0)),
                  br(), br(), br(), br(),
                  pl.BlockSpec((N_BRANCH, BRANCH_DIM, D_MODEL), lambda bi, i: (0, 0, 0)),
                  pl.BlockSpec((D_MODEL, D_MODEL), lambda bi, i: (0, 0))],
        out_specs=pl.BlockSpec((1, TM, D_MODEL), lambda bi, i: (bi, i, 0)),
        out_shape=jax.ShapeDtypeStruct((b, nt * TM, D_MODEL), F32),
        compiler_params=_cparams(("parallel", "parallel")),
        name="merge_branches",
    )(x, mods, norm_g, p, b_gate, ya, yb, yh, yd, w_branch, w_out)


def kernel(x, c, ctx, c_ctx, w_mod, b_mod, norm_g, ffn_w13, ffn_w2, w_in, b_gate, mla_norm_q, mla_norm_kv, mla_w_uq, mla_w_ukv, rwkv_mu, rwkv_w0, rwkv_w_up, rwkv_a0, rwkv_a_up, rwkv_g_up, rwkv_kvec, rwkv_r_k, rwkv_ln_g, rwkv_ln_b, hyena_conv, hyena_conv_b, hyena_w1, hyena_b1, hyena_w2, hyena_b2, hyena_w3, hyena_freq, hyena_bias, swa_sink, w_branch, w_out):
    b, n, _ = x.shape
    n_ctx = ctx.shape[1]
    nct = n_ctx // TM
    xall = jnp.concatenate([ctx, x], axis=1)
    c_all = jnp.concatenate([c, c_ctx[None]], axis=0)
    tabs_mla = _rope_tables(n, n_ctx, MLA_ROPE, MLA_NOPE, LANES)
    tabs_swa = _rope_tables(n, n_ctx, SWA_HEAD, 0, SWA_HEAD)
    depth = w_mod.shape[0]
    for l in range(depth):
        with_ctx = l + 1 < depth
        row_off = 0 if with_ctx else nct
        mods = _modulation(c_all, w_mod[l], b_mod[l])
        xall = _ffn(xall, mods, norm_g[l], ffn_w13[l, 0].astype(BF16), ffn_w2[l, 0].astype(BF16),
                    mod0=0, g0=0, n_ctx_tiles=nct, row_off=0)
        coef = _shift_coefficients(rwkv_mu[l], hyena_conv[l], hyena_conv_b[l])
        p = _inproj(xall, mods, norm_g[l], _permute_w_in(w_in[l]).astype(BF16), coef, n_ctx_tiles=nct)
        q, k, v = _mla_prep(p, mla_norm_q[l], mla_norm_kv[l], mla_w_uq[l], mla_w_ukv[l], tabs_mla)
        ya = _mla_attention(q, k, v, n_ctx=n_ctx, q_off=row_off)
        rbar, y0, a, g, bonus, gate = _rwkv_chunks(p, rwkv_kvec[l], rwkv_w0[l], rwkv_a0[l], rwkv_w_up[l],
                                                   rwkv_a_up[l], rwkv_g_up[l], rwkv_r_k[l])
        y = _rwkv_scan(rbar, y0, a, g, n_ctx=n_ctx)
        yb = _rwkv_readout(y, bonus, gate, rwkv_ln_g[l], rwkv_ln_b[l], row_off=row_off)
        filt = (hyena_w1[l], hyena_b1[l], hyena_w2[l], hyena_b2[l], hyena_w3[l], hyena_freq[l])
        def hyena_segment(rows, n_rows):
            ins = []
            for j in range(3):
                t = p[:, rows, P_HY + j * HYENA_DIM:P_HY + (j + 1) * HYENA_DIM]
                ins.append(jnp.pad(t, ((0, 0), (0, max(n_rows, HY_MIN_LEN) - n_rows), (0, 0))))
            return _hyena_operator(*ins, n_rows, filt, hyena_bias[l])[:, :n_rows]

        yh = hyena_segment(slice(n_ctx, None), n)
        if with_ctx:
            yh = jnp.concatenate([hyena_segment(slice(0, n_ctx), n_ctx), yh], axis=1)
        q, k, v = _swa_prep(p, tabs_swa)
        yd = _swa_attention(q, k, v, swa_sink[l], n_ctx=n_ctx, q_off=row_off * TM // SWA_TQ)
        xall = _merge(xall, mods, norm_g[l], p, b_gate[l], ya, yb, yh, yd, w_branch[l].astype(BF16),
                      w_out[l].astype(BF16), n_ctx_tiles=nct, row_off=row_off)
        xall = _ffn(xall, mods, norm_g[l], ffn_w13[l, 1].astype(BF16), ffn_w2[l, 1].astype(BF16),
                    mod0=6, g0=4, n_ctx_tiles=nct - row_off, row_off=0)
    return xall
```

```python
import functools
import math

import numpy as np
import jax
import jax.numpy as jnp
from jax import lax
from jax.experimental import pallas as pl
from jax.experimental.pallas import tpu as pltpu

F32 = jnp.float32
BF16 = jnp.bfloat16

D_MODEL = 1024
GRID_W = 64
N_BRANCH = 4
N_MOD = 9
FF_DIM = 2816
EPS = 1e-6
ROPE_BASE = 10000.0
NEG_INF = -1e30
BRANCH_DIM = 512
MLA_HEADS = 8
MLA_NOPE = 64
MLA_ROPE = 32
MLA_V = 64
MLA_Q_RANK = 256
MLA_KV_RANK = 128
RWKV_HEADS = 8
RWKV_HEAD = 64
RWKV_DIM = RWKV_HEADS * RWKV_HEAD
DECAY_LORA = 64
AAA_LORA = 64
GATE_LORA = 128
RWKV_LN_EPS = 64e-5
HYENA_DIM = 512
HYENA_ORDER = 2
HYENA_EMB = 33
HYENA_BANDS = (HYENA_EMB - 1) // 2
HYENA_FW = 64
HYENA_TARGET = 1e-2
HYENA_FAST = 0.3
HYENA_SLOW = 1.5
SWA_HEADS = 8
SWA_KV_HEADS = 2
SWA_HEAD = 64
SWA_GROUP = SWA_HEADS // SWA_KV_HEADS
WINDOW = 128
GATE_COLS = N_BRANCH * D_MODEL
MLA_COLS = MLA_Q_RANK + MLA_KV_RANK + MLA_ROPE
RWKV_COLS = 3 * RWKV_DIM + DECAY_LORA + AAA_LORA + GATE_LORA
HYENA_COLS = 3 * HYENA_DIM
SWA_COLS = (SWA_HEADS + 2 * SWA_KV_HEADS) * SWA_HEAD

LANES = 128
V7X_VMEM_LIMIT = 56 * 1024 * 1024

TM = 256
FF_CHUNK = 256
IN_CHUNK = 512
CHUNK = 64

P_GATE = 0
P_HY = 4096
P_RKV = 5632
P_SWAQ = 7168
P_LORA = 7680
P_CQ = 7936
P_CKV = 8192
P_KR = 8320
P_SWAK = 8448
P_SWAV = 8576
P_COLS = 8704
_SHIFT_COLS = ((P_HY, P_SWAQ), (P_LORA, P_CQ))
_SHIFT_CHUNKS = [any(lo < (j + 1) * IN_CHUNK and j * IN_CHUNK < hi for lo, hi in _SHIFT_COLS)
                 for j in range(P_COLS // IN_CHUNK)]


def _cparams(sem, vmem=V7X_VMEM_LIMIT):
    return pltpu.CompilerParams(dimension_semantics=sem, vmem_limit_bytes=vmem)


def _mm(a, b):
    return jnp.dot(a.astype(BF16), b.astype(BF16), preferred_element_type=F32)


def _mm_nt(a, b):
    return lax.dot_general(a.astype(BF16), b.astype(BF16), (((1,), (1,)), ((), ())),
                           preferred_element_type=F32)


def _mm_tn(a, b):
    return lax.dot_general(a.astype(BF16), b.astype(BF16), (((0,), (0,)), ((), ())),
                           preferred_element_type=F32)


def _mm_f32(a, b):
    return jnp.dot(a, b, preferred_element_type=F32, precision=lax.Precision.HIGHEST)


def _rms(x, g):
    return x * lax.rsqrt(jnp.mean(x * x, axis=-1, keepdims=True) + EPS) * g


def _sigmoid(x):
    return 1.0 / (1.0 + jnp.exp(-x))


def _mod_kernel(c_ref, w_ref, b_ref, o_ref):
    c = c_ref[...]
    o_ref[...] = _mm(c * _sigmoid(c), w_ref[...]) + b_ref[...]


def _modulation(c_all, w_mod, b_mod):
    r = c_all.shape[0]
    rp = -(-r // 8) * 8
    c_pad = jnp.zeros((rp, D_MODEL), F32).at[:r].set(c_all)
    tn = 1024
    out = pl.pallas_call(
        _mod_kernel,
        grid=(N_MOD * D_MODEL // tn,),
        in_specs=[pl.BlockSpec((rp, D_MODEL), lambda j: (0, 0)),
                  pl.BlockSpec((D_MODEL, tn), lambda j: (0, j)),
                  pl.BlockSpec((1, tn), lambda j: (0, j))],
        out_specs=pl.BlockSpec((rp, tn), lambda j: (0, j)),
        out_shape=jax.ShapeDtypeStruct((rp, N_MOD * D_MODEL), F32),
        compiler_params=_cparams(("arbitrary",)),
        name="modulation",
    )(c_pad, w_mod, b_mod.reshape(1, -1))
    return out[:r].reshape(r, N_MOD, D_MODEL)


def _ffn_kernel(x_ref, mod_ref, g_ref, w13_ref, w2_ref, o_ref, *, mod0, g0):
    x = x_ref[0]
    shift = mod_ref[0, mod0:mod0 + 1]
    scale = mod_ref[0, mod0 + 1:mod0 + 2]
    gate = mod_ref[0, mod0 + 2:mod0 + 3]
    u = (_rms(x, g_ref[g0:g0 + 1]) * (1.0 + scale) + shift).astype(BF16)
    acc = jnp.zeros(x.shape, F32)
    for f in range(FF_DIM // FF_CHUNK):
        lo = f * FF_CHUNK
        a = jnp.dot(u, w13_ref[:, lo:lo + FF_CHUNK], preferred_element_type=F32)
        b = jnp.dot(u, w13_ref[:, FF_DIM + lo:FF_DIM + lo + FF_CHUNK], preferred_element_type=F32)
        h = (a * _sigmoid(a) * b).astype(BF16)
        acc = acc + jnp.dot(h, w2_ref[lo:lo + FF_CHUNK, :], preferred_element_type=F32)
    o_ref[0] = x + 0.5 * gate * _rms(acc, g_ref[g0 + 1:g0 + 2])


def _ffn(x, mods, norm_g, w13, w2, *, mod0, g0, n_ctx_tiles, row_off):
    b, s, _ = x.shape
    nt = s // TM - row_off
    n_lat = mods.shape[0] - 1

    def mod_idx(bi, i):
        return (jnp.where(i + row_off < n_ctx_tiles, n_lat, bi), 0, 0)

    return pl.pallas_call(
        functools.partial(_ffn_kernel, mod0=mod0, g0=g0),
        grid=(b, nt),
        in_specs=[pl.BlockSpec((1, TM, D_MODEL), lambda bi, i: (bi, i + row_off, 0)),
                  pl.BlockSpec((1, N_MOD, D_MODEL), mod_idx),
                  pl.BlockSpec((6, D_MODEL), lambda bi, i: (0, 0)),
                  pl.BlockSpec(memory_space=pltpu.VMEM),
                  pl.BlockSpec(memory_space=pltpu.VMEM)],
        out_specs=pl.BlockSpec((1, TM, D_MODEL), lambda bi, i: (bi, i, 0)),
        out_shape=jax.ShapeDtypeStruct((b, nt * TM, D_MODEL), F32),
        compiler_params=_cparams(("parallel", "parallel")),
        name="ffn_half_step",
    )(x, mods, norm_g, w13, w2)


def _shift_rows(p, first_row, last_row):
    rows = lax.broadcasted_iota(jnp.int32, p.shape, 0)
    prev = jnp.where(rows == 0, first_row, pltpu.roll(p, 1, axis=0))
    nxt = jnp.where(rows == p.shape[0] - 1, last_row, pltpu.roll(p, p.shape[0] - 1, axis=0))
    return prev, nxt


def _inproj_kernel(x_ref, xp_ref, xn_ref, mod_ref, g_ref, w_ref, coef_ref, o_ref, *, n_ctx_tiles):
    i = pl.program_id(1)
    n_tiles = pl.num_programs(1)
    has_prev = jnp.where((i == 0) | (i == n_ctx_tiles), 0.0, 1.0)
    has_next = jnp.where((i == n_ctx_tiles - 1) | (i == n_tiles - 1), 0.0, 1.0)
    shift = mod_ref[0, 3:4]
    scale = 1.0 + mod_ref[0, 4:5]
    g = g_ref[2:3]
    u = (_rms(x_ref[0], g) * scale + shift).astype(BF16)
    u_prev = (_rms(xp_ref[0], g) * scale + shift).astype(BF16)
    u_next = (_rms(xn_ref[0], g) * scale + shift).astype(BF16)
    for j in range(P_COLS // IN_CHUNK):
        cols = slice(j * IN_CHUNK, (j + 1) * IN_CHUNK)
        p = jnp.dot(u, w_ref[:, cols], preferred_element_type=F32)
        if _SHIFT_CHUNKS[j]:
            p_first = jnp.dot(u_prev, w_ref[:, cols], preferred_element_type=F32)[7:8] * has_prev
            p_last = jnp.dot(u_next, w_ref[:, cols], preferred_element_type=F32)[0:1] * has_next
            prev, nxt = _shift_rows(p, p_first, p_last)
            p = (coef_ref[0:1, cols] * p + coef_ref[1:2, cols] * prev + coef_ref[2:3, cols] * nxt
                 + coef_ref[3:4, cols])
        o_ref[0, :, cols] = p


def _inproj(x, mods, norm_g, w_in_p, coef, *, n_ctx_tiles):
    b, s, _ = x.shape
    n_lat = mods.shape[0] - 1
    r8 = TM // 8
    return pl.pallas_call(
        functools.partial(_inproj_kernel, n_ctx_tiles=n_ctx_tiles),
        grid=(b, s // TM),
        in_specs=[pl.BlockSpec((1, TM, D_MODEL), lambda bi, i: (bi, i, 0)),
                  pl.BlockSpec((1, 8, D_MODEL), lambda bi, i: (bi, jnp.maximum(i * r8 - 1, 0), 0)),
                  pl.BlockSpec((1, 8, D_MODEL),
                               lambda bi, i: (bi, jnp.minimum((i + 1) * r8, s // 8 - 1), 0)),
                  pl.BlockSpec((1, N_MOD, D_MODEL),
                               lambda bi, i: (jnp.where(i < n_ctx_tiles, n_lat, bi), 0, 0)),
                  pl.BlockSpec((6, D_MODEL), lambda bi, i: (0, 0)),
                  pl.BlockSpec(memory_space=pltpu.VMEM),
                  pl.BlockSpec((4, P_COLS), lambda bi, i: (0, 0))],
        out_specs=pl.BlockSpec((1, TM, P_COLS), lambda bi, i: (bi, i, 0)),
        out_shape=jax.ShapeDtypeStruct((b, s, P_COLS), F32),
        compiler_params=_cparams(("parallel", "parallel")),
        name="in_projection",
    )(x, x, x, mods, norm_g, w_in_p, coef)


def _shift_coefficients(rwkv_mu, hyena_conv, hyena_conv_b):
    mu = rwkv_mu.astype(F32)
    coef = jnp.zeros((4, P_COLS), F32).at[0].set(1.0)
    for off, sl in ((P_RKV, slice(0, 3 * RWKV_DIM)), (P_LORA, slice(3 * RWKV_DIM, RWKV_COLS))):
        width = sl.stop - sl.start
        coef = coef.at[0, off:off + width].set(1.0 - mu[0, sl] - mu[1, sl])
        coef = coef.at[1, off:off + width].set(mu[0, sl])
        coef = coef.at[2, off:off + width].set(mu[1, sl])
    hy = slice(P_HY, P_HY + HYENA_COLS)
    coef = coef.at[0, hy].set(hyena_conv[1]).at[1, hy].set(hyena_conv[0]).at[2, hy].set(hyena_conv[2])
    return coef.at[3, hy].set(hyena_conv_b)


def _permute_w_in(w_in):
    o_mla = GATE_COLS
    o_rwkv = o_mla + MLA_COLS
    o_hy = o_rwkv + RWKV_COLS
    o_swa = o_hy + HYENA_COLS
    z = lambda n: jnp.zeros((D_MODEL, n), w_in.dtype)
    parts = [
        w_in[:, :GATE_COLS],
        w_in[:, o_hy:o_hy + HYENA_COLS],
        w_in[:, o_rwkv:o_rwkv + 3 * RWKV_DIM],
        w_in[:, o_swa:o_swa + SWA_HEADS * SWA_HEAD],
        w_in[:, o_rwkv + 3 * RWKV_DIM:o_rwkv + RWKV_COLS],
        w_in[:, o_mla:o_mla + MLA_Q_RANK],
        w_in[:, o_mla + MLA_Q_RANK:o_mla + MLA_Q_RANK + MLA_KV_RANK],
        z(MLA_NOPE), w_in[:, o_mla + MLA_Q_RANK + MLA_KV_RANK:o_mla + MLA_COLS],
        z(LANES - MLA_NOPE - MLA_ROPE),
        w_in[:, o_swa + SWA_HEADS * SWA_HEAD:o_swa + SWA_COLS],
    ]
    out = jnp.concatenate(parts, axis=1)
    assert out.shape[1] == P_COLS
    return out


def _rope_tables(n_lat, n_ctx, rot_dim, lane0, period):
    rows = n_lat // GRID_W
    row = jnp.repeat(jnp.arange(rows, dtype=F32), GRID_W)
    col = jnp.tile(jnp.arange(GRID_W, dtype=F32), rows)
    axis_dim = rot_dim // 2
    h = axis_dim // 2
    inv_freq = ROPE_BASE ** (-jnp.arange(0, axis_dim, 2, dtype=F32) / axis_dim)
    ang_r = row[:, None] * inv_freq
    ang_c = col[:, None] * inv_freq
    cos_rot = jnp.concatenate([jnp.cos(ang_r)] * 2 + [jnp.cos(ang_c)] * 2, axis=1)
    zeros = jnp.zeros_like(ang_r)
    sin_a = jnp.concatenate([-jnp.sin(ang_r), zeros, -jnp.sin(ang_c), zeros], axis=1)
    sin_b = jnp.concatenate([zeros, jnp.sin(ang_r), zeros, jnp.sin(ang_c)], axis=1)

    def widen(t, fill):
        g = jnp.full((n_lat, period), fill, F32).at[:, lane0:lane0 + rot_dim].set(t)
        g = jnp.tile(g, (1, LANES // period))
        ctx = jnp.full((n_ctx, LANES), fill, F32)
        return jnp.concatenate([ctx, g], axis=0)

    return widen(cos_rot, 1.0), widen(sin_a, 0.0), widen(sin_b, 0.0), h


def _rope128(x, cos, sin_a, sin_b, h):
    return x * cos + pltpu.roll(x, LANES - h, axis=1) * sin_a + pltpu.roll(x, h, axis=1) * sin_b


LOG2E = math.log2(math.e)
MLA_SCALE = (MLA_NOPE + MLA_ROPE) ** -0.5 * LOG2E
MLA_ONE_LANE = (MLA_V, 0)


def _mla_prep_kernel(cq_ref, ckv_ref, kr_ref, gq_ref, gkv_ref, wq_ref, wk_ref, wv_ref, vone_ref,
                     cos_ref, sa_ref, sb_ref, q_ref, k_ref, v_ref, *, h):
    cos, sa, sb = cos_ref[...], sa_ref[...], sb_ref[...]
    cq = _rms(cq_ref[0], gq_ref[...]).astype(BF16)
    ckv = _rms(ckv_ref[0], gkv_ref[...]).astype(BF16)
    q = jnp.dot(cq, wq_ref[...], preferred_element_type=F32)
    k = jnp.dot(ckv, wk_ref[...], preferred_element_type=F32)
    kr = _rope128(kr_ref[0], cos, sa, sb, h)
    for hd in range(MLA_HEADS):
        sl = slice(hd * LANES, (hd + 1) * LANES)
        q_ref[0, :, sl] = (_rope128(q[:, sl], cos, sa, sb, h) * MLA_SCALE).astype(BF16)
        k_ref[0, :, sl] = (k[:, sl] + kr).astype(BF16)
    v_ref[0] = (jnp.dot(ckv, wv_ref[...], preferred_element_type=F32) + vone_ref[...]).astype(BF16)


def _mla_prep(p, norm_q, norm_kv, w_uq, w_ukv, tabs):
    b, s, _ = p.shape
    cos, sa, sb, h = tabs
    hq = MLA_NOPE + MLA_ROPE
    wq = jnp.zeros((MLA_Q_RANK, MLA_HEADS, LANES), F32).at[:, :, :hq].set(
        w_uq.reshape(MLA_Q_RANK, MLA_HEADS, hq)).reshape(MLA_Q_RANK, MLA_HEADS * LANES).astype(BF16)
    wkv = w_ukv.reshape(MLA_KV_RANK, MLA_HEADS, MLA_NOPE + MLA_V)
    wk = jnp.zeros((MLA_KV_RANK, MLA_HEADS, LANES), F32).at[:, :, :MLA_NOPE].set(
        wkv[:, :, :MLA_NOPE]).reshape(MLA_KV_RANK, MLA_HEADS * LANES).astype(BF16)
    wv = jnp.zeros((MLA_KV_RANK, MLA_HEADS, LANES), F32)
    wv = wv.at[:, 0::2, :MLA_V].set(wkv[:, 0::2, MLA_NOPE:]).at[:, 1::2, LANES - MLA_V:].set(wkv[:, 1::2, MLA_NOPE:])
    wv = wv.reshape(MLA_KV_RANK, MLA_HEADS * LANES).astype(BF16)
    vone = jnp.zeros((MLA_HEADS, LANES), F32)
    vone = vone.at[0::2, MLA_ONE_LANE[0]].set(1.0).at[1::2, MLA_ONE_LANE[1]].set(1.0).reshape(1, -1)
    full = lambda shape: pl.BlockSpec(shape, lambda bi, i: (0,) * len(shape))
    tab = pl.BlockSpec((TM, LANES), lambda bi, i: (i, 0))
    return pl.pallas_call(
        functools.partial(_mla_prep_kernel, h=h),
        grid=(b, s // TM),
        in_specs=[pl.BlockSpec((1, TM, MLA_Q_RANK), lambda bi, i: (bi, i, P_CQ // MLA_Q_RANK)),
                  pl.BlockSpec((1, TM, LANES), lambda bi, i: (bi, i, P_CKV // LANES)),
                  pl.BlockSpec((1, TM, LANES), lambda bi, i: (bi, i, P_KR // LANES)),
                  full((1, MLA_Q_RANK)), full((1, MLA_KV_RANK)),
                  full(wq.shape), full(wk.shape), full(wv.shape), full(vone.shape), tab, tab, tab],
        out_specs=[pl.BlockSpec((1, TM, MLA_HEADS * LANES), lambda bi, i: (bi, i, 0))] * 3,
        out_shape=[jax.ShapeDtypeStruct((b, s, MLA_HEADS * LANES), BF16)] * 3,
        compiler_params=_cparams(("parallel", "parallel")),
        name="mla_prep",
    )(p, p, p, norm_q.reshape(1, -1), norm_kv.reshape(1, -1), wq, wk, wv, vone, cos, sa, sb)


def _mla_attn_kernel(q_ref, k_ref, v_ref, o_ref, *, n_ctx, q_off):
    i = pl.program_id(2) + q_off
    tq = q_ref.shape[1]

    def attend(n_keys):
        outs = []
        for hd in range(2):
            sl = slice(hd * LANES, (hd + 1) * LANES)
            s = lax.dot_general(q_ref[0, :, sl], k_ref[0, :n_keys, sl], (((1,), (1,)), ((), ())),
                                preferred_element_type=F32)
            e = jnp.exp2(s - jnp.max(s, axis=-1, keepdims=True)).astype(BF16)
            o = jnp.dot(e, v_ref[0, :n_keys, sl], preferred_element_type=F32)
            one = MLA_ONE_LANE[hd]
            outs.append(o / o[:, one:one + 1])
        lane = lax.broadcasted_iota(jnp.int32, outs[0].shape, 1)
        o_ref[0] = jnp.where(lane < MLA_V, outs[0], outs[1]).astype(BF16)

    @pl.when(i * tq < n_ctx)
    def _():
        attend(n_ctx)

    @pl.when(i * tq >= n_ctx)
    def _():
        attend(k_ref.shape[1])


def _mla_attention(q, k, v, *, n_ctx, q_off):
    b, s, _ = q.shape
    nq = s // TM - q_off
    return pl.pallas_call(
        functools.partial(_mla_attn_kernel, n_ctx=n_ctx, q_off=q_off),
        grid=(b, MLA_HEADS // 2, nq),
        in_specs=[pl.BlockSpec((1, TM, 2 * LANES), lambda bi, hp, i: (bi, i + q_off, hp)),
                  pl.BlockSpec((1, s, 2 * LANES), lambda bi, hp, i: (bi, 0, hp)),
                  pl.BlockSpec((1, s, 2 * LANES), lambda bi, hp, i: (bi, 0, hp))],
        out_specs=pl.BlockSpec((1, TM, LANES), lambda bi, hp, i: (bi, i, hp)),
        out_shape=jax.ShapeDtypeStruct((b, nq * TM, MLA_HEADS * MLA_V), BF16),
        compiler_params=_cparams(("parallel", "parallel", "parallel")),
        name="mla_attention",
    )(q, k, v)


SWA_SCALE = SWA_HEAD ** -0.5 * LOG2E
SWA_TQ = 128
SWA_ONE_LANE = (SWA_HEAD, 0)


def _swa_prep_kernel(q_ref, k_ref, v_ref, cos_ref, sa_ref, sb_ref, qo_ref, ko_ref, vo_ref, *, h):
    cos, sa, sb = cos_ref[...], sa_ref[...], sb_ref[...]
    lane = lax.broadcasted_iota(jnp.int32, cos.shape, 1)
    low = lane < SWA_HEAD
    for j in range(SWA_HEADS // 2):
        blk = _rope128(q_ref[0, :, j * LANES:(j + 1) * LANES], cos, sa, sb, h) * SWA_SCALE
        qo_ref[0, :, (2 * j) * LANES:(2 * j + 1) * LANES] = jnp.where(low, blk, 0.0).astype(BF16)
        qo_ref[0, :, (2 * j + 1) * LANES:(2 * j + 2) * LANES] = jnp.where(
            low, pltpu.roll(blk, SWA_HEAD, axis=1), 0.0).astype(BF16)
    kb = _rope128(k_ref[0], cos, sa, sb, h)
    ko_ref[0, :, :LANES] = jnp.where(low, kb, 0.0).astype(BF16)
    ko_ref[0, :, LANES:] = jnp.where(low, pltpu.roll(kb, SWA_HEAD, axis=1), 0.0).astype(BF16)
    vb = v_ref[0]
    vr = pltpu.roll(vb, SWA_HEAD, axis=1)
    one_lo = jnp.where(lane == SWA_ONE_LANE[0], 1.0, 0.0)
    one_hi = jnp.where(lane == SWA_ONE_LANE[1], 1.0, 0.0)
    vo_ref[0, :, 0 * LANES:1 * LANES] = jnp.where(low, vb, one_lo).astype(BF16)
    vo_ref[0, :, 1 * LANES:2 * LANES] = jnp.where(low, one_hi, vr).astype(BF16)
    vo_ref[0, :, 2 * LANES:3 * LANES] = jnp.where(low, vr, one_lo).astype(BF16)
    vo_ref[0, :, 3 * LANES:4 * LANES] = jnp.where(low, one_hi, vb).astype(BF16)


def _swa_prep(p, tabs):
    b, s, _ = p.shape
    cos, sa, sb, h = tabs
    tab = pl.BlockSpec((TM, LANES), lambda bi, i: (i, 0))
    nq = SWA_HEADS * SWA_HEAD
    return pl.pallas_call(
        functools.partial(_swa_prep_kernel, h=h),
        grid=(b, s // TM),
        in_specs=[pl.BlockSpec((1, TM, nq), lambda bi, i: (bi, i, P_SWAQ // nq)),
                  pl.BlockSpec((1, TM, LANES), lambda bi, i: (bi, i, P_SWAK // LANES)),
                  pl.BlockSpec((1, TM, LANES), lambda bi, i: (bi, i, P_SWAV // LANES)),
                  tab, tab, tab],
        out_specs=[pl.BlockSpec((1, TM, SWA_HEADS * LANES), lambda bi, i: (bi, i, 0)),
                   pl.BlockSpec((1, TM, SWA_KV_HEADS * LANES), lambda bi, i: (bi, i, 0)),
                   pl.BlockSpec((1, TM, 4 * LANES), lambda bi, i: (bi, i, 0))],
        out_shape=[jax.ShapeDtypeStruct((b, s, SWA_HEADS * LANES), BF16),
                   jax.ShapeDtypeStruct((b, s, SWA_KV_HEADS * LANES), BF16),
                   jax.ShapeDtypeStruct((b, s, 4 * LANES), BF16)],
        compiler_params=_cparams(("parallel", "parallel")),
        name="swa_prep",
    )(p, p, p, cos, sa, sb)


def _swa_attn_kernel(sink_ref, q_ref, k_ref, v_ref, o_ref, *, n_ctx, q_off):
    i = pl.program_id(1) + q_off
    s_len = k_ref.shape[1]
    tq = SWA_TQ
    n_loc = tq + 2 * WINDOW
    r0 = i * tq
    is_lat = r0 >= n_ctx
    start = pl.multiple_of(jnp.clip(r0 - WINDOW, 0, s_len - n_loc), LANES)
    rows_g = SWA_GROUP * tq
    row = lax.broadcasted_iota(jnp.int32, (rows_g, n_loc), 0)
    qpos = r0 - n_ctx + row % tq
    kpos = start - n_ctx + lax.broadcasted_iota(jnp.int32, (rows_g, n_loc), 1)
    loc_ok = (jnp.abs(kpos - qpos) <= WINDOW) & (kpos >= 0) & is_lat
    k_loc = k_ref[0, pl.ds(start, n_loc), :]
    v_loc = v_ref[0, pl.ds(start, n_loc), :]
    k_ctx = k_ref[0, 0:n_ctx, :]
    v_ctx = v_ref[0, 0:n_ctx, :]
    head_row = lax.broadcasted_iota(jnp.int32, (rows_g, 1), 0) // tq
    lane = lax.broadcasted_iota(jnp.int32, (tq, LANES), 1)
    stages = []
    for g in range(SWA_KV_HEADS):
        q = jnp.concatenate([q_ref[0, :, hd * LANES:(hd + 1) * LANES]
                             for hd in range(g * SWA_GROUP, (g + 1) * SWA_GROUP)], axis=0)
        kg = slice(g * LANES, (g + 1) * LANES)
        s_loc = lax.dot_general(q, k_loc[:, kg], (((1,), (1,)), ((), ())), preferred_element_type=F32)
        s_ctx = lax.dot_general(q, k_ctx[:, kg], (((1,), (1,)), ((), ())), preferred_element_type=F32)
        sink = jnp.zeros((rows_g, 1), F32)
        for hh in range(SWA_GROUP):
            sink = jnp.where(head_row == hh, sink_ref[g * SWA_GROUP + hh] * LOG2E, sink)
        stages.append((jnp.where(loc_ok, s_loc, NEG_INF), s_ctx, sink))
    for g, (s_loc, s_ctx, sink) in enumerate(stages):
        m = jnp.maximum(jnp.maximum(jnp.max(s_loc, axis=-1, keepdims=True),
                                    jnp.max(s_ctx, axis=-1, keepdims=True)), sink)
        e = jnp.concatenate([jnp.exp2(s_loc - m), jnp.exp2(s_ctx - m)], axis=1).astype(BF16)
        e_sink = jnp.exp2(sink - m)
        outs = []
        for par in range(2):
            vg = slice((2 * g + par) * LANES, (2 * g + par + 1) * LANES)
            o = jnp.dot(e, jnp.concatenate([v_loc[:, vg], v_ctx[:, vg]], axis=0), preferred_element_type=F32)
            one = SWA_ONE_LANE[par]
            outs.append(o / (o[:, one:one + 1] + e_sink))
        for pi in range(SWA_GROUP // 2):
            even = outs[0][(2 * pi) * tq:(2 * pi + 1) * tq]
            odd = outs[1][(2 * pi + 1) * tq:(2 * pi + 2) * tq]
            blk = g * (SWA_GROUP // 2) + pi
            o_ref[0, :, blk * LANES:(blk + 1) * LANES] = jnp.where(lane < SWA_HEAD, even, odd).astype(BF16)


def _swa_attention(q, k, v, sink, *, n_ctx, q_off):
    b, s, _ = q.shape
    nq = s // SWA_TQ - q_off
    return pl.pallas_call(
        functools.partial(_swa_attn_kernel, n_ctx=n_ctx, q_off=q_off),
        grid=(b, nq),
        in_specs=[pl.BlockSpec(memory_space=pltpu.SMEM),
                  pl.BlockSpec((1, SWA_TQ, SWA_HEADS * LANES), lambda bi, i: (bi, i + q_off, 0)),
                  pl.BlockSpec((1, s, SWA_KV_HEADS * LANES), lambda bi, i: (bi, 0, 0)),
                  pl.BlockSpec((1, s, 4 * LANES), lambda bi, i: (bi, 0, 0))],
        out_specs=pl.BlockSpec((1, SWA_TQ, SWA_HEADS * SWA_HEAD), lambda bi, i: (bi, i, 0)),
        out_shape=jax.ShapeDtypeStruct((b, nq * SWA_TQ, SWA_HEADS * SWA_HEAD), BF16),
        compiler_params=_cparams(("parallel", "parallel")),
        name="swa_attention",
    )(sink, q, k, v)


N_PAIR = RWKV_HEADS // 2
N_DOUBLINGS = int(math.log2(CHUNK))


def _softplus(x):
    return jnp.maximum(x, 0.0) + jnp.log(1.0 + jnp.exp(-jnp.abs(x)))


def _headsum(x, bd):
    hi = x.astype(BF16)
    lo = (x - hi.astype(F32)).astype(BF16)
    return (jnp.dot(hi, bd, preferred_element_type=F32) + jnp.dot(lo, bd, preferred_element_type=F32))


def _chunk_cumsum(x, reverse):
    rows = lax.broadcasted_iota(jnp.int32, x.shape, 0)
    s = 1
    while s < CHUNK:
        if reverse:
            x = x + jnp.where(rows < CHUNK - s, pltpu.roll(x, CHUNK - s, axis=0), 0.0)
        else:
            x = x + jnp.where(rows >= s, pltpu.roll(x, s, axis=0), 0.0)
        s *= 2
    return x


def _head_rows(x):
    first = lax.broadcasted_iota(jnp.int32, x.shape, 1) < RWKV_HEAD
    return jnp.concatenate([jnp.where(first, x, 0.0), jnp.where(first, 0.0, x)], axis=0)


def _rwkv_chunk_kernel(r_ref, k_ref, v_ref, lo_ref, kvec_ref, w0_ref, a0_ref, wup_ref, aup_ref, gup_ref,
                       rk_ref, bd_ref, rbar_ref, y0_ref, a_ref, g_ref, bonus_ref, gate_ref):
    r = r_ref[0]
    k = k_ref[0]
    v = v_ref[0]
    lora = lo_ref[0]
    bd = bd_ref[...]
    kk = k * kvec_ref[0:1]
    kk = kk * lax.rsqrt(_headsum(kk * kk, bd) + 1e-12)
    gate_ref[0] = _mm(_sigmoid(lora), gup_ref[...])
    tanh_lo = jnp.tanh(lora)

    trow = lax.broadcasted_iota(jnp.int32, (2 * CHUNK, 4 * CHUNK), 0) % CHUNK
    tcol = lax.broadcasted_iota(jnp.int32, (2 * CHUNK, 4 * CHUNK), 1) % CHUNK
    sq_r = lax.broadcasted_iota(jnp.int32, (LANES, LANES), 0)
    sq_c = lax.broadcasted_iota(jnp.int32, (LANES, LANES), 1)
    same_head = (sq_r // RWKV_HEAD) == (sq_c // RWKV_HEAD)
    eye = sq_r == sq_c

    k_sum = None
    chains = []
    for d in range(2):
        reverse = d == 1
        w_log = -_softplus(-(w0_ref[d] + _mm(tanh_lo, wup_ref[d]))) - 0.5
        ld = -jnp.exp(w_log)
        a = _sigmoid(a0_ref[d] + _mm(lora, aup_ref[d]))
        k_d = k * (1.0 + (a - 1.0) * kvec_ref[1:2])
        k_sum = k_d if k_sum is None else k_sum + k_d
        b_d = kk * a
        lg = _chunk_cumsum(ld, reverse)
        last = 0 if reverse else CHUNK - 1
        tot = lg[last:last + 1]
        e_neg = jnp.exp(-lg)
        e_end = jnp.exp(tot - lg)
        z_t = -kk * jnp.exp(lg - ld)
        r_t = r * jnp.exp(lg)
        b_t = b_d * e_neg
        k_t = k_d * e_neg
        b_e = b_d * e_end
        k_e = k_d * e_end
        e_tot = jnp.exp(tot)
        before = (tcol > trow) if reverse else (tcol < trow)
        before_eq = (tcol >= trow) if reverse else (tcol <= trow)
        for pr in range(N_PAIR):
            sl = slice(pr * LANES, (pr + 1) * LANES)
            ch = {"d": d, "sl": sl, "rp": r_t[:, sl], "vp": v[:, sl], "e_tot": e_tot[:, sl],
                  "be_ke": jnp.concatenate([b_e[:, sl], k_e[:, sl]], axis=0)}
            zst, rst, vst = _head_rows(z_t[:, sl]), _head_rows(r_t[:, sl]), _head_rows(v[:, sl])
            bkst = jnp.concatenate([_head_rows(b_t[:, sl]), _head_rows(k_t[:, sl])], axis=0)
            ch["lz"] = jnp.where(before, _mm_nt(zst, bkst), 0.0)
            ch["lr"] = jnp.where(before_eq, _mm_nt(rst, bkst), 0.0)
            ch["zst"], ch["vst"] = zst, vst
            chains.append(ch)
    bonus_ref[0] = _headsum(r * (0.5 * k_sum) * rk_ref[...], bd) * v

    for ch in chains:
        ch["pw"] = ch["lz"][:, :LANES]
        ch["x"] = jnp.concatenate([ch["zst"], _mm(ch["lz"][:, LANES:], ch["vst"])], axis=1)
    for it in range(N_DOUBLINGS):
        for ch in chains:
            ch["x"] = ch["x"] + _mm(ch["pw"], ch["x"])
        if it + 1 < N_DOUBLINGS:
            for ch in chains:
                ch["pw"] = _mm(ch["pw"], ch["pw"])
    for ch in chains:
        low = jnp.concatenate([jnp.zeros_like(ch["vst"]), ch["vst"]], axis=1)
        op = _mm(ch["lr"], jnp.concatenate([ch["x"], low], axis=0))
        ch["op"] = op[:CHUNK] + op[CHUNK:]
        ch["xp"] = ch["x"][:CHUNK] + ch["x"][CHUNK:]
    for ch in chains:
        d, sl = ch["d"], ch["sl"]
        rbar_ref[d, 0, :, sl] = ch["rp"] + ch["op"][:, :LANES]
        y0_ref[d, 0, :, sl] = ch["op"][:, LANES:]
        rhs = jnp.concatenate([ch["xp"], jnp.concatenate([jnp.zeros_like(ch["vp"]), ch["vp"]], axis=1)], axis=0)
        ag = _mm_tn(ch["be_ke"], rhs)
        a_full = ag[:, :LANES] + jnp.where(eye, jnp.broadcast_to(ch["e_tot"], (LANES, LANES)), 0.0)
        a_ref[d, 0, :, sl] = jnp.where(same_head, a_full, 0.0)
        g_ref[d, 0, :, sl] = jnp.where(same_head, ag[:, LANES:], 0.0)


def _head_block_diag():
    idx = np.arange(RWKV_DIM) // RWKV_HEAD
    return jnp.asarray(idx[:, None] == idx[None, :], BF16)


def _rwkv_chunks(p, kvec, w0, a0, w_up, a_up, g_up, r_k):
    b, s, _ = p.shape
    nc = s // CHUNK
    lora_w = DECAY_LORA + AAA_LORA + GATE_LORA
    wup = jnp.zeros((2, lora_w, RWKV_DIM), F32).at[:, :DECAY_LORA].set(w_up).astype(BF16)
    aup = jnp.zeros((2, lora_w, RWKV_DIM), F32).at[:, DECAY_LORA:DECAY_LORA + AAA_LORA].set(a_up).astype(BF16)
    gup = jnp.zeros((lora_w, RWKV_DIM), F32).at[DECAY_LORA + AAA_LORA:].set(g_up).astype(BF16)
    full = lambda shape: pl.BlockSpec(shape, lambda bi, c: (0,) * len(shape))
    col = lambda off: pl.BlockSpec((1, CHUNK, RWKV_DIM), lambda bi, c: (bi, c, off // RWKV_DIM))
    per_tok = pl.BlockSpec((2, 1, CHUNK, RWKV_DIM), lambda bi, c: (0, bi, c, 0))
    per_chunk = pl.BlockSpec((2, 1, 2 * CHUNK, RWKV_DIM), lambda bi, c: (0, bi, c, 0))
    tok = pl.BlockSpec((1, CHUNK, RWKV_DIM), lambda bi, c: (bi, c, 0))
    f32 = lambda *shape: jax.ShapeDtypeStruct(shape, F32)
    return pl.pallas_call(
        _rwkv_chunk_kernel,
        grid=(b, nc),
        in_specs=[col(P_RKV), col(P_RKV + RWKV_DIM), col(P_RKV + 2 * RWKV_DIM),
                  pl.BlockSpec((1, CHUNK, lora_w), lambda bi, c: (bi, c, P_LORA // lora_w)),
                  full((2, RWKV_DIM)), full((2, 1, RWKV_DIM)), full((2, 1, RWKV_DIM)),
                  full(wup.shape), full(aup.shape), full(gup.shape), full((1, RWKV_DIM)),
                  full((RWKV_DIM, RWKV_DIM))],
        out_specs=[per_tok, per_tok, per_chunk, per_chunk, tok, tok],
        out_shape=[f32(2, b, s, RWKV_DIM), f32(2, b, s, RWKV_DIM), f32(2, b, 2 * s, RWKV_DIM),
                   f32(2, b, 2 * s, RWKV_DIM), f32(b, s, RWKV_DIM), f32(b, s, RWKV_DIM)],
        compiler_params=_cparams(("parallel", "parallel")),
        name="rwkv_chunks",
    )(p, p, p, p, kvec, w0.reshape(2, 1, -1), a0.reshape(2, 1, -1), wup, aup, gup, r_k.reshape(1, -1),
      _head_block_diag())


RWKV_TS = 256


def _rwkv_scan_kernel(rbar_ref, y0_ref, a_ref, g_ref, y_ref, s_ref):
    d = pl.program_id(1)

    @pl.when(pl.program_id(2) == 0)
    def _():
        s_ref[...] = jnp.zeros_like(s_ref)

    n_sub = RWKV_TS // CHUNK
    for cc in range(n_sub):
        ci = jnp.where(d == 0, cc, n_sub - 1 - cc)
        rows = pl.ds(pl.multiple_of(ci * CHUNK, CHUNK), CHUNK)
        rows2 = pl.ds(pl.multiple_of(ci * 2 * CHUNK, 2 * CHUNK), 2 * CHUNK)
        for pr in range(N_PAIR):
            sl = slice(pr * LANES, (pr + 1) * LANES)
            st = s_ref[pr]
            y_ref[0, 0, rows, sl] = _mm_f32(rbar_ref[0, 0, rows, sl], st) + y0_ref[0, 0, rows, sl]
            s_ref[pr] = _mm_f32(a_ref[0, 0, rows2, sl], st) + g_ref[0, 0, rows2, sl]


def _rwkv_scan(rbar, y0, a, g, *, n_ctx):
    _, b, s, _ = rbar.shape
    nt = s // RWKV_TS
    nct = n_ctx // RWKV_TS

    def tile(d, j):
        back = jnp.where(j < nct, nct - 1 - j, nt - 1 - (j - nct))
        return jnp.where(d == 0, j, back)

    tok = pl.BlockSpec((1, 1, RWKV_TS, RWKV_DIM), lambda bi, d, j: (d, bi, tile(d, j), 0))
    chk = pl.BlockSpec((1, 1, 2 * RWKV_TS, RWKV_DIM), lambda bi, d, j: (d, bi, tile(d, j), 0))
    return pl.pallas_call(
        _rwkv_scan_kernel,
        grid=(b, 2, nt),
        in_specs=[tok, tok, chk, chk],
        out_specs=tok,
        out_shape=jax.ShapeDtypeStruct((2, b, s, RWKV_DIM), F32),
        scratch_shapes=[pltpu.VMEM((N_PAIR, LANES, LANES), F32)],
        compiler_params=_cparams(("parallel", "parallel", "arbitrary")),
        name="rwkv_scan",
    )(rbar, y0, a, g)


def _rwkv_readout_kernel(yf_ref, yb_ref, bonus_ref, gate_ref, lng_ref, lnb_ref, bd_ref, o_ref):
    bd = bd_ref[...]
    y = yf_ref[0, 0] + yb_ref[0, 0]
    inv_n = 1.0 / RWKV_HEAD
    dev = y - _headsum(y, bd) * inv_n
    var = _headsum(dev * dev, bd) * inv_n
    yn = dev * lax.rsqrt(var + RWKV_LN_EPS) * lng_ref[...] + lnb_ref[...]
    o_ref[0] = ((yn + bonus_ref[0]) * gate_ref[0]).astype(BF16)


def _rwkv_readout(y, bonus, gate, ln_g, ln_b, *, row_off):
    _, b, s, _ = y.shape
    nt = s // TM - row_off
    full = lambda shape: pl.BlockSpec(shape, lambda bi, i: (0,) * len(shape))
    tok = pl.BlockSpec((1, TM, RWKV_DIM), lambda bi, i: (bi, i + row_off, 0))
    return pl.pallas_call(
        _rwkv_readout_kernel,
        grid=(b, nt),
        in_specs=[pl.BlockSpec((1, 1, TM, RWKV_DIM), lambda bi, i: (0, bi, i + row_off, 0)),
                  pl.BlockSpec((1, 1, TM, RWKV_DIM), lambda bi, i: (1, bi, i + row_off, 0)),
                  tok, tok, full((1, RWKV_DIM)), full((1, RWKV_DIM)), full((RWKV_DIM, RWKV_DIM))],
        out_specs=pl.BlockSpec((1, TM, RWKV_DIM), lambda bi, i: (bi, i, 0)),
        out_shape=jax.ShapeDtypeStruct((b, nt * TM, RWKV_DIM), BF16),
        compiler_params=_cparams(("parallel", "parallel")),
        name="rwkv_readout",
    )(y, y, bonus, gate, ln_g.reshape(1, -1), ln_b.reshape(1, -1), _head_block_diag())


DFT_N2 = LANES
HY_MIN_LEN = 1024
HY_CT = 128


def _hyena_mlp_kernel(feats_ref, w1_ref, b1_ref, w2_ref, b2_ref, freq_ref, h_ref):
    h = jnp.sin(freq_ref[0:1] * (_mm_f32(feats_ref[0], w1_ref[...]) + b1_ref[...]))
    h_ref[0] = jnp.sin(freq_ref[1:2] * (_mm_f32(h, w2_ref[...]) + b2_ref[...]))


def _hyena_filter_kernel(h_ref, w3f_ref, w3b_ref, t_ref, delta_ref, k_ref):
    hf = _mm_f32(h_ref[0], w3f_ref[...]) * jnp.exp(-t_ref[0] * delta_ref[...])
    hb = _mm_f32(h_ref[1], w3b_ref[...]) * jnp.exp(-t_ref[1] * delta_ref[...])
    norm = (jnp.sum(jnp.abs(hf), axis=0, keepdims=True) + jnp.sum(jnp.abs(hb), axis=0, keepdims=True))
    r = pl.program_id(2)
    rows = lax.broadcasted_iota(jnp.int32, hf.shape, 0)
    tail = jnp.where(rows == 0, 0.0, hb)
    blk = jnp.where(r == 0, hf, jnp.where(r == pl.num_programs(2) - 1, tail, 0.0))
    k_ref[0] = blk / norm


def _hyena_filter_buffer(n, nc, w1, b1, w2, b2, w3, freq):
    lag = jnp.stack([jnp.arange(n), jnp.where(jnp.arange(n) == 0, 0, n - jnp.arange(n))]).astype(F32)
    t = (lag / (n - 1))[:, :, None]
    bands = jnp.linspace(1e-4, HYENA_BANDS - 1, HYENA_BANDS, dtype=F32)
    ang = (2.0 * math.pi / n) * lag[:, :, None] * bands[None, None, :]
    feats = jnp.concatenate([t, jnp.cos(ang), -jnp.sin(ang),
                             jnp.zeros((2, n, HYENA_FW - HYENA_EMB), F32)], axis=-1)
    w1p = jnp.zeros((HYENA_FW, HYENA_FW), F32).at[:HYENA_EMB].set(w1)
    deltas = jnp.abs(jnp.linspace(math.log(HYENA_TARGET) / HYENA_SLOW,
                                  math.log(HYENA_TARGET) / HYENA_FAST, HYENA_DIM, dtype=F32))[None, :]
    fixed = lambda shape: pl.BlockSpec(shape, lambda d: (0,) * len(shape))
    hidden = pl.pallas_call(
        _hyena_mlp_kernel,
        grid=(2,),
        in_specs=[pl.BlockSpec((1, n, HYENA_FW), lambda d: (d, 0, 0)), fixed((HYENA_FW, HYENA_FW)),
                  fixed((1, HYENA_FW)), fixed((HYENA_FW, HYENA_FW)), fixed((1, HYENA_FW)),
                  fixed((2, HYENA_FW))],
        out_specs=pl.BlockSpec((1, n, HYENA_FW), lambda d: (d, 0, 0)),
        out_shape=jax.ShapeDtypeStruct((2, n, HYENA_FW), F32),
        compiler_params=_cparams(("parallel",)),
        name="hyena_filter_mlp",
    )(feats, w1p, b1.reshape(1, -1), w2, b2.reshape(1, -1), freq)
    tc = 256
    nj = HYENA_DIM // tc
    full = lambda shape: pl.BlockSpec(shape, lambda o, j, r: (0,) * len(shape))
    return pl.pallas_call(
        _hyena_filter_kernel,
        grid=(HYENA_ORDER, nj, nc // n),
        in_specs=[full((2, n, HYENA_FW)),
                  pl.BlockSpec((HYENA_FW, tc), lambda o, j, r: (0, o * 2 * nj + j)),
                  pl.BlockSpec((HYENA_FW, tc), lambda o, j, r: (0, o * 2 * nj + nj + j)),
                  full((2, n, 1)), pl.BlockSpec((1, tc), lambda o, j, r: (0, j))],
        out_specs=pl.BlockSpec((1, n, tc), lambda o, j, r: (o, r, j)),
        out_shape=jax.ShapeDtypeStruct((HYENA_ORDER, nc, HYENA_DIM), F32),
        compiler_params=_cparams(("parallel", "parallel", "parallel")),
        name="hyena_filters",
    )(hidden, w3, w3, t, deltas)


def _dft_tables(n1):
    nc = n1 * DFT_N2
    t2 = np.arange(DFT_N2)[:, None, None]
    f1 = np.arange(n1)[None, :, None]
    t1 = np.arange(n1)[None, None, :]
    theta = 2.0 * np.pi * ((f1 * (DFT_N2 * t1 + t2)) % nc) / nc
    g_fwd = np.concatenate([np.cos(theta), -np.sin(theta)], axis=1)
    g_inv = np.concatenate([np.cos(theta), -np.sin(theta)], axis=1).transpose(0, 2, 1) / nc
    k = np.arange(DFT_N2)
    phi = 2.0 * np.pi * ((k[:, None] * k[None, :]) % DFT_N2) / DFT_N2
    c, s = np.cos(phi), np.sin(phi)
    f_fwd = np.block([[c, s], [-s, c]])
    f_inv = np.block([[c, -s], [s, c]])
    return tuple(jnp.asarray(t, BF16) for t in (g_fwd, g_inv, f_fwd, f_inv))


def _dft_rows_in(x_ref, g_ref, a_ref, t1n, n1):
    for t2 in range(DFT_N2):
        xs = x_ref[0, pl.ds(t2, t1n, stride=DFT_N2), :]
        a_ref[pl.ds(t2, 2 * n1, stride=DFT_N2), :] = _mm(g_ref[t2], xs)


def _slab(f1):
    return slice(f1 * DFT_N2, (f1 + 1) * DFT_N2)


def _spectrum_kernel(x_ref, g_ref, ff_ref, k_ref, a_ref, *, n1):
    _dft_rows_in(x_ref, g_ref, a_ref, n1, n1)
    for f1 in range(n1):
        a = jnp.concatenate([a_ref[_slab(f1)], a_ref[_slab(n1 + f1)]], axis=0)
        k_ref[0, f1] = _mm(ff_ref[...], a)


def _filter_spectrum(kbuf, g_fwd, f_fwd, n1):
    no, nc, _ = kbuf.shape
    nj = HYENA_DIM // HY_CT
    return pl.pallas_call(
        functools.partial(_spectrum_kernel, n1=n1),
        grid=(no, nj),
        in_specs=[pl.BlockSpec((1, nc, HY_CT), lambda o, j: (o, 0, j)),
                  pl.BlockSpec(g_fwd.shape, lambda o, j: (0, 0, 0)),
                  pl.BlockSpec(f_fwd.shape, lambda o, j: (0, 0))],
        out_specs=pl.BlockSpec((1, n1, 2 * DFT_N2, HY_CT), lambda o, j: (o, 0, 0, j)),
        out_shape=jax.ShapeDtypeStruct((no, n1, 2 * DFT_N2, HYENA_DIM), F32),
        scratch_shapes=[pltpu.VMEM((2 * n1 * DFT_N2, HY_CT), F32)],
        compiler_params=_cparams(("parallel", "parallel")),
        name="hyena_filter_spectrum",
    )(kbuf, g_fwd, f_fwd)


def _hyena_conv_kernel(x_ref, gate_ref, k_ref, g1_ref, ff_ref, fi_ref, g3_ref, bias_ref, o_ref, a_ref,
                       *, t1n, n1):
    _dft_rows_in(x_ref, g1_ref, a_ref, t1n, n1)
    for f1 in range(n1):
        re, im = _slab(f1), _slab(n1 + f1)
        x = _mm(ff_ref[...], jnp.concatenate([a_ref[re], a_ref[im]], axis=0))
        xre, xim = x[:DFT_N2], x[DFT_N2:]
        kre, kim = k_ref[0, f1, :DFT_N2], k_ref[0, f1, DFT_N2:]
        bm = _mm(fi_ref[...], jnp.concatenate([xre * kre - xim * kim, xre * kim + xim * kre], axis=0))
        a_ref[re] = bm[:DFT_N2]
        a_ref[im] = bm[DFT_N2:]
    bias = bias_ref[...]
    for t2 in range(DFT_N2):
        y = _mm(g3_ref[t2], a_ref[pl.ds(t2, 2 * n1, stride=DFT_N2), :])
        rows = pl.ds(t2, t1n, stride=DFT_N2)
        o_ref[0, rows, :] = gate_ref[0, rows, :] * (y + bias * x_ref[0, rows, :])


def _hyena_conv(z, gate, kspec, order, bias, tabs, n1):
    bx, n_pad, _ = z.shape
    t1n = n_pad // DFT_N2
    g_fwd, g_inv, f_fwd, f_inv = tabs
    g1 = g_fwd[:, :, :t1n]
    g3 = g_inv[:, :t1n, :]
    nj = HYENA_DIM // HY_CT
    tok = pl.BlockSpec((1, n_pad, HY_CT), lambda j, bi: (bi, 0, j))
    const = lambda a: pl.BlockSpec(a.shape, lambda j, bi: (0,) * a.ndim)
    return pl.pallas_call(
        functools.partial(_hyena_conv_kernel, t1n=t1n, n1=n1),
        grid=(nj, bx),
        in_specs=[tok, tok,
                  pl.BlockSpec((1, n1, 2 * DFT_N2, HY_CT), lambda j, bi: (order, 0, 0, j)),
                  const(g1), const(f_fwd), const(f_inv), const(g3),
                  pl.BlockSpec((1, HY_CT), lambda j, bi: (0, j))],
        out_specs=tok,
        out_shape=jax.ShapeDtypeStruct((bx, n_pad, HYENA_DIM), F32),
        scratch_shapes=[pltpu.VMEM((2 * n1 * DFT_N2, HY_CT), F32)],
        compiler_params=_cparams(("parallel", "parallel")),
        name="hyena_conv",
    )(z, gate, kspec, g1, f_fwd, f_inv, g3, bias.reshape(1, -1))


def _hyena_operator(v, x1, x2, n, filt_params, bias):
    n_pad = v.shape[1]
    nc = 2 * n_pad
    n1 = nc // DFT_N2
    tabs = _dft_tables(n1)
    kspec = _filter_spectrum(_hyena_filter_buffer(n, nc, *filt_params), tabs[0], tabs[2], n1)
    z = v
    for o, gate in enumerate((x1, x2)):
        z = _hyena_conv(z, gate, kspec, o, bias[o], tabs, n1)
    return z


def _merge_kernel(x_ref, mod_ref, g_ref, gates_ref, bg_ref, ya_ref, yb_ref, yh_ref, yd_ref,
                  wb_ref, wo_ref, o_ref):
    merged = None
    for br, y_ref in enumerate((ya_ref, yb_ref, yh_ref, yd_ref)):
        gate = _sigmoid(gates_ref[0, :, br * D_MODEL:(br + 1) * D_MODEL] + bg_ref[br:br + 1])
        term = gate * jnp.dot(y_ref[0].astype(BF16), wb_ref[br], preferred_element_type=F32)
        merged = term if merged is None else merged + term
    y = jnp.dot(merged.astype(BF16), wo_ref[...], preferred_element_type=F32)
    o_ref[0] = x_ref[0] + mod_ref[0, 5:6] * _rms(y, g_ref[3:4])


def _merge(x, mods, norm_g, p, b_gate, ya, yb, yh, yd, w_branch, w_out, *, n_ctx_tiles, row_off):
    b, s, _ = x.shape
    nt = s // TM - row_off
    n_lat = mods.shape[0] - 1
    br = lambda: pl.BlockSpec((1, TM, BRANCH_DIM), lambda bi, i: (bi, i, 0))
    return pl.pallas_call(
        _merge_kernel,
        grid=(b, nt),
        in_specs=[pl.BlockSpec((1, TM, D_MODEL), lambda bi, i: (bi, i + row_off, 0)),
                  pl.BlockSpec((1, N_MOD, D_MODEL),
                               lambda bi, i: (jnp.where(i + row_off < n_ctx_tiles, n_lat, bi), 0, 0)),
                  pl.BlockSpec((6, D_MODEL), lambda bi, i: (0, 0)),
                  pl.BlockSpec((1, TM, GATE_COLS), lambda bi, i: (bi, i + row_off, 0)),
                  pl.BlockSpec((N_BRANCH, D_MODEL), lambda bi, i: (0, 0)),
                  br(), br(), br(), br(),
                  pl.BlockSpec((N_BRANCH, BRANCH_DIM, D_MODEL), lambda bi, i: (0, 0, 0)),
                  pl.BlockSpec((D_MODEL, D_MODEL), lambda bi, i: (0, 0))],
        out_specs=pl.BlockSpec((1, TM, D_MODEL), lambda bi, i: (bi, i, 0)),
        out_shape=jax.ShapeDtypeStruct((b, nt * TM, D_MODEL), F32),
        compiler_params=_cparams(("parallel", "parallel")),
        name="merge_branches",
    )(x, mods, norm_g, p, b_gate, ya, yb, yh, yd, w_branch, w_out)


def kernel(x, c, ctx, c_ctx, w_mod, b_mod, norm_g, ffn_w13, ffn_w2, w_in, b_gate, mla_norm_q, mla_norm_kv, mla_w_uq, mla_w_ukv, rwkv_mu, rwkv_w0, rwkv_w_up, rwkv_a0, rwkv_a_up, rwkv_g_up, rwkv_kvec, rwkv_r_k, rwkv_ln_g, rwkv_ln_b, hyena_conv, hyena_conv_b, hyena_w1, hyena_b1, hyena_w2, hyena_b2, hyena_w3, hyena_freq, hyena_bias, swa_sink, w_branch, w_out):
    b, n, _ = x.shape
    n_ctx = ctx.shape[1]
    nct = n_ctx // TM
    xall = jnp.concatenate([ctx, x], axis=1)
    c_all = jnp.concatenate([c, c_ctx[None]], axis=0)
    tabs_mla = _rope_tables(n, n_ctx, MLA_ROPE, MLA_NOPE, LANES)
    tabs_swa = _rope_tables(n, n_ctx, SWA_HEAD, 0, SWA_HEAD)
    depth = w_mod.shape[0]
    for l in range(depth):
        with_ctx = l + 1 < depth
        row_off = 0 if with_ctx else nct
        mods = _modulation(c_all, w_mod[l], b_mod[l])
        xall = _ffn(xall, mods, norm_g[l], ffn_w13[l, 0].astype(BF16), ffn_w2[l, 0].astype(BF16),
                    mod0=0, g0=0, n_ctx_tiles=nct, row_off=0)
        coef = _shift_coefficients(rwkv_mu[l], hyena_conv[l], hyena_conv_b[l])
        p = _inproj(xall, mods, norm_g[l], _permute_w_in(w_in[l]).astype(BF16), coef, n_ctx_tiles=nct)
        q, k, v = _mla_prep(p, mla_norm_q[l], mla_norm_kv[l], mla_w_uq[l], mla_w_ukv[l], tabs_mla)
        ya = _mla_attention(q, k, v, n_ctx=n_ctx, q_off=row_off)
        rbar, y0, a, g, bonus, gate = _rwkv_chunks(p, rwkv_kvec[l], rwkv_w0[l], rwkv_a0[l], rwkv_w_up[l],
                                                   rwkv_a_up[l], rwkv_g_up[l], rwkv_r_k[l])
        y = _rwkv_scan(rbar, y0, a, g, n_ctx=n_ctx)
        yb = _rwkv_readout(y, bonus, gate, rwkv_ln_g[l], rwkv_ln_b[l], row_off=row_off)
        filt = (hyena_w1[l], hyena_b1[l], hyena_w2[l], hyena_b2[l], hyena_w3[l], hyena_freq[l])
        def hyena_segment(rows, n_rows):
            ins = []
            for j in range(3):
                t = p[:, rows, P_HY + j * HYENA_DIM:P_HY + (j + 1) * HYENA_DIM]
                ins.append(jnp.pad(t, ((0, 0), (0, max(n_rows, HY_MIN_LEN) - n_rows), (0, 0))))
            return _hyena_operator(*ins, n_rows, filt, hyena_bias[l])[:, :n_rows]

        yh = hyena_segment(slice(n_ctx, None), n)
        if with_ctx:
            yh = jnp.concatenate([hyena_segment(slice(0, n_ctx), n_ctx), yh], axis=1)
        q, k, v = _swa_prep(p, tabs_swa)
        yd = _swa_attention(q, k, v, swa_sink[l], n_ctx=n_ctx, q_off=row_off * TM // SWA_TQ)
        xall = _merge(xall, mods, norm_g[l], p, b_gate[l], ya, yb, yh, yd, w_branch[l].astype(BF16),
                      w_out[l].astype(BF16), n_ctx_tiles=nct, row_off=row_off)
        xall = _ffn(xall, mods, norm_g[l], ffn_w13[l, 1].astype(BF16), ffn_w2[l, 1].astype(BF16),
                    mod0=6, g0=4, n_ctx_tiles=nct - row_off, row_off=0)
    return xall
```

```python
import functools
import math

import numpy as np
import jax
import jax.numpy as jnp
from jax import lax
from jax.experimental import pallas as pl
from jax.experimental.pallas import tpu as pltpu

F32 = jnp.float32
BF16 = jnp.bfloat16

D_MODEL = 1024
GRID_W = 64
N_BRANCH = 4
N_MOD = 9
FF_DIM = 2816
EPS = 1e-6
ROPE_BASE = 10000.0
NEG_INF = -1e30
BRANCH_DIM = 512
MLA_HEADS = 8
MLA_NOPE = 64
MLA_ROPE = 32
MLA_V = 64
MLA_Q_RANK = 256
MLA_KV_RANK = 128
RWKV_HEADS = 8
RWKV_HEAD = 64
RWKV_DIM = RWKV_HEADS * RWKV_HEAD
DECAY_LORA = 64
AAA_LORA = 64
GATE_LORA = 128
RWKV_LN_EPS = 64e-5
HYENA_DIM = 512
HYENA_ORDER = 2
HYENA_EMB = 33
HYENA_BANDS = (HYENA_EMB - 1) // 2
HYENA_FW = 64
HYENA_TARGET = 1e-2
HYENA_FAST = 0.3
HYENA_SLOW = 1.5
SWA_HEADS = 8
SWA_KV_HEADS = 2
SWA_HEAD = 64
SWA_GROUP = SWA_HEADS // SWA_KV_HEADS
WINDOW = 128
GATE_COLS = N_BRANCH * D_MODEL
MLA_COLS = MLA_Q_RANK + MLA_KV_RANK + MLA_ROPE
RWKV_COLS = 3 * RWKV_DIM + DECAY_LORA + AAA_LORA + GATE_LORA
HYENA_COLS = 3 * HYENA_DIM
SWA_COLS = (SWA_HEADS + 2 * SWA_KV_HEADS) * SWA_HEAD

LANES = 128
V7X_VMEM_LIMIT = 56 * 1024 * 1024

TM = 256
FFN_SUB_TILES = 4
INPROJ_SUB_TILES = 2
FF_CHUNK = 256
IN_CHUNK = 512
CHUNK = 64

P_GATE = 0
P_RKV = 4096
P_SWAQ = 5632
P_LORA = 6144
P_CQ = 6400
P_CKV = 6656
P_KR = 6784
P_SWAK = 6912
P_SWAV = 7040
P_COLS = 7168
P_HY = 7168
W_COLS = P_HY + HYENA_COLS
_SHIFT_COLS = ((P_RKV, P_SWAQ), (P_LORA, P_CQ), (P_HY, W_COLS))
_SHIFT_CHUNKS = [any(lo < (j + 1) * IN_CHUNK and j * IN_CHUNK < hi for lo, hi in _SHIFT_COLS)
                 for j in range(W_COLS // IN_CHUNK)]


def _cparams(sem, vmem=V7X_VMEM_LIMIT):
    return pltpu.CompilerParams(dimension_semantics=sem, vmem_limit_bytes=vmem)


def _mm(a, b):
    return jnp.dot(a.astype(BF16), b.astype(BF16), preferred_element_type=F32)


def _mm_nt(a, b):
    return lax.dot_general(a.astype(BF16), b.astype(BF16), (((1,), (1,)), ((), ())),
                           preferred_element_type=F32)


def _mm_tn(a, b):
    return lax.dot_general(a.astype(BF16), b.astype(BF16), (((0,), (0,)), ((), ())),
                           preferred_element_type=F32)


def _mm_f32(a, b):
    return jnp.dot(a, b, preferred_element_type=F32, precision=lax.Precision.HIGHEST)


def _rms(x, g):
    return x * lax.rsqrt(jnp.mean(x * x, axis=-1, keepdims=True) + EPS) * g


def _sigmoid(x):
    return 1.0 / (1.0 + jnp.exp(-x))


def _mod_kernel(c_ref, w_ref, b_ref, o_ref):
    c = c_ref[...]
    o_ref[...] = _mm(c * _sigmoid(c), w_ref[...]) + b_ref[...]


def _modulation(c_all, w_mod, b_mod):
    r = c_all.shape[0]
    rp = -(-r // 8) * 8
    c_pad = jnp.zeros((rp, D_MODEL), F32).at[:r].set(c_all)
    tn = 1024
    out = pl.pallas_call(
        _mod_kernel,
        grid=(N_MOD * D_MODEL // tn,),
        in_specs=[pl.BlockSpec((rp, D_MODEL), lambda j: (0, 0)),
                  pl.BlockSpec((D_MODEL, tn), lambda j: (0, j)),
                  pl.BlockSpec((1, tn), lambda j: (0, j))],
        out_specs=pl.BlockSpec((rp, tn), lambda j: (0, j)),
        out_shape=jax.ShapeDtypeStruct((rp, N_MOD * D_MODEL), F32),
        compiler_params=_cparams(("arbitrary",)),
        name="modulation",
    )(c_pad, w_mod, b_mod.reshape(1, -1))
    return out[:r].reshape(r, N_MOD, D_MODEL)


def _sub_tiles(n_tiles, most):
    return max(g for g in range(1, most + 1) if n_tiles % g == 0)


def _mod_specs(n_sub, tiles_per_seq, n_ctx_tiles, ctx_row):
    def spec(k):
        def index(i):
            t = i * n_sub + k
            return (jnp.where(t % tiles_per_seq < n_ctx_tiles, ctx_row, t // tiles_per_seq), 0, 0)
        return pl.BlockSpec((1, N_MOD, D_MODEL), index)
    return [spec(k) for k in range(n_sub)]


def _ffn_kernel(x_ref, *refs, mod0, g0, n_sub):
    mod_refs = refs[:n_sub]
    g_ref, w13_ref, w2_ref, o_ref = refs[n_sub:]
    tiles = [slice(t * TM, (t + 1) * TM) for t in range(n_sub)]
    u = jnp.concatenate(
        [(_rms(x_ref[rows], g_ref[g0:g0 + 1]) * (1.0 + m[0, mod0 + 1:mod0 + 2]) + m[0, mod0:mod0 + 1]).astype(BF16)
         for rows, m in zip(tiles, mod_refs)], axis=0)
    acc = jnp.zeros(x_ref.shape, F32)
    for f in range(FF_DIM // FF_CHUNK):
        lo = f * FF_CHUNK
        a = jnp.dot(u, w13_ref[:, lo:lo + FF_CHUNK], preferred_element_type=F32)
        b = jnp.dot(u, w13_ref[:, FF_DIM + lo:FF_DIM + lo + FF_CHUNK], preferred_element_type=F32)
        h = (a * _sigmoid(a) * b).astype(BF16)
        acc = acc + jnp.dot(h, w2_ref[lo:lo + FF_CHUNK, :], preferred_element_type=F32)
    hn = _rms(acc, g_ref[g0 + 1:g0 + 2])
    for rows, m in zip(tiles, mod_refs):
        o_ref[rows] = x_ref[rows] + 0.5 * m[0, mod0 + 2:mod0 + 3] * hn[rows]


def _ffn(x, mods, norm_g, w13, w2, *, mod0, g0, n_ctx_tiles):
    b, s, _ = x.shape
    n_tiles = b * s // TM
    n_sub = _sub_tiles(n_tiles, FFN_SUB_TILES)
    rows = n_sub * TM
    out = pl.pallas_call(
        functools.partial(_ffn_kernel, mod0=mod0, g0=g0, n_sub=n_sub),
        grid=(n_tiles // n_sub,),
        in_specs=[pl.BlockSpec((rows, D_MODEL), lambda i: (i, 0)),
                  *_mod_specs(n_sub, s // TM, n_ctx_tiles, mods.shape[0] - 1),
                  pl.BlockSpec((6, D_MODEL), lambda i: (0, 0)),
                  pl.BlockSpec(memory_space=pltpu.VMEM),
                  pl.BlockSpec(memory_space=pltpu.VMEM)],
        out_specs=pl.BlockSpec((rows, D_MODEL), lambda i: (i, 0)),
        out_shape=jax.ShapeDtypeStruct((b * s, D_MODEL), F32),
        compiler_params=_cparams(("parallel",)),
        name="ffn_half_step",
    )(x.reshape(b * s, D_MODEL), *([mods] * n_sub), norm_g, w13, w2)
    return out.reshape(b, s, D_MODEL)


def _inproj_kernel(x_ref, xp_ref, xn_ref, *refs, n_sub, tiles_per_seq, n_ctx_tiles):
    mod_refs = refs[:n_sub]
    g_ref, w_ref, coef_ref, o_ref, hy_ref = refs[n_sub:]
    g = g_ref[2:3]

    def modulated(x, m):
        return (_rms(x, g) * (1.0 + m[0, 4:5]) + m[0, 3:4]).astype(BF16)

    u = jnp.concatenate([modulated(x_ref[t * TM:(t + 1) * TM], m) for t, m in enumerate(mod_refs)], axis=0)
    u_prev = modulated(xp_ref[...], mod_refs[0])
    u_next = modulated(xn_ref[...], mod_refs[-1])
    rows = lax.broadcasted_iota(jnp.int32, (n_sub * TM, 1), 0)
    keep_prev = jnp.ones((n_sub * TM, 1), F32)
    keep_next = jnp.ones((n_sub * TM, 1), F32)
    for t in range(n_sub):
        w = (pl.program_id(0) * n_sub + t) % tiles_per_seq
        seg_start = (w == 0) | (w == n_ctx_tiles)
        seg_end = (w == n_ctx_tiles - 1) | (w == tiles_per_seq - 1)
        keep_prev = jnp.where((rows == t * TM) & seg_start, 0.0, keep_prev)
        keep_next = jnp.where((rows == (t + 1) * TM - 1) & seg_end, 0.0, keep_next)
    for j in range(W_COLS // IN_CHUNK):
        cols = slice(j * IN_CHUNK, (j + 1) * IN_CHUNK)
        p = jnp.dot(u, w_ref[:, cols], preferred_element_type=F32)
        if _SHIFT_CHUNKS[j]:
            p_first = jnp.dot(u_prev, w_ref[:, cols], preferred_element_type=F32)[7:8]
            p_last = jnp.dot(u_next, w_ref[:, cols], preferred_element_type=F32)[0:1]
            prev = jnp.where(rows == 0, p_first, pltpu.roll(p, 1, axis=0)) * keep_prev
            nxt = jnp.where(rows == n_sub * TM - 1, p_last, pltpu.roll(p, n_sub * TM - 1, axis=0)) * keep_next
            p = (coef_ref[0:1, cols] * p + coef_ref[1:2, cols] * prev + coef_ref[2:3, cols] * nxt
                 + coef_ref[3:4, cols])
        if j * IN_CHUNK < P_COLS:
            o_ref[:, cols] = p.astype(BF16)
        else:
            hy_ref[:, j * IN_CHUNK - P_HY:(j + 1) * IN_CHUNK - P_HY] = p


def _inproj(x, mods, norm_g, w_in_p, coef, *, n_ctx_tiles):
    b, s, _ = x.shape
    n_tiles = b * s // TM
    n_sub = _sub_tiles(n_tiles, INPROJ_SUB_TILES)
    rows = n_sub * TM
    r8 = rows // 8
    p, hy = pl.pallas_call(
        functools.partial(_inproj_kernel, n_sub=n_sub, tiles_per_seq=s // TM, n_ctx_tiles=n_ctx_tiles),
        grid=(n_tiles // n_sub,),
        in_specs=[pl.BlockSpec((rows, D_MODEL), lambda i: (i, 0)),
                  pl.BlockSpec((8, D_MODEL), lambda i: (jnp.maximum(i * r8 - 1, 0), 0)),
                  pl.BlockSpec((8, D_MODEL), lambda i: (jnp.minimum((i + 1) * r8, b * s // 8 - 1), 0)),
                  *_mod_specs(n_sub, s // TM, n_ctx_tiles, mods.shape[0] - 1),
                  pl.BlockSpec((6, D_MODEL), lambda i: (0, 0)),
                  pl.BlockSpec(memory_space=pltpu.VMEM),
                  pl.BlockSpec((4, W_COLS), lambda i: (0, 0))],
        out_specs=[pl.BlockSpec((rows, P_COLS), lambda i: (i, 0)),
                   pl.BlockSpec((rows, HYENA_COLS), lambda i: (i, 0))],
        out_shape=[jax.ShapeDtypeStruct((b * s, P_COLS), BF16),
                   jax.ShapeDtypeStruct((b * s, HYENA_COLS), F32)],
        compiler_params=_cparams(("parallel",)),
        name="in_projection",
    )(*([x.reshape(b * s, D_MODEL)] * 3), *([mods] * n_sub), norm_g, w_in_p, coef)
    return p.reshape(b, s, P_COLS), hy.reshape(b, s, HYENA_COLS)


def _shift_coefficients(rwkv_mu, hyena_conv, hyena_conv_b):
    mu = rwkv_mu.astype(F32)
    coef = jnp.zeros((4, W_COLS), F32).at[0].set(1.0)
    for off, sl in ((P_RKV, slice(0, 3 * RWKV_DIM)), (P_LORA, slice(3 * RWKV_DIM, RWKV_COLS))):
        width = sl.stop - sl.start
        coef = coef.at[0, off:off + width].set(1.0 - mu[0, sl] - mu[1, sl])
        coef = coef.at[1, off:off + width].set(mu[0, sl])
        coef = coef.at[2, off:off + width].set(mu[1, sl])
    hy = slice(P_HY, P_HY + HYENA_COLS)
    coef = coef.at[0, hy].set(hyena_conv[1]).at[1, hy].set(hyena_conv[0]).at[2, hy].set(hyena_conv[2])
    return coef.at[3, hy].set(hyena_conv_b)


def _permute_w_in(w_in):
    o_mla = GATE_COLS
    o_rwkv = o_mla + MLA_COLS
    o_hy = o_rwkv + RWKV_COLS
    o_swa = o_hy + HYENA_COLS
    z = lambda n: jnp.zeros((D_MODEL, n), w_in.dtype)
    parts = [
        w_in[:, :GATE_COLS],
        w_in[:, o_rwkv:o_rwkv + 3 * RWKV_DIM],
        w_in[:, o_swa:o_swa + SWA_HEADS * SWA_HEAD],
        w_in[:, o_rwkv + 3 * RWKV_DIM:o_rwkv + RWKV_COLS],
        w_in[:, o_mla:o_mla + MLA_Q_RANK],
        w_in[:, o_mla + MLA_Q_RANK:o_mla + MLA_Q_RANK + MLA_KV_RANK],
        z(MLA_NOPE), w_in[:, o_mla + MLA_Q_RANK + MLA_KV_RANK:o_mla + MLA_COLS],
        z(LANES - MLA_NOPE - MLA_ROPE),
        w_in[:, o_swa + SWA_HEADS * SWA_HEAD:o_swa + SWA_COLS],
        w_in[:, o_hy:o_hy + HYENA_COLS],
    ]
    out = jnp.concatenate(parts, axis=1)
    assert out.shape[1] == W_COLS
    return out


def _rope_tables(n_lat, n_ctx, rot_dim, lane0, period):
    rows = n_lat // GRID_W
    row = jnp.repeat(jnp.arange(rows, dtype=F32), GRID_W)
    col = jnp.tile(jnp.arange(GRID_W, dtype=F32), rows)
    axis_dim = rot_dim // 2
    h = axis_dim // 2
    inv_freq = ROPE_BASE ** (-jnp.arange(0, axis_dim, 2, dtype=F32) / axis_dim)
    ang_r = row[:, None] * inv_freq
    ang_c = col[:, None] * inv_freq
    cos_rot = jnp.concatenate([jnp.cos(ang_r)] * 2 + [jnp.cos(ang_c)] * 2, axis=1)
    zeros = jnp.zeros_like(ang_r)
    sin_a = jnp.concatenate([-jnp.sin(ang_r), zeros, -jnp.sin(ang_c), zeros], axis=1)
    sin_b = jnp.concatenate([zeros, jnp.sin(ang_r), zeros, jnp.sin(ang_c)], axis=1)

    def widen(t, fill):
        g = jnp.full((n_lat, period), fill, F32).at[:, lane0:lane0 + rot_dim].set(t)
        g = jnp.tile(g, (1, LANES // period))
        ctx = jnp.full((n_ctx, LANES), fill, F32)
        return jnp.concatenate([ctx, g], axis=0)

    return widen(cos_rot, 1.0), widen(sin_a, 0.0), widen(sin_b, 0.0), h


def _rope128(x, cos, sin_a, sin_b, h):
    return x * cos + pltpu.roll(x, LANES - h, axis=1) * sin_a + pltpu.roll(x, h, axis=1) * sin_b


LOG2E = math.log2(math.e)
MLA_SCALE = (MLA_NOPE + MLA_ROPE) ** -0.5 * LOG2E
MLA_ONE_LANE = (MLA_V, 0)


def _mla_prep_kernel(cq_ref, ckv_ref, kr_ref, gq_ref, gkv_ref, wq_ref, wk_ref, wv_ref, vone_ref,
                     cos_ref, sa_ref, sb_ref, q_ref, k_ref, v_ref, *, h):
    cos, sa, sb = cos_ref[...], sa_ref[...], sb_ref[...]
    cq = _rms(cq_ref[0].astype(F32), gq_ref[...]).astype(BF16)
    ckv = _rms(ckv_ref[0].astype(F32), gkv_ref[...]).astype(BF16)
    q = jnp.dot(cq, wq_ref[...], preferred_element_type=F32)
    k = jnp.dot(ckv, wk_ref[...], preferred_element_type=F32)
    kr = _rope128(kr_ref[0].astype(F32), cos, sa, sb, h)
    for hd in range(MLA_HEADS):
        sl = slice(hd * LANES, (hd + 1) * LANES)
        q_ref[0, :, sl] = (_rope128(q[:, sl], cos, sa, sb, h) * MLA_SCALE).astype(BF16)
        k_ref[0, :, sl] = (k[:, sl] + kr).astype(BF16)
    v_ref[0] = (jnp.dot(ckv, wv_ref[...], preferred_element_type=F32) + vone_ref[...]).astype(BF16)


def _mla_prep(p, norm_q, norm_kv, w_uq, w_ukv, tabs):
    b, s, _ = p.shape
    cos, sa, sb, h = tabs
    hq = MLA_NOPE + MLA_ROPE
    wq = jnp.zeros((MLA_Q_RANK, MLA_HEADS, LANES), F32).at[:, :, :hq].set(
        w_uq.reshape(MLA_Q_RANK, MLA_HEADS, hq)).reshape(MLA_Q_RANK, MLA_HEADS * LANES).astype(BF16)
    wkv = w_ukv.reshape(MLA_KV_RANK, MLA_HEADS, MLA_NOPE + MLA_V)
    wk = jnp.zeros((MLA_KV_RANK, MLA_HEADS, LANES), F32).at[:, :, :MLA_NOPE].set(
        wkv[:, :, :MLA_NOPE]).reshape(MLA_KV_RANK, MLA_HEADS * LANES).astype(BF16)
    wv = jnp.zeros((MLA_KV_RANK, MLA_HEADS, LANES), F32)
    wv = wv.at[:, 0::2, :MLA_V].set(wkv[:, 0::2, MLA_NOPE:]).at[:, 1::2, LANES - MLA_V:].set(wkv[:, 1::2, MLA_NOPE:])
    wv = wv.reshape(MLA_KV_RANK, MLA_HEADS * LANES).astype(BF16)
    vone = jnp.zeros((MLA_HEADS, LANES), F32)
    vone = vone.at[0::2, MLA_ONE_LANE[0]].set(1.0).at[1::2, MLA_ONE_LANE[1]].set(1.0).reshape(1, -1)
    full = lambda shape: pl.BlockSpec(shape, lambda bi, i: (0,) * len(shape))
    tab = pl.BlockSpec((TM, LANES), lambda bi, i: (i, 0))
    return pl.pallas_call(
        functools.partial(_mla_prep_kernel, h=h),
        grid=(b, s // TM),
        in_specs=[pl.BlockSpec((1, TM, MLA_Q_RANK), lambda bi, i: (bi, i, P_CQ // MLA_Q_RANK)),
                  pl.BlockSpec((1, TM, LANES), lambda bi, i: (bi, i, P_CKV // LANES)),
                  pl.BlockSpec((1, TM, LANES), lambda bi, i: (bi, i, P_KR // LANES)),
                  full((1, MLA_Q_RANK)), full((1, MLA_KV_RANK)),
                  full(wq.shape), full(wk.shape), full(wv.shape), full(vone.shape), tab, tab, tab],
        out_specs=[pl.BlockSpec((1, TM, MLA_HEADS * LANES), lambda bi, i: (bi, i, 0))] * 3,
        out_shape=[jax.ShapeDtypeStruct((b, s, MLA_HEADS * LANES), BF16)] * 3,
        compiler_params=_cparams(("parallel", "parallel")),
        name="mla_prep",
    )(p, p, p, norm_q.reshape(1, -1), norm_kv.reshape(1, -1), wq, wk, wv, vone, cos, sa, sb)


MLA_Q_TILES = 2


def _mla_attn_kernel(*refs, n_ctx, n_lat_steps):
    q_refs = refs[:MLA_Q_TILES]
    k_ref, v_ref, o_ref = refs[MLA_Q_TILES:]

    def attend(n_keys):
        outs = []
        for hd in range(2):
            sl = slice(hd * LANES, (hd + 1) * LANES)
            q = jnp.concatenate([q_ref[0, :, sl] for q_ref in q_refs], axis=0)
            s = lax.dot_general(q, k_ref[0, :n_keys, sl], (((1,), (1,)), ((), ())),
                                preferred_element_type=F32)
            e = jnp.exp2(s - jnp.max(s, axis=-1, keepdims=True)).astype(BF16)
            o = jnp.dot(e, v_ref[0, :n_keys, sl], preferred_element_type=F32)
            one = MLA_ONE_LANE[hd]
            outs.append(o / o[:, one:one + 1])
        lane = lax.broadcasted_iota(jnp.int32, outs[0].shape, 1)
        o_ref[0] = jnp.where(lane < MLA_V, outs[0], outs[1]).astype(BF16)

    @pl.when(pl.program_id(2) < n_lat_steps)
    def _():
        attend(k_ref.shape[1])

    @pl.when(pl.program_id(2) >= n_lat_steps)
    def _():
        attend(n_ctx)


def _mla_attention(q, k, v, *, n_ctx, with_ctx):
    b, s, _ = q.shape
    nct = n_ctx // TM
    n_lat = s - n_ctx
    n_lat_steps = n_lat // (MLA_Q_TILES * TM)
    assert n_lat % (MLA_Q_TILES * TM) == 0 and (nct == 1 or not with_ctx)

    def q_spec(t):
        return pl.BlockSpec((1, TM, 2 * LANES), lambda bi, hp, j: (
            bi, jnp.where(j < n_lat_steps, nct + j * MLA_Q_TILES + t, 0), hp))

    kv = pl.BlockSpec((1, s, 2 * LANES), lambda bi, hp, j: (bi, 0, hp))
    return pl.pallas_call(
        functools.partial(_mla_attn_kernel, n_ctx=n_ctx, n_lat_steps=n_lat_steps),
        grid=(b, MLA_HEADS // 2, n_lat_steps + (1 if with_ctx else 0)),
        in_specs=[*[q_spec(t) for t in range(MLA_Q_TILES)], kv, kv],
        out_specs=pl.BlockSpec((1, MLA_Q_TILES * TM, LANES), lambda bi, hp, j: (bi, j, hp)),
        out_shape=jax.ShapeDtypeStruct((b, n_lat + (n_ctx if with_ctx else 0), MLA_HEADS * MLA_V), BF16),
        compiler_params=_cparams(("parallel", "parallel", "parallel")),
        name="mla_attention",
    )(*([q] * MLA_Q_TILES), k, v)


SWA_SCALE = SWA_HEAD ** -0.5 * LOG2E
SWA_TQ = 128
SWA_ONE_LANE = (SWA_HEAD, 0)


def _swa_prep_kernel(q_ref, k_ref, v_ref, cos_ref, sa_ref, sb_ref, qo_ref, ko_ref, vo_ref, *, h):
    cos, sa, sb = cos_ref[...], sa_ref[...], sb_ref[...]
    lane = lax.broadcasted_iota(jnp.int32, cos.shape, 1)
    low = lane < SWA_HEAD
    for j in range(SWA_HEADS // 2):
        blk = _rope128(q_ref[0, :, j * LANES:(j + 1) * LANES].astype(F32), cos, sa, sb, h) * SWA_SCALE
        qo_ref[0, :, (2 * j) * LANES:(2 * j + 1) * LANES] = jnp.where(low, blk, 0.0).astype(BF16)
        qo_ref[0, :, (2 * j + 1) * LANES:(2 * j + 2) * LANES] = jnp.where(
            low, pltpu.roll(blk, SWA_HEAD, axis=1), 0.0).astype(BF16)
    kb = _rope128(k_ref[0].astype(F32), cos, sa, sb, h)
    ko_ref[0, :, :LANES] = jnp.where(low, kb, 0.0).astype(BF16)
    ko_ref[0, :, LANES:] = jnp.where(low, pltpu.roll(kb, SWA_HEAD, axis=1), 0.0).astype(BF16)
    vb = v_ref[0].astype(F32)
    vr = pltpu.roll(vb, SWA_HEAD, axis=1)
    one_lo = jnp.where(lane == SWA_ONE_LANE[0], 1.0, 0.0)
    one_hi = jnp.where(lane == SWA_ONE_LANE[1], 1.0, 0.0)
    vo_ref[0, :, 0 * LANES:1 * LANES] = jnp.where(low, vb, one_lo).astype(BF16)
    vo_ref[0, :, 1 * LANES:2 * LANES] = jnp.where(low, one_hi, vr).astype(BF16)
    vo_ref[0, :, 2 * LANES:3 * LANES] = jnp.where(low, vr, one_lo).astype(BF16)
    vo_ref[0, :, 3 * LANES:4 * LANES] = jnp.where(low, one_hi, vb).astype(BF16)


def _swa_prep(p, tabs):
    b, s, _ = p.shape
    cos, sa, sb, h = tabs
    tab = pl.BlockSpec((TM, LANES), lambda bi, i: (i, 0))
    nq = SWA_HEADS * SWA_HEAD
    return pl.pallas_call(
        functools.partial(_swa_prep_kernel, h=h),
        grid=(b, s // TM),
        in_specs=[pl.BlockSpec((1, TM, nq), lambda bi, i: (bi, i, P_SWAQ // nq)),
                  pl.BlockSpec((1, TM, LANES), lambda bi, i: (bi, i, P_SWAK // LANES)),
                  pl.BlockSpec((1, TM, LANES), lambda bi, i: (bi, i, P_SWAV // LANES)),
                  tab, tab, tab],
        out_specs=[pl.BlockSpec((1, TM, SWA_HEADS * LANES), lambda bi, i: (bi, i, 0)),
                   pl.BlockSpec((1, TM, SWA_KV_HEADS * LANES), lambda bi, i: (bi, i, 0)),
                   pl.BlockSpec((1, TM, 4 * LANES), lambda bi, i: (bi, i, 0))],
        out_shape=[jax.ShapeDtypeStruct((b, s, SWA_HEADS * LANES), BF16),
                   jax.ShapeDtypeStruct((b, s, SWA_KV_HEADS * LANES), BF16),
                   jax.ShapeDtypeStruct((b, s, 4 * LANES), BF16)],
        compiler_params=_cparams(("parallel", "parallel")),
        name="swa_prep",
    )(p, p, p, cos, sa, sb)


def _swa_attn_kernel(sink_ref, q_ref, k_ref, v_ref, o_ref, *, n_ctx, q_off):
    i = pl.program_id(1) + q_off
    s_len = k_ref.shape[1]
    tq = SWA_TQ
    n_loc = tq + 2 * WINDOW
    r0 = i * tq
    is_lat = r0 >= n_ctx
    start = pl.multiple_of(jnp.clip(r0 - WINDOW, 0, s_len - n_loc), LANES)
    rows_g = SWA_GROUP * tq
    row = lax.broadcasted_iota(jnp.int32, (rows_g, n_loc), 0)
    qpos = r0 - n_ctx + row % tq
    kpos = start - n_ctx + lax.broadcasted_iota(jnp.int32, (rows_g, n_loc), 1)
    loc_ok = (jnp.abs(kpos - qpos) <= WINDOW) & (kpos >= 0) & is_lat
    k_loc = k_ref[0, pl.ds(start, n_loc), :]
    v_loc = v_ref[0, pl.ds(start, n_loc), :]
    k_ctx = k_ref[0, 0:n_ctx, :]
    v_ctx = v_ref[0, 0:n_ctx, :]
    head_row = lax.broadcasted_iota(jnp.int32, (rows_g, 1), 0) // tq
    lane = lax.broadcasted_iota(jnp.int32, (tq, LANES), 1)
    stages = []
    for g in range(SWA_KV_HEADS):
        q = jnp.concatenate([q_ref[0, :, hd * LANES:(hd + 1) * LANES]
                             for hd in range(g * SWA_GROUP, (g + 1) * SWA_GROUP)], axis=0)
        kg = slice(g * LANES, (g + 1) * LANES)
        s_loc = lax.dot_general(q, k_loc[:, kg], (((1,), (1,)), ((), ())), preferred_element_type=F32)
        s_ctx = lax.dot_general(q, k_ctx[:, kg], (((1,), (1,)), ((), ())), preferred_element_type=F32)
        sink = jnp.zeros((rows_g, 1), F32)
        for hh in range(SWA_GROUP):
            sink = jnp.where(head_row == hh, sink_ref[g * SWA_GROUP + hh] * LOG2E, sink)
        stages.append((jnp.where(loc_ok, s_loc, NEG_INF), s_ctx, sink))
    for g, (s_loc, s_ctx, sink) in enumerate(stages):
        m = jnp.maximum(jnp.maximum(jnp.max(s_loc, axis=-1, keepdims=True),
                                    jnp.max(s_ctx, axis=-1, keepdims=True)), sink)
        e = jnp.concatenate([jnp.exp2(s_loc - m), jnp.exp2(s_ctx - m)], axis=1).astype(BF16)
        e_sink = jnp.exp2(sink - m)
        outs = []
        for par in range(2):
            vg = slice((2 * g + par) * LANES, (2 * g + par + 1) * LANES)
            o = jnp.dot(e, jnp.concatenate([v_loc[:, vg], v_ctx[:, vg]], axis=0), preferred_element_type=F32)
            one = SWA_ONE_LANE[par]
            outs.append(o / (o[:, one:one + 1] + e_sink))
        for pi in range(SWA_GROUP // 2):
            even = outs[0][(2 * pi) * tq:(2 * pi + 1) * tq]
            odd = outs[1][(2 * pi + 1) * tq:(2 * pi + 2) * tq]
            blk = g * (SWA_GROUP // 2) + pi
            o_ref[0, :, blk * LANES:(blk + 1) * LANES] = jnp.where(lane < SWA_HEAD, even, odd).astype(BF16)


def _swa_attention(q, k, v, sink, *, n_ctx, q_off):
    b, s, _ = q.shape
    nq = s // SWA_TQ - q_off
    return pl.pallas_call(
        functools.partial(_swa_attn_kernel, n_ctx=n_ctx, q_off=q_off),
        grid=(b, nq),
        in_specs=[pl.BlockSpec(memory_space=pltpu.SMEM),
                  pl.BlockSpec((1, SWA_TQ, SWA_HEADS * LANES), lambda bi, i: (bi, i + q_off, 0)),
                  pl.BlockSpec((1, s, SWA_KV_HEADS * LANES), lambda bi, i: (bi, 0, 0)),
                  pl.BlockSpec((1, s, 4 * LANES), lambda bi, i: (bi, 0, 0))],
        out_specs=pl.BlockSpec((1, SWA_TQ, SWA_HEADS * SWA_HEAD), lambda bi, i: (bi, i, 0)),
        out_shape=jax.ShapeDtypeStruct((b, nq * SWA_TQ, SWA_HEADS * SWA_HEAD), BF16),
        compiler_params=_cparams(("parallel", "parallel")),
        name="swa_attention",
    )(sink, q, k, v)


N_PAIR = RWKV_HEADS // 2
N_DOUBLINGS = int(math.log2(CHUNK))


def _softplus(x):
    return jnp.maximum(x, 0.0) + jnp.log(1.0 + jnp.exp(-jnp.abs(x)))


def _headsum(x, bd):
    hi = x.astype(BF16)
    lo = (x - hi.astype(F32)).astype(BF16)
    return (jnp.dot(hi, bd, preferred_element_type=F32) + jnp.dot(lo, bd, preferred_element_type=F32))


def _chunk_cumsum(x, reverse):
    rows = lax.broadcasted_iota(jnp.int32, x.shape, 0)
    s = 1
    while s < CHUNK:
        if reverse:
            x = x + jnp.where(rows < CHUNK - s, pltpu.roll(x, CHUNK - s, axis=0), 0.0)
        else:
            x = x + jnp.where(rows >= s, pltpu.roll(x, s, axis=0), 0.0)
        s *= 2
    return x


def _head_rows(x):
    first = lax.broadcasted_iota(jnp.int32, x.shape, 1) < RWKV_HEAD
    return jnp.concatenate([jnp.where(first, x, 0.0), jnp.where(first, 0.0, x)], axis=0)


def _rwkv_chunk_kernel(r_ref, k_ref, v_ref, lo_ref, kvec_ref, w0_ref, a0_ref, wup_ref, aup_ref, gup_ref,
                       rk_ref, bd_ref, rbar_ref, y0_ref, a_ref, g_ref, bonus_ref, gate_ref):
    r = r_ref[0].astype(F32)
    k = k_ref[0].astype(F32)
    v = v_ref[0].astype(F32)
    lora = lo_ref[0].astype(F32)
    bd = bd_ref[...]
    kk = k * kvec_ref[0:1]
    kk = kk * lax.rsqrt(_headsum(kk * kk, bd) + 1e-12)
    gate_ref[0] = _mm(_sigmoid(lora), gup_ref[...])
    tanh_lo = jnp.tanh(lora)

    trow = lax.broadcasted_iota(jnp.int32, (2 * CHUNK, 4 * CHUNK), 0) % CHUNK
    tcol = lax.broadcasted_iota(jnp.int32, (2 * CHUNK, 4 * CHUNK), 1) % CHUNK
    sq_r = lax.broadcasted_iota(jnp.int32, (LANES, LANES), 0)
    sq_c = lax.broadcasted_iota(jnp.int32, (LANES, LANES), 1)
    same_head = (sq_r // RWKV_HEAD) == (sq_c // RWKV_HEAD)
    eye = sq_r == sq_c

    k_sum = None
    chains = []
    for d in range(2):
        reverse = d == 1
        w_log = -_softplus(-(w0_ref[d] + _mm(tanh_lo, wup_ref[d]))) - 0.5
        ld = -jnp.exp(w_log)
        a = _sigmoid(a0_ref[d] + _mm(lora, aup_ref[d]))
        k_d = k * (1.0 + (a - 1.0) * kvec_ref[1:2])
        k_sum = k_d if k_sum is None else k_sum + k_d
        b_d = kk * a
        lg = _chunk_cumsum(ld, reverse)
        last = 0 if reverse else CHUNK - 1
        tot = lg[last:last + 1]
        e_neg = jnp.exp(-lg)
        e_end = jnp.exp(tot - lg)
        z_t = -kk * jnp.exp(lg - ld)
        r_t = r * jnp.exp(lg)
        b_t = b_d * e_neg
        k_t = k_d * e_neg
        b_e = b_d * e_end
        k_e = k_d * e_end
        e_tot = jnp.exp(tot)
        before = (tcol > trow) if reverse else (tcol < trow)
        before_eq = (tcol >= trow) if reverse else (tcol <= trow)
        for pr in range(N_PAIR):
            sl = slice(pr * LANES, (pr + 1) * LANES)
            ch = {"d": d, "sl": sl, "rp": r_t[:, sl], "vp": v[:, sl], "e_tot": e_tot[:, sl],
                  "be_ke": jnp.concatenate([b_e[:, sl], k_e[:, sl]], axis=0)}
            zst, rst, vst = _head_rows(z_t[:, sl]), _head_rows(r_t[:, sl]), _head_rows(v[:, sl])
            bkst = jnp.concatenate([_head_rows(b_t[:, sl]), _head_rows(k_t[:, sl])], axis=0)
            ch["lz"] = jnp.where(before, _mm_nt(zst, bkst), 0.0)
            ch["lr"] = jnp.where(before_eq, _mm_nt(rst, bkst), 0.0)
            ch["zst"], ch["vst"] = zst, vst
            chains.append(ch)
    bonus_ref[0] = _headsum(r * (0.5 * k_sum) * rk_ref[...], bd) * v

    for ch in chains:
        ch["pw"] = ch["lz"][:, :LANES]
        ch["x"] = jnp.concatenate([ch["zst"], _mm(ch["lz"][:, LANES:], ch["vst"])], axis=1)
    for it in range(N_DOUBLINGS):
        for ch in chains:
            ch["x"] = ch["x"] + _mm(ch["pw"], ch["x"])
        if it + 1 < N_DOUBLINGS:
            for ch in chains:
                ch["pw"] = _mm(ch["pw"], ch["pw"])
    for ch in chains:
        low = jnp.concatenate([jnp.zeros_like(ch["vst"]), ch["vst"]], axis=1)
        op = _mm(ch["lr"], jnp.concatenate([ch["x"], low], axis=0))
        ch["op"] = op[:CHUNK] + op[CHUNK:]
        ch["xp"] = ch["x"][:CHUNK] + ch["x"][CHUNK:]
    for ch in chains:
        d, sl = ch["d"], ch["sl"]
        rbar_ref[d, 0, :, sl] = ch["rp"] + ch["op"][:, :LANES]
        y0_ref[d, 0, :, sl] = ch["op"][:, LANES:]
        rhs = jnp.concatenate([ch["xp"], jnp.concatenate([jnp.zeros_like(ch["vp"]), ch["vp"]], axis=1)], axis=0)
        ag = _mm_tn(ch["be_ke"], rhs)
        a_full = ag[:, :LANES] + jnp.where(eye, jnp.broadcast_to(ch["e_tot"], (LANES, LANES)), 0.0)
        a_ref[d, 0, :, sl] = jnp.where(same_head, a_full, 0.0)
        g_ref[d, 0, :, sl] = jnp.where(same_head, ag[:, LANES:], 0.0)


def _head_block_diag():
    idx = np.arange(RWKV_DIM) // RWKV_HEAD
    return jnp.asarray(idx[:, None] == idx[None, :], BF16)


def _rwkv_chunks(p, kvec, w0, a0, w_up, a_up, g_up, r_k):
    b, s, _ = p.shape
    nc = s // CHUNK
    lora_w = DECAY_LORA + AAA_LORA + GATE_LORA
    wup = jnp.zeros((2, lora_w, RWKV_DIM), F32).at[:, :DECAY_LORA].set(w_up).astype(BF16)
    aup = jnp.zeros((2, lora_w, RWKV_DIM), F32).at[:, DECAY_LORA:DECAY_LORA + AAA_LORA].set(a_up).astype(BF16)
    gup = jnp.zeros((lora_w, RWKV_DIM), F32).at[DECAY_LORA + AAA_LORA:].set(g_up).astype(BF16)
    full = lambda shape: pl.BlockSpec(shape, lambda bi, c: (0,) * len(shape))
    col = lambda off: pl.BlockSpec((1, CHUNK, RWKV_DIM), lambda bi, c: (bi, c, off // RWKV_DIM))
    per_tok = pl.BlockSpec((2, 1, CHUNK, RWKV_DIM), lambda bi, c: (0, bi, c, 0))
    per_chunk = pl.BlockSpec((2, 1, 2 * CHUNK, RWKV_DIM), lambda bi, c: (0, bi, c, 0))
    tok = pl.BlockSpec((1, CHUNK, RWKV_DIM), lambda bi, c: (bi, c, 0))
    f32 = lambda *shape: jax.ShapeDtypeStruct(shape, F32)
    return pl.pallas_call(
        _rwkv_chunk_kernel,
        grid=(b, nc),
        in_specs=[col(P_RKV), col(P_RKV + RWKV_DIM), col(P_RKV + 2 * RWKV_DIM),
                  pl.BlockSpec((1, CHUNK, lora_w), lambda bi, c: (bi, c, P_LORA // lora_w)),
                  full((2, RWKV_DIM)), full((2, 1, RWKV_DIM)), full((2, 1, RWKV_DIM)),
                  full(wup.shape), full(aup.shape), full(gup.shape), full((1, RWKV_DIM)),
                  full((RWKV_DIM, RWKV_DIM))],
        out_specs=[per_tok, per_tok, per_chunk, per_chunk, tok, tok],
        out_shape=[f32(2, b, s, RWKV_DIM), f32(2, b, s, RWKV_DIM), f32(2, b, 2 * s, RWKV_DIM),
                   f32(2, b, 2 * s, RWKV_DIM), f32(b, s, RWKV_DIM), f32(b, s, RWKV_DIM)],
        compiler_params=_cparams(("parallel", "parallel")),
        name="rwkv_chunks",
    )(p, p, p, p, kvec, w0.reshape(2, 1, -1), a0.reshape(2, 1, -1), wup, aup, gup, r_k.reshape(1, -1),
      _head_block_diag())


RWKV_TS = 256


def _rwkv_scan_kernel(rbar_ref, y0_ref, a_ref, g_ref, y_ref, s_ref):
    d = pl.program_id(1)

    @pl.when(pl.program_id(2) == 0)
    def _():
        s_ref[...] = jnp.zeros_like(s_ref)

    n_sub = RWKV_TS // CHUNK
    for cc in range(n_sub):
        ci = jnp.where(d == 0, cc, n_sub - 1 - cc)
        rows = pl.ds(pl.multiple_of(ci * CHUNK, CHUNK), CHUNK)
        rows2 = pl.ds(pl.multiple_of(ci * 2 * CHUNK, 2 * CHUNK), 2 * CHUNK)
        for pr in range(N_PAIR):
            sl = slice(pr * LANES, (pr + 1) * LANES)
            st = s_ref[pr]
            y_ref[0, 0, rows, sl] = _mm_f32(rbar_ref[0, 0, rows, sl], st) + y0_ref[0, 0, rows, sl]
            s_ref[pr] = _mm_f32(a_ref[0, 0, rows2, sl], st) + g_ref[0, 0, rows2, sl]


def _rwkv_scan(rbar, y0, a, g, *, n_ctx):
    _, b, s, _ = rbar.shape
    nt = s // RWKV_TS
    nct = n_ctx // RWKV_TS

    def tile(d, j):
        back = jnp.where(j < nct, nct - 1 - j, nt - 1 - (j - nct))
        return jnp.where(d == 0, j, back)

    tok = pl.BlockSpec((1, 1, RWKV_TS, RWKV_DIM), lambda bi, d, j: (d, bi, tile(d, j), 0))
    chk = pl.BlockSpec((1, 1, 2 * RWKV_TS, RWKV_DIM), lambda bi, d, j: (d, bi, tile(d, j), 0))
    return pl.pallas_call(
        _rwkv_scan_kernel,
        grid=(b, 2, nt),
        in_specs=[tok, tok, chk, chk],
        out_specs=tok,
        out_shape=jax.ShapeDtypeStruct((2, b, s, RWKV_DIM), F32),
        scratch_shapes=[pltpu.VMEM((N_PAIR, LANES, LANES), F32)],
        compiler_params=_cparams(("parallel", "parallel", "arbitrary")),
        name="rwkv_scan",
    )(rbar, y0, a, g)


def _rwkv_readout_kernel(yf_ref, yb_ref, bonus_ref, gate_ref, lng_ref, lnb_ref, bd_ref, o_ref):
    bd = bd_ref[...]
    y = yf_ref[0, 0] + yb_ref[0, 0]
    inv_n = 1.0 / RWKV_HEAD
    dev = y - _headsum(y, bd) * inv_n
    var = _headsum(dev * dev, bd) * inv_n
    yn = dev * lax.rsqrt(var + RWKV_LN_EPS) * lng_ref[...] + lnb_ref[...]
    o_ref[0] = ((yn + bonus_ref[0]) * gate_ref[0]).astype(BF16)


def _rwkv_readout(y, bonus, gate, ln_g, ln_b, *, row_off):
    _, b, s, _ = y.shape
    nt = s // TM - row_off
    full = lambda shape: pl.BlockSpec(shape, lambda bi, i: (0,) * len(shape))
    tok = pl.BlockSpec((1, TM, RWKV_DIM), lambda bi, i: (bi, i + row_off, 0))
    return pl.pallas_call(
        _rwkv_readout_kernel,
        grid=(b, nt),
        in_specs=[pl.BlockSpec((1, 1, TM, RWKV_DIM), lambda bi, i: (0, bi, i + row_off, 0)),
                  pl.BlockSpec((1, 1, TM, RWKV_DIM), lambda bi, i: (1, bi, i + row_off, 0)),
                  tok, tok, full((1, RWKV_DIM)), full((1, RWKV_DIM)), full((RWKV_DIM, RWKV_DIM))],
        out_specs=pl.BlockSpec((1, TM, RWKV_DIM), lambda bi, i: (bi, i, 0)),
        out_shape=jax.ShapeDtypeStruct((b, nt * TM, RWKV_DIM), BF16),
        compiler_params=_cparams(("parallel", "parallel")),
        name="rwkv_readout",
    )(y, y, bonus, gate, ln_g.reshape(1, -1), ln_b.reshape(1, -1), _head_block_diag())


DFT_N2 = LANES
HY_MIN_LEN = 1024
HY_CT = 128


def _hyena_mlp_kernel(feats_ref, w1_ref, b1_ref, w2_ref, b2_ref, freq_ref, h_ref):
    h = jnp.sin(freq_ref[0:1] * (_mm_f32(feats_ref[0], w1_ref[...]) + b1_ref[...]))
    h_ref[0] = jnp.sin(freq_ref[1:2] * (_mm_f32(h, w2_ref[...]) + b2_ref[...]))


def _hyena_filter_kernel(h_ref, w3f_ref, w3b_ref, t_ref, delta_ref, k_ref):
    hf = _mm_f32(h_ref[0], w3f_ref[...]) * jnp.exp(-t_ref[0] * delta_ref[...])
    hb = _mm_f32(h_ref[1], w3b_ref[...]) * jnp.exp(-t_ref[1] * delta_ref[...])
    norm = (jnp.sum(jnp.abs(hf), axis=0, keepdims=True) + jnp.sum(jnp.abs(hb), axis=0, keepdims=True))
    r = pl.program_id(2)
    rows = lax.broadcasted_iota(jnp.int32, hf.shape, 0)
    tail = jnp.where(rows == 0, 0.0, hb)
    blk = jnp.where(r == 0, hf, jnp.where(r == pl.num_programs(2) - 1, tail, 0.0))
    k_ref[0] = blk / norm


def _hyena_filter_buffer(n, nc, w1, b1, w2, b2, w3, freq):
    lag = jnp.stack([jnp.arange(n), jnp.where(jnp.arange(n) == 0, 0, n - jnp.arange(n))]).astype(F32)
    t = (lag / (n - 1))[:, :, None]
    bands = jnp.linspace(1e-4, HYENA_BANDS - 1, HYENA_BANDS, dtype=F32)
    ang = (2.0 * math.pi / n) * lag[:, :, None] * bands[None, None, :]
    feats = jnp.concatenate([t, jnp.cos(ang), -jnp.sin(ang),
                             jnp.zeros((2, n, HYENA_FW - HYENA_EMB), F32)], axis=-1)
    w1p = jnp.zeros((HYENA_FW, HYENA_FW), F32).at[:HYENA_EMB].set(w1)
    deltas = jnp.abs(jnp.linspace(math.log(HYENA_TARGET) / HYENA_SLOW,
                                  math.log(HYENA_TARGET) / HYENA_FAST, HYENA_DIM, dtype=F32))[None, :]
    fixed = lambda shape: pl.BlockSpec(shape, lambda d: (0,) * len(shape))
    hidden = pl.pallas_call(
        _hyena_mlp_kernel,
        grid=(2,),
        in_specs=[pl.BlockSpec((1, n, HYENA_FW), lambda d: (d, 0, 0)), fixed((HYENA_FW, HYENA_FW)),
                  fixed((1, HYENA_FW)), fixed((HYENA_FW, HYENA_FW)), fixed((1, HYENA_FW)),
                  fixed((2, HYENA_FW))],
        out_specs=pl.BlockSpec((1, n, HYENA_FW), lambda d: (d, 0, 0)),
        out_shape=jax.ShapeDtypeStruct((2, n, HYENA_FW), F32),
        compiler_params=_cparams(("parallel",)),
        name="hyena_filter_mlp",
    )(feats, w1p, b1.reshape(1, -1), w2, b2.reshape(1, -1), freq)
    tc = 256
    nj = HYENA_DIM // tc
    full = lambda shape: pl.BlockSpec(shape, lambda o, j, r: (0,) * len(shape))
    return pl.pallas_call(
        _hyena_filter_kernel,
        grid=(HYENA_ORDER, nj, nc // n),
        in_specs=[full((2, n, HYENA_FW)),
                  pl.BlockSpec((HYENA_FW, tc), lambda o, j, r: (0, o * 2 * nj + j)),
                  pl.BlockSpec((HYENA_FW, tc), lambda o, j, r: (0, o * 2 * nj + nj + j)),
                  full((2, n, 1)), pl.BlockSpec((1, tc), lambda o, j, r: (0, j))],
        out_specs=pl.BlockSpec((1, n, tc), lambda o, j, r: (o, r, j)),
        out_shape=jax.ShapeDtypeStruct((HYENA_ORDER, nc, HYENA_DIM), F32),
        compiler_params=_cparams(("parallel", "parallel", "parallel")),
        name="hyena_filters",
    )(hidden, w3, w3, t, deltas)


def _dft_tables(n1):
    nc = n1 * DFT_N2
    t2 = np.arange(DFT_N2)[:, None, None]
    f1 = np.arange(n1)[None, :, None]
    t1 = np.arange(n1)[None, None, :]
    theta = 2.0 * np.pi * ((f1 * (DFT_N2 * t1 + t2)) % nc) / nc
    g_fwd = np.concatenate([np.cos(theta), -np.sin(theta)], axis=1)
    g_inv = np.concatenate([np.cos(theta), -np.sin(theta)], axis=1).transpose(0, 2, 1) / nc
    k = np.arange(DFT_N2)
    phi = 2.0 * np.pi * ((k[:, None] * k[None, :]) % DFT_N2) / DFT_N2
    c, s = np.cos(phi), np.sin(phi)
    f_fwd = np.block([[c, s], [-s, c]])
    f_inv = np.block([[c, -s], [s, c]])
    return tuple(jnp.asarray(t, BF16) for t in (g_fwd, g_inv, f_fwd, f_inv))


def _dft_rows_in(x_ref, g_ref, a_ref, t1n, n1):
    for t2 in range(DFT_N2):
        xs = x_ref[0, pl.ds(t2, t1n, stride=DFT_N2), :]
        a_ref[pl.ds(t2, 2 * n1, stride=DFT_N2), :] = _mm(g_ref[t2], xs)


def _slab(f1):
    return slice(f1 * DFT_N2, (f1 + 1) * DFT_N2)


def _spectrum_kernel(x_ref, g_ref, ff_ref, k_ref, a_ref, *, n1):
    _dft_rows_in(x_ref, g_ref, a_ref, n1, n1)
    for f1 in range(n1):
        a = jnp.concatenate([a_ref[_slab(f1)], a_ref[_slab(n1 + f1)]], axis=0)
        k_ref[0, f1] = _mm(ff_ref[...], a)


def _filter_spectrum(kbuf, g_fwd, f_fwd, n1):
    no, nc, _ = kbuf.shape
    nj = HYENA_DIM // HY_CT
    return pl.pallas_call(
        functools.partial(_spectrum_kernel, n1=n1),
        grid=(no, nj),
        in_specs=[pl.BlockSpec((1, nc, HY_CT), lambda o, j: (o, 0, j)),
                  pl.BlockSpec(g_fwd.shape, lambda o, j: (0, 0, 0)),
                  pl.BlockSpec(f_fwd.shape, lambda o, j: (0, 0))],
        out_specs=pl.BlockSpec((1, n1, 2 * DFT_N2, HY_CT), lambda o, j: (o, 0, 0, j)),
        out_shape=jax.ShapeDtypeStruct((no, n1, 2 * DFT_N2, HYENA_DIM), F32),
        scratch_shapes=[pltpu.VMEM((2 * n1 * DFT_N2, HY_CT), F32)],
        compiler_params=_cparams(("parallel", "parallel")),
        name="hyena_filter_spectrum",
    )(kbuf, g_fwd, f_fwd)


def _hyena_conv_kernel(x_ref, gate_ref, k_ref, g1_ref, ff_ref, fi_ref, g3_ref, bias_ref, o_ref, a_ref,
                       *, t1n, n1):
    _dft_rows_in(x_ref, g1_ref, a_ref, t1n, n1)
    for f1 in range(n1):
        re, im = _slab(f1), _slab(n1 + f1)
        x = _mm(ff_ref[...], jnp.concatenate([a_ref[re], a_ref[im]], axis=0))
        xre, xim = x[:DFT_N2], x[DFT_N2:]
        kre, kim = k_ref[0, f1, :DFT_N2], k_ref[0, f1, DFT_N2:]
        bm = _mm(fi_ref[...], jnp.concatenate([xre * kre - xim * kim, xre * kim + xim * kre], axis=0))
        a_ref[re] = bm[:DFT_N2]
        a_ref[im] = bm[DFT_N2:]
    bias = bias_ref[...]
    for t2 in range(DFT_N2):
        y = _mm(g3_ref[t2], a_ref[pl.ds(t2, 2 * n1, stride=DFT_N2), :])
        rows = pl.ds(t2, t1n, stride=DFT_N2)
        o_ref[0, rows, :] = gate_ref[0, rows, :] * (y + bias * x_ref[0, rows, :])


def _hyena_conv(z, gate, kspec, order, bias, tabs, n1):
    bx, n_pad, _ = z.shape
    t1n = n_pad // DFT_N2
    g_fwd, g_inv, f_fwd, f_inv = tabs
    g1 = g_fwd[:, :, :t1n]
    g3 = g_inv[:, :t1n, :]
    nj = HYENA_DIM // HY_CT
    tok = pl.BlockSpec((1, n_pad, HY_CT), lambda j, bi: (bi, 0, j))
    const = lambda a: pl.BlockSpec(a.shape, lambda j, bi: (0,) * a.ndim)
    return pl.pallas_call(
        functools.partial(_hyena_conv_kernel, t1n=t1n, n1=n1),
        grid=(nj, bx),
        in_specs=[tok, tok,
                  pl.BlockSpec((1, n1, 2 * DFT_N2, HY_CT), lambda j, bi: (order, 0, 0, j)),
                  const(g1), const(f_fwd), const(f_inv), const(g3),
                  pl.BlockSpec((1, HY_CT), lambda j, bi: (0, j))],
        out_specs=tok,
        out_shape=jax.ShapeDtypeStruct((bx, n_pad, HYENA_DIM), F32),
        scratch_shapes=[pltpu.VMEM((2 * n1 * DFT_N2, HY_CT), F32)],
        compiler_params=_cparams(("parallel", "parallel")),
        name="hyena_conv",
    )(z, gate, kspec, g1, f_fwd, f_inv, g3, bias.reshape(1, -1))


def _hyena_operator(v, x1, x2, n, filt_params, bias):
    n_pad = v.shape[1]
    nc = 2 * n_pad
    n1 = nc // DFT_N2
    tabs = _dft_tables(n1)
    kspec = _filter_spectrum(_hyena_filter_buffer(n, nc, *filt_params), tabs[0], tabs[2], n1)
    z = v
    for o, gate in enumerate((x1, x2)):
        z = _hyena_conv(z, gate, kspec, o, bias[o], tabs, n1)
    return z


def _merge_kernel(x_ref, mod_ref, g_ref, gates_ref, bg_ref, ya_ref, yb_ref, yh_ref, yd_ref,
                  wb_ref, wo_ref, o_ref):
    merged = None
    for br, y_ref in enumerate((ya_ref, yb_ref, yh_ref, yd_ref)):
        gate = _sigmoid(gates_ref[0, :, br * D_MODEL:(br + 1) * D_MODEL] + bg_ref[br:br + 1])
        term = gate * jnp.dot(y_ref[0].astype(BF16), wb_ref[br], preferred_element_type=F32)
        merged = term if merged is None else merged + term
    y = jnp.dot(merged.astype(BF16), wo_ref[...], preferred_element_type=F32)
    o_ref[0] = x_ref[0] + mod_ref[0, 5:6] * _rms(y, g_ref[3:4])


def _merge(x, mods, norm_g, p, b_gate, ya, yb, yh, yd, w_branch, w_out, *, n_ctx_tiles, row_off):
    b, s, _ = x.shape
    nt = s // TM - row_off
    n_lat = mods.shape[0] - 1
    br = lambda: pl.BlockSpec((1, TM, BRANCH_DIM), lambda bi, i: (bi, i, 0))
    n_lat_tiles = s // TM - n_ctx_tiles
    br_a = pl.BlockSpec((1, TM, BRANCH_DIM), lambda bi, i: (
        bi, jnp.where(i + row_off < n_ctx_tiles, n_lat_tiles + i, i + row_off - n_ctx_tiles), 0))
    return pl.pallas_call(
        _merge_kernel,
        grid=(b, nt),
        in_specs=[pl.BlockSpec((1, TM, D_MODEL), lambda bi, i: (bi, i + row_off, 0)),
                  pl.BlockSpec((1, N_MOD, D_MODEL),
                               lambda bi, i: (jnp.where(i + row_off < n_ctx_tiles, n_lat, bi), 0, 0)),
                  pl.BlockSpec((6, D_MODEL), lambda bi, i: (0, 0)),
                  pl.BlockSpec((1, TM, GATE_COLS), lambda bi, i: (bi, i + row_off, 0)),
                  pl.BlockSpec((N_BRANCH, D_MODEL), lambda bi, i: (0, 0)),
                  br_a, br(), br(), br(),
                  pl.BlockSpec((N_BRANCH, BRANCH_DIM, D_MODEL), lambda bi, i: (0, 0, 0)),
                  pl.BlockSpec((D_MODEL, D_MODEL), lambda bi, i: (0, 0))],
        out_specs=pl.BlockSpec((1, TM, D_MODEL), lambda bi, i: (bi, i, 0)),
        out_shape=jax.ShapeDtypeStruct((b, nt * TM, D_MODEL), F32),
        compiler_params=_cparams(("parallel", "parallel")),
        name="merge_branches",
    )(x, mods, norm_g, p, b_gate, ya, yb, yh, yd, w_branch, w_out)


def kernel(x, c, ctx, c_ctx, w_mod, b_mod, norm_g, ffn_w13, ffn_w2, w_in, b_gate, mla_norm_q, mla_norm_kv, mla_w_uq, mla_w_ukv, rwkv_mu, rwkv_w0, rwkv_w_up, rwkv_a0, rwkv_a_up, rwkv_g_up, rwkv_kvec, rwkv_r_k, rwkv_ln_g, rwkv_ln_b, hyena_conv, hyena_conv_b, hyena_w1, hyena_b1, hyena_w2, hyena_b2, hyena_w3, hyena_freq, hyena_bias, swa_sink, w_branch, w_out):
    b, n, _ = x.shape
    n_ctx = ctx.shape[1]
    nct = n_ctx // TM
    xall = jnp.concatenate([ctx, x], axis=1)
    c_all = jnp.concatenate([c, c_ctx[None]], axis=0)
    tabs_mla = _rope_tables(n, n_ctx, MLA_ROPE, MLA_NOPE, LANES)
    tabs_swa = _rope_tables(n, n_ctx, SWA_HEAD, 0, SWA_HEAD)
    depth = w_mod.shape[0]
    for l in range(depth):
        with_ctx = l + 1 < depth
        row_off = 0 if with_ctx else nct
        mods = _modulation(c_all, w_mod[l], b_mod[l])
        xall = _ffn(xall, mods, norm_g[l], ffn_w13[l, 0].astype(BF16), ffn_w2[l, 0].astype(BF16),
                    mod0=0, g0=0, n_ctx_tiles=nct)
        coef = _shift_coefficients(rwkv_mu[l], hyena_conv[l], hyena_conv_b[l])
        p, p_hy = _inproj(xall, mods, norm_g[l], _permute_w_in(w_in[l]).astype(BF16), coef, n_ctx_tiles=nct)
        q, k, v = _mla_prep(p, mla_norm_q[l], mla_norm_kv[l], mla_w_uq[l], mla_w_ukv[l], tabs_mla)
        ya = _mla_attention(q, k, v, n_ctx=n_ctx, with_ctx=with_ctx)
        rbar, y0, a, g, bonus, gate = _rwkv_chunks(p, rwkv_kvec[l], rwkv_w0[l], rwkv_a0[l], rwkv_w_up[l],
                                                   rwkv_a_up[l], rwkv_g_up[l], rwkv_r_k[l])
        y = _rwkv_scan(rbar, y0, a, g, n_ctx=n_ctx)
        yb = _rwkv_readout(y, bonus, gate, rwkv_ln_g[l], rwkv_ln_b[l], row_off=row_off)
        filt = (hyena_w1[l], hyena_b1[l], hyena_w2[l], hyena_b2[l], hyena_w3[l], hyena_freq[l])
        def hyena_segment(rows, n_rows):
            ins = []
            for j in range(3):
                t = p_hy[:, rows, j * HYENA_DIM:(j + 1) * HYENA_DIM]
                ins.append(jnp.pad(t, ((0, 0), (0, max(n_rows, HY_MIN_LEN) - n_rows), (0, 0))))
            return _hyena_operator(*ins, n_rows, filt, hyena_bias[l])[:, :n_rows]

        yh = hyena_segment(slice(n_ctx, None), n)
        if with_ctx:
            yh = jnp.concatenate([hyena_segment(slice(0, n_ctx), n_ctx), yh], axis=1)
        q, k, v = _swa_prep(p, tabs_swa)
        yd = _swa_attention(q, k, v, swa_sink[l], n_ctx=n_ctx, q_off=row_off * TM // SWA_TQ)
        xall = _merge(xall, mods, norm_g[l], p, b_gate[l], ya, yb, yh, yd, w_branch[l].astype(BF16),
                      w_out[l].astype(BF16), n_ctx_tiles=nct, row_off=row_off)
        xall = _ffn(xall, mods, norm_g[l], ffn_w13[l, 1].astype(BF16), ffn_w2[l, 1].astype(BF16),
                    mod0=6, g0=4, n_ctx_tiles=nct - row_off)
    return xall
```

```python
import functools
import math

import numpy as np
import jax
import jax.numpy as jnp
from jax import lax
from jax.experimental import pallas as pl
from jax.experimental.pallas import tpu as pltpu

F32 = jnp.float32
BF16 = jnp.bfloat16

D_MODEL = 1024
GRID_W = 64
N_BRANCH = 4
N_MOD = 9
FF_DIM = 2816
EPS = 1e-6
ROPE_BASE = 10000.0
NEG_INF = -1e30
BRANCH_DIM = 512
MLA_HEADS = 8
MLA_NOPE = 64
MLA_ROPE = 32
MLA_V = 64
MLA_Q_RANK = 256
MLA_KV_RANK = 128
RWKV_HEADS = 8
RWKV_HEAD = 64
RWKV_DIM = RWKV_HEADS * RWKV_HEAD
DECAY_LORA = 64
AAA_LORA = 64
GATE_LORA = 128
RWKV_LN_EPS = 64e-5
HYENA_DIM = 512
HYENA_ORDER = 2
HYENA_EMB = 33
HYENA_BANDS = (HYENA_EMB - 1) // 2
HYENA_FW = 64
HYENA_TARGET = 1e-2
HYENA_FAST = 0.3
HYENA_SLOW = 1.5
SWA_HEADS = 8
SWA_KV_HEADS = 2
SWA_HEAD = 64
SWA_GROUP = SWA_HEADS // SWA_KV_HEADS
WINDOW = 128
GATE_COLS = N_BRANCH * D_MODEL
MLA_COLS = MLA_Q_RANK + MLA_KV_RANK + MLA_ROPE
RWKV_COLS = 3 * RWKV_DIM + DECAY_LORA + AAA_LORA + GATE_LORA
HYENA_COLS = 3 * HYENA_DIM
SWA_COLS = (SWA_HEADS + 2 * SWA_KV_HEADS) * SWA_HEAD

LANES = 128
V7X_VMEM_LIMIT = 56 * 1024 * 1024

TM = 256
FFN_SUB_TILES = 4
INPROJ_SUB_TILES = 2
FF_CHUNK = 256
IN_CHUNK = 512
CHUNK = 64

P_GATE = 0
P_RKV = 4096
P_SWAQ = 5632
P_LORA = 6144
P_CQ = 6400
P_CKV = 6656
P_KR = 6784
P_SWAK = 6912
P_SWAV = 7040
P_COLS = 7168
P_HY = 7168
W_COLS = P_HY + HYENA_COLS
_SHIFT_COLS = ((P_RKV, P_SWAQ), (P_LORA, P_CQ), (P_HY, W_COLS))
_SHIFT_CHUNKS = [any(lo < (j + 1) * IN_CHUNK and j * IN_CHUNK < hi for lo, hi in _SHIFT_COLS)
                 for j in range(W_COLS // IN_CHUNK)]


def _cparams(sem, vmem=V7X_VMEM_LIMIT):
    return pltpu.CompilerParams(dimension_semantics=sem, vmem_limit_bytes=vmem)


def _mm(a, b):
    return jnp.dot(a.astype(BF16), b.astype(BF16), preferred_element_type=F32)


def _mm_nt(a, b):
    return lax.dot_general(a.astype(BF16), b.astype(BF16), (((1,), (1,)), ((), ())),
                           preferred_element_type=F32)


def _mm_tn(a, b):
    return lax.dot_general(a.astype(BF16), b.astype(BF16), (((0,), (0,)), ((), ())),
                           preferred_element_type=F32)


def _mm_f32(a, b):
    return jnp.dot(a, b, preferred_element_type=F32, precision=lax.Precision.HIGHEST)


def _rms(x, g):
    return x * lax.rsqrt(jnp.mean(x * x, axis=-1, keepdims=True) + EPS) * g


def _sigmoid(x):
    return 1.0 / (1.0 + jnp.exp(-x))


def _mod_kernel(c_ref, w_ref, b_ref, o_ref):
    c = c_ref[...]
    o_ref[...] = _mm(c * _sigmoid(c), w_ref[...]) + b_ref[...]


def _modulation(c_all, w_mod, b_mod):
    r = c_all.shape[0]
    rp = -(-r // 8) * 8
    c_pad = jnp.zeros((rp, D_MODEL), F32).at[:r].set(c_all)
    tn = 1024
    out = pl.pallas_call(
        _mod_kernel,
        grid=(N_MOD * D_MODEL // tn,),
        in_specs=[pl.BlockSpec((rp, D_MODEL), lambda j: (0, 0)),
                  pl.BlockSpec((D_MODEL, tn), lambda j: (0, j)),
                  pl.BlockSpec((1, tn), lambda j: (0, j))],
        out_specs=pl.BlockSpec((rp, tn), lambda j: (0, j)),
        out_shape=jax.ShapeDtypeStruct((rp, N_MOD * D_MODEL), F32),
        compiler_params=_cparams(("arbitrary",)),
        name="modulation",
    )(c_pad, w_mod, b_mod.reshape(1, -1))
    return out[:r].reshape(r, N_MOD, D_MODEL)


def _sub_tiles(n_tiles, most):
    return max(g for g in range(1, most + 1) if n_tiles % g == 0)


def _mod_specs(n_sub, tiles_per_seq, n_ctx_tiles, ctx_row):
    def spec(k):
        def index(i):
            t = i * n_sub + k
            return (jnp.where(t % tiles_per_seq < n_ctx_tiles, ctx_row, t // tiles_per_seq), 0, 0)
        return pl.BlockSpec((1, N_MOD, D_MODEL), index)
    return [spec(k) for k in range(n_sub)]


def _ffn_kernel(x_ref, *refs, mod0, g0, n_sub):
    mod_refs = refs[:n_sub]
    g_ref, w13_ref, w2_ref, o_ref = refs[n_sub:]
    tiles = [slice(t * TM, (t + 1) * TM) for t in range(n_sub)]
    u = jnp.concatenate(
        [(_rms(x_ref[rows], g_ref[g0:g0 + 1]) * (1.0 + m[0, mod0 + 1:mod0 + 2]) + m[0, mod0:mod0 + 1]).astype(BF16)
         for rows, m in zip(tiles, mod_refs)], axis=0)
    acc = jnp.zeros(x_ref.shape, F32)
    for f in range(FF_DIM // FF_CHUNK):
        lo = f * FF_CHUNK
        a = jnp.dot(u, w13_ref[:, lo:lo + FF_CHUNK], preferred_element_type=F32)
        b = jnp.dot(u, w13_ref[:, FF_DIM + lo:FF_DIM + lo + FF_CHUNK], preferred_element_type=F32)
        h = (a * _sigmoid(a) * b).astype(BF16)
        acc = acc + jnp.dot(h, w2_ref[lo:lo + FF_CHUNK, :], preferred_element_type=F32)
    hn = _rms(acc, g_ref[g0 + 1:g0 + 2])
    for rows, m in zip(tiles, mod_refs):
        o_ref[rows] = x_ref[rows] + 0.5 * m[0, mod0 + 2:mod0 + 3] * hn[rows]


def _ffn(x, mods, norm_g, w13, w2, *, mod0, g0, n_ctx_tiles):
    b, s, _ = x.shape
    n_tiles = b * s // TM
    n_sub = _sub_tiles(n_tiles, FFN_SUB_TILES)
    rows = n_sub * TM
    out = pl.pallas_call(
        functools.partial(_ffn_kernel, mod0=mod0, g0=g0, n_sub=n_sub),
        grid=(n_tiles // n_sub,),
        in_specs=[pl.BlockSpec((rows, D_MODEL), lambda i: (i, 0)),
                  *_mod_specs(n_sub, s // TM, n_ctx_tiles, mods.shape[0] - 1),
                  pl.BlockSpec((6, D_MODEL), lambda i: (0, 0)),
                  pl.BlockSpec(memory_space=pltpu.VMEM),
                  pl.BlockSpec(memory_space=pltpu.VMEM)],
        out_specs=pl.BlockSpec((rows, D_MODEL), lambda i: (i, 0)),
        out_shape=jax.ShapeDtypeStruct((b * s, D_MODEL), F32),
        compiler_params=_cparams(("parallel",)),
        name="ffn_half_step",
    )(x.reshape(b * s, D_MODEL), *([mods] * n_sub), norm_g, w13, w2)
    return out.reshape(b, s, D_MODEL)


def _inproj_kernel(x_ref, xp_ref, xn_ref, *refs, n_sub, tiles_per_seq, n_ctx_tiles):
    mod_refs = refs[:n_sub]
    g_ref, w_ref, coef_ref, o_ref, hy_ref = refs[n_sub:]
    g = g_ref[2:3]

    def modulated(x, m):
        return (_rms(x, g) * (1.0 + m[0, 4:5]) + m[0, 3:4]).astype(BF16)

    u = jnp.concatenate([modulated(x_ref[t * TM:(t + 1) * TM], m) for t, m in enumerate(mod_refs)], axis=0)
    u_prev = modulated(xp_ref[...], mod_refs[0])
    u_next = modulated(xn_ref[...], mod_refs[-1])
    rows = lax.broadcasted_iota(jnp.int32, (n_sub * TM, 1), 0)
    keep_prev = jnp.ones((n_sub * TM, 1), F32)
    keep_next = jnp.ones((n_sub * TM, 1), F32)
    for t in range(n_sub):
        w = (pl.program_id(0) * n_sub + t) % tiles_per_seq
        seg_start = (w == 0) | (w == n_ctx_tiles)
        seg_end = (w == n_ctx_tiles - 1) | (w == tiles_per_seq - 1)
        keep_prev = jnp.where((rows == t * TM) & seg_start, 0.0, keep_prev)
        keep_next = jnp.where((rows == (t + 1) * TM - 1) & seg_end, 0.0, keep_next)
    for j in range(W_COLS // IN_CHUNK):
        cols = slice(j * IN_CHUNK, (j + 1) * IN_CHUNK)
        p = jnp.dot(u, w_ref[:, cols], preferred_element_type=F32)
        if _SHIFT_CHUNKS[j]:
            p_first = jnp.dot(u_prev, w_ref[:, cols], preferred_element_type=F32)[7:8]
            p_last = jnp.dot(u_next, w_ref[:, cols], preferred_element_type=F32)[0:1]
            prev = jnp.where(rows == 0, p_first, pltpu.roll(p, 1, axis=0)) * keep_prev
            nxt = jnp.where(rows == n_sub * TM - 1, p_last, pltpu.roll(p, n_sub * TM - 1, axis=0)) * keep_next
            p = (coef_ref[0:1, cols] * p + coef_ref[1:2, cols] * prev + coef_ref[2:3, cols] * nxt
                 + coef_ref[3:4, cols])
        if j * IN_CHUNK < P_COLS:
            o_ref[:, cols] = p.astype(BF16)
        else:
            hy_ref[:, j * IN_CHUNK - P_HY:(j + 1) * IN_CHUNK - P_HY] = p


def _inproj(x, mods, norm_g, w_in_p, coef, *, n_ctx_tiles):
    b, s, _ = x.shape
    n_tiles = b * s // TM
    n_sub = _sub_tiles(n_tiles, INPROJ_SUB_TILES)
    rows = n_sub * TM
    r8 = rows // 8
    p, hy = pl.pallas_call(
        functools.partial(_inproj_kernel, n_sub=n_sub, tiles_per_seq=s // TM, n_ctx_tiles=n_ctx_tiles),
        grid=(n_tiles // n_sub,),
        in_specs=[pl.BlockSpec((rows, D_MODEL), lambda i: (i, 0)),
                  pl.BlockSpec((8, D_MODEL), lambda i: (jnp.maximum(i * r8 - 1, 0), 0)),
                  pl.BlockSpec((8, D_MODEL), lambda i: (jnp.minimum((i + 1) * r8, b * s // 8 - 1), 0)),
                  *_mod_specs(n_sub, s // TM, n_ctx_tiles, mods.shape[0] - 1),
                  pl.BlockSpec((6, D_MODEL), lambda i: (0, 0)),
                  pl.BlockSpec(memory_space=pltpu.VMEM),
                  pl.BlockSpec((4, W_COLS), lambda i: (0, 0))],
        out_specs=[pl.BlockSpec((rows, P_COLS), lambda i: (i, 0)),
                   pl.BlockSpec((rows, HYENA_COLS), lambda i: (i, 0))],
        out_shape=[jax.ShapeDtypeStruct((b * s, P_COLS), BF16),
                   jax.ShapeDtypeStruct((b * s, HYENA_COLS), F32)],
        compiler_params=_cparams(("parallel",)),
        name="in_projection",
    )(*([x.reshape(b * s, D_MODEL)] * 3), *([mods] * n_sub), norm_g, w_in_p, coef)
    return p.reshape(b, s, P_COLS), hy.reshape(b, s, HYENA_COLS)


def _shift_coefficients(rwkv_mu, hyena_conv, hyena_conv_b):
    mu = rwkv_mu.astype(F32)
    coef = jnp.zeros((4, W_COLS), F32).at[0].set(1.0)
    for off, sl in ((P_RKV, slice(0, 3 * RWKV_DIM)), (P_LORA, slice(3 * RWKV_DIM, RWKV_COLS))):
        width = sl.stop - sl.start
        coef = coef.at[0, off:off + width].set(1.0 - mu[0, sl] - mu[1, sl])
        coef = coef.at[1, off:off + width].set(mu[0, sl])
        coef = coef.at[2, off:off + width].set(mu[1, sl])
    hy = slice(P_HY, P_HY + HYENA_COLS)
    coef = coef.at[0, hy].set(hyena_conv[1]).at[1, hy].set(hyena_conv[0]).at[2, hy].set(hyena_conv[2])
    return coef.at[3, hy].set(hyena_conv_b)


def _permute_w_in(w_in):
    o_mla = GATE_COLS
    o_rwkv = o_mla + MLA_COLS
    o_hy = o_rwkv + RWKV_COLS
    o_swa = o_hy + HYENA_COLS
    z = lambda n: jnp.zeros((D_MODEL, n), w_in.dtype)
    parts = [
        w_in[:, :GATE_COLS],
        w_in[:, o_rwkv:o_rwkv + 3 * RWKV_DIM],
        w_in[:, o_swa:o_swa + SWA_HEADS * SWA_HEAD],
        w_in[:, o_rwkv + 3 * RWKV_DIM:o_rwkv + RWKV_COLS],
        w_in[:, o_mla:o_mla + MLA_Q_RANK],
        w_in[:, o_mla + MLA_Q_RANK:o_mla + MLA_Q_RANK + MLA_KV_RANK],
        z(MLA_NOPE), w_in[:, o_mla + MLA_Q_RANK + MLA_KV_RANK:o_mla + MLA_COLS],
        z(LANES - MLA_NOPE - MLA_ROPE),
        w_in[:, o_swa + SWA_HEADS * SWA_HEAD:o_swa + SWA_COLS],
        w_in[:, o_hy:o_hy + HYENA_COLS],
    ]
    out = jnp.concatenate(parts, axis=1)
    assert out.shape[1] == W_COLS
    return out


def _rope_tables(n_lat, n_ctx, rot_dim, lane0, period):
    rows = n_lat // GRID_W
    row = jnp.repeat(jnp.arange(rows, dtype=F32), GRID_W)
    col = jnp.tile(jnp.arange(GRID_W, dtype=F32), rows)
    axis_dim = rot_dim // 2
    h = axis_dim // 2
    inv_freq = ROPE_BASE ** (-jnp.arange(0, axis_dim, 2, dtype=F32) / axis_dim)
    ang_r = row[:, None] * inv_freq
    ang_c = col[:, None] * inv_freq
    cos_rot = jnp.concatenate([jnp.cos(ang_r)] * 2 + [jnp.cos(ang_c)] * 2, axis=1)
    zeros = jnp.zeros_like(ang_r)
    sin_a = jnp.concatenate([-jnp.sin(ang_r), zeros, -jnp.sin(ang_c), zeros], axis=1)
    sin_b = jnp.concatenate([zeros, jnp.sin(ang_r), zeros, jnp.sin(ang_c)], axis=1)

    def widen(t, fill):
        g = jnp.full((n_lat, period), fill, F32).at[:, lane0:lane0 + rot_dim].set(t)
        g = jnp.tile(g, (1, LANES // period))
        ctx = jnp.full((n_ctx, LANES), fill, F32)
        return jnp.concatenate([ctx, g], axis=0)

    return widen(cos_rot, 1.0), widen(sin_a, 0.0), widen(sin_b, 0.0), h


def _rope128(x, cos, sin_a, sin_b, h):
    return x * cos + pltpu.roll(x, LANES - h, axis=1) * sin_a + pltpu.roll(x, h, axis=1) * sin_b


LOG2E = math.log2(math.e)
MLA_SCALE = (MLA_NOPE + MLA_ROPE) ** -0.5 * LOG2E
MLA_ONE_LANE = (MLA_V, 0)


def _mla_prep_kernel(cq_ref, ckv_ref, kr_ref, gq_ref, gkv_ref, wq_ref, wk_ref, wv_ref, vone_ref,
                     cos_ref, sa_ref, sb_ref, q_ref, k_ref, v_ref, *, h):
    cos, sa, sb = cos_ref[...], sa_ref[...], sb_ref[...]
    cq = _rms(cq_ref[0].astype(F32), gq_ref[...]).astype(BF16)
    ckv = _rms(ckv_ref[0].astype(F32), gkv_ref[...]).astype(BF16)
    q = jnp.dot(cq, wq_ref[...], preferred_element_type=F32)
    k = jnp.dot(ckv, wk_ref[...], preferred_element_type=F32)
    kr = _rope128(kr_ref[0].astype(F32), cos, sa, sb, h)
    for hd in range(MLA_HEADS):
        sl = slice(hd * LANES, (hd + 1) * LANES)
        q_ref[0, :, sl] = (_rope128(q[:, sl], cos, sa, sb, h) * MLA_SCALE).astype(BF16)
        k_ref[0, :, sl] = (k[:, sl] + kr).astype(BF16)
    v_ref[0] = (jnp.dot(ckv, wv_ref[...], preferred_element_type=F32) + vone_ref[...]).astype(BF16)


def _mla_prep(p, norm_q, norm_kv, w_uq, w_ukv, tabs):
    b, s, _ = p.shape
    cos, sa, sb, h = tabs
    hq = MLA_NOPE + MLA_ROPE
    wq = jnp.zeros((MLA_Q_RANK, MLA_HEADS, LANES), F32).at[:, :, :hq].set(
        w_uq.reshape(MLA_Q_RANK, MLA_HEADS, hq)).reshape(MLA_Q_RANK, MLA_HEADS * LANES).astype(BF16)
    wkv = w_ukv.reshape(MLA_KV_RANK, MLA_HEADS, MLA_NOPE + MLA_V)
    wk = jnp.zeros((MLA_KV_RANK, MLA_HEADS, LANES), F32).at[:, :, :MLA_NOPE].set(
        wkv[:, :, :MLA_NOPE]).reshape(MLA_KV_RANK, MLA_HEADS * LANES).astype(BF16)
    wv_pairs = wkv[:, :, MLA_NOPE:].reshape(MLA_KV_RANK, MLA_HEADS // 2, 2, MLA_V)
    gap = ((0, 0), (0, 0), (0, LANES - MLA_V))
    wv = jnp.stack([jnp.pad(wv_pairs[:, :, 0], gap), jnp.pad(wv_pairs[:, :, 1], gap[:2] + (gap[2][::-1],))],
                   axis=2).reshape(MLA_KV_RANK, MLA_HEADS * LANES).astype(BF16)
    lane_id = np.arange(MLA_HEADS * LANES) % (2 * LANES)
    vone = jnp.asarray((lane_id == MLA_ONE_LANE[0]) | (lane_id == LANES + MLA_ONE_LANE[1]), F32)[None, :]
    full = lambda shape: pl.BlockSpec(shape, lambda bi, i: (0,) * len(shape))
    tab = pl.BlockSpec((TM, LANES), lambda bi, i: (i, 0))
    return pl.pallas_call(
        functools.partial(_mla_prep_kernel, h=h),
        grid=(b, s // TM),
        in_specs=[pl.BlockSpec((1, TM, MLA_Q_RANK), lambda bi, i: (bi, i, P_CQ // MLA_Q_RANK)),
                  pl.BlockSpec((1, TM, LANES), lambda bi, i: (bi, i, P_CKV // LANES)),
                  pl.BlockSpec((1, TM, LANES), lambda bi, i: (bi, i, P_KR // LANES)),
                  full((1, MLA_Q_RANK)), full((1, MLA_KV_RANK)),
                  full(wq.shape), full(wk.shape), full(wv.shape), full(vone.shape), tab, tab, tab],
        out_specs=[pl.BlockSpec((1, TM, MLA_HEADS * LANES), lambda bi, i: (bi, i, 0))] * 3,
        out_shape=[jax.ShapeDtypeStruct((b, s, MLA_HEADS * LANES), BF16)] * 3,
        compiler_params=_cparams(("parallel", "parallel")),
        name="mla_prep",
    )(p, p, p, norm_q.reshape(1, -1), norm_kv.reshape(1, -1), wq, wk, wv, vone, cos, sa, sb)


MLA_Q_TILES = 1


def _mla_attn_kernel(*refs, n_ctx, n_lat_steps):
    q_refs = refs[:MLA_Q_TILES]
    k_ref, v_ref, o_ref = refs[MLA_Q_TILES:]

    def attend(n_keys):
        outs = []
        for hd in range(2):
            sl = slice(hd * LANES, (hd + 1) * LANES)
            q = jnp.concatenate([q_ref[0, :, sl] for q_ref in q_refs], axis=0)
            s = lax.dot_general(q, k_ref[0, :n_keys, sl], (((1,), (1,)), ((), ())),
                                preferred_element_type=F32)
            e = jnp.exp2(s - jnp.max(s, axis=-1, keepdims=True)).astype(BF16)
            o = jnp.dot(e, v_ref[0, :n_keys, sl], preferred_element_type=F32)
            one = MLA_ONE_LANE[hd]
            outs.append(o / o[:, one:one + 1])
        lane = lax.broadcasted_iota(jnp.int32, outs[0].shape, 1)
        o_ref[0] = jnp.where(lane < MLA_V, outs[0], outs[1]).astype(BF16)

    @pl.when(pl.program_id(2) < n_lat_steps)
    def _():
        attend(k_ref.shape[1])

    @pl.when(pl.program_id(2) >= n_lat_steps)
    def _():
        attend(n_ctx)


def _mla_attention(q, k, v, *, n_ctx, with_ctx):
    b, s, _ = q.shape
    nct = n_ctx // TM
    n_lat = s - n_ctx
    n_lat_steps = n_lat // (MLA_Q_TILES * TM)
    assert n_lat % (MLA_Q_TILES * TM) == 0 and (nct == 1 or not with_ctx)

    def q_spec(t):
        return pl.BlockSpec((1, TM, 2 * LANES), lambda bi, hp, j: (
            bi, jnp.where(j < n_lat_steps, nct + j * MLA_Q_TILES + t, 0), hp))

    kv = pl.BlockSpec((1, s, 2 * LANES), lambda bi, hp, j: (bi, 0, hp))
    return pl.pallas_call(
        functools.partial(_mla_attn_kernel, n_ctx=n_ctx, n_lat_steps=n_lat_steps),
        grid=(b, MLA_HEADS // 2, n_lat_steps + (1 if with_ctx else 0)),
        in_specs=[*[q_spec(t) for t in range(MLA_Q_TILES)], kv, kv],
        out_specs=pl.BlockSpec((1, MLA_Q_TILES * TM, LANES), lambda bi, hp, j: (bi, j, hp)),
        out_shape=jax.ShapeDtypeStruct((b, n_lat + (n_ctx if with_ctx else 0), MLA_HEADS * MLA_V), BF16),
        compiler_params=_cparams(("parallel", "parallel", "parallel")),
        name="mla_attention",
    )(*([q] * MLA_Q_TILES), k, v)


SWA_SCALE = SWA_HEAD ** -0.5 * LOG2E
SWA_TQ = 128
SWA_ONE_LANE = (SWA_HEAD, 0)


def _swa_prep_kernel(q_ref, k_ref, v_ref, cos_ref, sa_ref, sb_ref, qo_ref, ko_ref, vo_ref, *, h):
    cos, sa, sb = cos_ref[...], sa_ref[...], sb_ref[...]
    lane = lax.broadcasted_iota(jnp.int32, cos.shape, 1)
    low = lane < SWA_HEAD
    for j in range(SWA_HEADS // 2):
        blk = _rope128(q_ref[0, :, j * LANES:(j + 1) * LANES].astype(F32), cos, sa, sb, h) * SWA_SCALE
        qo_ref[0, :, (2 * j) * LANES:(2 * j + 1) * LANES] = jnp.where(low, blk, 0.0).astype(BF16)
        qo_ref[0, :, (2 * j + 1) * LANES:(2 * j + 2) * LANES] = jnp.where(
            low, pltpu.roll(blk, SWA_HEAD, axis=1), 0.0).astype(BF16)
    kb = _rope128(k_ref[0].astype(F32), cos, sa, sb, h)
    ko_ref[0, :, :LANES] = jnp.where(low, kb, 0.0).astype(BF16)
    ko_ref[0, :, LANES:] = jnp.where(low, pltpu.roll(kb, SWA_HEAD, axis=1), 0.0).astype(BF16)
    vb = v_ref[0].astype(F32)
    vr = pltpu.roll(vb, SWA_HEAD, axis=1)
    one_lo = jnp.where(lane == SWA_ONE_LANE[0], 1.0, 0.0)
    one_hi = jnp.where(lane == SWA_ONE_LANE[1], 1.0, 0.0)
    vo_ref[0, :, 0 * LANES:1 * LANES] = jnp.where(low, vb, one_lo).astype(BF16)
    vo_ref[0, :, 1 * LANES:2 * LANES] = jnp.where(low, one_hi, vr).astype(BF16)
    vo_ref[0, :, 2 * LANES:3 * LANES] = jnp.where(low, vr, one_lo).astype(BF16)
    vo_ref[0, :, 3 * LANES:4 * LANES] = jnp.where(low, one_hi, vb).astype(BF16)


def _swa_prep(p, tabs):
    b, s, _ = p.shape
    cos, sa, sb, h = tabs
    tab = pl.BlockSpec((TM, LANES), lambda bi, i: (i, 0))
    nq = SWA_HEADS * SWA_HEAD
    return pl.pallas_call(
        functools.partial(_swa_prep_kernel, h=h),
        grid=(b, s // TM),
        in_specs=[pl.BlockSpec((1, TM, nq), lambda bi, i: (bi, i, P_SWAQ // nq)),
                  pl.BlockSpec((1, TM, LANES), lambda bi, i: (bi, i, P_SWAK // LANES)),
                  pl.BlockSpec((1, TM, LANES), lambda bi, i: (bi, i, P_SWAV // LANES)),
                  tab, tab, tab],
        out_specs=[pl.BlockSpec((1, TM, SWA_HEADS * LANES), lambda bi, i: (bi, i, 0)),
                   pl.BlockSpec((1, TM, SWA_KV_HEADS * LANES), lambda bi, i: (bi, i, 0)),
                   pl.BlockSpec((1, TM, 4 * LANES), lambda bi, i: (bi, i, 0))],
        out_shape=[jax.ShapeDtypeStruct((b, s, SWA_HEADS * LANES), BF16),
                   jax.ShapeDtypeStruct((b, s, SWA_KV_HEADS * LANES), BF16),
                   jax.ShapeDtypeStruct((b, s, 4 * LANES), BF16)],
        compiler_params=_cparams(("parallel", "parallel")),
        name="swa_prep",
    )(p, p, p, cos, sa, sb)


def _swa_attn_kernel(sink_ref, q_ref, k_ref, v_ref, o_ref, *, n_ctx, q_off):
    i = pl.program_id(1) + q_off
    s_len = k_ref.shape[1]
    tq = SWA_TQ
    n_loc = tq + 2 * WINDOW
    r0 = i * tq
    is_lat = r0 >= n_ctx
    start = pl.multiple_of(jnp.clip(r0 - WINDOW, 0, s_len - n_loc), LANES)
    rows_g = SWA_GROUP * tq
    row = lax.broadcasted_iota(jnp.int32, (rows_g, n_loc), 0)
    qpos = r0 - n_ctx + row % tq
    kpos = start - n_ctx + lax.broadcasted_iota(jnp.int32, (rows_g, n_loc), 1)
    loc_ok = (jnp.abs(kpos - qpos) <= WINDOW) & (kpos >= 0) & is_lat
    k_loc = k_ref[0, pl.ds(start, n_loc), :]
    v_loc = v_ref[0, pl.ds(start, n_loc), :]
    k_ctx = k_ref[0, 0:n_ctx, :]
    v_ctx = v_ref[0, 0:n_ctx, :]
    head_row = lax.broadcasted_iota(jnp.int32, (rows_g, 1), 0) // tq
    lane = lax.broadcasted_iota(jnp.int32, (tq, LANES), 1)
    stages = []
    for g in range(SWA_KV_HEADS):
        q = jnp.concatenate([q_ref[0, :, hd * LANES:(hd + 1) * LANES]
                             for hd in range(g * SWA_GROUP, (g + 1) * SWA_GROUP)], axis=0)
        kg = slice(g * LANES, (g + 1) * LANES)
        s_loc = lax.dot_general(q, k_loc[:, kg], (((1,), (1,)), ((), ())), preferred_element_type=F32)
        s_ctx = lax.dot_general(q, k_ctx[:, kg], (((1,), (1,)), ((), ())), preferred_element_type=F32)
        sink = jnp.zeros((rows_g, 1), F32)
        for hh in range(SWA_GROUP):
            sink = jnp.where(head_row == hh, sink_ref[g * SWA_GROUP + hh] * LOG2E, sink)
        stages.append((jnp.where(loc_ok, s_loc, NEG_INF), s_ctx, sink))
    for g, (s_loc, s_ctx, sink) in enumerate(stages):
        m = jnp.maximum(jnp.maximum(jnp.max(s_loc, axis=-1, keepdims=True),
                                    jnp.max(s_ctx, axis=-1, keepdims=True)), sink)
        e = jnp.concatenate([jnp.exp2(s_loc - m), jnp.exp2(s_ctx - m)], axis=1).astype(BF16)
        e_sink = jnp.exp2(sink - m)
        outs = []
        for par in range(2):
            vg = slice((2 * g + par) * LANES, (2 * g + par + 1) * LANES)
            o = jnp.dot(e, jnp.concatenate([v_loc[:, vg], v_ctx[:, vg]], axis=0), preferred_element_type=F32)
            one = SWA_ONE_LANE[par]
            outs.append(o / (o[:, one:one + 1] + e_sink))
        for pi in range(SWA_GROUP // 2):
            even = outs[0][(2 * pi) * tq:(2 * pi + 1) * tq]
            odd = outs[1][(2 * pi + 1) * tq:(2 * pi + 2) * tq]
            blk = g * (SWA_GROUP // 2) + pi
            o_ref[0, :, blk * LANES:(blk + 1) * LANES] = jnp.where(lane < SWA_HEAD, even, odd).astype(BF16)


def _swa_attention(q, k, v, sink, *, n_ctx, q_off):
    b, s, _ = q.shape
    nq = s // SWA_TQ - q_off
    return pl.pallas_call(
        functools.partial(_swa_attn_kernel, n_ctx=n_ctx, q_off=q_off),
        grid=(b, nq),
        in_specs=[pl.BlockSpec(memory_space=pltpu.SMEM),
                  pl.BlockSpec((1, SWA_TQ, SWA_HEADS * LANES), lambda bi, i: (bi, i + q_off, 0)),
                  pl.BlockSpec((1, s, SWA_KV_HEADS * LANES), lambda bi, i: (bi, 0, 0)),
                  pl.BlockSpec((1, s, 4 * LANES), lambda bi, i: (bi, 0, 0))],
        out_specs=pl.BlockSpec((1, SWA_TQ, SWA_HEADS * SWA_HEAD), lambda bi, i: (bi, i, 0)),
        out_shape=jax.ShapeDtypeStruct((b, nq * SWA_TQ, SWA_HEADS * SWA_HEAD), BF16),
        compiler_params=_cparams(("parallel", "parallel")),
        name="swa_attention",
    )(sink, q, k, v)


N_PAIR = RWKV_HEADS // 2
N_DOUBLINGS = int(math.log2(CHUNK))


def _softplus(x):
    return jnp.maximum(x, 0.0) + jnp.log(1.0 + jnp.exp(-jnp.abs(x)))


def _headsum(x, bd):
    hi = x.astype(BF16)
    lo = (x - hi.astype(F32)).astype(BF16)
    return (jnp.dot(hi, bd, preferred_element_type=F32) + jnp.dot(lo, bd, preferred_element_type=F32))


def _chunk_cumsum(x, reverse):
    rows = lax.broadcasted_iota(jnp.int32, x.shape, 0)
    s = 1
    while s < CHUNK:
        if reverse:
            x = x + jnp.where(rows < CHUNK - s, pltpu.roll(x, CHUNK - s, axis=0), 0.0)
        else:
            x = x + jnp.where(rows >= s, pltpu.roll(x, s, axis=0), 0.0)
        s *= 2
    return x


def _head_rows(x):
    first = lax.broadcasted_iota(jnp.int32, x.shape, 1) < RWKV_HEAD
    return jnp.concatenate([jnp.where(first, x, 0.0), jnp.where(first, 0.0, x)], axis=0)


def _rwkv_chunk_kernel(r_ref, k_ref, v_ref, lo_ref, kvec_ref, w0_ref, a0_ref, wup_ref, aup_ref, gup_ref,
                       rk_ref, bd_ref, rbar_ref, y0_ref, a_ref, g_ref, bonus_ref, gate_ref):
    r = r_ref[0].astype(F32)
    k = k_ref[0].astype(F32)
    v = v_ref[0].astype(F32)
    lora = lo_ref[0].astype(F32)
    bd = bd_ref[...]
    kk = k * kvec_ref[0:1]
    kk = kk * lax.rsqrt(_headsum(kk * kk, bd) + 1e-12)
    gate_ref[0] = _mm(_sigmoid(lora), gup_ref[...])
    tanh_lo = jnp.tanh(lora)

    trow = lax.broadcasted_iota(jnp.int32, (2 * CHUNK, 4 * CHUNK), 0) % CHUNK
    tcol = lax.broadcasted_iota(jnp.int32, (2 * CHUNK, 4 * CHUNK), 1) % CHUNK
    sq_r = lax.broadcasted_iota(jnp.int32, (LANES, LANES), 0)
    sq_c = lax.broadcasted_iota(jnp.int32, (LANES, LANES), 1)
    same_head = (sq_r // RWKV_HEAD) == (sq_c // RWKV_HEAD)
    eye = sq_r == sq_c

    k_sum = None
    chains = []
    for d in range(2):
        reverse = d == 1
        w_log = -_softplus(-(w0_ref[d] + _mm(tanh_lo, wup_ref[d]))) - 0.5
        ld = -jnp.exp(w_log)
        a = _sigmoid(a0_ref[d] + _mm(lora, aup_ref[d]))
        k_d = k * (1.0 + (a - 1.0) * kvec_ref[1:2])
        k_sum = k_d if k_sum is None else k_sum + k_d
        b_d = kk * a
        lg = _chunk_cumsum(ld, reverse)
        last = 0 if reverse else CHUNK - 1
        tot = lg[last:last + 1]
        e_neg = jnp.exp(-lg)
        e_end = jnp.exp(tot - lg)
        z_t = -kk * jnp.exp(lg - ld)
        r_t = r * jnp.exp(lg)
        b_t = b_d * e_neg
        k_t = k_d * e_neg
        b_e = b_d * e_end
        k_e = k_d * e_end
        e_tot = jnp.exp(tot)
        before = (tcol > trow) if reverse else (tcol < trow)
        before_eq = (tcol >= trow) if reverse else (tcol <= trow)
        for pr in range(N_PAIR):
            sl = slice(pr * LANES, (pr + 1) * LANES)
            ch = {"d": d, "sl": sl, "rp": r_t[:, sl], "vp": v[:, sl], "e_tot": e_tot[:, sl],
                  "be_ke": jnp.concatenate([b_e[:, sl], k_e[:, sl]], axis=0)}
            zst, rst, vst = _head_rows(z_t[:, sl]), _head_rows(r_t[:, sl]), _head_rows(v[:, sl])
            bkst = jnp.concatenate([_head_rows(b_t[:, sl]), _head_rows(k_t[:, sl])], axis=0)
            ch["lz"] = jnp.where(before, _mm_nt(zst, bkst), 0.0)
            ch["lr"] = jnp.where(before_eq, _mm_nt(rst, bkst), 0.0)
            ch["zst"], ch["vst"] = zst, vst
            chains.append(ch)
    bonus_ref[0] = _headsum(r * (0.5 * k_sum) * rk_ref[...], bd) * v

    unit = jnp.where(eye, 1.0, 0.0)
    for ch in chains:
        ch["pw"] = ch["lz"][:, :LANES]
        ch["t"] = unit + ch["pw"]
        ch["x"] = jnp.concatenate([ch["zst"], _mm(ch["lz"][:, LANES:], ch["vst"])], axis=1)
    for it in range(1, N_DOUBLINGS):
        for ch in chains:
            ch["pw"] = _mm(ch["pw"], ch["pw"])
        for ch in chains:
            ch["t"] = ch["t"] + _mm(ch["pw"], ch["t"])
    for ch in chains:
        ch["x"] = _mm(ch["t"], ch["x"])
    for ch in chains:
        low = jnp.concatenate([jnp.zeros_like(ch["vst"]), ch["vst"]], axis=1)
        op = _mm(ch["lr"], jnp.concatenate([ch["x"], low], axis=0))
        ch["op"] = op[:CHUNK] + op[CHUNK:]
        ch["xp"] = ch["x"][:CHUNK] + ch["x"][CHUNK:]
    for ch in chains:
        d, sl = ch["d"], ch["sl"]
        rbar_ref[d, 0, :, sl] = ch["rp"] + ch["op"][:, :LANES]
        y0_ref[d, 0, :, sl] = ch["op"][:, LANES:]
        rhs = jnp.concatenate([ch["xp"], jnp.concatenate([jnp.zeros_like(ch["vp"]), ch["vp"]], axis=1)], axis=0)
        ag = _mm_tn(ch["be_ke"], rhs)
        a_full = ag[:, :LANES] + jnp.where(eye, jnp.broadcast_to(ch["e_tot"], (LANES, LANES)), 0.0)
        a_ref[d, 0, :, sl] = jnp.where(same_head, a_full, 0.0)
        g_ref[d, 0, :, sl] = jnp.where(same_head, ag[:, LANES:], 0.0)


def _head_block_diag():
    idx = np.arange(RWKV_DIM) // RWKV_HEAD
    return jnp.asarray(idx[:, None] == idx[None, :], BF16)


def _rwkv_chunks(p, kvec, w0, a0, w_up, a_up, g_up, r_k):
    b, s, _ = p.shape
    nc = s // CHUNK
    lora_w = DECAY_LORA + AAA_LORA + GATE_LORA
    wup = jnp.zeros((2, lora_w, RWKV_DIM), F32).at[:, :DECAY_LORA].set(w_up).astype(BF16)
    aup = jnp.zeros((2, lora_w, RWKV_DIM), F32).at[:, DECAY_LORA:DECAY_LORA + AAA_LORA].set(a_up).astype(BF16)
    gup = jnp.zeros((lora_w, RWKV_DIM), F32).at[DECAY_LORA + AAA_LORA:].set(g_up).astype(BF16)
    full = lambda shape: pl.BlockSpec(shape, lambda bi, c: (0,) * len(shape))
    col = lambda off: pl.BlockSpec((1, CHUNK, RWKV_DIM), lambda bi, c: (bi, c, off // RWKV_DIM))
    per_tok = pl.BlockSpec((2, 1, CHUNK, RWKV_DIM), lambda bi, c: (0, bi, c, 0))
    per_chunk = pl.BlockSpec((2, 1, 2 * CHUNK, RWKV_DIM), lambda bi, c: (0, bi, c, 0))
    tok = pl.BlockSpec((1, CHUNK, RWKV_DIM), lambda bi, c: (bi, c, 0))
    f32 = lambda *shape: jax.ShapeDtypeStruct(shape, F32)
    return pl.pallas_call(
        _rwkv_chunk_kernel,
        grid=(b, nc),
        in_specs=[col(P_RKV), col(P_RKV + RWKV_DIM), col(P_RKV + 2 * RWKV_DIM),
                  pl.BlockSpec((1, CHUNK, lora_w), lambda bi, c: (bi, c, P_LORA // lora_w)),
                  full((2, RWKV_DIM)), full((2, 1, RWKV_DIM)), full((2, 1, RWKV_DIM)),
                  full(wup.shape), full(aup.shape), full(gup.shape), full((1, RWKV_DIM)),
                  full((RWKV_DIM, RWKV_DIM))],
        out_specs=[per_tok, per_tok, per_chunk, per_chunk, tok, tok],
        out_shape=[f32(2, b, s, RWKV_DIM), f32(2, b, s, RWKV_DIM), f32(2, b, 2 * s, RWKV_DIM),
                   f32(2, b, 2 * s, RWKV_DIM), f32(b, s, RWKV_DIM), f32(b, s, RWKV_DIM)],
        compiler_params=_cparams(("parallel", "parallel")),
        name="rwkv_chunks",
    )(p, p, p, p, kvec, w0.reshape(2, 1, -1), a0.reshape(2, 1, -1), wup, aup, gup, r_k.reshape(1, -1),
      _head_block_diag())


RWKV_TS = 256


def _rwkv_scan_kernel(rbar_ref, y0_ref, a_ref, g_ref, y_ref, s_ref):
    d = pl.program_id(1)

    @pl.when(pl.program_id(2) == 0)
    def _():
        s_ref[...] = jnp.zeros_like(s_ref)

    n_sub = RWKV_TS // CHUNK
    for cc in range(n_sub):
        ci = jnp.where(d == 0, cc, n_sub - 1 - cc)
        rows = pl.ds(pl.multiple_of(ci * CHUNK, CHUNK), CHUNK)
        rows2 = pl.ds(pl.multiple_of(ci * 2 * CHUNK, 2 * CHUNK), 2 * CHUNK)
        for pr in range(N_PAIR):
            sl = slice(pr * LANES, (pr + 1) * LANES)
            st = s_ref[pr]
            y_ref[0, 0, rows, sl] = _mm_f32(rbar_ref[0, 0, rows, sl], st) + y0_ref[0, 0, rows, sl]
            s_ref[pr] = _mm_f32(a_ref[0, 0, rows2, sl], st) + g_ref[0, 0, rows2, sl]


def _rwkv_scan(rbar, y0, a, g, *, n_ctx):
    _, b, s, _ = rbar.shape
    nt = s // RWKV_TS
    nct = n_ctx // RWKV_TS

    def tile(d, j):
        back = jnp.where(j < nct, nct - 1 - j, nt - 1 - (j - nct))
        return jnp.where(d == 0, j, back)

    tok = pl.BlockSpec((1, 1, RWKV_TS, RWKV_DIM), lambda bi, d, j: (d, bi, tile(d, j), 0))
    chk = pl.BlockSpec((1, 1, 2 * RWKV_TS, RWKV_DIM), lambda bi, d, j: (d, bi, tile(d, j), 0))
    return pl.pallas_call(
        _rwkv_scan_kernel,
        grid=(b, 2, nt),
        in_specs=[tok, tok, chk, chk],
        out_specs=tok,
        out_shape=jax.ShapeDtypeStruct((2, b, s, RWKV_DIM), F32),
        scratch_shapes=[pltpu.VMEM((N_PAIR, LANES, LANES), F32)],
        compiler_params=_cparams(("parallel", "parallel", "arbitrary")),
        name="rwkv_scan",
    )(rbar, y0, a, g)


def _rwkv_readout_kernel(yf_ref, yb_ref, bonus_ref, gate_ref, lng_ref, lnb_ref, bd_ref, o_ref):
    bd = bd_ref[...]
    y = yf_ref[0, 0] + yb_ref[0, 0]
    inv_n = 1.0 / RWKV_HEAD
    dev = y - _headsum(y, bd) * inv_n
    var = _headsum(dev * dev, bd) * inv_n
    yn = dev * lax.rsqrt(var + RWKV_LN_EPS) * lng_ref[...] + lnb_ref[...]
    o_ref[0] = ((yn + bonus_ref[0]) * gate_ref[0]).astype(BF16)


def _rwkv_readout(y, bonus, gate, ln_g, ln_b, *, row_off):
    _, b, s, _ = y.shape
    nt = s // TM - row_off
    full = lambda shape: pl.BlockSpec(shape, lambda bi, i: (0,) * len(shape))
    tok = pl.BlockSpec((1, TM, RWKV_DIM), lambda bi, i: (bi, i + row_off, 0))
    return pl.pallas_call(
        _rwkv_readout_kernel,
        grid=(b, nt),
        in_specs=[pl.BlockSpec((1, 1, TM, RWKV_DIM), lambda bi, i: (0, bi, i + row_off, 0)),
                  pl.BlockSpec((1, 1, TM, RWKV_DIM), lambda bi, i: (1, bi, i + row_off, 0)),
                  tok, tok, full((1, RWKV_DIM)), full((1, RWKV_DIM)), full((RWKV_DIM, RWKV_DIM))],
        out_specs=pl.BlockSpec((1, TM, RWKV_DIM), lambda bi, i: (bi, i, 0)),
        out_shape=jax.ShapeDtypeStruct((b, nt * TM, RWKV_DIM), BF16),
        compiler_params=_cparams(("parallel", "parallel")),
        name="rwkv_readout",
    )(y, y, bonus, gate, ln_g.reshape(1, -1), ln_b.reshape(1, -1), _head_block_diag())


DFT_N2 = LANES
HY_MIN_LEN = 1024
HY_CT = 128


def _hyena_mlp_kernel(feats_ref, w1_ref, b1_ref, w2_ref, b2_ref, freq_ref, h_ref):
    h = jnp.sin(freq_ref[0:1] * (_mm_f32(feats_ref[0], w1_ref[...]) + b1_ref[...]))
    h_ref[0] = jnp.sin(freq_ref[1:2] * (_mm_f32(h, w2_ref[...]) + b2_ref[...]))


def _hyena_filter_kernel(h_ref, w3f_ref, w3b_ref, t_ref, delta_ref, k_ref):
    hf = _mm_f32(h_ref[0], w3f_ref[...]) * jnp.exp(-t_ref[0] * delta_ref[...])
    hb = _mm_f32(h_ref[1], w3b_ref[...]) * jnp.exp(-t_ref[1] * delta_ref[...])
    norm = (jnp.sum(jnp.abs(hf), axis=0, keepdims=True) + jnp.sum(jnp.abs(hb), axis=0, keepdims=True))
    r = pl.program_id(2)
    rows = lax.broadcasted_iota(jnp.int32, hf.shape, 0)
    tail = jnp.where(rows == 0, 0.0, hb)
    blk = jnp.where(r == 0, hf, jnp.where(r == pl.num_programs(2) - 1, tail, 0.0))
    k_ref[0] = blk / norm


def _hyena_filter_buffer(n, nc, w1, b1, w2, b2, w3, freq):
    lag = jnp.stack([jnp.arange(n), jnp.where(jnp.arange(n) == 0, 0, n - jnp.arange(n))]).astype(F32)
    t = (lag / (n - 1))[:, :, None]
    bands = jnp.linspace(1e-4, HYENA_BANDS - 1, HYENA_BANDS, dtype=F32)
    ang = (2.0 * math.pi / n) * lag[:, :, None] * bands[None, None, :]
    feats = jnp.concatenate([t, jnp.cos(ang), -jnp.sin(ang),
                             jnp.zeros((2, n, HYENA_FW - HYENA_EMB), F32)], axis=-1)
    w1p = jnp.zeros((HYENA_FW, HYENA_FW), F32).at[:HYENA_EMB].set(w1)
    deltas = jnp.abs(jnp.linspace(math.log(HYENA_TARGET) / HYENA_SLOW,
                                  math.log(HYENA_TARGET) / HYENA_FAST, HYENA_DIM, dtype=F32))[None, :]
    fixed = lambda shape: pl.BlockSpec(shape, lambda d: (0,) * len(shape))
    hidden = pl.pallas_call(
        _hyena_mlp_kernel,
        grid=(2,),
        in_specs=[pl.BlockSpec((1, n, HYENA_FW), lambda d: (d, 0, 0)), fixed((HYENA_FW, HYENA_FW)),
                  fixed((1, HYENA_FW)), fixed((HYENA_FW, HYENA_FW)), fixed((1, HYENA_FW)),
                  fixed((2, HYENA_FW))],
        out_specs=pl.BlockSpec((1, n, HYENA_FW), lambda d: (d, 0, 0)),
        out_shape=jax.ShapeDtypeStruct((2, n, HYENA_FW), F32),
        compiler_params=_cparams(("parallel",)),
        name="hyena_filter_mlp",
    )(feats, w1p, b1.reshape(1, -1), w2, b2.reshape(1, -1), freq)
    tc = 256
    nj = HYENA_DIM // tc
    full = lambda shape: pl.BlockSpec(shape, lambda o, j, r: (0,) * len(shape))
    return pl.pallas_call(
        _hyena_filter_kernel,
        grid=(HYENA_ORDER, nj, nc // n),
        in_specs=[full((2, n, HYENA_FW)),
                  pl.BlockSpec((HYENA_FW, tc), lambda o, j, r: (0, o * 2 * nj + j)),
                  pl.BlockSpec((HYENA_FW, tc), lambda o, j, r: (0, o * 2 * nj + nj + j)),
                  full((2, n, 1)), pl.BlockSpec((1, tc), lambda o, j, r: (0, j))],
        out_specs=pl.BlockSpec((1, n, tc), lambda o, j, r: (o, r, j)),
        out_shape=jax.ShapeDtypeStruct((HYENA_ORDER, nc, HYENA_DIM), F32),
        compiler_params=_cparams(("parallel", "parallel", "parallel")),
        name="hyena_filters",
    )(hidden, w3, w3, t, deltas)


def _dft_tables(n1):
    nc = n1 * DFT_N2
    t2 = np.arange(DFT_N2)[:, None, None]
    f1 = np.arange(n1)[None, :, None]
    t1 = np.arange(n1)[None, None, :]
    theta = 2.0 * np.pi * ((f1 * (DFT_N2 * t1 + t2)) % nc) / nc
    g_fwd = np.concatenate([np.cos(theta), -np.sin(theta)], axis=1)
    g_inv = np.concatenate([np.cos(theta), -np.sin(theta)], axis=1).transpose(0, 2, 1) / nc
    k = np.arange(DFT_N2)
    phi = 2.0 * np.pi * ((k[:, None] * k[None, :]) % DFT_N2) / DFT_N2
    c, s = np.cos(phi), np.sin(phi)
    f_fwd = np.block([[c, s], [-s, c]])
    f_inv = np.block([[c, -s], [s, c]])
    return tuple(jnp.asarray(t, BF16) for t in (g_fwd, g_inv, f_fwd, f_inv))


def _dft_rows_in(x_ref, g_ref, a_ref, t1n, n1, t1_valid):
    keep = lax.broadcasted_iota(jnp.int32, (t1n, x_ref.shape[-1]), 0) < t1_valid
    for t2 in range(DFT_N2):
        xs = x_ref[0, pl.ds(t2, t1n, stride=DFT_N2), :]
        if t1_valid < t1n:
            xs = jnp.where(keep, xs, 0.0)
        a_ref[pl.ds(t2, 2 * n1, stride=DFT_N2), :] = _mm(g_ref[t2], xs)


def _slab(f1):
    return slice(f1 * DFT_N2, (f1 + 1) * DFT_N2)


def _spectrum_kernel(x_ref, g_ref, ff_ref, k_ref, a_ref, *, n1):
    _dft_rows_in(x_ref, g_ref, a_ref, n1, n1, n1)
    for f1 in range(n1):
        a = jnp.concatenate([a_ref[_slab(f1)], a_ref[_slab(n1 + f1)]], axis=0)
        k_ref[0, f1] = _mm(ff_ref[...], a)


def _filter_spectrum(kbuf, g_fwd, f_fwd, n1):
    no, nc, _ = kbuf.shape
    nj = HYENA_DIM // HY_CT
    return pl.pallas_call(
        functools.partial(_spectrum_kernel, n1=n1),
        grid=(no, nj),
        in_specs=[pl.BlockSpec((1, nc, HY_CT), lambda o, j: (o, 0, j)),
                  pl.BlockSpec(g_fwd.shape, lambda o, j: (0, 0, 0)),
                  pl.BlockSpec(f_fwd.shape, lambda o, j: (0, 0))],
        out_specs=pl.BlockSpec((1, n1, 2 * DFT_N2, HY_CT), lambda o, j: (o, 0, 0, j)),
        out_shape=jax.ShapeDtypeStruct((no, n1, 2 * DFT_N2, HYENA_DIM), F32),
        scratch_shapes=[pltpu.VMEM((2 * n1 * DFT_N2, HY_CT), F32)],
        compiler_params=_cparams(("parallel", "parallel")),
        name="hyena_filter_spectrum",
    )(kbuf, g_fwd, f_fwd)


def _hyena_conv_kernel(x_ref, gate_ref, k_ref, g1_ref, ff_ref, fi_ref, g3_ref, bias_ref, o_ref, a_ref,
                       *, t1n, n1, t1_valid):
    _dft_rows_in(x_ref, g1_ref, a_ref, t1n, n1, t1_valid)
    for f1 in range(n1):
        re, im = _slab(f1), _slab(n1 + f1)
        x = _mm(ff_ref[...], jnp.concatenate([a_ref[re], a_ref[im]], axis=0))
        xre, xim = x[:DFT_N2], x[DFT_N2:]
        kre, kim = k_ref[0, f1, :DFT_N2], k_ref[0, f1, DFT_N2:]
        bm = _mm(fi_ref[...], jnp.concatenate([xre * kre - xim * kim, xre * kim + xim * kre], axis=0))
        a_ref[re] = bm[:DFT_N2]
        a_ref[im] = bm[DFT_N2:]
    bias = bias_ref[...]
    for t2 in range(DFT_N2):
        y = _mm(g3_ref[t2], a_ref[pl.ds(t2, 2 * n1, stride=DFT_N2), :])
        rows = pl.ds(t2, t1n, stride=DFT_N2)
        o_ref[0, rows, :] = gate_ref[0, rows, :] * (y + bias * x_ref[0, rows, :])


def _hyena_conv(z, z_spec, gate, gate_spec, kspec, order, bias, tabs, n1, n_pad, n):
    bx = z.shape[0]
    t1n = n_pad // DFT_N2
    g_fwd, g_inv, f_fwd, f_inv = tabs
    g1 = g_fwd[:, :, :t1n]
    g3 = g_inv[:, :t1n, :]
    const = lambda a: pl.BlockSpec(a.shape, lambda j, bi: (0,) * a.ndim)
    return pl.pallas_call(
        functools.partial(_hyena_conv_kernel, t1n=t1n, n1=n1, t1_valid=-(-n // DFT_N2)),
        grid=(HYENA_DIM // HY_CT, bx),
        in_specs=[z_spec, gate_spec,
                  pl.BlockSpec((1, n1, 2 * DFT_N2, HY_CT), lambda j, bi: (order, 0, 0, j)),
                  const(g1), const(f_fwd), const(f_inv), const(g3),
                  pl.BlockSpec((1, HY_CT), lambda j, bi: (0, j))],
        out_specs=pl.BlockSpec((1, n_pad, HY_CT), lambda j, bi: (bi, 0, j)),
        out_shape=jax.ShapeDtypeStruct((bx, n_pad, HYENA_DIM), F32),
        scratch_shapes=[pltpu.VMEM((2 * n1 * DFT_N2, HY_CT), F32)],
        compiler_params=_cparams(("parallel", "parallel")),
        name="hyena_conv",
    )(z, gate, kspec, g1, f_fwd, f_inv, g3, bias.reshape(1, -1))


def _hyena_operator(p_hy, row0, n, filt_params, bias):
    n_pad = max(n, HY_MIN_LEN)
    nc = 2 * n_pad
    n1 = nc // DFT_N2
    tabs = _dft_tables(n1)
    kspec = _filter_spectrum(_hyena_filter_buffer(n, nc, *filt_params), tabs[0], tabs[2], n1)
    nj = HYENA_DIM // HY_CT
    window = lambda part: pl.BlockSpec((pl.Element(1), pl.Element(n_pad), pl.Element(HY_CT)),
                                       lambda j, bi: (bi, row0, (part * nj + j) * HY_CT))
    own = pl.BlockSpec((1, n_pad, HY_CT), lambda j, bi: (bi, 0, j))
    z = _hyena_conv(p_hy, window(0), p_hy, window(1), kspec, 0, bias[0], tabs, n1, n_pad, n)
    return _hyena_conv(z, own, p_hy, window(2), kspec, 1, bias[1], tabs, n1, n_pad, n)


def _merge_kernel(x_ref, mod_ref, g_ref, gates_ref, bg_ref, ya_ref, yb_ref, yh_ref, yhc_ref, yd_ref,
                  wb_ref, wo_ref, o_ref, *, n_ctx_tiles, row_off):
    is_ctx = pl.program_id(1) + row_off < n_ctx_tiles
    yh = jnp.where(is_ctx, yhc_ref[0], yh_ref[0])
    merged = None
    for br, y in enumerate((ya_ref[0], yb_ref[0], yh, yd_ref[0])):
        gate = _sigmoid(gates_ref[0, :, br * D_MODEL:(br + 1) * D_MODEL] + bg_ref[br:br + 1])
        term = gate * jnp.dot(y.astype(BF16), wb_ref[br], preferred_element_type=F32)
        merged = term if merged is None else merged + term
    y = jnp.dot(merged.astype(BF16), wo_ref[...], preferred_element_type=F32)
    o_ref[0] = x_ref[0] + mod_ref[0, 5:6] * _rms(y, g_ref[3:4])


def _merge(x, mods, norm_g, p, b_gate, ya, yb, yh, yh_ctx, yd, w_branch, w_out, *, n_ctx_tiles, row_off):
    b, s, _ = x.shape
    nt = s // TM - row_off
    n_lat = mods.shape[0] - 1
    br = lambda: pl.BlockSpec((1, TM, BRANCH_DIM), lambda bi, i: (bi, i, 0))
    n_lat_tiles = s // TM - n_ctx_tiles
    br_a = pl.BlockSpec((1, TM, BRANCH_DIM), lambda bi, i: (
        bi, jnp.where(i + row_off < n_ctx_tiles, n_lat_tiles + i, i + row_off - n_ctx_tiles), 0))
    br_h = pl.BlockSpec((1, TM, BRANCH_DIM), lambda bi, i: (bi, jnp.maximum(i + row_off - n_ctx_tiles, 0), 0))
    br_hc = pl.BlockSpec((1, TM, BRANCH_DIM), lambda bi, i: (
        bi, jnp.minimum(i + row_off, max(n_ctx_tiles - 1, 0)) if yh_ctx is not None else 0, 0))
    return pl.pallas_call(
        functools.partial(_merge_kernel, n_ctx_tiles=n_ctx_tiles if yh_ctx is not None else 0, row_off=row_off),
        grid=(b, nt),
        in_specs=[pl.BlockSpec((1, TM, D_MODEL), lambda bi, i: (bi, i + row_off, 0)),
                  pl.BlockSpec((1, N_MOD, D_MODEL),
                               lambda bi, i: (jnp.where(i + row_off < n_ctx_tiles, n_lat, bi), 0, 0)),
                  pl.BlockSpec((6, D_MODEL), lambda bi, i: (0, 0)),
                  pl.BlockSpec((1, TM, GATE_COLS), lambda bi, i: (bi, i + row_off, 0)),
                  pl.BlockSpec((N_BRANCH, D_MODEL), lambda bi, i: (0, 0)),
                  br_a, br(), br_h, br_hc, br(),
                  pl.BlockSpec((N_BRANCH, BRANCH_DIM, D_MODEL), lambda bi, i: (0, 0, 0)),
                  pl.BlockSpec((D_MODEL, D_MODEL), lambda bi, i: (0, 0))],
        out_specs=pl.BlockSpec((1, TM, D_MODEL), lambda bi, i: (bi, i, 0)),
        out_shape=jax.ShapeDtypeStruct((b, nt * TM, D_MODEL), F32),
        compiler_params=_cparams(("parallel", "parallel")),
        name="merge_branches",
    )(x, mods, norm_g, p, b_gate, ya, yb, yh, yh if yh_ctx is None else yh_ctx, yd, w_branch, w_out)


def kernel(x, c, ctx, c_ctx, w_mod, b_mod, norm_g, ffn_w13, ffn_w2, w_in, b_gate, mla_norm_q, mla_norm_kv, mla_w_uq, mla_w_ukv, rwkv_mu, rwkv_w0, rwkv_w_up, rwkv_a0, rwkv_a_up, rwkv_g_up, rwkv_kvec, rwkv_r_k, rwkv_ln_g, rwkv_ln_b, hyena_conv, hyena_conv_b, hyena_w1, hyena_b1, hyena_w2, hyena_b2, hyena_w3, hyena_freq, hyena_bias, swa_sink, w_branch, w_out):
    b, n, _ = x.shape
    n_ctx = ctx.shape[1]
    nct = n_ctx // TM
    xall = jnp.concatenate([ctx, x], axis=1)
    c_all = jnp.concatenate([c, c_ctx[None]], axis=0)
    tabs_mla = _rope_tables(n, n_ctx, MLA_ROPE, MLA_NOPE, LANES)
    tabs_swa = _rope_tables(n, n_ctx, SWA_HEAD, 0, SWA_HEAD)
    depth = w_mod.shape[0]
    for l in range(depth):
        with_ctx = l + 1 < depth
        row_off = 0 if with_ctx else nct
        mods = _modulation(c_all, w_mod[l], b_mod[l])
        xall = _ffn(xall, mods, norm_g[l], ffn_w13[l, 0].astype(BF16), ffn_w2[l, 0].astype(BF16),
                    mod0=0, g0=0, n_ctx_tiles=nct)
        coef = _shift_coefficients(rwkv_mu[l], hyena_conv[l], hyena_conv_b[l])
        p, p_hy = _inproj(xall, mods, norm_g[l], _permute_w_in(w_in[l]).astype(BF16), coef, n_ctx_tiles=nct)
        q, k, v = _mla_prep(p, mla_norm_q[l], mla_norm_kv[l], mla_w_uq[l], mla_w_ukv[l], tabs_mla)
        ya = _mla_attention(q, k, v, n_ctx=n_ctx, with_ctx=with_ctx)
        rbar, y0, a, g, bonus, gate = _rwkv_chunks(p, rwkv_kvec[l], rwkv_w0[l], rwkv_a0[l], rwkv_w_up[l],
                                                   rwkv_a_up[l], rwkv_g_up[l], rwkv_r_k[l])
        y = _rwkv_scan(rbar, y0, a, g, n_ctx=n_ctx)
        yb = _rwkv_readout(y, bonus, gate, rwkv_ln_g[l], rwkv_ln_b[l], row_off=row_off)
        filt = (hyena_w1[l], hyena_b1[l], hyena_w2[l], hyena_b2[l], hyena_w3[l], hyena_freq[l])
        yh = _hyena_operator(p_hy, n_ctx, n, filt, hyena_bias[l])
        yh_ctx = _hyena_operator(p_hy, 0, n_ctx, filt, hyena_bias[l]) if with_ctx else None
        q, k, v = _swa_prep(p, tabs_swa)
        yd = _swa_attention(q, k, v, swa_sink[l], n_ctx=n_ctx, q_off=row_off * TM // SWA_TQ)
        xall = _merge(xall, mods, norm_g[l], p, b_gate[l], ya, yb, yh, yh_ctx, yd, w_branch[l].astype(BF16),
                      w_out[l].astype(BF16), n_ctx_tiles=nct, row_off=row_off)
        xall = _ffn(xall, mods, norm_g[l], ffn_w13[l, 1].astype(BF16), ffn_w2[l, 1].astype(BF16),
                    mod0=6, g0=4, n_ctx_tiles=nct - row_off)
    return xall
```

```python
import functools
import math

import numpy as np
import jax
import jax.numpy as jnp
from jax import lax
from jax.experimental import pallas as pl
from jax.experimental.pallas import tpu as pltpu

F32 = jnp.float32
BF16 = jnp.bfloat16

D_MODEL = 1024
GRID_W = 64
N_BRANCH = 4
N_MOD = 9
FF_DIM = 2816
EPS = 1e-6
ROPE_BASE = 10000.0
NEG_INF = -1e30
BRANCH_DIM = 512
MLA_HEADS = 8
MLA_NOPE = 64
MLA_ROPE = 32
MLA_V = 64
MLA_Q_RANK = 256
MLA_KV_RANK = 128
RWKV_HEADS = 8
RWKV_HEAD = 64
RWKV_DIM = RWKV_HEADS * RWKV_HEAD
DECAY_LORA = 64
AAA_LORA = 64
GATE_LORA = 128
RWKV_LN_EPS = 64e-5
HYENA_DIM = 512
HYENA_ORDER = 2
HYENA_EMB = 33
HYENA_BANDS = (HYENA_EMB - 1) // 2
HYENA_FW = 64
HYENA_TARGET = 1e-2
HYENA_FAST = 0.3
HYENA_SLOW = 1.5
SWA_HEADS = 8
SWA_KV_HEADS = 2
SWA_HEAD = 64
SWA_GROUP = SWA_HEADS // SWA_KV_HEADS
WINDOW = 128
GATE_COLS = N_BRANCH * D_MODEL
MLA_COLS = MLA_Q_RANK + MLA_KV_RANK + MLA_ROPE
RWKV_COLS = 3 * RWKV_DIM + DECAY_LORA + AAA_LORA + GATE_LORA
HYENA_COLS = 3 * HYENA_DIM
SWA_COLS = (SWA_HEADS + 2 * SWA_KV_HEADS) * SWA_HEAD

LANES = 128
V7X_VMEM_LIMIT = 56 * 1024 * 1024

TM = 256
FFN_SUB_TILES = 4
INPROJ_SUB_TILES = 2
FF_CHUNK = 256
IN_CHUNK = 512
CHUNK = 64

P_GATE = 0
P_RKV = 4096
P_SWAQ = 5632
P_LORA = 6144
P_CQ = 6400
P_CKV = 6656
P_KR = 6784
P_SWAK = 6912
P_SWAV = 7040
P_COLS = 7168
P_HY = 7168
W_COLS = P_HY + HYENA_COLS
_SHIFT_COLS = ((P_RKV, P_SWAQ), (P_LORA, P_CQ), (P_HY, W_COLS))
_SHIFT_CHUNKS = [any(lo < (j + 1) * IN_CHUNK and j * IN_CHUNK < hi for lo, hi in _SHIFT_COLS)
                 for j in range(W_COLS // IN_CHUNK)]


def _cparams(sem, vmem=V7X_VMEM_LIMIT):
    return pltpu.CompilerParams(dimension_semantics=sem, vmem_limit_bytes=vmem)


def _mm(a, b):
    return jnp.dot(a.astype(BF16), b.astype(BF16), preferred_element_type=F32)


def _mm_nt(a, b):
    return lax.dot_general(a.astype(BF16), b.astype(BF16), (((1,), (1,)), ((), ())),
                           preferred_element_type=F32)


def _mm_tn(a, b):
    return lax.dot_general(a.astype(BF16), b.astype(BF16), (((0,), (0,)), ((), ())),
                           preferred_element_type=F32)


def _mm_f32(a, b):
    return jnp.dot(a, b, preferred_element_type=F32, precision=lax.Precision.HIGHEST)


def _rms(x, g):
    return x * lax.rsqrt(jnp.mean(x * x, axis=-1, keepdims=True) + EPS) * g


def _sigmoid(x):
    return 1.0 / (1.0 + jnp.exp(-x))


def _mod_kernel(c_ref, w_ref, b_ref, o_ref):
    c = c_ref[...]
    o_ref[...] = _mm(c * _sigmoid(c), w_ref[...]) + b_ref[...]


def _modulation(c_all, w_mod, b_mod):
    r = c_all.shape[0]
    rp = -(-r // 8) * 8
    c_pad = jnp.zeros((rp, D_MODEL), F32).at[:r].set(c_all)
    tn = 1024
    out = pl.pallas_call(
        _mod_kernel,
        grid=(N_MOD * D_MODEL // tn,),
        in_specs=[pl.BlockSpec((rp, D_MODEL), lambda j: (0, 0)),
                  pl.BlockSpec((D_MODEL, tn), lambda j: (0, j)),
                  pl.BlockSpec((1, tn), lambda j: (0, j))],
        out_specs=pl.BlockSpec((rp, tn), lambda j: (0, j)),
        out_shape=jax.ShapeDtypeStruct((rp, N_MOD * D_MODEL), F32),
        compiler_params=_cparams(("arbitrary",)),
        name="modulation",
    )(c_pad, w_mod, b_mod.reshape(1, -1))
    return out[:r].reshape(r, N_MOD, D_MODEL)


def _sub_tiles(n_tiles, most):
    return max(g for g in range(1, most + 1) if n_tiles % g == 0)


def _mod_specs(n_sub, tiles_per_seq, n_ctx_tiles, ctx_row):
    def spec(k):
        def index(i):
            t = i * n_sub + k
            return (jnp.where(t % tiles_per_seq < n_ctx_tiles, ctx_row, t // tiles_per_seq), 0, 0)
        return pl.BlockSpec((1, N_MOD, D_MODEL), index)
    return [spec(k) for k in range(n_sub)]


def _ffn_kernel(x_ref, *refs, mod0, g0, n_sub):
    mod_refs = refs[:n_sub]
    g_ref, w13_ref, w2_ref, o_ref = refs[n_sub:]
    tiles = [slice(t * TM, (t + 1) * TM) for t in range(n_sub)]
    u = jnp.concatenate(
        [(_rms(x_ref[rows], g_ref[g0:g0 + 1]) * (1.0 + m[0, mod0 + 1:mod0 + 2]) + m[0, mod0:mod0 + 1]).astype(BF16)
         for rows, m in zip(tiles, mod_refs)], axis=0)
    acc = jnp.zeros(x_ref.shape, F32)
    for f in range(FF_DIM // FF_CHUNK):
        lo = f * FF_CHUNK
        a = jnp.dot(u, w13_ref[:, lo:lo + FF_CHUNK], preferred_element_type=F32)
        b = jnp.dot(u, w13_ref[:, FF_DIM + lo:FF_DIM + lo + FF_CHUNK], preferred_element_type=F32)
        h = (a * _sigmoid(a) * b).astype(BF16)
        acc = acc + jnp.dot(h, w2_ref[lo:lo + FF_CHUNK, :], preferred_element_type=F32)
    hn = _rms(acc, g_ref[g0 + 1:g0 + 2])
    for rows, m in zip(tiles, mod_refs):
        o_ref[rows] = x_ref[rows] + 0.5 * m[0, mod0 + 2:mod0 + 3] * hn[rows]


def _ffn(x, mods, norm_g, w13, w2, *, mod0, g0, n_ctx_tiles):
    b, s, _ = x.shape
    n_tiles = b * s // TM
    n_sub = _sub_tiles(n_tiles, FFN_SUB_TILES)
    rows = n_sub * TM
    out = pl.pallas_call(
        functools.partial(_ffn_kernel, mod0=mod0, g0=g0, n_sub=n_sub),
        grid=(n_tiles // n_sub,),
        in_specs=[pl.BlockSpec((rows, D_MODEL), lambda i: (i, 0)),
                  *_mod_specs(n_sub, s // TM, n_ctx_tiles, mods.shape[0] - 1),
                  pl.BlockSpec((6, D_MODEL), lambda i: (0, 0)),
                  pl.BlockSpec(memory_space=pltpu.VMEM),
                  pl.BlockSpec(memory_space=pltpu.VMEM)],
        out_specs=pl.BlockSpec((rows, D_MODEL), lambda i: (i, 0)),
        out_shape=jax.ShapeDtypeStruct((b * s, D_MODEL), F32),
        compiler_params=_cparams(("parallel",)),
        name="ffn_half_step",
    )(x.reshape(b * s, D_MODEL), *([mods] * n_sub), norm_g, w13, w2)
    return out.reshape(b, s, D_MODEL)


def _inproj_kernel(x_ref, xp_ref, xn_ref, *refs, n_sub, tiles_per_seq, n_ctx_tiles):
    mod_refs = refs[:n_sub]
    g_ref, w_ref, coef_ref, o_ref, hy_ref = refs[n_sub:]
    g = g_ref[2:3]

    def modulated(x, m):
        return (_rms(x, g) * (1.0 + m[0, 4:5]) + m[0, 3:4]).astype(BF16)

    u = jnp.concatenate([modulated(x_ref[t * TM:(t + 1) * TM], m) for t, m in enumerate(mod_refs)], axis=0)
    u_prev = modulated(xp_ref[...], mod_refs[0])
    u_next = modulated(xn_ref[...], mod_refs[-1])
    rows = lax.broadcasted_iota(jnp.int32, (n_sub * TM, 1), 0)
    keep_prev = jnp.ones((n_sub * TM, 1), F32)
    keep_next = jnp.ones((n_sub * TM, 1), F32)
    for t in range(n_sub):
        w = (pl.program_id(0) * n_sub + t) % tiles_per_seq
        seg_start = (w == 0) | (w == n_ctx_tiles)
        seg_end = (w == n_ctx_tiles - 1) | (w == tiles_per_seq - 1)
        keep_prev = jnp.where((rows == t * TM) & seg_start, 0.0, keep_prev)
        keep_next = jnp.where((rows == (t + 1) * TM - 1) & seg_end, 0.0, keep_next)
    for j in range(W_COLS // IN_CHUNK):
        cols = slice(j * IN_CHUNK, (j + 1) * IN_CHUNK)
        p = jnp.dot(u, w_ref[:, cols], preferred_element_type=F32)
        if _SHIFT_CHUNKS[j]:
            p_first = jnp.dot(u_prev, w_ref[:, cols], preferred_element_type=F32)[7:8]
            p_last = jnp.dot(u_next, w_ref[:, cols], preferred_element_type=F32)[0:1]
            prev = jnp.where(rows == 0, p_first, pltpu.roll(p, 1, axis=0)) * keep_prev
            nxt = jnp.where(rows == n_sub * TM - 1, p_last, pltpu.roll(p, n_sub * TM - 1, axis=0)) * keep_next
            p = (coef_ref[0:1, cols] * p + coef_ref[1:2, cols] * prev + coef_ref[2:3, cols] * nxt
                 + coef_ref[3:4, cols])
        if j * IN_CHUNK < P_COLS:
            o_ref[:, cols] = p.astype(BF16)
        else:
            hy_ref[:, j * IN_CHUNK - P_HY:(j + 1) * IN_CHUNK - P_HY] = p


def _inproj(x, mods, norm_g, w_in_p, coef, *, n_ctx_tiles):
    b, s, _ = x.shape
    n_tiles = b * s // TM
    n_sub = _sub_tiles(n_tiles, INPROJ_SUB_TILES)
    rows = n_sub * TM
    r8 = rows // 8
    p, hy = pl.pallas_call(
        functools.partial(_inproj_kernel, n_sub=n_sub, tiles_per_seq=s // TM, n_ctx_tiles=n_ctx_tiles),
        grid=(n_tiles // n_sub,),
        in_specs=[pl.BlockSpec((rows, D_MODEL), lambda i: (i, 0)),
                  pl.BlockSpec((8, D_MODEL), lambda i: (jnp.maximum(i * r8 - 1, 0), 0)),
                  pl.BlockSpec((8, D_MODEL), lambda i: (jnp.minimum((i + 1) * r8, b * s // 8 - 1), 0)),
                  *_mod_specs(n_sub, s // TM, n_ctx_tiles, mods.shape[0] - 1),
                  pl.BlockSpec((6, D_MODEL), lambda i: (0, 0)),
                  pl.BlockSpec(memory_space=pltpu.VMEM),
                  pl.BlockSpec((4, W_COLS), lambda i: (0, 0))],
        out_specs=[pl.BlockSpec((rows, P_COLS), lambda i: (i, 0)),
                   pl.BlockSpec((rows, HYENA_COLS), lambda i: (i, 0))],
        out_shape=[jax.ShapeDtypeStruct((b * s, P_COLS), BF16),
                   jax.ShapeDtypeStruct((b * s, HYENA_COLS), F32)],
        compiler_params=_cparams(("parallel",)),
        name="in_projection",
    )(*([x.reshape(b * s, D_MODEL)] * 3), *([mods] * n_sub), norm_g, w_in_p, coef)
    return p.reshape(b, s, P_COLS), hy.reshape(b, s, HYENA_COLS)


def _shift_coefficients(rwkv_mu, hyena_conv, hyena_conv_b):
    mu = rwkv_mu.astype(F32)
    coef = jnp.zeros((4, W_COLS), F32).at[0].set(1.0)
    for off, sl in ((P_RKV, slice(0, 3 * RWKV_DIM)), (P_LORA, slice(3 * RWKV_DIM, RWKV_COLS))):
        width = sl.stop - sl.start
        coef = coef.at[0, off:off + width].set(1.0 - mu[0, sl] - mu[1, sl])
        coef = coef.at[1, off:off + width].set(mu[0, sl])
        coef = coef.at[2, off:off + width].set(mu[1, sl])
    hy = slice(P_HY, P_HY + HYENA_COLS)
    coef = coef.at[0, hy].set(hyena_conv[1]).at[1, hy].set(hyena_conv[0]).at[2, hy].set(hyena_conv[2])
    return coef.at[3, hy].set(hyena_conv_b)


def _permute_w_in(w_in):
    o_mla = GATE_COLS
    o_rwkv = o_mla + MLA_COLS
    o_hy = o_rwkv + RWKV_COLS
    o_swa = o_hy + HYENA_COLS
    z = lambda n: jnp.zeros((D_MODEL, n), w_in.dtype)
    parts = [
        w_in[:, :GATE_COLS],
        w_in[:, o_rwkv:o_rwkv + 3 * RWKV_DIM],
        w_in[:, o_swa:o_swa + SWA_HEADS * SWA_HEAD],
        w_in[:, o_rwkv + 3 * RWKV_DIM:o_rwkv + RWKV_COLS],
        w_in[:, o_mla:o_mla + MLA_Q_RANK],
        w_in[:, o_mla + MLA_Q_RANK:o_mla + MLA_Q_RANK + MLA_KV_RANK],
        z(MLA_NOPE), w_in[:, o_mla + MLA_Q_RANK + MLA_KV_RANK:o_mla + MLA_COLS],
        z(LANES - MLA_NOPE - MLA_ROPE),
        w_in[:, o_swa + SWA_HEADS * SWA_HEAD:o_swa + SWA_COLS],
        w_in[:, o_hy:o_hy + HYENA_COLS],
    ]
    out = jnp.concatenate(parts, axis=1)
    assert out.shape[1] == W_COLS
    return out


def _rope_tables(n_lat, n_ctx, rot_dim, lane0, period):
    rows = n_lat // GRID_W
    row = jnp.repeat(jnp.arange(rows, dtype=F32), GRID_W)
    col = jnp.tile(jnp.arange(GRID_W, dtype=F32), rows)
    axis_dim = rot_dim // 2
    h = axis_dim // 2
    inv_freq = ROPE_BASE ** (-jnp.arange(0, axis_dim, 2, dtype=F32) / axis_dim)
    ang_r = row[:, None] * inv_freq
    ang_c = col[:, None] * inv_freq
    cos_rot = jnp.concatenate([jnp.cos(ang_r)] * 2 + [jnp.cos(ang_c)] * 2, axis=1)
    zeros = jnp.zeros_like(ang_r)
    sin_a = jnp.concatenate([-jnp.sin(ang_r), zeros, -jnp.sin(ang_c), zeros], axis=1)
    sin_b = jnp.concatenate([zeros, jnp.sin(ang_r), zeros, jnp.sin(ang_c)], axis=1)

    def widen(t, fill):
        g = jnp.full((n_lat, period), fill, F32).at[:, lane0:lane0 + rot_dim].set(t)
        g = jnp.tile(g, (1, LANES // period))
        ctx = jnp.full((n_ctx, LANES), fill, F32)
        return jnp.concatenate([ctx, g], axis=0)

    return widen(cos_rot, 1.0), widen(sin_a, 0.0), widen(sin_b, 0.0), h


def _rope128(x, cos, sin_a, sin_b, h):
    return x * cos + pltpu.roll(x, LANES - h, axis=1) * sin_a + pltpu.roll(x, h, axis=1) * sin_b


LOG2E = math.log2(math.e)
MLA_SCALE = (MLA_NOPE + MLA_ROPE) ** -0.5 * LOG2E
MLA_ONE_LANE = (MLA_V, 0)


def _mla_prep_kernel(cq_ref, ckv_ref, kr_ref, gq_ref, gkv_ref, wq_ref, wk_ref, wv_ref, vone_ref,
                     cos_ref, sa_ref, sb_ref, q_ref, k_ref, v_ref, *, h):
    cos, sa, sb = cos_ref[...], sa_ref[...], sb_ref[...]
    cq = _rms(cq_ref[0].astype(F32), gq_ref[...]).astype(BF16)
    ckv = _rms(ckv_ref[0].astype(F32), gkv_ref[...]).astype(BF16)
    q = jnp.dot(cq, wq_ref[...], preferred_element_type=F32)
    k = jnp.dot(ckv, wk_ref[...], preferred_element_type=F32)
    kr = _rope128(kr_ref[0].astype(F32), cos, sa, sb, h)
    for hd in range(MLA_HEADS):
        sl = slice(hd * LANES, (hd + 1) * LANES)
        q_ref[0, :, sl] = (_rope128(q[:, sl], cos, sa, sb, h) * MLA_SCALE).astype(BF16)
        k_ref[0, :, sl] = (k[:, sl] + kr).astype(BF16)
    v_ref[0] = (jnp.dot(ckv, wv_ref[...], preferred_element_type=F32) + vone_ref[...]).astype(BF16)


def _mla_prep(p, norm_q, norm_kv, w_uq, w_ukv, tabs):
    b, s, _ = p.shape
    cos, sa, sb, h = tabs
    hq = MLA_NOPE + MLA_ROPE
    wq = jnp.zeros((MLA_Q_RANK, MLA_HEADS, LANES), F32).at[:, :, :hq].set(
        w_uq.reshape(MLA_Q_RANK, MLA_HEADS, hq)).reshape(MLA_Q_RANK, MLA_HEADS * LANES).astype(BF16)
    wkv = w_ukv.reshape(MLA_KV_RANK, MLA_HEADS, MLA_NOPE + MLA_V)
    wk = jnp.zeros((MLA_KV_RANK, MLA_HEADS, LANES), F32).at[:, :, :MLA_NOPE].set(
        wkv[:, :, :MLA_NOPE]).reshape(MLA_KV_RANK, MLA_HEADS * LANES).astype(BF16)
    wv_pairs = wkv[:, :, MLA_NOPE:].reshape(MLA_KV_RANK, MLA_HEADS // 2, 2, MLA_V)
    gap = ((0, 0), (0, 0), (0, LANES - MLA_V))
    wv = jnp.stack([jnp.pad(wv_pairs[:, :, 0], gap), jnp.pad(wv_pairs[:, :, 1], gap[:2] + (gap[2][::-1],))],
                   axis=2).reshape(MLA_KV_RANK, MLA_HEADS * LANES).astype(BF16)
    lane_id = np.arange(MLA_HEADS * LANES) % (2 * LANES)
    vone = jnp.asarray((lane_id == MLA_ONE_LANE[0]) | (lane_id == LANES + MLA_ONE_LANE[1]), F32)[None, :]
    full = lambda shape: pl.BlockSpec(shape, lambda bi, i: (0,) * len(shape))
    tab = pl.BlockSpec((TM, LANES), lambda bi, i: (i, 0))
    return pl.pallas_call(
        functools.partial(_mla_prep_kernel, h=h),
        grid=(b, s // TM),
        in_specs=[pl.BlockSpec((1, TM, MLA_Q_RANK), lambda bi, i: (bi, i, P_CQ // MLA_Q_RANK)),
                  pl.BlockSpec((1, TM, LANES), lambda bi, i: (bi, i, P_CKV // LANES)),
                  pl.BlockSpec((1, TM, LANES), lambda bi, i: (bi, i, P_KR // LANES)),
                  full((1, MLA_Q_RANK)), full((1, MLA_KV_RANK)),
                  full(wq.shape), full(wk.shape), full(wv.shape), full(vone.shape), tab, tab, tab],
        out_specs=[pl.BlockSpec((1, TM, MLA_HEADS * LANES), lambda bi, i: (bi, i, 0))] * 3,
        out_shape=[jax.ShapeDtypeStruct((b, s, MLA_HEADS * LANES), BF16)] * 3,
        compiler_params=_cparams(("parallel", "parallel")),
        name="mla_prep",
    )(p, p, p, norm_q.reshape(1, -1), norm_kv.reshape(1, -1), wq, wk, wv, vone, cos, sa, sb)


MLA_Q_TILES = 1


def _mla_attn_kernel(*refs, n_ctx, n_lat_steps):
    q_refs = refs[:MLA_Q_TILES]
    k_ref, v_ref, o_ref = refs[MLA_Q_TILES:]

    def attend(n_keys):
        outs = []
        for hd in range(2):
            sl = slice(hd * LANES, (hd + 1) * LANES)
            q = jnp.concatenate([q_ref[0, :, sl] for q_ref in q_refs], axis=0)
            s = lax.dot_general(q, k_ref[0, :n_keys, sl], (((1,), (1,)), ((), ())),
                                preferred_element_type=F32)
            e = jnp.exp2(s - jnp.max(s, axis=-1, keepdims=True)).astype(BF16)
            o = jnp.dot(e, v_ref[0, :n_keys, sl], preferred_element_type=F32)
            one = MLA_ONE_LANE[hd]
            outs.append(o / o[:, one:one + 1])
        lane = lax.broadcasted_iota(jnp.int32, outs[0].shape, 1)
        o_ref[0] = jnp.where(lane < MLA_V, outs[0], outs[1]).astype(BF16)

    @pl.when(pl.program_id(2) < n_lat_steps)
    def _():
        attend(k_ref.shape[1])

    @pl.when(pl.program_id(2) >= n_lat_steps)
    def _():
        attend(n_ctx)


def _mla_attention(q, k, v, *, n_ctx, with_ctx):
    b, s, _ = q.shape
    nct = n_ctx // TM
    n_lat = s - n_ctx
    n_lat_steps = n_lat // (MLA_Q_TILES * TM)
    assert n_lat % (MLA_Q_TILES * TM) == 0 and (nct == 1 or not with_ctx)

    def q_spec(t):
        return pl.BlockSpec((1, TM, 2 * LANES), lambda bi, hp, j: (
            bi, jnp.where(j < n_lat_steps, nct + j * MLA_Q_TILES + t, 0), hp))

    kv = pl.BlockSpec((1, s, 2 * LANES), lambda bi, hp, j: (bi, 0, hp))
    return pl.pallas_call(
        functools.partial(_mla_attn_kernel, n_ctx=n_ctx, n_lat_steps=n_lat_steps),
        grid=(b, MLA_HEADS // 2, n_lat_steps + (1 if with_ctx else 0)),
        in_specs=[*[q_spec(t) for t in range(MLA_Q_TILES)], kv, kv],
        out_specs=pl.BlockSpec((1, MLA_Q_TILES * TM, LANES), lambda bi, hp, j: (bi, j, hp)),
        out_shape=jax.ShapeDtypeStruct((b, n_lat + (n_ctx if with_ctx else 0), MLA_HEADS * MLA_V), BF16),
        compiler_params=_cparams(("parallel", "parallel", "parallel")),
        name="mla_attention",
    )(*([q] * MLA_Q_TILES), k, v)


SWA_SCALE = SWA_HEAD ** -0.5 * LOG2E
SWA_TQ = 128
SWA_ONE_LANE = (SWA_HEAD, 0)


def _swa_prep_kernel(q_ref, k_ref, v_ref, cos_ref, sa_ref, sb_ref, qo_ref, ko_ref, vo_ref, *, h):
    cos, sa, sb = cos_ref[...], sa_ref[...], sb_ref[...]
    lane = lax.broadcasted_iota(jnp.int32, cos.shape, 1)
    low = lane < SWA_HEAD
    for j in range(SWA_HEADS // 2):
        blk = _rope128(q_ref[0, :, j * LANES:(j + 1) * LANES].astype(F32), cos, sa, sb, h) * SWA_SCALE
        qo_ref[0, :, (2 * j) * LANES:(2 * j + 1) * LANES] = jnp.where(low, blk, 0.0).astype(BF16)
        qo_ref[0, :, (2 * j + 1) * LANES:(2 * j + 2) * LANES] = jnp.where(
            low, pltpu.roll(blk, SWA_HEAD, axis=1), 0.0).astype(BF16)
    kb = _rope128(k_ref[0].astype(F32), cos, sa, sb, h)
    ko_ref[0, :, :LANES] = jnp.where(low, kb, 0.0).astype(BF16)
    ko_ref[0, :, LANES:] = jnp.where(low, pltpu.roll(kb, SWA_HEAD, axis=1), 0.0).astype(BF16)
    vb = v_ref[0].astype(F32)
    vr = pltpu.roll(vb, SWA_HEAD, axis=1)
    one_lo = jnp.where(lane == SWA_ONE_LANE[0], 1.0, 0.0)
    one_hi = jnp.where(lane == SWA_ONE_LANE[1], 1.0, 0.0)
    vo_ref[0, :, 0 * LANES:1 * LANES] = jnp.where(low, vb, one_lo).astype(BF16)
    vo_ref[0, :, 1 * LANES:2 * LANES] = jnp.where(low, one_hi, vr).astype(BF16)
    vo_ref[0, :, 2 * LANES:3 * LANES] = jnp.where(low, vr, one_lo).astype(BF16)
    vo_ref[0, :, 3 * LANES:4 * LANES] = jnp.where(low, one_hi, vb).astype(BF16)


def _swa_prep(p, tabs):
    b, s, _ = p.shape
    cos, sa, sb, h = tabs
    tab = pl.BlockSpec((TM, LANES), lambda bi, i: (i, 0))
    nq = SWA_HEADS * SWA_HEAD
    return pl.pallas_call(
        functools.partial(_swa_prep_kernel, h=h),
        grid=(b, s // TM),
        in_specs=[pl.BlockSpec((1, TM, nq), lambda bi, i: (bi, i, P_SWAQ // nq)),
                  pl.BlockSpec((1, TM, LANES), lambda bi, i: (bi, i, P_SWAK // LANES)),
                  pl.BlockSpec((1, TM, LANES), lambda bi, i: (bi, i, P_SWAV // LANES)),
                  tab, tab, tab],
        out_specs=[pl.BlockSpec((1, TM, SWA_HEADS * LANES), lambda bi, i: (bi, i, 0)),
                   pl.BlockSpec((1, TM, SWA_KV_HEADS * LANES), lambda bi, i: (bi, i, 0)),
                   pl.BlockSpec((1, TM, 4 * LANES), lambda bi, i: (bi, i, 0))],
        out_shape=[jax.ShapeDtypeStruct((b, s, SWA_HEADS * LANES), BF16),
                   jax.ShapeDtypeStruct((b, s, SWA_KV_HEADS * LANES), BF16),
                   jax.ShapeDtypeStruct((b, s, 4 * LANES), BF16)],
        compiler_params=_cparams(("parallel", "parallel")),
        name="swa_prep",
    )(p, p, p, cos, sa, sb)


def _swa_attn_kernel(sink_ref, q_ref, k_ref, v_ref, o_ref, *, n_ctx, q_off):
    i = pl.program_id(1) + q_off
    s_len = k_ref.shape[1]
    tq = SWA_TQ
    n_loc = tq + 2 * WINDOW
    r0 = i * tq
    is_lat = r0 >= n_ctx
    start = pl.multiple_of(jnp.clip(r0 - WINDOW, 0, s_len - n_loc), LANES)
    rows_g = SWA_GROUP * tq
    row = lax.broadcasted_iota(jnp.int32, (rows_g, n_loc), 0)
    qpos = r0 - n_ctx + row % tq
    kpos = start - n_ctx + lax.broadcasted_iota(jnp.int32, (rows_g, n_loc), 1)
    loc_ok = (jnp.abs(kpos - qpos) <= WINDOW) & (kpos >= 0) & is_lat
    k_loc = k_ref[0, pl.ds(start, n_loc), :]
    v_loc = v_ref[0, pl.ds(start, n_loc), :]
    k_ctx = k_ref[0, 0:n_ctx, :]
    v_ctx = v_ref[0, 0:n_ctx, :]
    head_row = lax.broadcasted_iota(jnp.int32, (rows_g, 1), 0) // tq
    lane = lax.broadcasted_iota(jnp.int32, (tq, LANES), 1)
    stages = []
    for g in range(SWA_KV_HEADS):
        q = jnp.concatenate([q_ref[0, :, hd * LANES:(hd + 1) * LANES]
                             for hd in range(g * SWA_GROUP, (g + 1) * SWA_GROUP)], axis=0)
        kg = slice(g * LANES, (g + 1) * LANES)
        s_loc = lax.dot_general(q, k_loc[:, kg], (((1,), (1,)), ((), ())), preferred_element_type=F32)
        s_ctx = lax.dot_general(q, k_ctx[:, kg], (((1,), (1,)), ((), ())), preferred_element_type=F32)
        sink = jnp.zeros((rows_g, 1), F32)
        for hh in range(SWA_GROUP):
            sink = jnp.where(head_row == hh, sink_ref[g * SWA_GROUP + hh] * LOG2E, sink)
        stages.append((jnp.where(loc_ok, s_loc, NEG_INF), s_ctx, sink))
    for g, (s_loc, s_ctx, sink) in enumerate(stages):
        m = jnp.maximum(jnp.maximum(jnp.max(s_loc, axis=-1, keepdims=True),
                                    jnp.max(s_ctx, axis=-1, keepdims=True)), sink)
        e = jnp.concatenate([jnp.exp2(s_loc - m), jnp.exp2(s_ctx - m)], axis=1).astype(BF16)
        e_sink = jnp.exp2(sink - m)
        outs = []
        for par in range(2):
            vg = slice((2 * g + par) * LANES, (2 * g + par + 1) * LANES)
            o = jnp.dot(e, jnp.concatenate([v_loc[:, vg], v_ctx[:, vg]], axis=0), preferred_element_type=F32)
            one = SWA_ONE_LANE[par]
            outs.append(o / (o[:, one:one + 1] + e_sink))
        for pi in range(SWA_GROUP // 2):
            even = outs[0][(2 * pi) * tq:(2 * pi + 1) * tq]
            odd = outs[1][(2 * pi + 1) * tq:(2 * pi + 2) * tq]
            blk = g * (SWA_GROUP // 2) + pi
            o_ref[0, :, blk * LANES:(blk + 1) * LANES] = jnp.where(lane < SWA_HEAD, even, odd).astype(BF16)


def _swa_attention(q, k, v, sink, *, n_ctx, q_off):
    b, s, _ = q.shape
    nq = s // SWA_TQ - q_off
    return pl.pallas_call(
        functools.partial(_swa_attn_kernel, n_ctx=n_ctx, q_off=q_off),
        grid=(b, nq),
        in_specs=[pl.BlockSpec(memory_space=pltpu.SMEM),
                  pl.BlockSpec((1, SWA_TQ, SWA_HEADS * LANES), lambda bi, i: (bi, i + q_off, 0)),
                  pl.BlockSpec((1, s, SWA_KV_HEADS * LANES), lambda bi, i: (bi, 0, 0)),
                  pl.BlockSpec((1, s, 4 * LANES), lambda bi, i: (bi, 0, 0))],
        out_specs=pl.BlockSpec((1, SWA_TQ, SWA_HEADS * SWA_HEAD), lambda bi, i: (bi, i, 0)),
        out_shape=jax.ShapeDtypeStruct((b, nq * SWA_TQ, SWA_HEADS * SWA_HEAD), BF16),
        compiler_params=_cparams(("parallel", "parallel")),
        name="swa_attention",
    )(sink, q, k, v)


N_PAIR = RWKV_HEADS // 2
N_DOUBLINGS = int(math.log2(CHUNK))


def _softplus(x):
    return jnp.maximum(x, 0.0) + jnp.log(1.0 + jnp.exp(-jnp.abs(x)))


def _headsum(x, bd):
    hi = x.astype(BF16)
    lo = (x - hi.astype(F32)).astype(BF16)
    return (jnp.dot(hi, bd, preferred_element_type=F32) + jnp.dot(lo, bd, preferred_element_type=F32))


def _chunk_cumsum(x, reverse):
    rows = lax.broadcasted_iota(jnp.int32, x.shape, 0)
    s = 1
    while s < CHUNK:
        if reverse:
            x = x + jnp.where(rows < CHUNK - s, pltpu.roll(x, CHUNK - s, axis=0), 0.0)
        else:
            x = x + jnp.where(rows >= s, pltpu.roll(x, s, axis=0), 0.0)
        s *= 2
    return x


def _head_rows(x):
    first = lax.broadcasted_iota(jnp.int32, x.shape, 1) < RWKV_HEAD
    return jnp.concatenate([jnp.where(first, x, 0.0), jnp.where(first, 0.0, x)], axis=0)


def _mm_x3(a, b):
    a_hi = a.astype(BF16)
    b_hi = b.astype(BF16)
    a_lo = (a - a_hi.astype(F32)).astype(BF16)
    b_lo = (b - b_hi.astype(F32)).astype(BF16)
    dot = functools.partial(jnp.dot, preferred_element_type=F32)
    return dot(a_hi, b_hi) + dot(a_hi, b_lo) + dot(a_lo, b_hi)


def _rwkv_chunk_kernel(rf_ref, kf_ref, vf_ref, lof_ref, rb_ref, kb_ref, vb_ref, lob_ref,
                       kvec_ref, w0_ref, a0_ref, wup_ref, aup_ref, gup_ref, rk_ref, bd_ref,
                       yf_ref, yb_ref, bonus_ref, gate_ref, s_ref):
    @pl.when(pl.program_id(1) == 0)
    def _():
        s_ref[...] = jnp.zeros_like(s_ref)

    bd = bd_ref[...]
    data = []
    for refs in ((rf_ref, kf_ref, vf_ref, lof_ref), (rb_ref, kb_ref, vb_ref, lob_ref)):
        r, k, v, lora = (ref[0].astype(F32) for ref in refs)
        kk = k * kvec_ref[0:1]
        kk = kk * lax.rsqrt(_headsum(kk * kk, bd) + 1e-12)
        data.append((r, k, v, lora, kk))
    r, k, v, lora, _ = data[0]
    gate_ref[0] = _mm(_sigmoid(lora), gup_ref[...])
    k_both = sum(k * (1.0 + (_sigmoid(a0_ref[d] + _mm(lora, aup_ref[d])) - 1.0) * kvec_ref[1:2]) for d in range(2))
    bonus_ref[0] = _headsum(r * (0.5 * k_both) * rk_ref[...], bd) * v

    trow = lax.broadcasted_iota(jnp.int32, (2 * CHUNK, 4 * CHUNK), 0) % CHUNK
    tcol = lax.broadcasted_iota(jnp.int32, (2 * CHUNK, 4 * CHUNK), 1) % CHUNK
    sq_r = lax.broadcasted_iota(jnp.int32, (LANES, LANES), 0)
    sq_c = lax.broadcasted_iota(jnp.int32, (LANES, LANES), 1)
    same_head = (sq_r // RWKV_HEAD) == (sq_c // RWKV_HEAD)
    eye = sq_r == sq_c

    chains = []
    for d in range(2):
        reverse = d == 1
        r, k, v, lora, kk = data[d]
        w_log = -_softplus(-(w0_ref[d] + _mm(jnp.tanh(lora), wup_ref[d]))) - 0.5
        ld = -jnp.exp(w_log)
        a = _sigmoid(a0_ref[d] + _mm(lora, aup_ref[d]))
        k_d = k * (1.0 + (a - 1.0) * kvec_ref[1:2])
        b_d = kk * a
        lg = _chunk_cumsum(ld, reverse)
        last = 0 if reverse else CHUNK - 1
        tot = lg[last:last + 1]
        e_neg = jnp.exp(-lg)
        e_end = jnp.exp(tot - lg)
        z_t = -kk * jnp.exp(lg - ld)
        r_t = r * jnp.exp(lg)
        b_t = b_d * e_neg
        k_t = k_d * e_neg
        b_e = b_d * e_end
        k_e = k_d * e_end
        e_tot = jnp.exp(tot)
        before = (tcol > trow) if reverse else (tcol < trow)
        before_eq = (tcol >= trow) if reverse else (tcol <= trow)
        for pr in range(N_PAIR):
            sl = slice(pr * LANES, (pr + 1) * LANES)
            ch = {"d": d, "sl": sl, "rp": r_t[:, sl], "vp": v[:, sl], "e_tot": e_tot[:, sl],
                  "be_ke": jnp.concatenate([b_e[:, sl], k_e[:, sl]], axis=0)}
            zst, rst, vst = _head_rows(z_t[:, sl]), _head_rows(r_t[:, sl]), _head_rows(v[:, sl])
            bkst = jnp.concatenate([_head_rows(b_t[:, sl]), _head_rows(k_t[:, sl])], axis=0)
            ch["lz"] = jnp.where(before, _mm_nt(zst, bkst), 0.0)
            ch["lr"] = jnp.where(before_eq, _mm_nt(rst, bkst), 0.0)
            ch["zst"], ch["vst"] = zst, vst
            chains.append(ch)

    unit = jnp.where(eye, 1.0, 0.0)
    for ch in chains:
        ch["pw"] = ch["lz"][:, :LANES]
        ch["t"] = unit + ch["pw"]
        ch["x"] = jnp.concatenate([ch["zst"], _mm(ch["lz"][:, LANES:], ch["vst"])], axis=1)
    for it in range(1, N_DOUBLINGS):
        for ch in chains:
            ch["pw"] = _mm(ch["pw"], ch["pw"])
        for ch in chains:
            ch["t"] = ch["t"] + _mm(ch["pw"], ch["t"])
    for ch in chains:
        ch["x"] = _mm(ch["t"], ch["x"])
    for ch in chains:
        low = jnp.concatenate([jnp.zeros_like(ch["vst"]), ch["vst"]], axis=1)
        op = _mm(ch["lr"], jnp.concatenate([ch["x"], low], axis=0))
        ch["op"] = op[:CHUNK] + op[CHUNK:]
        ch["xp"] = ch["x"][:CHUNK] + ch["x"][CHUNK:]
    for ch in chains:
        rhs = jnp.concatenate([ch["xp"], jnp.concatenate([jnp.zeros_like(ch["vp"]), ch["vp"]], axis=1)], axis=0)
        ag = _mm_tn(ch["be_ke"], rhs)
        a_full = ag[:, :LANES] + jnp.where(eye, jnp.broadcast_to(ch["e_tot"], (LANES, LANES)), 0.0)
        ch["a"] = jnp.where(same_head, a_full, 0.0)
        ch["g"] = jnp.where(same_head, ag[:, LANES:], 0.0)
    for idx, ch in enumerate(chains):
        st = s_ref[idx]
        y_ref = yf_ref if ch["d"] == 0 else yb_ref
        y_ref[0, :, ch["sl"]] = _mm_x3(ch["rp"] + ch["op"][:, :LANES], st) + ch["op"][:, LANES:]
        s_ref[idx] = _mm_x3(ch["a"], st) + ch["g"]


def _head_block_diag():
    idx = np.arange(RWKV_DIM) // RWKV_HEAD
    return jnp.asarray(idx[:, None] == idx[None, :], BF16)


def _rwkv_chunks(p, kvec, w0, a0, w_up, a_up, g_up, r_k, *, n_ctx):
    b, s, _ = p.shape
    nc = s // CHUNK
    ncc = n_ctx // CHUNK
    lora_w = DECAY_LORA + AAA_LORA + GATE_LORA
    wup = jnp.zeros((2, lora_w, RWKV_DIM), F32).at[:, :DECAY_LORA].set(w_up).astype(BF16)
    aup = jnp.zeros((2, lora_w, RWKV_DIM), F32).at[:, DECAY_LORA:DECAY_LORA + AAA_LORA].set(a_up).astype(BF16)
    gup = jnp.zeros((lora_w, RWKV_DIM), F32).at[DECAY_LORA + AAA_LORA:].set(g_up).astype(BF16)

    def fwd(c):
        return c

    def back(c):
        return jnp.where(c < ncc, ncc - 1 - c, nc - 1 - (c - ncc))

    full = lambda shape: pl.BlockSpec(shape, lambda bi, c: (0,) * len(shape))

    def inputs(chunk):
        col = lambda off: pl.BlockSpec((1, CHUNK, RWKV_DIM), lambda bi, c: (bi, chunk(c), off // RWKV_DIM))
        return [col(P_RKV), col(P_RKV + RWKV_DIM), col(P_RKV + 2 * RWKV_DIM),
                pl.BlockSpec((1, CHUNK, lora_w), lambda bi, c: (bi, chunk(c), P_LORA // lora_w))]

    tok = lambda chunk: pl.BlockSpec((1, CHUNK, RWKV_DIM), lambda bi, c: (bi, chunk(c), 0))
    return pl.pallas_call(
        _rwkv_chunk_kernel,
        grid=(b, nc),
        in_specs=[*inputs(fwd), *inputs(back),
                  full((2, RWKV_DIM)), full((2, 1, RWKV_DIM)), full((2, 1, RWKV_DIM)),
                  full(wup.shape), full(aup.shape), full(gup.shape), full((1, RWKV_DIM)),
                  full((RWKV_DIM, RWKV_DIM))],
        out_specs=[tok(fwd), tok(back), tok(fwd), tok(fwd)],
        out_shape=[jax.ShapeDtypeStruct((b, s, RWKV_DIM), F32)] * 4,
        scratch_shapes=[pltpu.VMEM((2 * N_PAIR, LANES, LANES), F32)],
        compiler_params=_cparams(("parallel", "arbitrary")),
        name="rwkv_chunks",
    )(*([p] * 8), kvec, w0.reshape(2, 1, -1), a0.reshape(2, 1, -1), wup, aup, gup, r_k.reshape(1, -1),
      _head_block_diag())


def _rwkv_readout_kernel(yf_ref, yb_ref, bonus_ref, gate_ref, lng_ref, lnb_ref, bd_ref, o_ref):
    bd = bd_ref[...]
    y = yf_ref[0] + yb_ref[0]
    inv_n = 1.0 / RWKV_HEAD
    dev = y - _headsum(y, bd) * inv_n
    var = _headsum(dev * dev, bd) * inv_n
    yn = dev * lax.rsqrt(var + RWKV_LN_EPS) * lng_ref[...] + lnb_ref[...]
    o_ref[0] = ((yn + bonus_ref[0]) * gate_ref[0]).astype(BF16)


def _rwkv_readout(y_fwd, y_bwd, bonus, gate, ln_g, ln_b, *, row_off):
    b, s, _ = y_fwd.shape
    nt = s // TM - row_off
    full = lambda shape: pl.BlockSpec(shape, lambda bi, i: (0,) * len(shape))
    tok = pl.BlockSpec((1, TM, RWKV_DIM), lambda bi, i: (bi, i + row_off, 0))
    return pl.pallas_call(
        _rwkv_readout_kernel,
        grid=(b, nt),
        in_specs=[tok, tok, tok, tok, full((1, RWKV_DIM)), full((1, RWKV_DIM)), full((RWKV_DIM, RWKV_DIM))],
        out_specs=pl.BlockSpec((1, TM, RWKV_DIM), lambda bi, i: (bi, i, 0)),
        out_shape=jax.ShapeDtypeStruct((b, nt * TM, RWKV_DIM), BF16),
        compiler_params=_cparams(("parallel", "parallel")),
        name="rwkv_readout",
    )(y_fwd, y_bwd, bonus, gate, ln_g.reshape(1, -1), ln_b.reshape(1, -1), _head_block_diag())


DFT_N2 = LANES
HY_MIN_LEN = 1024
HY_CT = 128


def _hyena_mlp_kernel(feats_ref, w1_ref, b1_ref, w2_ref, b2_ref, freq_ref, h_ref):
    h = jnp.sin(freq_ref[0:1] * (_mm_f32(feats_ref[0], w1_ref[...]) + b1_ref[...]))
    h_ref[0] = jnp.sin(freq_ref[1:2] * (_mm_f32(h, w2_ref[...]) + b2_ref[...]))


def _hyena_filter_kernel(h_ref, w3f_ref, w3b_ref, t_ref, delta_ref, k_ref):
    hf = _mm_f32(h_ref[0], w3f_ref[...]) * jnp.exp(-t_ref[0] * delta_ref[...])
    hb = _mm_f32(h_ref[1], w3b_ref[...]) * jnp.exp(-t_ref[1] * delta_ref[...])
    norm = (jnp.sum(jnp.abs(hf), axis=0, keepdims=True) + jnp.sum(jnp.abs(hb), axis=0, keepdims=True))
    r = pl.program_id(2)
    rows = lax.broadcasted_iota(jnp.int32, hf.shape, 0)
    tail = jnp.where(rows == 0, 0.0, hb)
    blk = jnp.where(r == 0, hf, jnp.where(r == pl.num_programs(2) - 1, tail, 0.0))
    k_ref[0] = blk / norm


def _hyena_filter_buffer(n, nc, w1, b1, w2, b2, w3, freq):
    lag = jnp.stack([jnp.arange(n), jnp.where(jnp.arange(n) == 0, 0, n - jnp.arange(n))]).astype(F32)
    t = (lag / (n - 1))[:, :, None]
    bands = jnp.linspace(1e-4, HYENA_BANDS - 1, HYENA_BANDS, dtype=F32)
    ang = (2.0 * math.pi / n) * lag[:, :, None] * bands[None, None, :]
    feats = jnp.concatenate([t, jnp.cos(ang), -jnp.sin(ang),
                             jnp.zeros((2, n, HYENA_FW - HYENA_EMB), F32)], axis=-1)
    w1p = jnp.zeros((HYENA_FW, HYENA_FW), F32).at[:HYENA_EMB].set(w1)
    deltas = jnp.abs(jnp.linspace(math.log(HYENA_TARGET) / HYENA_SLOW,
                                  math.log(HYENA_TARGET) / HYENA_FAST, HYENA_DIM, dtype=F32))[None, :]
    fixed = lambda shape: pl.BlockSpec(shape, lambda d: (0,) * len(shape))
    hidden = pl.pallas_call(
        _hyena_mlp_kernel,
        grid=(2,),
        in_specs=[pl.BlockSpec((1, n, HYENA_FW), lambda d: (d, 0, 0)), fixed((HYENA_FW, HYENA_FW)),
                  fixed((1, HYENA_FW)), fixed((HYENA_FW, HYENA_FW)), fixed((1, HYENA_FW)),
                  fixed((2, HYENA_FW))],
        out_specs=pl.BlockSpec((1, n, HYENA_FW), lambda d: (d, 0, 0)),
        out_shape=jax.ShapeDtypeStruct((2, n, HYENA_FW), F32),
        compiler_params=_cparams(("parallel",)),
        name="hyena_filter_mlp",
    )(feats, w1p, b1.reshape(1, -1), w2, b2.reshape(1, -1), freq)
    tc = 256
    nj = HYENA_DIM // tc
    full = lambda shape: pl.BlockSpec(shape, lambda o, j, r: (0,) * len(shape))
    return pl.pallas_call(
        _hyena_filter_kernel,
        grid=(HYENA_ORDER, nj, nc // n),
        in_specs=[full((2, n, HYENA_FW)),
                  pl.BlockSpec((HYENA_FW, tc), lambda o, j, r: (0, o * 2 * nj + j)),
                  pl.BlockSpec((HYENA_FW, tc), lambda o, j, r: (0, o * 2 * nj + nj + j)),
                  full((2, n, 1)), pl.BlockSpec((1, tc), lambda o, j, r: (0, j))],
        out_specs=pl.BlockSpec((1, n, tc), lambda o, j, r: (o, r, j)),
        out_shape=jax.ShapeDtypeStruct((HYENA_ORDER, nc, HYENA_DIM), F32),
        compiler_params=_cparams(("parallel", "parallel", "parallel")),
        name="hyena_filters",
    )(hidden, w3, w3, t, deltas)


def _dft_tables(n1):
    nc = n1 * DFT_N2
    t2 = np.arange(DFT_N2)[:, None, None]
    f1 = np.arange(n1)[None, :, None]
    t1 = np.arange(n1)[None, None, :]
    theta = 2.0 * np.pi * ((f1 * (DFT_N2 * t1 + t2)) % nc) / nc
    g_fwd = np.concatenate([np.cos(theta), -np.sin(theta)], axis=1)
    g_inv = np.concatenate([np.cos(theta), -np.sin(theta)], axis=1).transpose(0, 2, 1) / nc
    k = np.arange(DFT_N2)
    phi = 2.0 * np.pi * ((k[:, None] * k[None, :]) % DFT_N2) / DFT_N2
    c, s = np.cos(phi), np.sin(phi)
    f_fwd = np.block([[c, s], [-s, c]])
    f_inv = np.block([[c, -s], [s, c]])
    return tuple(jnp.asarray(t, BF16) for t in (g_fwd, g_inv, f_fwd, f_inv))


def _dft_rows_in(x_ref, g_ref, a_ref, t1n, n1, t1_valid):
    keep = lax.broadcasted_iota(jnp.int32, (t1n, x_ref.shape[-1]), 0) < t1_valid
    for t2 in range(DFT_N2):
        xs = x_ref[0, pl.ds(t2, t1n, stride=DFT_N2), :]
        if t1_valid < t1n:
            xs = jnp.where(keep, xs, 0.0)
        a_ref[pl.ds(t2, 2 * n1, stride=DFT_N2), :] = _mm(g_ref[t2], xs)


def _slab(f1):
    return slice(f1 * DFT_N2, (f1 + 1) * DFT_N2)


def _spectrum_kernel(x_ref, g_ref, ff_ref, k_ref, a_ref, *, n1):
    _dft_rows_in(x_ref, g_ref, a_ref, n1, n1, n1)
    for f1 in range(n1):
        a = jnp.concatenate([a_ref[_slab(f1)], a_ref[_slab(n1 + f1)]], axis=0)
        k_ref[0, f1] = _mm(ff_ref[...], a)


def _filter_spectrum(kbuf, g_fwd, f_fwd, n1):
    no, nc, _ = kbuf.shape
    nj = HYENA_DIM // HY_CT
    return pl.pallas_call(
        functools.partial(_spectrum_kernel, n1=n1),
        grid=(no, nj),
        in_specs=[pl.BlockSpec((1, nc, HY_CT), lambda o, j: (o, 0, j)),
                  pl.BlockSpec(g_fwd.shape, lambda o, j: (0, 0, 0)),
                  pl.BlockSpec(f_fwd.shape, lambda o, j: (0, 0))],
        out_specs=pl.BlockSpec((1, n1, 2 * DFT_N2, HY_CT), lambda o, j: (o, 0, 0, j)),
        out_shape=jax.ShapeDtypeStruct((no, n1, 2 * DFT_N2, HYENA_DIM), F32),
        scratch_shapes=[pltpu.VMEM((2 * n1 * DFT_N2, HY_CT), F32)],
        compiler_params=_cparams(("parallel", "parallel")),
        name="hyena_filter_spectrum",
    )(kbuf, g_fwd, f_fwd)


def _hyena_conv_kernel(x_ref, gate_ref, k_ref, g1_ref, ff_ref, fi_ref, g3_ref, bias_ref, o_ref, a_ref,
                       *, t1n, n1, t1_valid):
    _dft_rows_in(x_ref, g1_ref, a_ref, t1n, n1, t1_valid)
    for f1 in range(n1):
        re, im = _slab(f1), _slab(n1 + f1)
        x = _mm(ff_ref[...], jnp.concatenate([a_ref[re], a_ref[im]], axis=0))
        xre, xim = x[:DFT_N2], x[DFT_N2:]
        kre, kim = k_ref[0, f1, :DFT_N2], k_ref[0, f1, DFT_N2:]
        bm = _mm(fi_ref[...], jnp.concatenate([xre * kre - xim * kim, xre * kim + xim * kre], axis=0))
        a_ref[re] = bm[:DFT_N2]
        a_ref[im] = bm[DFT_N2:]
    bias = bias_ref[...]
    for t2 in range(DFT_N2):
        y = _mm(g3_ref[t2], a_ref[pl.ds(t2, 2 * n1, stride=DFT_N2), :])
        rows = pl.ds(t2, t1n, stride=DFT_N2)
        o_ref[0, rows, :] = gate_ref[0, rows, :] * (y + bias * x_ref[0, rows, :])


def _hyena_conv(z, z_spec, gate, gate_spec, kspec, order, bias, tabs, n1, n_pad, n):
    bx = z.shape[0]
    t1n = n_pad // DFT_N2
    g_fwd, g_inv, f_fwd, f_inv = tabs
    g1 = g_fwd[:, :, :t1n]
    g3 = g_inv[:, :t1n, :]
    const = lambda a: pl.BlockSpec(a.shape, lambda j, bi: (0,) * a.ndim)
    return pl.pallas_call(
        functools.partial(_hyena_conv_kernel, t1n=t1n, n1=n1, t1_valid=-(-n // DFT_N2)),
        grid=(HYENA_DIM // HY_CT, bx),
        in_specs=[z_spec, gate_spec,
                  pl.BlockSpec((1, n1, 2 * DFT_N2, HY_CT), lambda j, bi: (order, 0, 0, j)),
                  const(g1), const(f_fwd), const(f_inv), const(g3),
                  pl.BlockSpec((1, HY_CT), lambda j, bi: (0, j))],
        out_specs=pl.BlockSpec((1, n_pad, HY_CT), lambda j, bi: (bi, 0, j)),
        out_shape=jax.ShapeDtypeStruct((bx, n_pad, HYENA_DIM), F32),
        scratch_shapes=[pltpu.VMEM((2 * n1 * DFT_N2, HY_CT), F32)],
        compiler_params=_cparams(("parallel", "parallel")),
        name="hyena_conv",
    )(z, gate, kspec, g1, f_fwd, f_inv, g3, bias.reshape(1, -1))


def _hyena_operator(p_hy, row0, n, filt_params, bias):
    n_pad = max(n, HY_MIN_LEN)
    nc = 2 * n_pad
    n1 = nc // DFT_N2
    tabs = _dft_tables(n1)
    kspec = _filter_spectrum(_hyena_filter_buffer(n, nc, *filt_params), tabs[0], tabs[2], n1)
    nj = HYENA_DIM // HY_CT
    window = lambda part: pl.BlockSpec((pl.Element(1), pl.Element(n_pad), pl.Element(HY_CT)),
                                       lambda j, bi: (bi, row0, (part * nj + j) * HY_CT))
    own = pl.BlockSpec((1, n_pad, HY_CT), lambda j, bi: (bi, 0, j))
    z = _hyena_conv(p_hy, window(0), p_hy, window(1), kspec, 0, bias[0], tabs, n1, n_pad, n)
    return _hyena_conv(z, own, p_hy, window(2), kspec, 1, bias[1], tabs, n1, n_pad, n)


def _merge_kernel(x_ref, mod_ref, g_ref, gates_ref, bg_ref, ya_ref, yb_ref, yh_ref, yhc_ref, yd_ref,
                  wb_ref, wo_ref, o_ref, *, n_ctx_tiles, row_off):
    is_ctx = pl.program_id(1) + row_off < n_ctx_tiles
    yh = jnp.where(is_ctx, yhc_ref[0], yh_ref[0])
    merged = None
    for br, y in enumerate((ya_ref[0], yb_ref[0], yh, yd_ref[0])):
        gate = _sigmoid(gates_ref[0, :, br * D_MODEL:(br + 1) * D_MODEL] + bg_ref[br:br + 1])
        term = gate * jnp.dot(y.astype(BF16), wb_ref[br], preferred_element_type=F32)
        merged = term if merged is None else merged + term
    y = jnp.dot(merged.astype(BF16), wo_ref[...], preferred_element_type=F32)
    o_ref[0] = x_ref[0] + mod_ref[0, 5:6] * _rms(y, g_ref[3:4])


def _merge(x, mods, norm_g, p, b_gate, ya, yb, yh, yh_ctx, yd, w_branch, w_out, *, n_ctx_tiles, row_off):
    b, s, _ = x.shape
    nt = s // TM - row_off
    n_lat = mods.shape[0] - 1
    br = lambda: pl.BlockSpec((1, TM, BRANCH_DIM), lambda bi, i: (bi, i, 0))
    n_lat_tiles = s // TM - n_ctx_tiles
    br_a = pl.BlockSpec((1, TM, BRANCH_DIM), lambda bi, i: (
        bi, jnp.where(i + row_off < n_ctx_tiles, n_lat_tiles + i, i + row_off - n_ctx_tiles), 0))
    br_h = pl.BlockSpec((1, TM, BRANCH_DIM), lambda bi, i: (bi, jnp.maximum(i + row_off - n_ctx_tiles, 0), 0))
    br_hc = pl.BlockSpec((1, TM, BRANCH_DIM), lambda bi, i: (
        bi, jnp.minimum(i + row_off, max(n_ctx_tiles - 1, 0)) if yh_ctx is not None else 0, 0))
    return pl.pallas_call(
        functools.partial(_merge_kernel, n_ctx_tiles=n_ctx_tiles if yh_ctx is not None else 0, row_off=row_off),
        grid=(b, nt),
        in_specs=[pl.BlockSpec((1, TM, D_MODEL), lambda bi, i: (bi, i + row_off, 0)),
                  pl.BlockSpec((1, N_MOD, D_MODEL),
                               lambda bi, i: (jnp.where(i + row_off < n_ctx_tiles, n_lat, bi), 0, 0)),
                  pl.BlockSpec((6, D_MODEL), lambda bi, i: (0, 0)),
                  pl.BlockSpec((1, TM, GATE_COLS), lambda bi, i: (bi, i + row_off, 0)),
                  pl.BlockSpec((N_BRANCH, D_MODEL), lambda bi, i: (0, 0)),
                  br_a, br(), br_h, br_hc, br(),
                  pl.BlockSpec((N_BRANCH, BRANCH_DIM, D_MODEL), lambda bi, i: (0, 0, 0)),
                  pl.BlockSpec((D_MODEL, D_MODEL), lambda bi, i: (0, 0))],
        out_specs=pl.BlockSpec((1, TM, D_MODEL), lambda bi, i: (bi, i, 0)),
        out_shape=jax.ShapeDtypeStruct((b, nt * TM, D_MODEL), F32),
        compiler_params=_cparams(("parallel", "parallel")),
        name="merge_branches",
    )(x, mods, norm_g, p, b_gate, ya, yb, yh, yh if yh_ctx is None else yh_ctx, yd, w_branch, w_out)


def kernel(x, c, ctx, c_ctx, w_mod, b_mod, norm_g, ffn_w13, ffn_w2, w_in, b_gate, mla_norm_q, mla_norm_kv, mla_w_uq, mla_w_ukv, rwkv_mu, rwkv_w0, rwkv_w_up, rwkv_a0, rwkv_a_up, rwkv_g_up, rwkv_kvec, rwkv_r_k, rwkv_ln_g, rwkv_ln_b, hyena_conv, hyena_conv_b, hyena_w1, hyena_b1, hyena_w2, hyena_b2, hyena_w3, hyena_freq, hyena_bias, swa_sink, w_branch, w_out):
    b, n, _ = x.shape
    n_ctx = ctx.shape[1]
    nct = n_ctx // TM
    xall = jnp.concatenate([ctx, x], axis=1)
    c_all = jnp.concatenate([c, c_ctx[None]], axis=0)
    tabs_mla = _rope_tables(n, n_ctx, MLA_ROPE, MLA_NOPE, LANES)
    tabs_swa = _rope_tables(n, n_ctx, SWA_HEAD, 0, SWA_HEAD)
    depth = w_mod.shape[0]
    for l in range(depth):
        with_ctx = l + 1 < depth
        row_off = 0 if with_ctx else nct
        mods = _modulation(c_all, w_mod[l], b_mod[l])
        xall = _ffn(xall, mods, norm_g[l], ffn_w13[l, 0].astype(BF16), ffn_w2[l, 0].astype(BF16),
                    mod0=0, g0=0, n_ctx_tiles=nct)
        coef = _shift_coefficients(rwkv_mu[l], hyena_conv[l], hyena_conv_b[l])
        p, p_hy = _inproj(xall, mods, norm_g[l], _permute_w_in(w_in[l]).astype(BF16), coef, n_ctx_tiles=nct)
        q, k, v = _mla_prep(p, mla_norm_q[l], mla_norm_kv[l], mla_w_uq[l], mla_w_ukv[l], tabs_mla)
        ya = _mla_attention(q, k, v, n_ctx=n_ctx, with_ctx=with_ctx)
        y_fwd, y_bwd, bonus, gate = _rwkv_chunks(p, rwkv_kvec[l], rwkv_w0[l], rwkv_a0[l], rwkv_w_up[l],
                                                 rwkv_a_up[l], rwkv_g_up[l], rwkv_r_k[l], n_ctx=n_ctx)
        yb = _rwkv_readout(y_fwd, y_bwd, bonus, gate, rwkv_ln_g[l], rwkv_ln_b[l], row_off=row_off)
        filt = (hyena_w1[l], hyena_b1[l], hyena_w2[l], hyena_b2[l], hyena_w3[l], hyena_freq[l])
        yh = _hyena_operator(p_hy, n_ctx, n, filt, hyena_bias[l])
        yh_ctx = _hyena_operator(p_hy, 0, n_ctx, filt, hyena_bias[l]) if with_ctx else None
        q, k, v = _swa_prep(p, tabs_swa)
        yd = _swa_attention(q, k, v, swa_sink[l], n_ctx=n_ctx, q_off=row_off * TM // SWA_TQ)
        xall = _merge(xall, mods, norm_g[l], p, b_gate[l], ya, yb, yh, yh_ctx, yd, w_branch[l].astype(BF16),
                      w_out[l].astype(BF16), n_ctx_tiles=nct, row_off=row_off)
        xall = _ffn(xall, mods, norm_g[l], ffn_w13[l, 1].astype(BF16), ffn_w2[l, 1].astype(BF16),
                    mod0=6, g0=4, n_ctx_tiles=nct - row_off)
    return xall
```

```python
import functools
import math

import numpy as np
import jax
import jax.numpy as jnp
from jax import lax
from jax.experimental import pallas as pl
from jax.experimental.pallas import tpu as pltpu

F32 = jnp.float32
BF16 = jnp.bfloat16

D_MODEL = 1024
GRID_W = 64
N_BRANCH = 4
N_MOD = 9
FF_DIM = 2816
EPS = 1e-6
ROPE_BASE = 10000.0
NEG_INF = -1e30
BRANCH_DIM = 512
MLA_HEADS = 8
MLA_NOPE = 64
MLA_ROPE = 32
MLA_V = 64
MLA_Q_RANK = 256
MLA_KV_RANK = 128
RWKV_HEADS = 8
RWKV_HEAD = 64
RWKV_DIM = RWKV_HEADS * RWKV_HEAD
DECAY_LORA = 64
AAA_LORA = 64
GATE_LORA = 128
RWKV_LN_EPS = 64e-5
HYENA_DIM = 512
HYENA_ORDER = 2
HYENA_EMB = 33
HYENA_BANDS = (HYENA_EMB - 1) // 2
HYENA_FW = 64
HYENA_TARGET = 1e-2
HYENA_FAST = 0.3
HYENA_SLOW = 1.5
SWA_HEADS = 8
SWA_KV_HEADS = 2
SWA_HEAD = 64
SWA_GROUP = SWA_HEADS // SWA_KV_HEADS
WINDOW = 128
GATE_COLS = N_BRANCH * D_MODEL
MLA_COLS = MLA_Q_RANK + MLA_KV_RANK + MLA_ROPE
RWKV_COLS = 3 * RWKV_DIM + DECAY_LORA + AAA_LORA + GATE_LORA
HYENA_COLS = 3 * HYENA_DIM
SWA_COLS = (SWA_HEADS + 2 * SWA_KV_HEADS) * SWA_HEAD

LANES = 128
V7X_VMEM_LIMIT = 56 * 1024 * 1024

TM = 256
FFN_SUB_TILES = 4
INPROJ_SUB_TILES = 2
FF_CHUNK = 256
IN_CHUNK = 512
CHUNK = 64

P_GATE = 0
P_RKV = 4096
P_SWAQ = 5632
P_LORA = 6144
P_CQ = 6400
P_CKV = 6656
P_KR = 6784
P_SWAK = 6912
P_SWAV = 7040
P_COLS = 7168
P_HY = 7168
W_COLS = P_HY + HYENA_COLS
_SHIFT_COLS = ((P_RKV, P_SWAQ), (P_LORA, P_CQ), (P_HY, W_COLS))
_SHIFT_CHUNKS = [any(lo < (j + 1) * IN_CHUNK and j * IN_CHUNK < hi for lo, hi in _SHIFT_COLS)
                 for j in range(W_COLS // IN_CHUNK)]


def _cparams(sem, vmem=V7X_VMEM_LIMIT):
    return pltpu.CompilerParams(dimension_semantics=sem, vmem_limit_bytes=vmem)


def _mm(a, b):
    return jnp.dot(a.astype(BF16), b.astype(BF16), preferred_element_type=F32)


def _mm_nt(a, b):
    return lax.dot_general(a.astype(BF16), b.astype(BF16), (((1,), (1,)), ((), ())),
                           preferred_element_type=F32)


def _mm_tn(a, b):
    return lax.dot_general(a.astype(BF16), b.astype(BF16), (((0,), (0,)), ((), ())),
                           preferred_element_type=F32)


def _mm_f32(a, b):
    return jnp.dot(a, b, preferred_element_type=F32, precision=lax.Precision.HIGHEST)


def _rms(x, g):
    return x * lax.rsqrt(jnp.mean(x * x, axis=-1, keepdims=True) + EPS) * g


def _sigmoid(x):
    return 1.0 / (1.0 + jnp.exp(-x))


def _mod_kernel(c_ref, w_ref, b_ref, o_ref):
    c = c_ref[...]
    o_ref[...] = _mm(c * _sigmoid(c), w_ref[...]) + b_ref[...]


def _modulation(c_all, w_mod, b_mod):
    r = c_all.shape[0]
    rp = -(-r // 8) * 8
    c_pad = jnp.zeros((rp, D_MODEL), F32).at[:r].set(c_all)
    tn = 1024
    out = pl.pallas_call(
        _mod_kernel,
        grid=(N_MOD * D_MODEL // tn,),
        in_specs=[pl.BlockSpec((rp, D_MODEL), lambda j: (0, 0)),
                  pl.BlockSpec((D_MODEL, tn), lambda j: (0, j)),
                  pl.BlockSpec((1, tn), lambda j: (0, j))],
        out_specs=pl.BlockSpec((rp, tn), lambda j: (0, j)),
        out_shape=jax.ShapeDtypeStruct((rp, N_MOD * D_MODEL), F32),
        compiler_params=_cparams(("arbitrary",)),
        name="modulation",
    )(c_pad, w_mod, b_mod.reshape(1, -1))
    return out[:r].reshape(r, N_MOD, D_MODEL)


def _sub_tiles(n_tiles, most):
    return max(g for g in range(1, most + 1) if n_tiles % g == 0)


def _mod_specs(n_sub, tiles_per_seq, n_ctx_tiles, ctx_row):
    def spec(k):
        def index(i):
            t = i * n_sub + k
            return (jnp.where(t % tiles_per_seq < n_ctx_tiles, ctx_row, t // tiles_per_seq), 0, 0)
        return pl.BlockSpec((1, N_MOD, D_MODEL), index)
    return [spec(k) for k in range(n_sub)]


def _ffn_kernel(x_ref, *refs, mod0, g0, n_sub):
    mod_refs = refs[:n_sub]
    g_ref, w13_ref, w2_ref, o_ref = refs[n_sub:]
    tiles = [slice(t * TM, (t + 1) * TM) for t in range(n_sub)]
    u = jnp.concatenate(
        [(_rms(x_ref[rows], g_ref[g0:g0 + 1]) * (1.0 + m[0, mod0 + 1:mod0 + 2]) + m[0, mod0:mod0 + 1]).astype(BF16)
         for rows, m in zip(tiles, mod_refs)], axis=0)
    acc = jnp.zeros(x_ref.shape, F32)
    for f in range(FF_DIM // FF_CHUNK):
        lo = f * FF_CHUNK
        a = jnp.dot(u, w13_ref[:, lo:lo + FF_CHUNK], preferred_element_type=F32)
        b = jnp.dot(u, w13_ref[:, FF_DIM + lo:FF_DIM + lo + FF_CHUNK], preferred_element_type=F32)
        h = (a * _sigmoid(a) * b).astype(BF16)
        acc = acc + jnp.dot(h, w2_ref[lo:lo + FF_CHUNK, :], preferred_element_type=F32)
    hn = _rms(acc, g_ref[g0 + 1:g0 + 2])
    for rows, m in zip(tiles, mod_refs):
        o_ref[rows] = x_ref[rows] + 0.5 * m[0, mod0 + 2:mod0 + 3] * hn[rows]


def _ffn(x, mods, norm_g, w13, w2, *, mod0, g0, n_ctx_tiles):
    b, s, _ = x.shape
    n_tiles = b * s // TM
    n_sub = _sub_tiles(n_tiles, FFN_SUB_TILES)
    rows = n_sub * TM
    out = pl.pallas_call(
        functools.partial(_ffn_kernel, mod0=mod0, g0=g0, n_sub=n_sub),
        grid=(n_tiles // n_sub,),
        in_specs=[pl.BlockSpec((rows, D_MODEL), lambda i: (i, 0)),
                  *_mod_specs(n_sub, s // TM, n_ctx_tiles, mods.shape[0] - 1),
                  pl.BlockSpec((6, D_MODEL), lambda i: (0, 0)),
                  pl.BlockSpec(memory_space=pltpu.VMEM),
                  pl.BlockSpec(memory_space=pltpu.VMEM)],
        out_specs=pl.BlockSpec((rows, D_MODEL), lambda i: (i, 0)),
        out_shape=jax.ShapeDtypeStruct((b * s, D_MODEL), F32),
        compiler_params=_cparams(("parallel",)),
        name="ffn_half_step",
    )(x.reshape(b * s, D_MODEL), *([mods] * n_sub), norm_g, w13, w2)
    return out.reshape(b, s, D_MODEL)


def _inproj_kernel(x_ref, xp_ref, xn_ref, *refs, n_sub, tiles_per_seq, n_ctx_tiles):
    mod_refs = refs[:n_sub]
    g_ref, w_ref, coef_ref, o_ref, hy_ref = refs[n_sub:]
    g = g_ref[2:3]

    def modulated(x, m):
        return (_rms(x, g) * (1.0 + m[0, 4:5]) + m[0, 3:4]).astype(BF16)

    u = jnp.concatenate([modulated(x_ref[t * TM:(t + 1) * TM], m) for t, m in enumerate(mod_refs)], axis=0)
    u_prev = modulated(xp_ref[...], mod_refs[0])
    u_next = modulated(xn_ref[...], mod_refs[-1])
    rows = lax.broadcasted_iota(jnp.int32, (n_sub * TM, 1), 0)
    keep_prev = jnp.ones((n_sub * TM, 1), F32)
    keep_next = jnp.ones((n_sub * TM, 1), F32)
    for t in range(n_sub):
        w = (pl.program_id(0) * n_sub + t) % tiles_per_seq
        seg_start = (w == 0) | (w == n_ctx_tiles)
        seg_end = (w == n_ctx_tiles - 1) | (w == tiles_per_seq - 1)
        keep_prev = jnp.where((rows == t * TM) & seg_start, 0.0, keep_prev)
        keep_next = jnp.where((rows == (t + 1) * TM - 1) & seg_end, 0.0, keep_next)
    for j in range(W_COLS // IN_CHUNK):
        cols = slice(j * IN_CHUNK, (j + 1) * IN_CHUNK)
        p = jnp.dot(u, w_ref[:, cols], preferred_element_type=F32)
        if _SHIFT_CHUNKS[j]:
            p_first = jnp.dot(u_prev, w_ref[:, cols], preferred_element_type=F32)[7:8]
            p_last = jnp.dot(u_next, w_ref[:, cols], preferred_element_type=F32)[0:1]
            prev = jnp.where(rows == 0, p_first, pltpu.roll(p, 1, axis=0)) * keep_prev
            nxt = jnp.where(rows == n_sub * TM - 1, p_last, pltpu.roll(p, n_sub * TM - 1, axis=0)) * keep_next
            p = (coef_ref[0:1, cols] * p + coef_ref[1:2, cols] * prev + coef_ref[2:3, cols] * nxt
                 + coef_ref[3:4, cols])
        if j * IN_CHUNK < P_COLS:
            o_ref[:, cols] = p.astype(BF16)
        else:
            hy_ref[:, j * IN_CHUNK - P_HY:(j + 1) * IN_CHUNK - P_HY] = p


def _inproj(x, mods, norm_g, w_in_p, coef, *, n_ctx_tiles):
    b, s, _ = x.shape
    n_tiles = b * s // TM
    n_sub = _sub_tiles(n_tiles, INPROJ_SUB_TILES)
    rows = n_sub * TM
    r8 = rows // 8
    p, hy = pl.pallas_call(
        functools.partial(_inproj_kernel, n_sub=n_sub, tiles_per_seq=s // TM, n_ctx_tiles=n_ctx_tiles),
        grid=(n_tiles // n_sub,),
        in_specs=[pl.BlockSpec((rows, D_MODEL), lambda i: (i, 0)),
                  pl.BlockSpec((8, D_MODEL), lambda i: (jnp.maximum(i * r8 - 1, 0), 0)),
                  pl.BlockSpec((8, D_MODEL), lambda i: (jnp.minimum((i + 1) * r8, b * s // 8 - 1), 0)),
                  *_mod_specs(n_sub, s // TM, n_ctx_tiles, mods.shape[0] - 1),
                  pl.BlockSpec((6, D_MODEL), lambda i: (0, 0)),
                  pl.BlockSpec(memory_space=pltpu.VMEM),
                  pl.BlockSpec((4, W_COLS), lambda i: (0, 0))],
        out_specs=[pl.BlockSpec((rows, P_COLS), lambda i: (i, 0)),
                   pl.BlockSpec((rows, HYENA_COLS), lambda i: (i, 0))],
        out_shape=[jax.ShapeDtypeStruct((b * s, P_COLS), BF16),
                   jax.ShapeDtypeStruct((b * s, HYENA_COLS), F32)],
        compiler_params=_cparams(("parallel",)),
        name="in_projection",
    )(*([x.reshape(b * s, D_MODEL)] * 3), *([mods] * n_sub), norm_g, w_in_p, coef)
    return p.reshape(b, s, P_COLS), hy.reshape(b, s, HYENA_COLS)


def _shift_coefficients(rwkv_mu, hyena_conv, hyena_conv_b):
    mu = rwkv_mu.astype(F32)
    coef = jnp.zeros((4, W_COLS), F32).at[0].set(1.0)
    for off, sl in ((P_RKV, slice(0, 3 * RWKV_DIM)), (P_LORA, slice(3 * RWKV_DIM, RWKV_COLS))):
        width = sl.stop - sl.start
        coef = coef.at[0, off:off + width].set(1.0 - mu[0, sl] - mu[1, sl])
        coef = coef.at[1, off:off + width].set(mu[0, sl])
        coef = coef.at[2, off:off + width].set(mu[1, sl])
    hy = slice(P_HY, P_HY + HYENA_COLS)
    coef = coef.at[0, hy].set(hyena_conv[1]).at[1, hy].set(hyena_conv[0]).at[2, hy].set(hyena_conv[2])
    return coef.at[3, hy].set(hyena_conv_b)


def _permute_w_in(w_in):
    o_mla = GATE_COLS
    o_rwkv = o_mla + MLA_COLS
    o_hy = o_rwkv + RWKV_COLS
    o_swa = o_hy + HYENA_COLS
    z = lambda n: jnp.zeros((D_MODEL, n), w_in.dtype)
    parts = [
        w_in[:, :GATE_COLS],
        w_in[:, o_rwkv:o_rwkv + 3 * RWKV_DIM],
        w_in[:, o_swa:o_swa + SWA_HEADS * SWA_HEAD],
        w_in[:, o_rwkv + 3 * RWKV_DIM:o_rwkv + RWKV_COLS],
        w_in[:, o_mla:o_mla + MLA_Q_RANK],
        w_in[:, o_mla + MLA_Q_RANK:o_mla + MLA_Q_RANK + MLA_KV_RANK],
        z(MLA_NOPE), w_in[:, o_mla + MLA_Q_RANK + MLA_KV_RANK:o_mla + MLA_COLS],
        z(LANES - MLA_NOPE - MLA_ROPE),
        w_in[:, o_swa + SWA_HEADS * SWA_HEAD:o_swa + SWA_COLS],
        w_in[:, o_hy:o_hy + HYENA_COLS],
    ]
    out = jnp.concatenate(parts, axis=1)
    assert out.shape[1] == W_COLS
    return out


def _rope_tables(n_lat, n_ctx, rot_dim, lane0, period):
    rows = n_lat // GRID_W
    row = jnp.repeat(jnp.arange(rows, dtype=F32), GRID_W)
    col = jnp.tile(jnp.arange(GRID_W, dtype=F32), rows)
    axis_dim = rot_dim // 2
    h = axis_dim // 2
    inv_freq = ROPE_BASE ** (-jnp.arange(0, axis_dim, 2, dtype=F32) / axis_dim)
    ang_r = row[:, None] * inv_freq
    ang_c = col[:, None] * inv_freq
    cos_rot = jnp.concatenate([jnp.cos(ang_r)] * 2 + [jnp.cos(ang_c)] * 2, axis=1)
    zeros = jnp.zeros_like(ang_r)
    sin_a = jnp.concatenate([-jnp.sin(ang_r), zeros, -jnp.sin(ang_c), zeros], axis=1)
    sin_b = jnp.concatenate([zeros, jnp.sin(ang_r), zeros, jnp.sin(ang_c)], axis=1)

    def widen(t, fill):
        g = jnp.full((n_lat, period), fill, F32).at[:, lane0:lane0 + rot_dim].set(t)
        g = jnp.tile(g, (1, LANES // period))
        ctx = jnp.full((n_ctx, LANES), fill, F32)
        return jnp.concatenate([ctx, g], axis=0)

    return widen(cos_rot, 1.0), widen(sin_a, 0.0), widen(sin_b, 0.0), h


def _rope128(x, cos, sin_a, sin_b, h):
    return x * cos + pltpu.roll(x, LANES - h, axis=1) * sin_a + pltpu.roll(x, h, axis=1) * sin_b


LOG2E = math.log2(math.e)
MLA_SCALE = (MLA_NOPE + MLA_ROPE) ** -0.5 * LOG2E
MLA_ONE_LANE = (MLA_V, 0)


def _mla_prep_kernel(cq_ref, ckv_ref, kr_ref, gq_ref, gkv_ref, wq_ref, wk_ref, wv_ref, vone_ref,
                     cos_ref, sa_ref, sb_ref, q_ref, k_ref, v_ref, *, h):
    cos, sa, sb = cos_ref[...], sa_ref[...], sb_ref[...]
    cq = _rms(cq_ref[0].astype(F32), gq_ref[...]).astype(BF16)
    ckv = _rms(ckv_ref[0].astype(F32), gkv_ref[...]).astype(BF16)
    q = jnp.dot(cq, wq_ref[...], preferred_element_type=F32)
    k = jnp.dot(ckv, wk_ref[...], preferred_element_type=F32)
    kr = _rope128(kr_ref[0].astype(F32), cos, sa, sb, h)
    for hd in range(MLA_HEADS):
        sl = slice(hd * LANES, (hd + 1) * LANES)
        q_ref[0, :, sl] = (_rope128(q[:, sl], cos, sa, sb, h) * MLA_SCALE).astype(BF16)
        k_ref[0, :, sl] = (k[:, sl] + kr).astype(BF16)
    v_ref[0] = (jnp.dot(ckv, wv_ref[...], preferred_element_type=F32) + vone_ref[...]).astype(BF16)


def _mla_prep(p, norm_q, norm_kv, w_uq, w_ukv, tabs):
    b, s, _ = p.shape
    cos, sa, sb, h = tabs
    hq = MLA_NOPE + MLA_ROPE
    wq = jnp.zeros((MLA_Q_RANK, MLA_HEADS, LANES), F32).at[:, :, :hq].set(
        w_uq.reshape(MLA_Q_RANK, MLA_HEADS, hq)).reshape(MLA_Q_RANK, MLA_HEADS * LANES).astype(BF16)
    wkv = w_ukv.reshape(MLA_KV_RANK, MLA_HEADS, MLA_NOPE + MLA_V)
    wk = jnp.zeros((MLA_KV_RANK, MLA_HEADS, LANES), F32).at[:, :, :MLA_NOPE].set(
        wkv[:, :, :MLA_NOPE]).reshape(MLA_KV_RANK, MLA_HEADS * LANES).astype(BF16)
    wv_pairs = wkv[:, :, MLA_NOPE:].reshape(MLA_KV_RANK, MLA_HEADS // 2, 2, MLA_V)
    gap = ((0, 0), (0, 0), (0, LANES - MLA_V))
    wv = jnp.stack([jnp.pad(wv_pairs[:, :, 0], gap), jnp.pad(wv_pairs[:, :, 1], gap[:2] + (gap[2][::-1],))],
                   axis=2).reshape(MLA_KV_RANK, MLA_HEADS * LANES).astype(BF16)
    lane_id = np.arange(MLA_HEADS * LANES) % (2 * LANES)
    vone = jnp.asarray((lane_id == MLA_ONE_LANE[0]) | (lane_id == LANES + MLA_ONE_LANE[1]), F32)[None, :]
    full = lambda shape: pl.BlockSpec(shape, lambda bi, i: (0,) * len(shape))
    tab = pl.BlockSpec((TM, LANES), lambda bi, i: (i, 0))
    return pl.pallas_call(
        functools.partial(_mla_prep_kernel, h=h),
        grid=(b, s // TM),
        in_specs=[pl.BlockSpec((1, TM, MLA_Q_RANK), lambda bi, i: (bi, i, P_CQ // MLA_Q_RANK)),
                  pl.BlockSpec((1, TM, LANES), lambda bi, i: (bi, i, P_CKV // LANES)),
                  pl.BlockSpec((1, TM, LANES), lambda bi, i: (bi, i, P_KR // LANES)),
                  full((1, MLA_Q_RANK)), full((1, MLA_KV_RANK)),
                  full(wq.shape), full(wk.shape), full(wv.shape), full(vone.shape), tab, tab, tab],
        out_specs=[pl.BlockSpec((1, TM, MLA_HEADS * LANES), lambda bi, i: (bi, i, 0))] * 3,
        out_shape=[jax.ShapeDtypeStruct((b, s, MLA_HEADS * LANES), BF16)] * 3,
        compiler_params=_cparams(("parallel", "parallel")),
        name="mla_prep",
    )(p, p, p, norm_q.reshape(1, -1), norm_kv.reshape(1, -1), wq, wk, wv, vone, cos, sa, sb)


MLA_Q_TILES = 1


def _mla_attn_kernel(*refs, n_ctx, n_lat_steps):
    q_refs = refs[:MLA_Q_TILES]
    k_ref, v_ref, o_ref = refs[MLA_Q_TILES:]

    def attend(n_keys):
        outs = []
        for hd in range(2):
            sl = slice(hd * LANES, (hd + 1) * LANES)
            q = jnp.concatenate([q_ref[0, :, sl] for q_ref in q_refs], axis=0)
            s = lax.dot_general(q, k_ref[0, :n_keys, sl], (((1,), (1,)), ((), ())),
                                preferred_element_type=F32)
            e = jnp.exp2(s - jnp.max(s, axis=-1, keepdims=True)).astype(BF16)
            o = jnp.dot(e, v_ref[0, :n_keys, sl], preferred_element_type=F32)
            one = MLA_ONE_LANE[hd]
            outs.append(o / o[:, one:one + 1])
        lane = lax.broadcasted_iota(jnp.int32, outs[0].shape, 1)
        o_ref[0] = jnp.where(lane < MLA_V, outs[0], outs[1]).astype(BF16)

    @pl.when(pl.program_id(2) < n_lat_steps)
    def _():
        attend(k_ref.shape[1])

    @pl.when(pl.program_id(2) >= n_lat_steps)
    def _():
        attend(n_ctx)


def _mla_attention(q, k, v, *, n_ctx, with_ctx):
    b, s, _ = q.shape
    nct = n_ctx // TM
    n_lat = s - n_ctx
    n_lat_steps = n_lat // (MLA_Q_TILES * TM)
    assert n_lat % (MLA_Q_TILES * TM) == 0 and (nct == 1 or not with_ctx)

    def q_spec(t):
        return pl.BlockSpec((1, TM, 2 * LANES), lambda bi, hp, j: (
            bi, jnp.where(j < n_lat_steps, nct + j * MLA_Q_TILES + t, 0), hp))

    kv = pl.BlockSpec((1, s, 2 * LANES), lambda bi, hp, j: (bi, 0, hp))
    return pl.pallas_call(
        functools.partial(_mla_attn_kernel, n_ctx=n_ctx, n_lat_steps=n_lat_steps),
        grid=(b, MLA_HEADS // 2, n_lat_steps + (1 if with_ctx else 0)),
        in_specs=[*[q_spec(t) for t in range(MLA_Q_TILES)], kv, kv],
        out_specs=pl.BlockSpec((1, MLA_Q_TILES * TM, LANES), lambda bi, hp, j: (bi, j, hp)),
        out_shape=jax.ShapeDtypeStruct((b, n_lat + (n_ctx if with_ctx else 0), MLA_HEADS * MLA_V), BF16),
        compiler_params=_cparams(("parallel", "parallel", "parallel")),
        name="mla_attention",
    )(*([q] * MLA_Q_TILES), k, v)


SWA_SCALE = SWA_HEAD ** -0.5 * LOG2E
SWA_TQ = 128
SWA_ONE_LANE = (SWA_HEAD, 0)


def _swa_prep_kernel(q_ref, k_ref, v_ref, cos_ref, sa_ref, sb_ref, qo_ref, ko_ref, vo_ref, *, h):
    cos, sa, sb = cos_ref[...], sa_ref[...], sb_ref[...]
    lane = lax.broadcasted_iota(jnp.int32, cos.shape, 1)
    low = lane < SWA_HEAD
    for j in range(SWA_HEADS // 2):
        blk = _rope128(q_ref[0, :, j * LANES:(j + 1) * LANES].astype(F32), cos, sa, sb, h) * SWA_SCALE
        qo_ref[0, :, (2 * j) * LANES:(2 * j + 1) * LANES] = jnp.where(low, blk, 0.0).astype(BF16)
        qo_ref[0, :, (2 * j + 1) * LANES:(2 * j + 2) * LANES] = jnp.where(
            low, pltpu.roll(blk, SWA_HEAD, axis=1), 0.0).astype(BF16)
    kb = _rope128(k_ref[0].astype(F32), cos, sa, sb, h)
    ko_ref[0, :, :LANES] = jnp.where(low, kb, 0.0).astype(BF16)
    ko_ref[0, :, LANES:] = jnp.where(low, pltpu.roll(kb, SWA_HEAD, axis=1), 0.0).astype(BF16)
    vb = v_ref[0].astype(F32)
    vr = pltpu.roll(vb, SWA_HEAD, axis=1)
    one_lo = jnp.where(lane == SWA_ONE_LANE[0], 1.0, 0.0)
    one_hi = jnp.where(lane == SWA_ONE_LANE[1], 1.0, 0.0)
    vo_ref[0, :, 0 * LANES:1 * LANES] = jnp.where(low, vb, one_lo).astype(BF16)
    vo_ref[0, :, 1 * LANES:2 * LANES] = jnp.where(low, one_hi, vr).astype(BF16)
    vo_ref[0, :, 2 * LANES:3 * LANES] = jnp.where(low, vr, one_lo).astype(BF16)
    vo_ref[0, :, 3 * LANES:4 * LANES] = jnp.where(low, one_hi, vb).astype(BF16)


def _swa_prep(p, tabs):
    b, s, _ = p.shape
    cos, sa, sb, h = tabs
    tab = pl.BlockSpec((TM, LANES), lambda bi, i: (i, 0))
    nq = SWA_HEADS * SWA_HEAD
    return pl.pallas_call(
        functools.partial(_swa_prep_kernel, h=h),
        grid=(b, s // TM),
        in_specs=[pl.BlockSpec((1, TM, nq), lambda bi, i: (bi, i, P_SWAQ // nq)),
                  pl.BlockSpec((1, TM, LANES), lambda bi, i: (bi, i, P_SWAK // LANES)),
                  pl.BlockSpec((1, TM, LANES), lambda bi, i: (bi, i, P_SWAV // LANES)),
                  tab, tab, tab],
        out_specs=[pl.BlockSpec((1, TM, SWA_HEADS * LANES), lambda bi, i: (bi, i, 0)),
                   pl.BlockSpec((1, TM, SWA_KV_HEADS * LANES), lambda bi, i: (bi, i, 0)),
                   pl.BlockSpec((1, TM, 4 * LANES), lambda bi, i: (bi, i, 0))],
        out_shape=[jax.ShapeDtypeStruct((b, s, SWA_HEADS * LANES), BF16),
                   jax.ShapeDtypeStruct((b, s, SWA_KV_HEADS * LANES), BF16),
                   jax.ShapeDtypeStruct((b, s, 4 * LANES), BF16)],
        compiler_params=_cparams(("parallel", "parallel")),
        name="swa_prep",
    )(p, p, p, cos, sa, sb)


def _swa_attn_kernel(sink_ref, q_ref, k_ref, v_ref, o_ref, *, n_ctx, q_off):
    i = pl.program_id(1) + q_off
    s_len = k_ref.shape[1]
    tq = SWA_TQ
    n_loc = tq + 2 * WINDOW
    r0 = i * tq
    is_lat = r0 >= n_ctx
    start = pl.multiple_of(jnp.clip(r0 - WINDOW, 0, s_len - n_loc), LANES)
    rows_g = SWA_GROUP * tq
    row = lax.broadcasted_iota(jnp.int32, (rows_g, n_loc), 0)
    qpos = r0 - n_ctx + row % tq
    kpos = start - n_ctx + lax.broadcasted_iota(jnp.int32, (rows_g, n_loc), 1)
    loc_ok = (jnp.abs(kpos - qpos) <= WINDOW) & (kpos >= 0) & is_lat
    k_loc = k_ref[0, pl.ds(start, n_loc), :]
    v_loc = v_ref[0, pl.ds(start, n_loc), :]
    k_ctx = k_ref[0, 0:n_ctx, :]
    v_ctx = v_ref[0, 0:n_ctx, :]
    head_row = lax.broadcasted_iota(jnp.int32, (rows_g, 1), 0) // tq
    lane = lax.broadcasted_iota(jnp.int32, (tq, LANES), 1)
    stages = []
    for g in range(SWA_KV_HEADS):
        q = jnp.concatenate([q_ref[0, :, hd * LANES:(hd + 1) * LANES]
                             for hd in range(g * SWA_GROUP, (g + 1) * SWA_GROUP)], axis=0)
        kg = slice(g * LANES, (g + 1) * LANES)
        s_loc = lax.dot_general(q, k_loc[:, kg], (((1,), (1,)), ((), ())), preferred_element_type=F32)
        s_ctx = lax.dot_general(q, k_ctx[:, kg], (((1,), (1,)), ((), ())), preferred_element_type=F32)
        sink = jnp.zeros((rows_g, 1), F32)
        for hh in range(SWA_GROUP):
            sink = jnp.where(head_row == hh, sink_ref[g * SWA_GROUP + hh] * LOG2E, sink)
        stages.append((jnp.where(loc_ok, s_loc, NEG_INF), s_ctx, sink))
    for g, (s_loc, s_ctx, sink) in enumerate(stages):
        m = jnp.maximum(jnp.maximum(jnp.max(s_loc, axis=-1, keepdims=True),
                                    jnp.max(s_ctx, axis=-1, keepdims=True)), sink)
        e = jnp.concatenate([jnp.exp2(s_loc - m), jnp.exp2(s_ctx - m)], axis=1).astype(BF16)
        e_sink = jnp.exp2(sink - m)
        outs = []
        for par in range(2):
            vg = slice((2 * g + par) * LANES, (2 * g + par + 1) * LANES)
            o = jnp.dot(e, jnp.concatenate([v_loc[:, vg], v_ctx[:, vg]], axis=0), preferred_element_type=F32)
            one = SWA_ONE_LANE[par]
            outs.append(o / (o[:, one:one + 1] + e_sink))
        for pi in range(SWA_GROUP // 2):
            even = outs[0][(2 * pi) * tq:(2 * pi + 1) * tq]
            odd = outs[1][(2 * pi + 1) * tq:(2 * pi + 2) * tq]
            blk = g * (SWA_GROUP // 2) + pi
            o_ref[0, :, blk * LANES:(blk + 1) * LANES] = jnp.where(lane < SWA_HEAD, even, odd).astype(BF16)


def _swa_attention(q, k, v, sink, *, n_ctx, q_off):
    b, s, _ = q.shape
    nq = s // SWA_TQ - q_off
    return pl.pallas_call(
        functools.partial(_swa_attn_kernel, n_ctx=n_ctx, q_off=q_off),
        grid=(b, nq),
        in_specs=[pl.BlockSpec(memory_space=pltpu.SMEM),
                  pl.BlockSpec((1, SWA_TQ, SWA_HEADS * LANES), lambda bi, i: (bi, i + q_off, 0)),
                  pl.BlockSpec((1, s, SWA_KV_HEADS * LANES), lambda bi, i: (bi, 0, 0)),
                  pl.BlockSpec((1, s, 4 * LANES), lambda bi, i: (bi, 0, 0))],
        out_specs=pl.BlockSpec((1, SWA_TQ, SWA_HEADS * SWA_HEAD), lambda bi, i: (bi, i, 0)),
        out_shape=jax.ShapeDtypeStruct((b, nq * SWA_TQ, SWA_HEADS * SWA_HEAD), BF16),
        compiler_params=_cparams(("parallel", "parallel")),
        name="swa_attention",
    )(sink, q, k, v)


N_PAIR = RWKV_HEADS // 2
N_DOUBLINGS = int(math.log2(CHUNK))


def _softplus(x):
    return jnp.maximum(x, 0.0) + jnp.log(1.0 + jnp.exp(-jnp.abs(x)))


def _headsum(x, bd):
    hi = x.astype(BF16)
    lo = (x - hi.astype(F32)).astype(BF16)
    return (jnp.dot(hi, bd, preferred_element_type=F32) + jnp.dot(lo, bd, preferred_element_type=F32))


def _chunk_cumsum(x, reverse):
    rows = lax.broadcasted_iota(jnp.int32, x.shape, 0)
    s = 1
    while s < CHUNK:
        if reverse:
            x = x + jnp.where(rows < CHUNK - s, pltpu.roll(x, CHUNK - s, axis=0), 0.0)
        else:
            x = x + jnp.where(rows >= s, pltpu.roll(x, s, axis=0), 0.0)
        s *= 2
    return x


def _head_rows(x):
    first = lax.broadcasted_iota(jnp.int32, x.shape, 1) < RWKV_HEAD
    return jnp.concatenate([jnp.where(first, x, 0.0), jnp.where(first, 0.0, x)], axis=0)


def _mm_x3(a, b):
    a_hi = a.astype(BF16)
    b_hi = b.astype(BF16)
    a_lo = (a - a_hi.astype(F32)).astype(BF16)
    b_lo = (b - b_hi.astype(F32)).astype(BF16)
    dot = functools.partial(jnp.dot, preferred_element_type=F32)
    return dot(a_hi, b_hi) + dot(a_hi, b_lo) + dot(a_lo, b_hi)


def _rwkv_chunk_kernel(rf_ref, kf_ref, vf_ref, lof_ref, rb_ref, kb_ref, vb_ref, lob_ref,
                       kvec_ref, w0_ref, a0_ref, wup_ref, aup_ref, gup_ref, rk_ref, bd_ref,
                       yf_ref, yb_ref, bonus_ref, gate_ref, s_ref):
    @pl.when(pl.program_id(1) == 0)
    def _():
        s_ref[...] = jnp.zeros_like(s_ref)

    bd = bd_ref[...]
    data = []
    for refs in ((rf_ref, kf_ref, vf_ref, lof_ref), (rb_ref, kb_ref, vb_ref, lob_ref)):
        r, k, v, lora = (ref[0].astype(F32) for ref in refs)
        kk = k * kvec_ref[0:1]
        kk = kk * lax.rsqrt(_headsum(kk * kk, bd) + 1e-12)
        data.append((r, k, v, lora, kk))
    r, k, v, lora, _ = data[0]
    gate_ref[0] = _mm(_sigmoid(lora), gup_ref[...])
    k_both = sum(k * (1.0 + (_sigmoid(a0_ref[d] + _mm(lora, aup_ref[d])) - 1.0) * kvec_ref[1:2]) for d in range(2))
    bonus_ref[0] = _headsum(r * (0.5 * k_both) * rk_ref[...], bd) * v

    trow = lax.broadcasted_iota(jnp.int32, (2 * CHUNK, 4 * CHUNK), 0) % CHUNK
    tcol = lax.broadcasted_iota(jnp.int32, (2 * CHUNK, 4 * CHUNK), 1) % CHUNK
    sq_r = lax.broadcasted_iota(jnp.int32, (LANES, LANES), 0)
    sq_c = lax.broadcasted_iota(jnp.int32, (LANES, LANES), 1)
    same_head = (sq_r // RWKV_HEAD) == (sq_c // RWKV_HEAD)
    eye = sq_r == sq_c

    chains = []
    for d in range(2):
        reverse = d == 1
        r, k, v, lora, kk = data[d]
        w_log = -_softplus(-(w0_ref[d] + _mm(jnp.tanh(lora), wup_ref[d]))) - 0.5
        ld = -jnp.exp(w_log)
        a = _sigmoid(a0_ref[d] + _mm(lora, aup_ref[d]))
        k_d = k * (1.0 + (a - 1.0) * kvec_ref[1:2])
        b_d = kk * a
        lg = _chunk_cumsum(ld, reverse)
        last = 0 if reverse else CHUNK - 1
        tot = lg[last:last + 1]
        e_neg = jnp.exp(-lg)
        e_end = jnp.exp(tot - lg)
        z_t = -kk * jnp.exp(lg - ld)
        r_t = r * jnp.exp(lg)
        b_t = b_d * e_neg
        k_t = k_d * e_neg
        b_e = b_d * e_end
        k_e = k_d * e_end
        e_tot = jnp.exp(tot)
        before = (tcol > trow) if reverse else (tcol < trow)
        before_eq = (tcol >= trow) if reverse else (tcol <= trow)
        for pr in range(N_PAIR):
            sl = slice(pr * LANES, (pr + 1) * LANES)
            ch = {"d": d, "sl": sl, "rp": r_t[:, sl], "vp": v[:, sl], "e_tot": e_tot[:, sl],
                  "be_ke": jnp.concatenate([b_e[:, sl], k_e[:, sl]], axis=0)}
            zst, rst, vst = _head_rows(z_t[:, sl]), _head_rows(r_t[:, sl]), _head_rows(v[:, sl])
            bkst = jnp.concatenate([_head_rows(b_t[:, sl]), _head_rows(k_t[:, sl])], axis=0)
            ch["lz"] = jnp.where(before, _mm_nt(zst, bkst), 0.0)
            ch["lr"] = jnp.where(before_eq, _mm_nt(rst, bkst), 0.0)
            ch["zst"], ch["vst"] = zst, vst
            chains.append(ch)

    unit = jnp.where(eye, 1.0, 0.0)
    for ch in chains:
        ch["pw"] = ch["lz"][:, :LANES]
        ch["t"] = unit + ch["pw"]
        ch["x"] = jnp.concatenate([ch["zst"], _mm(ch["lz"][:, LANES:], ch["vst"])], axis=1)
    for it in range(1, N_DOUBLINGS):
        for ch in chains:
            ch["pw"] = _mm(ch["pw"], ch["pw"])
        for ch in chains:
            ch["t"] = ch["t"] + _mm(ch["pw"], ch["t"])
    for ch in chains:
        ch["x"] = _mm(ch["t"], ch["x"])
    for ch in chains:
        low = jnp.concatenate([jnp.zeros_like(ch["vst"]), ch["vst"]], axis=1)
        op = _mm(ch["lr"], jnp.concatenate([ch["x"], low], axis=0))
        ch["op"] = op[:CHUNK] + op[CHUNK:]
        ch["xp"] = ch["x"][:CHUNK] + ch["x"][CHUNK:]
    for ch in chains:
        rhs = jnp.concatenate([ch["xp"], jnp.concatenate([jnp.zeros_like(ch["vp"]), ch["vp"]], axis=1)], axis=0)
        ag = _mm_tn(ch["be_ke"], rhs)
        a_full = ag[:, :LANES] + jnp.where(eye, jnp.broadcast_to(ch["e_tot"], (LANES, LANES)), 0.0)
        ch["a"] = jnp.where(same_head, a_full, 0.0)
        ch["g"] = jnp.where(same_head, ag[:, LANES:], 0.0)
    for idx, ch in enumerate(chains):
        st = s_ref[idx]
        y_ref = yf_ref if ch["d"] == 0 else yb_ref
        y_ref[0, :, ch["sl"]] = _mm(ch["rp"] + ch["op"][:, :LANES], st) + ch["op"][:, LANES:]
        s_ref[idx] = _mm_x3(ch["a"], st) + ch["g"]


def _head_block_diag():
    idx = np.arange(RWKV_DIM) // RWKV_HEAD
    return jnp.asarray(idx[:, None] == idx[None, :], BF16)


def _rwkv_chunks(p, kvec, w0, a0, w_up, a_up, g_up, r_k, *, n_ctx):
    b, s, _ = p.shape
    nc = s // CHUNK
    ncc = n_ctx // CHUNK
    lora_w = DECAY_LORA + AAA_LORA + GATE_LORA
    wup = jnp.zeros((2, lora_w, RWKV_DIM), F32).at[:, :DECAY_LORA].set(w_up).astype(BF16)
    aup = jnp.zeros((2, lora_w, RWKV_DIM), F32).at[:, DECAY_LORA:DECAY_LORA + AAA_LORA].set(a_up).astype(BF16)
    gup = jnp.zeros((lora_w, RWKV_DIM), F32).at[DECAY_LORA + AAA_LORA:].set(g_up).astype(BF16)

    def fwd(c):
        return c

    def back(c):
        return jnp.where(c < ncc, ncc - 1 - c, nc - 1 - (c - ncc))

    full = lambda shape: pl.BlockSpec(shape, lambda bi, c: (0,) * len(shape))

    def inputs(chunk):
        col = lambda off: pl.BlockSpec((1, CHUNK, RWKV_DIM), lambda bi, c: (bi, chunk(c), off // RWKV_DIM))
        return [col(P_RKV), col(P_RKV + RWKV_DIM), col(P_RKV + 2 * RWKV_DIM),
                pl.BlockSpec((1, CHUNK, lora_w), lambda bi, c: (bi, chunk(c), P_LORA // lora_w))]

    tok = lambda chunk: pl.BlockSpec((1, CHUNK, RWKV_DIM), lambda bi, c: (bi, chunk(c), 0))
    return pl.pallas_call(
        _rwkv_chunk_kernel,
        grid=(b, nc),
        in_specs=[*inputs(fwd), *inputs(back),
                  full((2, RWKV_DIM)), full((2, 1, RWKV_DIM)), full((2, 1, RWKV_DIM)),
                  full(wup.shape), full(aup.shape), full(gup.shape), full((1, RWKV_DIM)),
                  full((RWKV_DIM, RWKV_DIM))],
        out_specs=[tok(fwd), tok(back), tok(fwd), tok(fwd)],
        out_shape=[jax.ShapeDtypeStruct((b, s, RWKV_DIM), F32)] * 4,
        scratch_shapes=[pltpu.VMEM((2 * N_PAIR, LANES, LANES), F32)],
        compiler_params=_cparams(("parallel", "arbitrary")),
        name="rwkv_chunks",
    )(*([p] * 8), kvec, w0.reshape(2, 1, -1), a0.reshape(2, 1, -1), wup, aup, gup, r_k.reshape(1, -1),
      _head_block_diag())


def _rwkv_readout_kernel(yf_ref, yb_ref, bonus_ref, gate_ref, lng_ref, lnb_ref, bd_ref, o_ref):
    bd = bd_ref[...]
    y = yf_ref[0] + yb_ref[0]
    inv_n = 1.0 / RWKV_HEAD
    dev = y - _headsum(y, bd) * inv_n
    var = _headsum(dev * dev, bd) * inv_n
    yn = dev * lax.rsqrt(var + RWKV_LN_EPS) * lng_ref[...] + lnb_ref[...]
    o_ref[0] = ((yn + bonus_ref[0]) * gate_ref[0]).astype(BF16)


def _rwkv_readout(y_fwd, y_bwd, bonus, gate, ln_g, ln_b, *, row_off):
    b, s, _ = y_fwd.shape
    nt = s // TM - row_off
    full = lambda shape: pl.BlockSpec(shape, lambda bi, i: (0,) * len(shape))
    tok = pl.BlockSpec((1, TM, RWKV_DIM), lambda bi, i: (bi, i + row_off, 0))
    return pl.pallas_call(
        _rwkv_readout_kernel,
        grid=(b, nt),
        in_specs=[tok, tok, tok, tok, full((1, RWKV_DIM)), full((1, RWKV_DIM)), full((RWKV_DIM, RWKV_DIM))],
        out_specs=pl.BlockSpec((1, TM, RWKV_DIM), lambda bi, i: (bi, i, 0)),
        out_shape=jax.ShapeDtypeStruct((b, nt * TM, RWKV_DIM), BF16),
        compiler_params=_cparams(("parallel", "parallel")),
        name="rwkv_readout",
    )(y_fwd, y_bwd, bonus, gate, ln_g.reshape(1, -1), ln_b.reshape(1, -1), _head_block_diag())


DFT_N2 = LANES
HY_MIN_LEN = 1024
HY_CT = 128


def _hyena_mlp_kernel(feats_ref, w1_ref, b1_ref, w2_ref, b2_ref, freq_ref, h_ref):
    h = jnp.sin(freq_ref[0:1] * (_mm_f32(feats_ref[0], w1_ref[...]) + b1_ref[...]))
    h_ref[0] = jnp.sin(freq_ref[1:2] * (_mm_f32(h, w2_ref[...]) + b2_ref[...]))


def _hyena_filter_kernel(h_ref, w3f_ref, w3b_ref, t_ref, delta_ref, k_ref):
    hf = _mm_f32(h_ref[0], w3f_ref[...]) * jnp.exp(-t_ref[0] * delta_ref[...])
    hb = _mm_f32(h_ref[1], w3b_ref[...]) * jnp.exp(-t_ref[1] * delta_ref[...])
    norm = (jnp.sum(jnp.abs(hf), axis=0, keepdims=True) + jnp.sum(jnp.abs(hb), axis=0, keepdims=True))
    r = pl.program_id(2)
    rows = lax.broadcasted_iota(jnp.int32, hf.shape, 0)
    tail = jnp.where(rows == 0, 0.0, hb)
    blk = jnp.where(r == 0, hf, jnp.where(r == pl.num_programs(2) - 1, tail, 0.0))
    k_ref[0] = blk / norm


def _hyena_filter_buffer(n, nc, w1, b1, w2, b2, w3, freq):
    lag = jnp.stack([jnp.arange(n), jnp.where(jnp.arange(n) == 0, 0, n - jnp.arange(n))]).astype(F32)
    t = (lag / (n - 1))[:, :, None]
    bands = jnp.linspace(1e-4, HYENA_BANDS - 1, HYENA_BANDS, dtype=F32)
    ang = (2.0 * math.pi / n) * lag[:, :, None] * bands[None, None, :]
    feats = jnp.concatenate([t, jnp.cos(ang), -jnp.sin(ang),
                             jnp.zeros((2, n, HYENA_FW - HYENA_EMB), F32)], axis=-1)
    w1p = jnp.zeros((HYENA_FW, HYENA_FW), F32).at[:HYENA_EMB].set(w1)
    deltas = jnp.abs(jnp.linspace(math.log(HYENA_TARGET) / HYENA_SLOW,
                                  math.log(HYENA_TARGET) / HYENA_FAST, HYENA_DIM, dtype=F32))[None, :]
    fixed = lambda shape: pl.BlockSpec(shape, lambda d: (0,) * len(shape))
    hidden = pl.pallas_call(
        _hyena_mlp_kernel,
        grid=(2,),
        in_specs=[pl.BlockSpec((1, n, HYENA_FW), lambda d: (d, 0, 0)), fixed((HYENA_FW, HYENA_FW)),
                  fixed((1, HYENA_FW)), fixed((HYENA_FW, HYENA_FW)), fixed((1, HYENA_FW)),
                  fixed((2, HYENA_FW))],
        out_specs=pl.BlockSpec((1, n, HYENA_FW), lambda d: (d, 0, 0)),
        out_shape=jax.ShapeDtypeStruct((2, n, HYENA_FW), F32),
        compiler_params=_cparams(("parallel",)),
        name="hyena_filter_mlp",
    )(feats, w1p, b1.reshape(1, -1), w2, b2.reshape(1, -1), freq)
    tc = 256
    nj = HYENA_DIM // tc
    full = lambda shape: pl.BlockSpec(shape, lambda o, j, r: (0,) * len(shape))
    return pl.pallas_call(
        _hyena_filter_kernel,
        grid=(HYENA_ORDER, nj, nc // n),
        in_specs=[full((2, n, HYENA_FW)),
                  pl.BlockSpec((HYENA_FW, tc), lambda o, j, r: (0, o * 2 * nj + j)),
                  pl.BlockSpec((HYENA_FW, tc), lambda o, j, r: (0, o * 2 * nj + nj + j)),
                  full((2, n, 1)), pl.BlockSpec((1, tc), lambda o, j, r: (0, j))],
        out_specs=pl.BlockSpec((1, n, tc), lambda o, j, r: (o, r, j)),
        out_shape=jax.ShapeDtypeStruct((HYENA_ORDER, nc, HYENA_DIM), F32),
        compiler_params=_cparams(("parallel", "parallel", "parallel")),
        name="hyena_filters",
    )(hidden, w3, w3, t, deltas)


HY_J = 8
HY_GROUPS = DFT_N2 // HY_J


def _dft_tables(n1):
    nc = n1 * DFT_N2
    f1 = np.arange(n1)
    ang = 2.0 * np.pi * ((f1[:, None] * f1[None, :]) % n1) / n1
    eye = np.eye(HY_J)
    w1 = np.kron(np.concatenate([np.cos(ang), -np.sin(ang)], axis=0), eye)
    v3 = np.kron(np.concatenate([np.cos(ang), -np.sin(ang)], axis=1), eye) / nc
    t2 = np.arange(DFT_N2).reshape(HY_GROUPS, 1, HY_J)
    tw = 2.0 * np.pi * ((f1[None, :, None] * t2) % nc) / nc
    tw = np.broadcast_to(tw.reshape(HY_GROUPS, n1 * HY_J, 1), (HY_GROUPS, n1 * HY_J, LANES))
    k = np.arange(DFT_N2)
    phi = 2.0 * np.pi * ((k[:, None] * k[None, :]) % DFT_N2) / DFT_N2
    c, s = np.cos(phi), np.sin(phi)
    f_fwd = np.block([[c, s], [-s, c]])
    f_inv = np.block([[c, -s], [s, c]])
    names = ("w1", "v3", "ctw", "stw", "f_fwd", "f_inv")
    return {n: jnp.asarray(t, BF16) for n, t in zip(names, (w1, v3, np.cos(tw), np.sin(tw), f_fwd, f_inv))}


def _tiles(ref, rows, g, lead=()):
    return jnp.concatenate([ref[lead + (pl.ds(r * DFT_N2 + g * HY_J, HY_J), slice(None))] for r in rows], axis=0)


def _dft_rows_in(x_ref, w_ref, ctw_ref, stw_ref, a_ref, t1n, n1, t1_valid):
    half = n1 * HY_J
    for g in range(HY_GROUPS):
        xg = _tiles(x_ref, range(t1_valid), g, lead=(0,))
        pq = _mm(w_ref[:, :t1_valid * HY_J], xg)
        p, q = pq[:half], pq[half:]
        c, s = ctw_ref[g].astype(F32), stw_ref[g].astype(F32)
        re = c * p + s * q
        im = c * q - s * p
        for m in range(n1):
            rows = slice(m * HY_J, (m + 1) * HY_J)
            a_ref[pl.ds(m * DFT_N2 + g * HY_J, HY_J), :] = re[rows]
            a_ref[pl.ds((n1 + m) * DFT_N2 + g * HY_J, HY_J), :] = im[rows]


def _slab(f1):
    return slice(f1 * DFT_N2, (f1 + 1) * DFT_N2)


def _spectrum_kernel(x_ref, w_ref, ctw_ref, stw_ref, ff_ref, k_ref, a_ref, *, n1):
    _dft_rows_in(x_ref, w_ref, ctw_ref, stw_ref, a_ref, n1, n1, n1)
    for f1 in range(n1):
        a = jnp.concatenate([a_ref[_slab(f1)], a_ref[_slab(n1 + f1)]], axis=0)
        k_ref[0, f1] = _mm(ff_ref[...], a)


def _filter_spectrum(kbuf, tabs, n1):
    no, nc, _ = kbuf.shape
    nj = HYENA_DIM // HY_CT
    const = lambda a: pl.BlockSpec(a.shape, lambda o, j: (0,) * a.ndim)
    consts = [tabs[n] for n in ("w1", "ctw", "stw", "f_fwd")]
    return pl.pallas_call(
        functools.partial(_spectrum_kernel, n1=n1),
        grid=(no, nj),
        in_specs=[pl.BlockSpec((1, nc, HY_CT), lambda o, j: (o, 0, j)), *[const(a) for a in consts]],
        out_specs=pl.BlockSpec((1, n1, 2 * DFT_N2, HY_CT), lambda o, j: (o, 0, 0, j)),
        out_shape=jax.ShapeDtypeStruct((no, n1, 2 * DFT_N2, HYENA_DIM), F32),
        scratch_shapes=[pltpu.VMEM((2 * n1 * DFT_N2, HY_CT), F32)],
        compiler_params=_cparams(("parallel", "parallel")),
        name="hyena_filter_spectrum",
    )(kbuf, *consts)


def _hyena_conv_kernel(x_ref, gate_ref, k_ref, w_ref, ctw_ref, stw_ref, ff_ref, fi_ref, v_ref, bias_ref,
                       o_ref, a_ref, *, t1n, n1, t1_valid):
    _dft_rows_in(x_ref, w_ref, ctw_ref, stw_ref, a_ref, t1n, n1, t1_valid)
    for f1 in range(n1):
        re, im = _slab(f1), _slab(n1 + f1)
        x = _mm(ff_ref[...], jnp.concatenate([a_ref[re], a_ref[im]], axis=0))
        xre, xim = x[:DFT_N2], x[DFT_N2:]
        kre, kim = k_ref[0, f1, :DFT_N2], k_ref[0, f1, DFT_N2:]
        bm = _mm(fi_ref[...], jnp.concatenate([xre * kre - xim * kim, xre * kim + xim * kre], axis=0))
        a_ref[re] = bm[:DFT_N2]
        a_ref[im] = bm[DFT_N2:]
    bias = bias_ref[...]
    for g in range(HY_GROUPS):
        br, bi = _tiles(a_ref, range(n1), g), _tiles(a_ref, range(n1, 2 * n1), g)
        c, s = ctw_ref[g].astype(F32), stw_ref[g].astype(F32)
        y = _mm(v_ref[:t1n * HY_J], jnp.concatenate([c * br - s * bi, s * br + c * bi], axis=0))
        for t1 in range(t1n):
            rows = pl.ds(t1 * DFT_N2 + g * HY_J, HY_J)
            o_ref[0, rows, :] = gate_ref[0, rows, :] * (y[t1 * HY_J:(t1 + 1) * HY_J] + bias * x_ref[0, rows, :])


def _hyena_conv(z, z_spec, gate, gate_spec, kspec, order, bias, tabs, n1, n_pad, n):
    bx = z.shape[0]
    t1n = n_pad // DFT_N2
    consts = [tabs[name] for name in ("w1", "ctw", "stw", "f_fwd", "f_inv", "v3")]
    const = lambda a: pl.BlockSpec(a.shape, lambda j, bi: (0,) * a.ndim)
    return pl.pallas_call(
        functools.partial(_hyena_conv_kernel, t1n=t1n, n1=n1, t1_valid=-(-n // DFT_N2)),
        grid=(HYENA_DIM // HY_CT, bx),
        in_specs=[z_spec, gate_spec,
                  pl.BlockSpec((1, n1, 2 * DFT_N2, HY_CT), lambda j, bi: (order, 0, 0, j)),
                  *[const(a) for a in consts],
                  pl.BlockSpec((1, HY_CT), lambda j, bi: (0, j))],
        out_specs=pl.BlockSpec((1, n_pad, HY_CT), lambda j, bi: (bi, 0, j)),
        out_shape=jax.ShapeDtypeStruct((bx, n_pad, HYENA_DIM), F32),
        scratch_shapes=[pltpu.VMEM((2 * n1 * DFT_N2, HY_CT), F32)],
        compiler_params=_cparams(("parallel", "parallel")),
        name="hyena_conv",
    )(z, gate, kspec, *consts, bias.reshape(1, -1))


def _hyena_operator(p_hy, row0, n, filt_params, bias):
    n_pad = max(n, HY_MIN_LEN)
    nc = 2 * n_pad
    n1 = nc // DFT_N2
    tabs = _dft_tables(n1)
    kspec = _filter_spectrum(_hyena_filter_buffer(n, nc, *filt_params), tabs, n1)
    nj = HYENA_DIM // HY_CT
    window = lambda part: pl.BlockSpec((pl.Element(1), pl.Element(n_pad), pl.Element(HY_CT)),
                                       lambda j, bi: (bi, row0, (part * nj + j) * HY_CT))
    own = pl.BlockSpec((1, n_pad, HY_CT), lambda j, bi: (bi, 0, j))
    z = _hyena_conv(p_hy, window(0), p_hy, window(1), kspec, 0, bias[0], tabs, n1, n_pad, n)
    return _hyena_conv(z, own, p_hy, window(2), kspec, 1, bias[1], tabs, n1, n_pad, n)


def _merge_kernel(x_ref, mod_ref, g_ref, gates_ref, bg_ref, ya_ref, yb_ref, yh_ref, yhc_ref, yd_ref,
                  wb_ref, wo_ref, o_ref, *, n_ctx_tiles, row_off):
    is_ctx = pl.program_id(1) + row_off < n_ctx_tiles
    yh = jnp.where(is_ctx, yhc_ref[0], yh_ref[0])
    merged = None
    for br, y in enumerate((ya_ref[0], yb_ref[0], yh, yd_ref[0])):
        gate = _sigmoid(gates_ref[0, :, br * D_MODEL:(br + 1) * D_MODEL] + bg_ref[br:br + 1])
        term = gate * jnp.dot(y.astype(BF16), wb_ref[br], preferred_element_type=F32)
        merged = term if merged is None else merged + term
    y = jnp.dot(merged.astype(BF16), wo_ref[...], preferred_element_type=F32)
    o_ref[0] = x_ref[0] + mod_ref[0, 5:6] * _rms(y, g_ref[3:4])


def _merge(x, mods, norm_g, p, b_gate, ya, yb, yh, yh_ctx, yd, w_branch, w_out, *, n_ctx_tiles, row_off):
    b, s, _ = x.shape
    nt = s // TM - row_off
    n_lat = mods.shape[0] - 1
    br = lambda: pl.BlockSpec((1, TM, BRANCH_DIM), lambda bi, i: (bi, i, 0))
    n_lat_tiles = s // TM - n_ctx_tiles
    br_a = pl.BlockSpec((1, TM, BRANCH_DIM), lambda bi, i: (
        bi, jnp.where(i + row_off < n_ctx_tiles, n_lat_tiles + i, i + row_off - n_ctx_tiles), 0))
    br_h = pl.BlockSpec((1, TM, BRANCH_DIM), lambda bi, i: (bi, jnp.maximum(i + row_off - n_ctx_tiles, 0), 0))
    br_hc = pl.BlockSpec((1, TM, BRANCH_DIM), lambda bi, i: (
        bi, jnp.minimum(i + row_off, max(n_ctx_tiles - 1, 0)) if yh_ctx is not None else 0, 0))
    return pl.pallas_call(
        functools.partial(_merge_kernel, n_ctx_tiles=n_ctx_tiles if yh_ctx is not None else 0, row_off=row_off),
        grid=(b, nt),
        in_specs=[pl.BlockSpec((1, TM, D_MODEL), lambda bi, i: (bi, i + row_off, 0)),
                  pl.BlockSpec((1, N_MOD, D_MODEL),
                               lambda bi, i: (jnp.where(i + row_off < n_ctx_tiles, n_lat, bi), 0, 0)),
                  pl.BlockSpec((6, D_MODEL), lambda bi, i: (0, 0)),
                  pl.BlockSpec((1, TM, GATE_COLS), lambda bi, i: (bi, i + row_off, 0)),
                  pl.BlockSpec((N_BRANCH, D_MODEL), lambda bi, i: (0, 0)),
                  br_a, br(), br_h, br_hc, br(),
                  pl.BlockSpec((N_BRANCH, BRANCH_DIM, D_MODEL), lambda bi, i: (0, 0, 0)),
                  pl.BlockSpec((D_MODEL, D_MODEL), lambda bi, i: (0, 0))],
        out_specs=pl.BlockSpec((1, TM, D_MODEL), lambda bi, i: (bi, i, 0)),
        out_shape=jax.ShapeDtypeStruct((b, nt * TM, D_MODEL), F32),
        compiler_params=_cparams(("parallel", "parallel")),
        name="merge_branches",
    )(x, mods, norm_g, p, b_gate, ya, yb, yh, yh if yh_ctx is None else yh_ctx, yd, w_branch, w_out)


def kernel(x, c, ctx, c_ctx, w_mod, b_mod, norm_g, ffn_w13, ffn_w2, w_in, b_gate, mla_norm_q, mla_norm_kv, mla_w_uq, mla_w_ukv, rwkv_mu, rwkv_w0, rwkv_w_up, rwkv_a0, rwkv_a_up, rwkv_g_up, rwkv_kvec, rwkv_r_k, rwkv_ln_g, rwkv_ln_b, hyena_conv, hyena_conv_b, hyena_w1, hyena_b1, hyena_w2, hyena_b2, hyena_w3, hyena_freq, hyena_bias, swa_sink, w_branch, w_out):
    b, n, _ = x.shape
    n_ctx = ctx.shape[1]
    nct = n_ctx // TM
    xall = jnp.concatenate([ctx, x], axis=1)
    c_all = jnp.concatenate([c, c_ctx[None]], axis=0)
    tabs_mla = _rope_tables(n, n_ctx, MLA_ROPE, MLA_NOPE, LANES)
    tabs_swa = _rope_tables(n, n_ctx, SWA_HEAD, 0, SWA_HEAD)
    depth = w_mod.shape[0]
    for l in range(depth):
        with_ctx = l + 1 < depth
        row_off = 0 if with_ctx else nct
        mods = _modulation(c_all, w_mod[l], b_mod[l])
        xall = _ffn(xall, mods, norm_g[l], ffn_w13[l, 0].astype(BF16), ffn_w2[l, 0].astype(BF16),
                    mod0=0, g0=0, n_ctx_tiles=nct)
        coef = _shift_coefficients(rwkv_mu[l], hyena_conv[l], hyena_conv_b[l])
        p, p_hy = _inproj(xall, mods, norm_g[l], _permute_w_in(w_in[l]).astype(BF16), coef, n_ctx_tiles=nct)
        q, k, v = _mla_prep(p, mla_norm_q[l], mla_norm_kv[l], mla_w_uq[l], mla_w_ukv[l], tabs_mla)
        ya = _mla_attention(q, k, v, n_ctx=n_ctx, with_ctx=with_ctx)
        y_fwd, y_bwd, bonus, gate = _rwkv_chunks(p, rwkv_kvec[l], rwkv_w0[l], rwkv_a0[l], rwkv_w_up[l],
                                                 rwkv_a_up[l], rwkv_g_up[l], rwkv_r_k[l], n_ctx=n_ctx)
        yb = _rwkv_readout(y_fwd, y_bwd, bonus, gate, rwkv_ln_g[l], rwkv_ln_b[l], row_off=row_off)
        filt = (hyena_w1[l], hyena_b1[l], hyena_w2[l], hyena_b2[l], hyena_w3[l], hyena_freq[l])
        yh = _hyena_operator(p_hy, n_ctx, n, filt, hyena_bias[l])
        yh_ctx = _hyena_operator(p_hy, 0, n_ctx, filt, hyena_bias[l]) if with_ctx else None
        q, k, v = _swa_prep(p, tabs_swa)
        yd = _swa_attention(q, k, v, swa_sink[l], n_ctx=n_ctx, q_off=row_off * TM // SWA_TQ)
        xall = _merge(xall, mods, norm_g[l], p, b_gate[l], ya, yb, yh, yh_ctx, yd, w_branch[l].astype(BF16),
                      w_out[l].astype(BF16), n_ctx_tiles=nct, row_off=row_off)
        xall = _ffn(xall, mods, norm_g[l], ffn_w13[l, 1].astype(BF16), ffn_w2[l, 1].astype(BF16),
                    mod0=6, g0=4, n_ctx_tiles=nct - row_off)
    return xall
```

```python
import functools
import math

import numpy as np
import jax
import jax.numpy as jnp
from jax import lax
from jax.experimental import pallas as pl
from jax.experimental.pallas import tpu as pltpu

F32 = jnp.float32
BF16 = jnp.bfloat16

D_MODEL = 1024
GRID_W = 64
N_BRANCH = 4
N_MOD = 9
FF_DIM = 2816
EPS = 1e-6
ROPE_BASE = 10000.0
NEG_INF = -1e30
BRANCH_DIM = 512
MLA_HEADS = 8
MLA_NOPE = 64
MLA_ROPE = 32
MLA_V = 64
MLA_Q_RANK = 256
MLA_KV_RANK = 128
RWKV_HEADS = 8
RWKV_HEAD = 64
RWKV_DIM = RWKV_HEADS * RWKV_HEAD
DECAY_LORA = 64
AAA_LORA = 64
GATE_LORA = 128
RWKV_LN_EPS = 64e-5
HYENA_DIM = 512
HYENA_ORDER = 2
HYENA_EMB = 33
HYENA_BANDS = (HYENA_EMB - 1) // 2
HYENA_FW = 64
HYENA_TARGET = 1e-2
HYENA_FAST = 0.3
HYENA_SLOW = 1.5
SWA_HEADS = 8
SWA_KV_HEADS = 2
SWA_HEAD = 64
SWA_GROUP = SWA_HEADS // SWA_KV_HEADS
WINDOW = 128
GATE_COLS = N_BRANCH * D_MODEL
MLA_COLS = MLA_Q_RANK + MLA_KV_RANK + MLA_ROPE
RWKV_COLS = 3 * RWKV_DIM + DECAY_LORA + AAA_LORA + GATE_LORA
HYENA_COLS = 3 * HYENA_DIM
SWA_COLS = (SWA_HEADS + 2 * SWA_KV_HEADS) * SWA_HEAD

LANES = 128
V7X_VMEM_LIMIT = 56 * 1024 * 1024

TM = 256
FFN_SUB_TILES = 4
INPROJ_SUB_TILES = 2
FF_CHUNK = 256
IN_CHUNK = 512
CHUNK = 64

P_GATE = 0
P_RKV = 4096
P_SWAQ = 5632
P_LORA = 6144
P_CQ = 6400
P_CKV = 6656
P_KR = 6784
P_SWAK = 6912
P_SWAV = 7040
P_COLS = 7168
P_HY = 7168
W_COLS = P_HY + HYENA_COLS
_SHIFT_COLS = ((P_RKV, P_SWAQ), (P_LORA, P_CQ), (P_HY, W_COLS))
_SHIFT_CHUNKS = [any(lo < (j + 1) * IN_CHUNK and j * IN_CHUNK < hi for lo, hi in _SHIFT_COLS)
                 for j in range(W_COLS // IN_CHUNK)]


def _cparams(sem, vmem=V7X_VMEM_LIMIT):
    return pltpu.CompilerParams(dimension_semantics=sem, vmem_limit_bytes=vmem)


def _mm(a, b):
    return jnp.dot(a.astype(BF16), b.astype(BF16), preferred_element_type=F32)


def _mm_nt(a, b):
    return lax.dot_general(a.astype(BF16), b.astype(BF16), (((1,), (1,)), ((), ())),
                           preferred_element_type=F32)


def _mm_tn(a, b):
    return lax.dot_general(a.astype(BF16), b.astype(BF16), (((0,), (0,)), ((), ())),
                           preferred_element_type=F32)


def _mm_f32(a, b):
    return jnp.dot(a, b, preferred_element_type=F32, precision=lax.Precision.HIGHEST)


def _rms(x, g):
    return x * lax.rsqrt(jnp.mean(x * x, axis=-1, keepdims=True) + EPS) * g


def _sigmoid(x):
    return 1.0 / (1.0 + jnp.exp(-x))


def _mod_kernel(c_ref, w_ref, b_ref, o_ref):
    c = c_ref[...]
    o_ref[...] = _mm(c * _sigmoid(c), w_ref[...]) + b_ref[...]


def _modulation(c_all, w_mod, b_mod):
    r = c_all.shape[0]
    rp = -(-r // 8) * 8
    c_pad = jnp.zeros((rp, D_MODEL), F32).at[:r].set(c_all)
    tn = 1024
    out = pl.pallas_call(
        _mod_kernel,
        grid=(N_MOD * D_MODEL // tn,),
        in_specs=[pl.BlockSpec((rp, D_MODEL), lambda j: (0, 0)),
                  pl.BlockSpec((D_MODEL, tn), lambda j: (0, j)),
                  pl.BlockSpec((1, tn), lambda j: (0, j))],
        out_specs=pl.BlockSpec((rp, tn), lambda j: (0, j)),
        out_shape=jax.ShapeDtypeStruct((rp, N_MOD * D_MODEL), F32),
        compiler_params=_cparams(("arbitrary",)),
        name="modulation",
    )(c_pad, w_mod, b_mod.reshape(1, -1))
    return out[:r].reshape(r, N_MOD, D_MODEL)


def _sub_tiles(n_tiles, most):
    return max(g for g in range(1, most + 1) if n_tiles % g == 0)


def _mod_specs(n_sub, tiles_per_seq, n_ctx_tiles, ctx_row):
    def spec(k):
        def index(i):
            t = i * n_sub + k
            return (jnp.where(t % tiles_per_seq < n_ctx_tiles, ctx_row, t // tiles_per_seq), 0, 0)
        return pl.BlockSpec((1, N_MOD, D_MODEL), index)
    return [spec(k) for k in range(n_sub)]


def _ffn_kernel(x_ref, *refs, mod0, g0, n_sub):
    mod_refs = refs[:n_sub]
    g_ref, w13_ref, w2_ref, o_ref = refs[n_sub:]
    tiles = [slice(t * TM, (t + 1) * TM) for t in range(n_sub)]
    u = jnp.concatenate(
        [(_rms(x_ref[rows], g_ref[g0:g0 + 1]) * (1.0 + m[0, mod0 + 1:mod0 + 2]) + m[0, mod0:mod0 + 1]).astype(BF16)
         for rows, m in zip(tiles, mod_refs)], axis=0)
    acc = jnp.zeros(x_ref.shape, F32)
    for f in range(FF_DIM // FF_CHUNK):
        lo = f * FF_CHUNK
        a = jnp.dot(u, w13_ref[:, lo:lo + FF_CHUNK], preferred_element_type=F32)
        b = jnp.dot(u, w13_ref[:, FF_DIM + lo:FF_DIM + lo + FF_CHUNK], preferred_element_type=F32)
        h = (a * _sigmoid(a) * b).astype(BF16)
        acc = acc + jnp.dot(h, w2_ref[lo:lo + FF_CHUNK, :], preferred_element_type=F32)
    hn = _rms(acc, g_ref[g0 + 1:g0 + 2])
    for rows, m in zip(tiles, mod_refs):
        o_ref[rows] = x_ref[rows] + 0.5 * m[0, mod0 + 2:mod0 + 3] * hn[rows]


def _ffn(x, mods, norm_g, w13, w2, *, mod0, g0, n_ctx_tiles):
    b, s, _ = x.shape
    n_tiles = b * s // TM
    n_sub = _sub_tiles(n_tiles, FFN_SUB_TILES)
    rows = n_sub * TM
    out = pl.pallas_call(
        functools.partial(_ffn_kernel, mod0=mod0, g0=g0, n_sub=n_sub),
        grid=(n_tiles // n_sub,),
        in_specs=[pl.BlockSpec((rows, D_MODEL), lambda i: (i, 0)),
                  *_mod_specs(n_sub, s // TM, n_ctx_tiles, mods.shape[0] - 1),
                  pl.BlockSpec((6, D_MODEL), lambda i: (0, 0)),
                  pl.BlockSpec(memory_space=pltpu.VMEM),
                  pl.BlockSpec(memory_space=pltpu.VMEM)],
        out_specs=pl.BlockSpec((rows, D_MODEL), lambda i: (i, 0)),
        out_shape=jax.ShapeDtypeStruct((b * s, D_MODEL), F32),
        compiler_params=_cparams(("parallel",)),
        name="ffn_half_step",
    )(x.reshape(b * s, D_MODEL), *([mods] * n_sub), norm_g, w13, w2)
    return out.reshape(b, s, D_MODEL)


def _inproj_kernel(x_ref, xp_ref, xn_ref, *refs, n_sub, tiles_per_seq, n_ctx_tiles):
    mod_refs = refs[:n_sub]
    g_ref, w_ref, coef_ref, o_ref, hy_ref = refs[n_sub:]
    g = g_ref[2:3]

    def modulated(x, m):
        return (_rms(x, g) * (1.0 + m[0, 4:5]) + m[0, 3:4]).astype(BF16)

    u = jnp.concatenate([modulated(x_ref[t * TM:(t + 1) * TM], m) for t, m in enumerate(mod_refs)], axis=0)
    u_prev = modulated(xp_ref[...], mod_refs[0])
    u_next = modulated(xn_ref[...], mod_refs[-1])
    rows = lax.broadcasted_iota(jnp.int32, (n_sub * TM, 1), 0)
    keep_prev = jnp.ones((n_sub * TM, 1), F32)
    keep_next = jnp.ones((n_sub * TM, 1), F32)
    for t in range(n_sub):
        w = (pl.program_id(0) * n_sub + t) % tiles_per_seq
        seg_start = (w == 0) | (w == n_ctx_tiles)
        seg_end = (w == n_ctx_tiles - 1) | (w == tiles_per_seq - 1)
        keep_prev = jnp.where((rows == t * TM) & seg_start, 0.0, keep_prev)
        keep_next = jnp.where((rows == (t + 1) * TM - 1) & seg_end, 0.0, keep_next)
    for j in range(W_COLS // IN_CHUNK):
        cols = slice(j * IN_CHUNK, (j + 1) * IN_CHUNK)
        p = jnp.dot(u, w_ref[:, cols], preferred_element_type=F32)
        if _SHIFT_CHUNKS[j]:
            p_first = jnp.dot(u_prev, w_ref[:, cols], preferred_element_type=F32)[7:8]
            p_last = jnp.dot(u_next, w_ref[:, cols], preferred_element_type=F32)[0:1]
            prev = jnp.where(rows == 0, p_first, pltpu.roll(p, 1, axis=0)) * keep_prev
            nxt = jnp.where(rows == n_sub * TM - 1, p_last, pltpu.roll(p, n_sub * TM - 1, axis=0)) * keep_next
            p = (coef_ref[0:1, cols] * p + coef_ref[1:2, cols] * prev + coef_ref[2:3, cols] * nxt
                 + coef_ref[3:4, cols])
        if j * IN_CHUNK < P_COLS:
            o_ref[:, cols] = p.astype(BF16)
        else:
            hy_ref[:, j * IN_CHUNK - P_HY:(j + 1) * IN_CHUNK - P_HY] = p


def _inproj(x, mods, norm_g, w_in_p, coef, *, n_ctx_tiles):
    b, s, _ = x.shape
    n_tiles = b * s // TM
    n_sub = _sub_tiles(n_tiles, INPROJ_SUB_TILES)
    rows = n_sub * TM
    r8 = rows // 8
    p, hy = pl.pallas_call(
        functools.partial(_inproj_kernel, n_sub=n_sub, tiles_per_seq=s // TM, n_ctx_tiles=n_ctx_tiles),
        grid=(n_tiles // n_sub,),
        in_specs=[pl.BlockSpec((rows, D_MODEL), lambda i: (i, 0)),
                  pl.BlockSpec((8, D_MODEL), lambda i: (jnp.maximum(i * r8 - 1, 0), 0)),
                  pl.BlockSpec((8, D_MODEL), lambda i: (jnp.minimum((i + 1) * r8, b * s // 8 - 1), 0)),
                  *_mod_specs(n_sub, s // TM, n_ctx_tiles, mods.shape[0] - 1),
                  pl.BlockSpec((6, D_MODEL), lambda i: (0, 0)),
                  pl.BlockSpec(memory_space=pltpu.VMEM),
                  pl.BlockSpec((4, W_COLS), lambda i: (0, 0))],
        out_specs=[pl.BlockSpec((rows, P_COLS), lambda i: (i, 0)),
                   pl.BlockSpec((rows, HYENA_COLS), lambda i: (i, 0))],
        out_shape=[jax.ShapeDtypeStruct((b * s, P_COLS), BF16),
                   jax.ShapeDtypeStruct((b * s, HYENA_COLS), F32)],
        compiler_params=_cparams(("parallel",)),
        name="in_projection",
    )(*([x.reshape(b * s, D_MODEL)] * 3), *([mods] * n_sub), norm_g, w_in_p, coef)
    return p.reshape(b, s, P_COLS), hy.reshape(b, s, HYENA_COLS)


def _shift_coefficients(rwkv_mu, hyena_conv, hyena_conv_b):
    mu = rwkv_mu.astype(F32)
    coef = jnp.zeros((4, W_COLS), F32).at[0].set(1.0)
    for off, sl in ((P_RKV, slice(0, 3 * RWKV_DIM)), (P_LORA, slice(3 * RWKV_DIM, RWKV_COLS))):
        width = sl.stop - sl.start
        coef = coef.at[0, off:off + width].set(1.0 - mu[0, sl] - mu[1, sl])
        coef = coef.at[1, off:off + width].set(mu[0, sl])
        coef = coef.at[2, off:off + width].set(mu[1, sl])
    hy = slice(P_HY, P_HY + HYENA_COLS)
    coef = coef.at[0, hy].set(hyena_conv[1]).at[1, hy].set(hyena_conv[0]).at[2, hy].set(hyena_conv[2])
    return coef.at[3, hy].set(hyena_conv_b)


def _permute_w_in(w_in):
    o_mla = GATE_COLS
    o_rwkv = o_mla + MLA_COLS
    o_hy = o_rwkv + RWKV_COLS
    o_swa = o_hy + HYENA_COLS
    z = lambda n: jnp.zeros((D_MODEL, n), w_in.dtype)
    parts = [
        w_in[:, :GATE_COLS],
        w_in[:, o_rwkv:o_rwkv + 3 * RWKV_DIM],
        w_in[:, o_swa:o_swa + SWA_HEADS * SWA_HEAD],
        w_in[:, o_rwkv + 3 * RWKV_DIM:o_rwkv + RWKV_COLS],
        w_in[:, o_mla:o_mla + MLA_Q_RANK],
        w_in[:, o_mla + MLA_Q_RANK:o_mla + MLA_Q_RANK + MLA_KV_RANK],
        z(MLA_NOPE), w_in[:, o_mla + MLA_Q_RANK + MLA_KV_RANK:o_mla + MLA_COLS],
        z(LANES - MLA_NOPE - MLA_ROPE),
        w_in[:, o_swa + SWA_HEADS * SWA_HEAD:o_swa + SWA_COLS],
        w_in[:, o_hy:o_hy + HYENA_COLS],
    ]
    out = jnp.concatenate(parts, axis=1)
    assert out.shape[1] == W_COLS
    return out


def _rope_tables(n_lat, n_ctx, rot_dim, lane0, period):
    rows = n_lat // GRID_W
    row = jnp.repeat(jnp.arange(rows, dtype=F32), GRID_W)
    col = jnp.tile(jnp.arange(GRID_W, dtype=F32), rows)
    axis_dim = rot_dim // 2
    h = axis_dim // 2
    inv_freq = ROPE_BASE ** (-jnp.arange(0, axis_dim, 2, dtype=F32) / axis_dim)
    ang_r = row[:, None] * inv_freq
    ang_c = col[:, None] * inv_freq
    cos_rot = jnp.concatenate([jnp.cos(ang_r)] * 2 + [jnp.cos(ang_c)] * 2, axis=1)
    zeros = jnp.zeros_like(ang_r)
    sin_a = jnp.concatenate([-jnp.sin(ang_r), zeros, -jnp.sin(ang_c), zeros], axis=1)
    sin_b = jnp.concatenate([zeros, jnp.sin(ang_r), zeros, jnp.sin(ang_c)], axis=1)

    def widen(t, fill):
        g = jnp.full((n_lat, period), fill, F32).at[:, lane0:lane0 + rot_dim].set(t)
        g = jnp.tile(g, (1, LANES // period))
        ctx = jnp.full((n_ctx, LANES), fill, F32)
        return jnp.concatenate([ctx, g], axis=0)

    return widen(cos_rot, 1.0), widen(sin_a, 0.0), widen(sin_b, 0.0), h


def _rope128(x, cos, sin_a, sin_b, h):
    return x * cos + pltpu.roll(x, LANES - h, axis=1) * sin_a + pltpu.roll(x, h, axis=1) * sin_b


LOG2E = math.log2(math.e)
MLA_SCALE = (MLA_NOPE + MLA_ROPE) ** -0.5 * LOG2E
MLA_ONE_LANE = (MLA_V, 0)


def _mla_prep_kernel(cq_ref, ckv_ref, kr_ref, gq_ref, gkv_ref, wq_ref, wk_ref, wv_ref, vone_ref,
                     cos_ref, sa_ref, sb_ref, q_ref, k_ref, v_ref, *, h):
    cos, sa, sb = cos_ref[...], sa_ref[...], sb_ref[...]
    cq = _rms(cq_ref[0].astype(F32), gq_ref[...]).astype(BF16)
    ckv = _rms(ckv_ref[0].astype(F32), gkv_ref[...]).astype(BF16)
    q = jnp.dot(cq, wq_ref[...], preferred_element_type=F32)
    k = jnp.dot(ckv, wk_ref[...], preferred_element_type=F32)
    kr = _rope128(kr_ref[0].astype(F32), cos, sa, sb, h)
    for hd in range(MLA_HEADS):
        sl = slice(hd * LANES, (hd + 1) * LANES)
        q_ref[0, :, sl] = (_rope128(q[:, sl], cos, sa, sb, h) * MLA_SCALE).astype(BF16)
        k_ref[0, :, sl] = (k[:, sl] + kr).astype(BF16)
    v_ref[0] = (jnp.dot(ckv, wv_ref[...], preferred_element_type=F32) + vone_ref[...]).astype(BF16)


def _mla_prep(p, norm_q, norm_kv, w_uq, w_ukv, tabs):
    b, s, _ = p.shape
    cos, sa, sb, h = tabs
    hq = MLA_NOPE + MLA_ROPE
    wq = jnp.zeros((MLA_Q_RANK, MLA_HEADS, LANES), F32).at[:, :, :hq].set(
        w_uq.reshape(MLA_Q_RANK, MLA_HEADS, hq)).reshape(MLA_Q_RANK, MLA_HEADS * LANES).astype(BF16)
    wkv = w_ukv.reshape(MLA_KV_RANK, MLA_HEADS, MLA_NOPE + MLA_V)
    wk = jnp.zeros((MLA_KV_RANK, MLA_HEADS, LANES), F32).at[:, :, :MLA_NOPE].set(
        wkv[:, :, :MLA_NOPE]).reshape(MLA_KV_RANK, MLA_HEADS * LANES).astype(BF16)
    wv_pairs = wkv[:, :, MLA_NOPE:].reshape(MLA_KV_RANK, MLA_HEADS // 2, 2, MLA_V)
    gap = ((0, 0), (0, 0), (0, LANES - MLA_V))
    wv = jnp.stack([jnp.pad(wv_pairs[:, :, 0], gap), jnp.pad(wv_pairs[:, :, 1], gap[:2] + (gap[2][::-1],))],
                   axis=2).reshape(MLA_KV_RANK, MLA_HEADS * LANES).astype(BF16)
    lane_id = np.arange(MLA_HEADS * LANES) % (2 * LANES)
    vone = jnp.asarray((lane_id == MLA_ONE_LANE[0]) | (lane_id == LANES + MLA_ONE_LANE[1]), F32)[None, :]
    full = lambda shape: pl.BlockSpec(shape, lambda bi, i: (0,) * len(shape))
    tab = pl.BlockSpec((TM, LANES), lambda bi, i: (i, 0))
    return pl.pallas_call(
        functools.partial(_mla_prep_kernel, h=h),
        grid=(b, s // TM),
        in_specs=[pl.BlockSpec((1, TM, MLA_Q_RANK), lambda bi, i: (bi, i, P_CQ // MLA_Q_RANK)),
                  pl.BlockSpec((1, TM, LANES), lambda bi, i: (bi, i, P_CKV // LANES)),
                  pl.BlockSpec((1, TM, LANES), lambda bi, i: (bi, i, P_KR // LANES)),
                  full((1, MLA_Q_RANK)), full((1, MLA_KV_RANK)),
                  full(wq.shape), full(wk.shape), full(wv.shape), full(vone.shape), tab, tab, tab],
        out_specs=[pl.BlockSpec((1, TM, MLA_HEADS * LANES), lambda bi, i: (bi, i, 0))] * 3,
        out_shape=[jax.ShapeDtypeStruct((b, s, MLA_HEADS * LANES), BF16)] * 3,
        compiler_params=_cparams(("parallel", "parallel")),
        name="mla_prep",
    )(p, p, p, norm_q.reshape(1, -1), norm_kv.reshape(1, -1), wq, wk, wv, vone, cos, sa, sb)


MLA_Q_TILES = 1


def _mla_attn_kernel(*refs, n_ctx, n_lat_steps):
    q_refs = refs[:MLA_Q_TILES]
    k_ref, v_ref, o_ref = refs[MLA_Q_TILES:]

    def attend(n_keys):
        outs = []
        for hd in range(2):
            sl = slice(hd * LANES, (hd + 1) * LANES)
            q = jnp.concatenate([q_ref[0, :, sl] for q_ref in q_refs], axis=0)
            s = lax.dot_general(q, k_ref[0, :n_keys, sl], (((1,), (1,)), ((), ())),
                                preferred_element_type=F32)
            e = jnp.exp2(s - jnp.max(s, axis=-1, keepdims=True)).astype(BF16)
            o = jnp.dot(e, v_ref[0, :n_keys, sl], preferred_element_type=F32)
            one = MLA_ONE_LANE[hd]
            outs.append(o / o[:, one:one + 1])
        lane = lax.broadcasted_iota(jnp.int32, outs[0].shape, 1)
        o_ref[0] = jnp.where(lane < MLA_V, outs[0], outs[1]).astype(BF16)

    @pl.when(pl.program_id(2) < n_lat_steps)
    def _():
        attend(k_ref.shape[1])

    @pl.when(pl.program_id(2) >= n_lat_steps)
    def _():
        attend(n_ctx)


def _mla_attention(q, k, v, *, n_ctx, with_ctx):
    b, s, _ = q.shape
    nct = n_ctx // TM
    n_lat = s - n_ctx
    n_lat_steps = n_lat // (MLA_Q_TILES * TM)
    assert n_lat % (MLA_Q_TILES * TM) == 0 and (nct == 1 or not with_ctx)

    def q_spec(t):
        return pl.BlockSpec((1, TM, 2 * LANES), lambda bi, hp, j: (
            bi, jnp.where(j < n_lat_steps, nct + j * MLA_Q_TILES + t, 0), hp))

    kv = pl.BlockSpec((1, s, 2 * LANES), lambda bi, hp, j: (bi, 0, hp))
    return pl.pallas_call(
        functools.partial(_mla_attn_kernel, n_ctx=n_ctx, n_lat_steps=n_lat_steps),
        grid=(b, MLA_HEADS // 2, n_lat_steps + (1 if with_ctx else 0)),
        in_specs=[*[q_spec(t) for t in range(MLA_Q_TILES)], kv, kv],
        out_specs=pl.BlockSpec((1, MLA_Q_TILES * TM, LANES), lambda bi, hp, j: (bi, j, hp)),
        out_shape=jax.ShapeDtypeStruct((b, n_lat + (n_ctx if with_ctx else 0), MLA_HEADS * MLA_V), BF16),
        compiler_params=_cparams(("parallel", "parallel", "parallel")),
        name="mla_attention",
    )(*([q] * MLA_Q_TILES), k, v)


SWA_SCALE = SWA_HEAD ** -0.5 * LOG2E
SWA_TQ = 128
SWA_ONE_LANE = (SWA_HEAD, 0)


def _swa_prep_kernel(q_ref, k_ref, v_ref, cos_ref, sa_ref, sb_ref, qo_ref, ko_ref, vo_ref, *, h):
    cos, sa, sb = cos_ref[...], sa_ref[...], sb_ref[...]
    lane = lax.broadcasted_iota(jnp.int32, cos.shape, 1)
    low = lane < SWA_HEAD
    for j in range(SWA_HEADS // 2):
        blk = _rope128(q_ref[0, :, j * LANES:(j + 1) * LANES].astype(F32), cos, sa, sb, h) * SWA_SCALE
        qo_ref[0, :, (2 * j) * LANES:(2 * j + 1) * LANES] = jnp.where(low, blk, 0.0).astype(BF16)
        qo_ref[0, :, (2 * j + 1) * LANES:(2 * j + 2) * LANES] = jnp.where(
            low, pltpu.roll(blk, SWA_HEAD, axis=1), 0.0).astype(BF16)
    kb = _rope128(k_ref[0].astype(F32), cos, sa, sb, h)
    ko_ref[0, :, :LANES] = jnp.where(low, kb, 0.0).astype(BF16)
    ko_ref[0, :, LANES:] = jnp.where(low, pltpu.roll(kb, SWA_HEAD, axis=1), 0.0).astype(BF16)
    vb = v_ref[0].astype(F32)
    vr = pltpu.roll(vb, SWA_HEAD, axis=1)
    one_lo = jnp.where(lane == SWA_ONE_LANE[0], 1.0, 0.0)
    one_hi = jnp.where(lane == SWA_ONE_LANE[1], 1.0, 0.0)
    vo_ref[0, :, 0 * LANES:1 * LANES] = jnp.where(low, vb, one_lo).astype(BF16)
    vo_ref[0, :, 1 * LANES:2 * LANES] = jnp.where(low, one_hi, vr).astype(BF16)
    vo_ref[0, :, 2 * LANES:3 * LANES] = jnp.where(low, vr, one_lo).astype(BF16)
    vo_ref[0, :, 3 * LANES:4 * LANES] = jnp.where(low, one_hi, vb).astype(BF16)


def _swa_prep(p, tabs):
    b, s, _ = p.shape
    cos, sa, sb, h = tabs
    tab = pl.BlockSpec((TM, LANES), lambda bi, i: (i, 0))
    nq = SWA_HEADS * SWA_HEAD
    return pl.pallas_call(
        functools.partial(_swa_prep_kernel, h=h),
        grid=(b, s // TM),
        in_specs=[pl.BlockSpec((1, TM, nq), lambda bi, i: (bi, i, P_SWAQ // nq)),
                  pl.BlockSpec((1, TM, LANES), lambda bi, i: (bi, i, P_SWAK // LANES)),
                  pl.BlockSpec((1, TM, LANES), lambda bi, i: (bi, i, P_SWAV // LANES)),
                  tab, tab, tab],
        out_specs=[pl.BlockSpec((1, TM, SWA_HEADS * LANES), lambda bi, i: (bi, i, 0)),
                   pl.BlockSpec((1, TM, SWA_KV_HEADS * LANES), lambda bi, i: (bi, i, 0)),
                   pl.BlockSpec((1, TM, 4 * LANES), lambda bi, i: (bi, i, 0))],
        out_shape=[jax.ShapeDtypeStruct((b, s, SWA_HEADS * LANES), BF16),
                   jax.ShapeDtypeStruct((b, s, SWA_KV_HEADS * LANES), BF16),
                   jax.ShapeDtypeStruct((b, s, 4 * LANES), BF16)],
        compiler_params=_cparams(("parallel", "parallel")),
        name="swa_prep",
    )(p, p, p, cos, sa, sb)


def _swa_attn_kernel(sink_ref, q_ref, k_ref, v_ref, o_ref, *, n_ctx, q_off):
    i = pl.program_id(1) + q_off
    s_len = k_ref.shape[1]
    tq = SWA_TQ
    n_loc = tq + 2 * WINDOW
    r0 = i * tq
    is_lat = r0 >= n_ctx
    start = pl.multiple_of(jnp.clip(r0 - WINDOW, 0, s_len - n_loc), LANES)
    rows_g = SWA_GROUP * tq
    row = lax.broadcasted_iota(jnp.int32, (rows_g, n_loc), 0)
    qpos = r0 - n_ctx + row % tq
    kpos = start - n_ctx + lax.broadcasted_iota(jnp.int32, (rows_g, n_loc), 1)
    loc_ok = (jnp.abs(kpos - qpos) <= WINDOW) & (kpos >= 0) & is_lat
    k_loc = k_ref[0, pl.ds(start, n_loc), :]
    v_loc = v_ref[0, pl.ds(start, n_loc), :]
    k_ctx = k_ref[0, 0:n_ctx, :]
    v_ctx = v_ref[0, 0:n_ctx, :]
    head_row = lax.broadcasted_iota(jnp.int32, (rows_g, 1), 0) // tq
    lane = lax.broadcasted_iota(jnp.int32, (tq, LANES), 1)
    stages = []
    for g in range(SWA_KV_HEADS):
        q = jnp.concatenate([q_ref[0, :, hd * LANES:(hd + 1) * LANES]
                             for hd in range(g * SWA_GROUP, (g + 1) * SWA_GROUP)], axis=0)
        kg = slice(g * LANES, (g + 1) * LANES)
        s_loc = lax.dot_general(q, k_loc[:, kg], (((1,), (1,)), ((), ())), preferred_element_type=F32)
        s_ctx = lax.dot_general(q, k_ctx[:, kg], (((1,), (1,)), ((), ())), preferred_element_type=F32)
        sink = jnp.zeros((rows_g, 1), F32)
        for hh in range(SWA_GROUP):
            sink = jnp.where(head_row == hh, sink_ref[g * SWA_GROUP + hh] * LOG2E, sink)
        stages.append((jnp.where(loc_ok, s_loc, NEG_INF), s_ctx, sink))
    for g, (s_loc, s_ctx, sink) in enumerate(stages):
        m = jnp.maximum(jnp.maximum(jnp.max(s_loc, axis=-1, keepdims=True),
                                    jnp.max(s_ctx, axis=-1, keepdims=True)), sink)
        e = jnp.concatenate([jnp.exp2(s_loc - m), jnp.exp2(s_ctx - m)], axis=1).astype(BF16)
        e_sink = jnp.exp2(sink - m)
        outs = []
        for par in range(2):
            vg = slice((2 * g + par) * LANES, (2 * g + par + 1) * LANES)
            o = jnp.dot(e, jnp.concatenate([v_loc[:, vg], v_ctx[:, vg]], axis=0), preferred_element_type=F32)
            one = SWA_ONE_LANE[par]
            outs.append(o / (o[:, one:one + 1] + e_sink))
        for pi in range(SWA_GROUP // 2):
            even = outs[0][(2 * pi) * tq:(2 * pi + 1) * tq]
            odd = outs[1][(2 * pi + 1) * tq:(2 * pi + 2) * tq]
            blk = g * (SWA_GROUP // 2) + pi
            o_ref[0, :, blk * LANES:(blk + 1) * LANES] = jnp.where(lane < SWA_HEAD, even, odd).astype(BF16)


def _swa_attention(q, k, v, sink, *, n_ctx, q_off):
    b, s, _ = q.shape
    nq = s // SWA_TQ - q_off
    return pl.pallas_call(
        functools.partial(_swa_attn_kernel, n_ctx=n_ctx, q_off=q_off),
        grid=(b, nq),
        in_specs=[pl.BlockSpec(memory_space=pltpu.SMEM),
                  pl.BlockSpec((1, SWA_TQ, SWA_HEADS * LANES), lambda bi, i: (bi, i + q_off, 0)),
                  pl.BlockSpec((1, s, SWA_KV_HEADS * LANES), lambda bi, i: (bi, 0, 0)),
                  pl.BlockSpec((1, s, 4 * LANES), lambda bi, i: (bi, 0, 0))],
        out_specs=pl.BlockSpec((1, SWA_TQ, SWA_HEADS * SWA_HEAD), lambda bi, i: (bi, i, 0)),
        out_shape=jax.ShapeDtypeStruct((b, nq * SWA_TQ, SWA_HEADS * SWA_HEAD), BF16),
        compiler_params=_cparams(("parallel", "parallel")),
        name="swa_attention",
    )(sink, q, k, v)


N_PAIR = RWKV_HEADS // 2
RWKV_NB = 2
N_DOUBLINGS = int(math.log2(CHUNK))


def _softplus(x):
    return jnp.maximum(x, 0.0) + jnp.log(1.0 + jnp.exp(-jnp.abs(x)))


def _headsum(x, bd):
    hi = x.astype(BF16)
    lo = (x - hi.astype(F32)).astype(BF16)
    return (jnp.dot(hi, bd, preferred_element_type=F32) + jnp.dot(lo, bd, preferred_element_type=F32))


def _chunk_cumsum(x, reverse):
    rows = lax.broadcasted_iota(jnp.int32, x.shape, 0)
    s = 1
    while s < CHUNK:
        if reverse:
            x = x + jnp.where(rows < CHUNK - s, pltpu.roll(x, CHUNK - s, axis=0), 0.0)
        else:
            x = x + jnp.where(rows >= s, pltpu.roll(x, s, axis=0), 0.0)
        s *= 2
    return x


def _head_rows(x):
    first = lax.broadcasted_iota(jnp.int32, x.shape, 1) < RWKV_HEAD
    return jnp.concatenate([jnp.where(first, x, 0.0), jnp.where(first, 0.0, x)], axis=0)


def _mm_x3(a, b):
    a_hi = a.astype(BF16)
    b_hi = b.astype(BF16)
    a_lo = (a - a_hi.astype(F32)).astype(BF16)
    b_lo = (b - b_hi.astype(F32)).astype(BF16)
    dot = functools.partial(jnp.dot, preferred_element_type=F32)
    return dot(a_hi, b_hi) + dot(a_hi, b_lo) + dot(a_lo, b_hi)


def _rwkv_chunk_kernel(rf_ref, kf_ref, vf_ref, lof_ref, rb_ref, kb_ref, vb_ref, lob_ref,
                       kvec_ref, w0_ref, a0_ref, wup_ref, aup_ref, gup_ref, rk_ref, bd_ref,
                       yf_ref, yb_ref, bonus_ref, gate_ref, s_ref):
    @pl.when(pl.program_id(1) == 0)
    def _():
        s_ref[...] = jnp.zeros_like(s_ref)

    bd = bd_ref[...]
    data = []
    for nb in range(RWKV_NB):
        per_dir = []
        for refs in ((rf_ref, kf_ref, vf_ref, lof_ref), (rb_ref, kb_ref, vb_ref, lob_ref)):
            r, k, v, lora = (ref[nb].astype(F32) for ref in refs)
            kk = k * kvec_ref[0:1]
            kk = kk * lax.rsqrt(_headsum(kk * kk, bd) + 1e-12)
            per_dir.append((r, k, v, lora, kk))
        data.append(per_dir)
        r, k, v, lora, _ = per_dir[0]
        gate_ref[nb] = _mm(_sigmoid(lora), gup_ref[...])
        k_both = sum(k * (1.0 + (_sigmoid(a0_ref[d] + _mm(lora, aup_ref[d])) - 1.0) * kvec_ref[1:2])
                     for d in range(2))
        bonus_ref[nb] = _headsum(r * (0.5 * k_both) * rk_ref[...], bd) * v

    trow = lax.broadcasted_iota(jnp.int32, (2 * CHUNK, 4 * CHUNK), 0) % CHUNK
    tcol = lax.broadcasted_iota(jnp.int32, (2 * CHUNK, 4 * CHUNK), 1) % CHUNK
    sq_r = lax.broadcasted_iota(jnp.int32, (LANES, LANES), 0)
    sq_c = lax.broadcasted_iota(jnp.int32, (LANES, LANES), 1)
    same_head = (sq_r // RWKV_HEAD) == (sq_c // RWKV_HEAD)
    eye = sq_r == sq_c

    chains = []
    for nb, d in ((nb, d) for nb in range(RWKV_NB) for d in range(2)):
        reverse = d == 1
        r, k, v, lora, kk = data[nb][d]
        w_log = -_softplus(-(w0_ref[d] + _mm(jnp.tanh(lora), wup_ref[d]))) - 0.5
        ld = -jnp.exp(w_log)
        a = _sigmoid(a0_ref[d] + _mm(lora, aup_ref[d]))
        k_d = k * (1.0 + (a - 1.0) * kvec_ref[1:2])
        b_d = kk * a
        lg = _chunk_cumsum(ld, reverse)
        last = 0 if reverse else CHUNK - 1
        tot = lg[last:last + 1]
        e_neg = jnp.exp(-lg)
        e_end = jnp.exp(tot - lg)
        z_t = -kk * jnp.exp(lg - ld)
        r_t = r * jnp.exp(lg)
        b_t = b_d * e_neg
        k_t = k_d * e_neg
        b_e = b_d * e_end
        k_e = k_d * e_end
        e_tot = jnp.exp(tot)
        before = (tcol > trow) if reverse else (tcol < trow)
        before_eq = (tcol >= trow) if reverse else (tcol <= trow)
        for pr in range(N_PAIR):
            sl = slice(pr * LANES, (pr + 1) * LANES)
            ch = {"nb": nb, "d": d, "sl": sl, "rp": r_t[:, sl], "vp": v[:, sl], "e_tot": e_tot[:, sl],
                  "be_ke": jnp.concatenate([b_e[:, sl], k_e[:, sl]], axis=0)}
            zst, rst, vst = _head_rows(z_t[:, sl]), _head_rows(r_t[:, sl]), _head_rows(v[:, sl])
            bkst = jnp.concatenate([_head_rows(b_t[:, sl]), _head_rows(k_t[:, sl])], axis=0)
            ch["lz"] = jnp.where(before, _mm_nt(zst, bkst), 0.0)
            ch["lr"] = jnp.where(before_eq, _mm_nt(rst, bkst), 0.0)
            ch["zst"], ch["vst"] = zst, vst
            chains.append(ch)

    unit = jnp.where(eye, 1.0, 0.0)
    for ch in chains:
        ch["pw"] = ch["lz"][:, :LANES]
        ch["t"] = unit + ch["pw"]
        ch["x"] = jnp.concatenate([ch["zst"], _mm(ch["lz"][:, LANES:], ch["vst"])], axis=1)
    for it in range(1, N_DOUBLINGS):
        for ch in chains:
            ch["pw"] = _mm(ch["pw"], ch["pw"])
        for ch in chains:
            ch["t"] = ch["t"] + _mm(ch["pw"], ch["t"])
    for ch in chains:
        ch["x"] = _mm(ch["t"], ch["x"])
    for ch in chains:
        low = jnp.concatenate([jnp.zeros_like(ch["vst"]), ch["vst"]], axis=1)
        op = _mm(ch["lr"], jnp.concatenate([ch["x"], low], axis=0))
        ch["op"] = op[:CHUNK] + op[CHUNK:]
        ch["xp"] = ch["x"][:CHUNK] + ch["x"][CHUNK:]
    for ch in chains:
        rhs = jnp.concatenate([ch["xp"], jnp.concatenate([jnp.zeros_like(ch["vp"]), ch["vp"]], axis=1)], axis=0)
        ag = _mm_tn(ch["be_ke"], rhs)
        a_full = ag[:, :LANES] + jnp.where(eye, jnp.broadcast_to(ch["e_tot"], (LANES, LANES)), 0.0)
        ch["a"] = jnp.where(same_head, a_full, 0.0)
        ch["g"] = jnp.where(same_head, ag[:, LANES:], 0.0)
    for idx, ch in enumerate(chains):
        st = s_ref[idx]
        y_ref = yf_ref if ch["d"] == 0 else yb_ref
        y_ref[ch["nb"], :, ch["sl"]] = _mm(ch["rp"] + ch["op"][:, :LANES], st) + ch["op"][:, LANES:]
        s_ref[idx] = _mm_x3(ch["a"], st) + ch["g"]


def _head_block_diag():
    idx = np.arange(RWKV_DIM) // RWKV_HEAD
    return jnp.asarray(idx[:, None] == idx[None, :], BF16)


def _rwkv_chunks(p, kvec, w0, a0, w_up, a_up, g_up, r_k, *, n_ctx):
    b, s, _ = p.shape
    nc = s // CHUNK
    ncc = n_ctx // CHUNK
    lora_w = DECAY_LORA + AAA_LORA + GATE_LORA
    wup = jnp.zeros((2, lora_w, RWKV_DIM), F32).at[:, :DECAY_LORA].set(w_up).astype(BF16)
    aup = jnp.zeros((2, lora_w, RWKV_DIM), F32).at[:, DECAY_LORA:DECAY_LORA + AAA_LORA].set(a_up).astype(BF16)
    gup = jnp.zeros((lora_w, RWKV_DIM), F32).at[DECAY_LORA + AAA_LORA:].set(g_up).astype(BF16)

    def fwd(c):
        return c

    def back(c):
        return jnp.where(c < ncc, ncc - 1 - c, nc - 1 - (c - ncc))

    full = lambda shape: pl.BlockSpec(shape, lambda bi, c: (0,) * len(shape))

    def inputs(chunk):
        col = lambda off: pl.BlockSpec((RWKV_NB, CHUNK, RWKV_DIM), lambda bi, c: (bi, chunk(c), off // RWKV_DIM))
        return [col(P_RKV), col(P_RKV + RWKV_DIM), col(P_RKV + 2 * RWKV_DIM),
                pl.BlockSpec((RWKV_NB, CHUNK, lora_w), lambda bi, c: (bi, chunk(c), P_LORA // lora_w))]

    tok = lambda chunk: pl.BlockSpec((RWKV_NB, CHUNK, RWKV_DIM), lambda bi, c: (bi, chunk(c), 0))
    assert b % RWKV_NB == 0
    return pl.pallas_call(
        _rwkv_chunk_kernel,
        grid=(b // RWKV_NB, nc),
        in_specs=[*inputs(fwd), *inputs(back),
                  full((2, RWKV_DIM)), full((2, 1, RWKV_DIM)), full((2, 1, RWKV_DIM)),
                  full(wup.shape), full(aup.shape), full(gup.shape), full((1, RWKV_DIM)),
                  full((RWKV_DIM, RWKV_DIM))],
        out_specs=[tok(fwd), tok(back), tok(fwd), tok(fwd)],
        out_shape=[jax.ShapeDtypeStruct((b, s, RWKV_DIM), F32)] * 4,
        scratch_shapes=[pltpu.VMEM((RWKV_NB * 2 * N_PAIR, LANES, LANES), F32)],
        compiler_params=_cparams(("parallel", "arbitrary")),
        name="rwkv_chunks",
    )(*([p] * 8), kvec, w0.reshape(2, 1, -1), a0.reshape(2, 1, -1), wup, aup, gup, r_k.reshape(1, -1),
      _head_block_diag())


def _rwkv_readout_kernel(yf_ref, yb_ref, bonus_ref, gate_ref, lng_ref, lnb_ref, bd_ref, o_ref):
    bd = bd_ref[...]
    y = yf_ref[0] + yb_ref[0]
    inv_n = 1.0 / RWKV_HEAD
    dev = y - _headsum(y, bd) * inv_n
    var = _headsum(dev * dev, bd) * inv_n
    yn = dev * lax.rsqrt(var + RWKV_LN_EPS) * lng_ref[...] + lnb_ref[...]
    o_ref[0] = ((yn + bonus_ref[0]) * gate_ref[0]).astype(BF16)


def _rwkv_readout(y_fwd, y_bwd, bonus, gate, ln_g, ln_b, *, row_off):
    b, s, _ = y_fwd.shape
    nt = s // TM - row_off
    full = lambda shape: pl.BlockSpec(shape, lambda bi, i: (0,) * len(shape))
    tok = pl.BlockSpec((1, TM, RWKV_DIM), lambda bi, i: (bi, i + row_off, 0))
    return pl.pallas_call(
        _rwkv_readout_kernel,
        grid=(b, nt),
        in_specs=[tok, tok, tok, tok, full((1, RWKV_DIM)), full((1, RWKV_DIM)), full((RWKV_DIM, RWKV_DIM))],
        out_specs=pl.BlockSpec((1, TM, RWKV_DIM), lambda bi, i: (bi, i, 0)),
        out_shape=jax.ShapeDtypeStruct((b, nt * TM, RWKV_DIM), BF16),
        compiler_params=_cparams(("parallel", "parallel")),
        name="rwkv_readout",
    )(y_fwd, y_bwd, bonus, gate, ln_g.reshape(1, -1), ln_b.reshape(1, -1), _head_block_diag())


DFT_N2 = LANES
HY_MIN_LEN = 1024
HY_CT = 128


def _hyena_mlp_kernel(feats_ref, w1_ref, b1_ref, w2_ref, b2_ref, freq_ref, h_ref):
    h = jnp.sin(freq_ref[0:1] * (_mm_f32(feats_ref[0], w1_ref[...]) + b1_ref[...]))
    h_ref[0] = jnp.sin(freq_ref[1:2] * (_mm_f32(h, w2_ref[...]) + b2_ref[...]))


def _hyena_filter_kernel(h_ref, w3f_ref, w3b_ref, t_ref, delta_ref, k_ref):
    hf = _mm_f32(h_ref[0], w3f_ref[...]) * jnp.exp(-t_ref[0] * delta_ref[...])
    hb = _mm_f32(h_ref[1], w3b_ref[...]) * jnp.exp(-t_ref[1] * delta_ref[...])
    norm = (jnp.sum(jnp.abs(hf), axis=0, keepdims=True) + jnp.sum(jnp.abs(hb), axis=0, keepdims=True))
    r = pl.program_id(2)
    rows = lax.broadcasted_iota(jnp.int32, hf.shape, 0)
    tail = jnp.where(rows == 0, 0.0, hb)
    blk = jnp.where(r == 0, hf, jnp.where(r == pl.num_programs(2) - 1, tail, 0.0))
    k_ref[0] = blk / norm


def _hyena_filter_buffer(n, nc, w1, b1, w2, b2, w3, freq):
    lag = jnp.stack([jnp.arange(n), jnp.where(jnp.arange(n) == 0, 0, n - jnp.arange(n))]).astype(F32)
    t = (lag / (n - 1))[:, :, None]
    bands = jnp.linspace(1e-4, HYENA_BANDS - 1, HYENA_BANDS, dtype=F32)
    ang = (2.0 * math.pi / n) * lag[:, :, None] * bands[None, None, :]
    feats = jnp.concatenate([t, jnp.cos(ang), -jnp.sin(ang),
                             jnp.zeros((2, n, HYENA_FW - HYENA_EMB), F32)], axis=-1)
    w1p = jnp.zeros((HYENA_FW, HYENA_FW), F32).at[:HYENA_EMB].set(w1)
    deltas = jnp.abs(jnp.linspace(math.log(HYENA_TARGET) / HYENA_SLOW,
                                  math.log(HYENA_TARGET) / HYENA_FAST, HYENA_DIM, dtype=F32))[None, :]
    fixed = lambda shape: pl.BlockSpec(shape, lambda d: (0,) * len(shape))
    hidden = pl.pallas_call(
        _hyena_mlp_kernel,
        grid=(2,),
        in_specs=[pl.BlockSpec((1, n, HYENA_FW), lambda d: (d, 0, 0)), fixed((HYENA_FW, HYENA_FW)),
                  fixed((1, HYENA_FW)), fixed((HYENA_FW, HYENA_FW)), fixed((1, HYENA_FW)),
                  fixed((2, HYENA_FW))],
        out_specs=pl.BlockSpec((1, n, HYENA_FW), lambda d: (d, 0, 0)),
        out_shape=jax.ShapeDtypeStruct((2, n, HYENA_FW), F32),
        compiler_params=_cparams(("parallel",)),
        name="hyena_filter_mlp",
    )(feats, w1p, b1.reshape(1, -1), w2, b2.reshape(1, -1), freq)
    tc = 256
    nj = HYENA_DIM // tc
    full = lambda shape: pl.BlockSpec(shape, lambda o, j, r: (0,) * len(shape))
    return pl.pallas_call(
        _hyena_filter_kernel,
        grid=(HYENA_ORDER, nj, nc // n),
        in_specs=[full((2, n, HYENA_FW)),
                  pl.BlockSpec((HYENA_FW, tc), lambda o, j, r: (0, o * 2 * nj + j)),
                  pl.BlockSpec((HYENA_FW, tc), lambda o, j, r: (0, o * 2 * nj + nj + j)),
                  full((2, n, 1)), pl.BlockSpec((1, tc), lambda o, j, r: (0, j))],
        out_specs=pl.BlockSpec((1, n, tc), lambda o, j, r: (o, r, j)),
        out_shape=jax.ShapeDtypeStruct((HYENA_ORDER, nc, HYENA_DIM), F32),
        compiler_params=_cparams(("parallel", "parallel", "parallel")),
        name="hyena_filters",
    )(hidden, w3, w3, t, deltas)


HY_J = 8
HY_GROUPS = DFT_N2 // HY_J


def _dft_tables(n1):
    nc = n1 * DFT_N2
    f1 = np.arange(n1)
    ang = 2.0 * np.pi * ((f1[:, None] * f1[None, :]) % n1) / n1
    eye = np.eye(HY_J)
    w1 = np.kron(np.concatenate([np.cos(ang), -np.sin(ang)], axis=0), eye)
    v3 = np.kron(np.concatenate([np.cos(ang), -np.sin(ang)], axis=1), eye) / nc
    t2 = np.arange(DFT_N2).reshape(HY_GROUPS, 1, HY_J)
    tw = 2.0 * np.pi * ((f1[None, :, None] * t2) % nc) / nc
    tw = np.broadcast_to(tw.reshape(HY_GROUPS, n1 * HY_J, 1), (HY_GROUPS, n1 * HY_J, LANES))
    k = np.arange(DFT_N2)
    phi = 2.0 * np.pi * ((k[:, None] * k[None, :]) % DFT_N2) / DFT_N2
    c, s = np.cos(phi), np.sin(phi)
    f_fwd = np.block([[c, s], [-s, c]])
    f_inv = np.block([[c, -s], [s, c]])
    names = ("w1", "v3", "ctw", "stw", "f_fwd", "f_inv")
    return {n: jnp.asarray(t, BF16) for n, t in zip(names, (w1, v3, np.cos(tw), np.sin(tw), f_fwd, f_inv))}


def _tiles(ref, rows, g, lead=()):
    return jnp.concatenate([ref[lead + (pl.ds(r * DFT_N2 + g * HY_J, HY_J), slice(None))] for r in rows], axis=0)


def _dft_rows_in(x_ref, w_ref, ctw_ref, stw_ref, a_ref, t1n, n1, t1_valid):
    half = n1 * HY_J
    for g in range(HY_GROUPS):
        xg = _tiles(x_ref, range(t1_valid), g, lead=(0,))
        pq = _mm(w_ref[:, :t1_valid * HY_J], xg)
        p, q = pq[:half], pq[half:]
        c, s = ctw_ref[g].astype(F32), stw_ref[g].astype(F32)
        re = c * p + s * q
        im = c * q - s * p
        for m in range(n1):
            rows = slice(m * HY_J, (m + 1) * HY_J)
            a_ref[pl.ds(m * DFT_N2 + g * HY_J, HY_J), :] = re[rows]
            a_ref[pl.ds((n1 + m) * DFT_N2 + g * HY_J, HY_J), :] = im[rows]


def _slab(f1):
    return slice(f1 * DFT_N2, (f1 + 1) * DFT_N2)


def _spectrum_kernel(x_ref, w_ref, ctw_ref, stw_ref, ff_ref, k_ref, a_ref, *, n1):
    _dft_rows_in(x_ref, w_ref, ctw_ref, stw_ref, a_ref, n1, n1, n1)
    for f1 in range(n1):
        a = jnp.concatenate([a_ref[_slab(f1)], a_ref[_slab(n1 + f1)]], axis=0)
        k_ref[0, f1] = _mm(ff_ref[...], a)


def _filter_spectrum(kbuf, tabs, n1):
    no, nc, _ = kbuf.shape
    nj = HYENA_DIM // HY_CT
    const = lambda a: pl.BlockSpec(a.shape, lambda o, j: (0,) * a.ndim)
    consts = [tabs[n] for n in ("w1", "ctw", "stw", "f_fwd")]
    return pl.pallas_call(
        functools.partial(_spectrum_kernel, n1=n1),
        grid=(no, nj),
        in_specs=[pl.BlockSpec((1, nc, HY_CT), lambda o, j: (o, 0, j)), *[const(a) for a in consts]],
        out_specs=pl.BlockSpec((1, n1, 2 * DFT_N2, HY_CT), lambda o, j: (o, 0, 0, j)),
        out_shape=jax.ShapeDtypeStruct((no, n1, 2 * DFT_N2, HYENA_DIM), F32),
        scratch_shapes=[pltpu.VMEM((2 * n1 * DFT_N2, HY_CT), F32)],
        compiler_params=_cparams(("parallel", "parallel")),
        name="hyena_filter_spectrum",
    )(kbuf, *consts)


def _hyena_conv_kernel(x_ref, gate_ref, k_ref, w_ref, ctw_ref, stw_ref, ff_ref, fi_ref, v_ref, bias_ref,
                       o_ref, a_ref, *, t1n, n1, t1_valid):
    _dft_rows_in(x_ref, w_ref, ctw_ref, stw_ref, a_ref, t1n, n1, t1_valid)
    for f1 in range(n1):
        re, im = _slab(f1), _slab(n1 + f1)
        x = _mm(ff_ref[...], jnp.concatenate([a_ref[re], a_ref[im]], axis=0))
        xre, xim = x[:DFT_N2], x[DFT_N2:]
        kre, kim = k_ref[0, f1, :DFT_N2], k_ref[0, f1, DFT_N2:]
        bm = _mm(fi_ref[...], jnp.concatenate([xre * kre - xim * kim, xre * kim + xim * kre], axis=0))
        a_ref[re] = bm[:DFT_N2]
        a_ref[im] = bm[DFT_N2:]
    bias = bias_ref[...]
    for g in range(HY_GROUPS):
        br, bi = _tiles(a_ref, range(n1), g), _tiles(a_ref, range(n1, 2 * n1), g)
        c, s = ctw_ref[g].astype(F32), stw_ref[g].astype(F32)
        y = _mm(v_ref[:t1n * HY_J], jnp.concatenate([c * br - s * bi, s * br + c * bi], axis=0))
        for t1 in range(t1n):
            rows = pl.ds(t1 * DFT_N2 + g * HY_J, HY_J)
            o_ref[0, rows, :] = gate_ref[0, rows, :] * (y[t1 * HY_J:(t1 + 1) * HY_J] + bias * x_ref[0, rows, :])


def _hyena_conv(z, z_spec, gate, gate_spec, kspec, order, bias, tabs, n1, n_pad, n):
    bx = z.shape[0]
    t1n = n_pad // DFT_N2
    consts = [tabs[name] for name in ("w1", "ctw", "stw", "f_fwd", "f_inv", "v3")]
    const = lambda a: pl.BlockSpec(a.shape, lambda j, bi: (0,) * a.ndim)
    return pl.pallas_call(
        functools.partial(_hyena_conv_kernel, t1n=t1n, n1=n1, t1_valid=-(-n // DFT_N2)),
        grid=(HYENA_DIM // HY_CT, bx),
        in_specs=[z_spec, gate_spec,
                  pl.BlockSpec((1, n1, 2 * DFT_N2, HY_CT), lambda j, bi: (order, 0, 0, j)),
                  *[const(a) for a in consts],
                  pl.BlockSpec((1, HY_CT), lambda j, bi: (0, j))],
        out_specs=pl.BlockSpec((1, n_pad, HY_CT), lambda j, bi: (bi, 0, j)),
        out_shape=jax.ShapeDtypeStruct((bx, n_pad, HYENA_DIM), F32),
        scratch_shapes=[pltpu.VMEM((2 * n1 * DFT_N2, HY_CT), F32)],
        compiler_params=_cparams(("parallel", "parallel")),
        name="hyena_conv",
    )(z, gate, kspec, *consts, bias.reshape(1, -1))


def _hyena_operator(p_hy, row0, n, filt_params, bias):
    n_pad = max(n, HY_MIN_LEN)
    nc = 2 * n_pad
    n1 = nc // DFT_N2
    tabs = _dft_tables(n1)
    kspec = _filter_spectrum(_hyena_filter_buffer(n, nc, *filt_params), tabs, n1)
    nj = HYENA_DIM // HY_CT
    window = lambda part: pl.BlockSpec((pl.Element(1), pl.Element(n_pad), pl.Element(HY_CT)),
                                       lambda j, bi: (bi, row0, (part * nj + j) * HY_CT))
    own = pl.BlockSpec((1, n_pad, HY_CT), lambda j, bi: (bi, 0, j))
    z = _hyena_conv(p_hy, window(0), p_hy, window(1), kspec, 0, bias[0], tabs, n1, n_pad, n)
    return _hyena_conv(z, own, p_hy, window(2), kspec, 1, bias[1], tabs, n1, n_pad, n)


def _merge_kernel(x_ref, mod_ref, g_ref, gates_ref, bg_ref, ya_ref, yb_ref, yh_ref, yhc_ref, yd_ref,
                  wb_ref, wo_ref, o_ref, *, n_ctx_tiles, row_off):
    is_ctx = pl.program_id(1) + row_off < n_ctx_tiles
    yh = jnp.where(is_ctx, yhc_ref[0], yh_ref[0])
    merged = None
    for br, y in enumerate((ya_ref[0], yb_ref[0], yh, yd_ref[0])):
        gate = _sigmoid(gates_ref[0, :, br * D_MODEL:(br + 1) * D_MODEL] + bg_ref[br:br + 1])
        term = gate * jnp.dot(y.astype(BF16), wb_ref[br], preferred_element_type=F32)
        merged = term if merged is None else merged + term
    y = jnp.dot(merged.astype(BF16), wo_ref[...], preferred_element_type=F32)
    o_ref[0] = x_ref[0] + mod_ref[0, 5:6] * _rms(y, g_ref[3:4])


def _merge(x, mods, norm_g, p, b_gate, ya, yb, yh, yh_ctx, yd, w_branch, w_out, *, n_ctx_tiles, row_off):
    b, s, _ = x.shape
    nt = s // TM - row_off
    n_lat = mods.shape[0] - 1
    br = lambda: pl.BlockSpec((1, TM, BRANCH_DIM), lambda bi, i: (bi, i, 0))
    n_lat_tiles = s // TM - n_ctx_tiles
    br_a = pl.BlockSpec((1, TM, BRANCH_DIM), lambda bi, i: (
        bi, jnp.where(i + row_off < n_ctx_tiles, n_lat_tiles + i, i + row_off - n_ctx_tiles), 0))
    br_h = pl.BlockSpec((1, TM, BRANCH_DIM), lambda bi, i: (bi, jnp.maximum(i + row_off - n_ctx_tiles, 0), 0))
    br_hc = pl.BlockSpec((1, TM, BRANCH_DIM), lambda bi, i: (
        bi, jnp.minimum(i + row_off, max(n_ctx_tiles - 1, 0)) if yh_ctx is not None else 0, 0))
    return pl.pallas_call(
        functools.partial(_merge_kernel, n_ctx_tiles=n_ctx_tiles if yh_ctx is not None else 0, row_off=row_off),
        grid=(b, nt),
        in_specs=[pl.BlockSpec((1, TM, D_MODEL), lambda bi, i: (bi, i + row_off, 0)),
                  pl.BlockSpec((1, N_MOD, D_MODEL),
                               lambda bi, i: (jnp.where(i + row_off < n_ctx_tiles, n_lat, bi), 0, 0)),
                  pl.BlockSpec((6, D_MODEL), lambda bi, i: (0, 0)),
                  pl.BlockSpec((1, TM, GATE_COLS), lambda bi, i: (bi, i + row_off, 0)),
                  pl.BlockSpec((N_BRANCH, D_MODEL), lambda bi, i: (0, 0)),
                  br_a, br(), br_h, br_hc, br(),
                  pl.BlockSpec((N_BRANCH, BRANCH_DIM, D_MODEL), lambda bi, i: (0, 0, 0)),
                  pl.BlockSpec((D_MODEL, D_MODEL), lambda bi, i: (0, 0))],
        out_specs=pl.BlockSpec((1, TM, D_MODEL), lambda bi, i: (bi, i, 0)),
        out_shape=jax.ShapeDtypeStruct((b, nt * TM, D_MODEL), F32),
        compiler_params=_cparams(("parallel", "parallel")),
        name="merge_branches",
    )(x, mods, norm_g, p, b_gate, ya, yb, yh, yh if yh_ctx is None else yh_ctx, yd, w_branch, w_out)


def kernel(x, c, ctx, c_ctx, w_mod, b_mod, norm_g, ffn_w13, ffn_w2, w_in, b_gate, mla_norm_q, mla_norm_kv, mla_w_uq, mla_w_ukv, rwkv_mu, rwkv_w0, rwkv_w_up, rwkv_a0, rwkv_a_up, rwkv_g_up, rwkv_kvec, rwkv_r_k, rwkv_ln_g, rwkv_ln_b, hyena_conv, hyena_conv_b, hyena_w1, hyena_b1, hyena_w2, hyena_b2, hyena_w3, hyena_freq, hyena_bias, swa_sink, w_branch, w_out):
    b, n, _ = x.shape
    n_ctx = ctx.shape[1]
    nct = n_ctx // TM
    xall = jnp.concatenate([ctx, x], axis=1)
    c_all = jnp.concatenate([c, c_ctx[None]], axis=0)
    tabs_mla = _rope_tables(n, n_ctx, MLA_ROPE, MLA_NOPE, LANES)
    tabs_swa = _rope_tables(n, n_ctx, SWA_HEAD, 0, SWA_HEAD)
    depth = w_mod.shape[0]
    for l in range(depth):
        with_ctx = l + 1 < depth
        row_off = 0 if with_ctx else nct
        mods = _modulation(c_all, w_mod[l], b_mod[l])
        xall = _ffn(xall, mods, norm_g[l], ffn_w13[l, 0].astype(BF16), ffn_w2[l, 0].astype(BF16),
                    mod0=0, g0=0, n_ctx_tiles=nct)
        coef = _shift_coefficients(rwkv_mu[l], hyena_conv[l], hyena_conv_b[l])
        p, p_hy = _inproj(xall, mods, norm_g[l], _permute_w_in(w_in[l]).astype(BF16), coef, n_ctx_tiles=nct)
        q, k, v = _mla_prep(p, mla_norm_q[l], mla_norm_kv[l], mla_w_uq[l], mla_w_ukv[l], tabs_mla)
        ya = _mla_attention(q, k, v, n_ctx=n_ctx, with_ctx=with_ctx)
        y_fwd, y_bwd, bonus, gate = _rwkv_chunks(p, rwkv_kvec[l], rwkv_w0[l], rwkv_a0[l], rwkv_w_up[l],
                                                 rwkv_a_up[l], rwkv_g_up[l], rwkv_r_k[l], n_ctx=n_ctx)
        yb = _rwkv_readout(y_fwd, y_bwd, bonus, gate, rwkv_ln_g[l], rwkv_ln_b[l], row_off=row_off)
        filt = (hyena_w1[l], hyena_b1[l], hyena_w2[l], hyena_b2[l], hyena_w3[l], hyena_freq[l])
        yh = _hyena_operator(p_hy, n_ctx, n, filt, hyena_bias[l])
        yh_ctx = _hyena_operator(p_hy, 0, n_ctx, filt, hyena_bias[l]) if with_ctx else None
        q, k, v = _swa_prep(p, tabs_swa)
        yd = _swa_attention(q, k, v, swa_sink[l], n_ctx=n_ctx, q_off=row_off * TM // SWA_TQ)
        xall = _merge(xall, mods, norm_g[l], p, b_gate[l], ya, yb, yh, yh_ctx, yd, w_branch[l].astype(BF16),
                      w_out[l].astype(BF16), n_ctx_tiles=nct, row_off=row_off)
        xall = _ffn(xall, mods, norm_g[l], ffn_w13[l, 1].astype(BF16), ffn_w2[l, 1].astype(BF16),
                    mod0=6, g0=4, n_ctx_tiles=nct - row_off)
    return xall
```

```python
import functools
import math

import numpy as np
import jax
import jax.numpy as jnp
from jax import lax
from jax.experimental import pallas as pl
from jax.experimental.pallas import tpu as pltpu

F32 = jnp.float32
BF16 = jnp.bfloat16

D_MODEL = 1024
GRID_W = 64
N_BRANCH = 4
N_MOD = 9
FF_DIM = 2816
EPS = 1e-6
ROPE_BASE = 10000.0
NEG_INF = -1e30
BRANCH_DIM = 512
MLA_HEADS = 8
MLA_NOPE = 64
MLA_ROPE = 32
MLA_V = 64
MLA_Q_RANK = 256
MLA_KV_RANK = 128
RWKV_HEADS = 8
RWKV_HEAD = 64
RWKV_DIM = RWKV_HEADS * RWKV_HEAD
DECAY_LORA = 64
AAA_LORA = 64
GATE_LORA = 128
RWKV_LN_EPS = 64e-5
HYENA_DIM = 512
HYENA_ORDER = 2
HYENA_EMB = 33
HYENA_BANDS = (HYENA_EMB - 1) // 2
HYENA_FW = 64
HYENA_TARGET = 1e-2
HYENA_FAST = 0.3
HYENA_SLOW = 1.5
SWA_HEADS = 8
SWA_KV_HEADS = 2
SWA_HEAD = 64
SWA_GROUP = SWA_HEADS // SWA_KV_HEADS
WINDOW = 128
GATE_COLS = N_BRANCH * D_MODEL
MLA_COLS = MLA_Q_RANK + MLA_KV_RANK + MLA_ROPE
RWKV_COLS = 3 * RWKV_DIM + DECAY_LORA + AAA_LORA + GATE_LORA
HYENA_COLS = 3 * HYENA_DIM
SWA_COLS = (SWA_HEADS + 2 * SWA_KV_HEADS) * SWA_HEAD

LANES = 128
V7X_VMEM_LIMIT = 56 * 1024 * 1024

TM = 256
FFN_SUB_TILES = 4
INPROJ_SUB_TILES = 2
FF_CHUNK = 256
IN_CHUNK = 512
CHUNK = 64

P_GATE = 0
P_RKV = 4096
P_SWAQ = 5632
P_LORA = 6144
P_CQ = 6400
P_CKV = 6656
P_KR = 6784
P_SWAK = 6912
P_SWAV = 7040
P_COLS = 7168
P_HY = 7168
W_COLS = P_HY + HYENA_COLS
_SHIFT_COLS = ((P_RKV, P_SWAQ), (P_LORA, P_CQ), (P_HY, W_COLS))
_SHIFT_CHUNKS = [any(lo < (j + 1) * IN_CHUNK and j * IN_CHUNK < hi for lo, hi in _SHIFT_COLS)
                 for j in range(W_COLS // IN_CHUNK)]


def _cparams(sem, vmem=V7X_VMEM_LIMIT):
    return pltpu.CompilerParams(dimension_semantics=sem, vmem_limit_bytes=vmem)


def _mm(a, b):
    return jnp.dot(a.astype(BF16), b.astype(BF16), preferred_element_type=F32)


def _mm_nt(a, b):
    return lax.dot_general(a.astype(BF16), b.astype(BF16), (((1,), (1,)), ((), ())),
                           preferred_element_type=F32)


def _mm_tn(a, b):
    return lax.dot_general(a.astype(BF16), b.astype(BF16), (((0,), (0,)), ((), ())),
                           preferred_element_type=F32)


def _mm_f32(a, b):
    return jnp.dot(a, b, preferred_element_type=F32, precision=lax.Precision.HIGHEST)


def _rms(x, g):
    return x * lax.rsqrt(jnp.mean(x * x, axis=-1, keepdims=True) + EPS) * g


def _sigmoid(x):
    return 1.0 / (1.0 + jnp.exp(-x))


def _mod_kernel(c_ref, w_ref, b_ref, o_ref):
    c = c_ref[...]
    o_ref[...] = _mm(c * _sigmoid(c), w_ref[...]) + b_ref[...]


def _modulation(c_all, w_mod, b_mod):
    r = c_all.shape[0]
    rp = -(-r // 8) * 8
    c_pad = jnp.zeros((rp, D_MODEL), F32).at[:r].set(c_all)
    tn = 1024
    out = pl.pallas_call(
        _mod_kernel,
        grid=(N_MOD * D_MODEL // tn,),
        in_specs=[pl.BlockSpec((rp, D_MODEL), lambda j: (0, 0)),
                  pl.BlockSpec((D_MODEL, tn), lambda j: (0, j)),
                  pl.BlockSpec((1, tn), lambda j: (0, j))],
        out_specs=pl.BlockSpec((rp, tn), lambda j: (0, j)),
        out_shape=jax.ShapeDtypeStruct((rp, N_MOD * D_MODEL), F32),
        compiler_params=_cparams(("arbitrary",)),
        name="modulation",
    )(c_pad, w_mod, b_mod.reshape(1, -1))
    return out[:r].reshape(r, N_MOD, D_MODEL)


def _sub_tiles(n_tiles, most):
    return max(g for g in range(1, most + 1) if n_tiles % g == 0)


def _mod_specs(n_sub, tiles_per_seq, n_ctx_tiles, ctx_row):
    def spec(k):
        def index(i):
            t = i * n_sub + k
            return (jnp.where(t % tiles_per_seq < n_ctx_tiles, ctx_row, t // tiles_per_seq), 0, 0)
        return pl.BlockSpec((1, N_MOD, D_MODEL), index)
    return [spec(k) for k in range(n_sub)]


def _ffn_kernel(x_ref, *refs, mod0, g0, n_sub):
    mod_refs = refs[:n_sub]
    g_ref, w13_ref, w2_ref, o_ref = refs[n_sub:]
    tiles = [slice(t * TM, (t + 1) * TM) for t in range(n_sub)]
    u = jnp.concatenate(
        [(_rms(x_ref[rows], g_ref[g0:g0 + 1]) * (1.0 + m[0, mod0 + 1:mod0 + 2]) + m[0, mod0:mod0 + 1]).astype(BF16)
         for rows, m in zip(tiles, mod_refs)], axis=0)
    acc = jnp.zeros(x_ref.shape, F32)
    for f in range(FF_DIM // FF_CHUNK):
        lo = f * FF_CHUNK
        a = jnp.dot(u, w13_ref[:, lo:lo + FF_CHUNK], preferred_element_type=F32)
        b = jnp.dot(u, w13_ref[:, FF_DIM + lo:FF_DIM + lo + FF_CHUNK], preferred_element_type=F32)
        h = (a * _sigmoid(a) * b).astype(BF16)
        acc = acc + jnp.dot(h, w2_ref[lo:lo + FF_CHUNK, :], preferred_element_type=F32)
    hn = _rms(acc, g_ref[g0 + 1:g0 + 2])
    for rows, m in zip(tiles, mod_refs):
        o_ref[rows] = x_ref[rows] + 0.5 * m[0, mod0 + 2:mod0 + 3] * hn[rows]


def _ffn(x, mods, norm_g, w13, w2, *, mod0, g0, n_ctx_tiles):
    b, s, _ = x.shape
    n_tiles = b * s // TM
    n_sub = _sub_tiles(n_tiles, FFN_SUB_TILES)
    rows = n_sub * TM
    out = pl.pallas_call(
        functools.partial(_ffn_kernel, mod0=mod0, g0=g0, n_sub=n_sub),
        grid=(n_tiles // n_sub,),
        in_specs=[pl.BlockSpec((rows, D_MODEL), lambda i: (i, 0)),
                  *_mod_specs(n_sub, s // TM, n_ctx_tiles, mods.shape[0] - 1),
                  pl.BlockSpec((6, D_MODEL), lambda i: (0, 0)),
                  pl.BlockSpec(memory_space=pltpu.VMEM),
                  pl.BlockSpec(memory_space=pltpu.VMEM)],
        out_specs=pl.BlockSpec((rows, D_MODEL), lambda i: (i, 0)),
        out_shape=jax.ShapeDtypeStruct((b * s, D_MODEL), F32),
        compiler_params=_cparams(("parallel",)),
        name="ffn_half_step",
    )(x.reshape(b * s, D_MODEL), *([mods] * n_sub), norm_g, w13, w2)
    return out.reshape(b, s, D_MODEL)


def _inproj_kernel(x_ref, xp_ref, xn_ref, *refs, n_sub, tiles_per_seq, n_ctx_tiles):
    mod_refs = refs[:n_sub]
    g_ref, w_ref, coef_ref, o_ref, hy_ref = refs[n_sub:]
    g = g_ref[2:3]

    def modulated(x, m):
        return (_rms(x, g) * (1.0 + m[0, 4:5]) + m[0, 3:4]).astype(BF16)

    u = jnp.concatenate([modulated(x_ref[t * TM:(t + 1) * TM], m) for t, m in enumerate(mod_refs)], axis=0)
    u_prev = modulated(xp_ref[...], mod_refs[0])
    u_next = modulated(xn_ref[...], mod_refs[-1])
    rows = lax.broadcasted_iota(jnp.int32, (n_sub * TM, 1), 0)
    keep_prev = jnp.ones((n_sub * TM, 1), F32)
    keep_next = jnp.ones((n_sub * TM, 1), F32)
    for t in range(n_sub):
        w = (pl.program_id(0) * n_sub + t) % tiles_per_seq
        seg_start = (w == 0) | (w == n_ctx_tiles)
        seg_end = (w == n_ctx_tiles - 1) | (w == tiles_per_seq - 1)
        keep_prev = jnp.where((rows == t * TM) & seg_start, 0.0, keep_prev)
        keep_next = jnp.where((rows == (t + 1) * TM - 1) & seg_end, 0.0, keep_next)
    for j in range(W_COLS // IN_CHUNK):
        cols = slice(j * IN_CHUNK, (j + 1) * IN_CHUNK)
        p = jnp.dot(u, w_ref[:, cols], preferred_element_type=F32)
        if _SHIFT_CHUNKS[j]:
            p_first = jnp.dot(u_prev, w_ref[:, cols], preferred_element_type=F32)[7:8]
            p_last = jnp.dot(u_next, w_ref[:, cols], preferred_element_type=F32)[0:1]
            prev = jnp.where(rows == 0, p_first, pltpu.roll(p, 1, axis=0)) * keep_prev
            nxt = jnp.where(rows == n_sub * TM - 1, p_last, pltpu.roll(p, n_sub * TM - 1, axis=0)) * keep_next
            p = (coef_ref[0:1, cols] * p + coef_ref[1:2, cols] * prev + coef_ref[2:3, cols] * nxt
                 + coef_ref[3:4, cols])
        if j * IN_CHUNK < P_COLS:
            o_ref[:, cols] = p.astype(BF16)
        else:
            hy_ref[:, j * IN_CHUNK - P_HY:(j + 1) * IN_CHUNK - P_HY] = p


def _inproj(x, mods, norm_g, w_in_p, coef, *, n_ctx_tiles):
    b, s, _ = x.shape
    n_tiles = b * s // TM
    n_sub = _sub_tiles(n_tiles, INPROJ_SUB_TILES)
    rows = n_sub * TM
    r8 = rows // 8
    p, hy = pl.pallas_call(
        functools.partial(_inproj_kernel, n_sub=n_sub, tiles_per_seq=s // TM, n_ctx_tiles=n_ctx_tiles),
        grid=(n_tiles // n_sub,),
        in_specs=[pl.BlockSpec((rows, D_MODEL), lambda i: (i, 0)),
                  pl.BlockSpec((8, D_MODEL), lambda i: (jnp.maximum(i * r8 - 1, 0), 0)),
                  pl.BlockSpec((8, D_MODEL), lambda i: (jnp.minimum((i + 1) * r8, b * s // 8 - 1), 0)),
                  *_mod_specs(n_sub, s // TM, n_ctx_tiles, mods.shape[0] - 1),
                  pl.BlockSpec((6, D_MODEL), lambda i: (0, 0)),
                  pl.BlockSpec(memory_space=pltpu.VMEM),
                  pl.BlockSpec((4, W_COLS), lambda i: (0, 0))],
        out_specs=[pl.BlockSpec((rows, P_COLS), lambda i: (i, 0)),
                   pl.BlockSpec((rows, HYENA_COLS), lambda i: (i, 0))],
        out_shape=[jax.ShapeDtypeStruct((b * s, P_COLS), BF16),
                   jax.ShapeDtypeStruct((b * s, HYENA_COLS), F32)],
        compiler_params=_cparams(("parallel",)),
        name="in_projection",
    )(*([x.reshape(b * s, D_MODEL)] * 3), *([mods] * n_sub), norm_g, w_in_p, coef)
    return p.reshape(b, s, P_COLS), hy.reshape(b, s, HYENA_COLS)


def _shift_coefficients(rwkv_mu, hyena_conv, hyena_conv_b):
    mu = rwkv_mu.astype(F32)
    coef = jnp.zeros((4, W_COLS), F32).at[0].set(1.0)
    for off, sl in ((P_RKV, slice(0, 3 * RWKV_DIM)), (P_LORA, slice(3 * RWKV_DIM, RWKV_COLS))):
        width = sl.stop - sl.start
        coef = coef.at[0, off:off + width].set(1.0 - mu[0, sl] - mu[1, sl])
        coef = coef.at[1, off:off + width].set(mu[0, sl])
        coef = coef.at[2, off:off + width].set(mu[1, sl])
    hy = slice(P_HY, P_HY + HYENA_COLS)
    coef = coef.at[0, hy].set(hyena_conv[1]).at[1, hy].set(hyena_conv[0]).at[2, hy].set(hyena_conv[2])
    return coef.at[3, hy].set(hyena_conv_b)


def _permute_w_in(w_in):
    o_mla = GATE_COLS
    o_rwkv = o_mla + MLA_COLS
    o_hy = o_rwkv + RWKV_COLS
    o_swa = o_hy + HYENA_COLS
    z = lambda n: jnp.zeros((D_MODEL, n), w_in.dtype)
    parts = [
        w_in[:, :GATE_COLS],
        w_in[:, o_rwkv:o_rwkv + 3 * RWKV_DIM],
        w_in[:, o_swa:o_swa + SWA_HEADS * SWA_HEAD],
        w_in[:, o_rwkv + 3 * RWKV_DIM:o_rwkv + RWKV_COLS],
        w_in[:, o_mla:o_mla + MLA_Q_RANK],
        w_in[:, o_mla + MLA_Q_RANK:o_mla + MLA_Q_RANK + MLA_KV_RANK],
        z(MLA_NOPE), w_in[:, o_mla + MLA_Q_RANK + MLA_KV_RANK:o_mla + MLA_COLS],
        z(LANES - MLA_NOPE - MLA_ROPE),
        w_in[:, o_swa + SWA_HEADS * SWA_HEAD:o_swa + SWA_COLS],
        w_in[:, o_hy:o_hy + HYENA_COLS],
    ]
    out = jnp.concatenate(parts, axis=1)
    assert out.shape[1] == W_COLS
    return out


def _rope_tables(n_lat, n_ctx, rot_dim, lane0, period):
    rows = n_lat // GRID_W
    row = jnp.repeat(jnp.arange(rows, dtype=F32), GRID_W)
    col = jnp.tile(jnp.arange(GRID_W, dtype=F32), rows)
    axis_dim = rot_dim // 2
    h = axis_dim // 2
    inv_freq = ROPE_BASE ** (-jnp.arange(0, axis_dim, 2, dtype=F32) / axis_dim)
    ang_r = row[:, None] * inv_freq
    ang_c = col[:, None] * inv_freq
    cos_rot = jnp.concatenate([jnp.cos(ang_r)] * 2 + [jnp.cos(ang_c)] * 2, axis=1)
    zeros = jnp.zeros_like(ang_r)
    sin_a = jnp.concatenate([-jnp.sin(ang_r), zeros, -jnp.sin(ang_c), zeros], axis=1)
    sin_b = jnp.concatenate([zeros, jnp.sin(ang_r), zeros, jnp.sin(ang_c)], axis=1)

    def widen(t, fill):
        g = jnp.full((n_lat, period), fill, F32).at[:, lane0:lane0 + rot_dim].set(t)
        g = jnp.tile(g, (1, LANES // period))
        ctx = jnp.full((n_ctx, LANES), fill, F32)
        return jnp.concatenate([ctx, g], axis=0)

    return widen(cos_rot, 1.0), widen(sin_a, 0.0), widen(sin_b, 0.0), h


def _rope128(x, cos, sin_a, sin_b, h):
    return x * cos + pltpu.roll(x, LANES - h, axis=1) * sin_a + pltpu.roll(x, h, axis=1) * sin_b


LOG2E = math.log2(math.e)
MLA_SCALE = (MLA_NOPE + MLA_ROPE) ** -0.5 * LOG2E
MLA_ONE_LANE = (MLA_V, 0)


def _mla_prep_kernel(cq_ref, ckv_ref, kr_ref, gq_ref, gkv_ref, wq_ref, wk_ref, wv_ref, vone_ref,
                     cos_ref, sa_ref, sb_ref, q_ref, k_ref, v_ref, *, h):
    cos, sa, sb = cos_ref[...], sa_ref[...], sb_ref[...]
    cq = _rms(cq_ref[0].astype(F32), gq_ref[...]).astype(BF16)
    ckv = _rms(ckv_ref[0].astype(F32), gkv_ref[...]).astype(BF16)
    q = jnp.dot(cq, wq_ref[...], preferred_element_type=F32)
    k = jnp.dot(ckv, wk_ref[...], preferred_element_type=F32)
    kr = _rope128(kr_ref[0].astype(F32), cos, sa, sb, h)
    for hd in range(MLA_HEADS):
        sl = slice(hd * LANES, (hd + 1) * LANES)
        q_ref[0, :, sl] = (_rope128(q[:, sl], cos, sa, sb, h) * MLA_SCALE).astype(BF16)
        k_ref[0, :, sl] = (k[:, sl] + kr).astype(BF16)
    v_ref[0] = (jnp.dot(ckv, wv_ref[...], preferred_element_type=F32) + vone_ref[...]).astype(BF16)


def _mla_prep(p, norm_q, norm_kv, w_uq, w_ukv, tabs):
    b, s, _ = p.shape
    cos, sa, sb, h = tabs
    hq = MLA_NOPE + MLA_ROPE
    wq = jnp.zeros((MLA_Q_RANK, MLA_HEADS, LANES), F32).at[:, :, :hq].set(
        w_uq.reshape(MLA_Q_RANK, MLA_HEADS, hq)).reshape(MLA_Q_RANK, MLA_HEADS * LANES).astype(BF16)
    wkv = w_ukv.reshape(MLA_KV_RANK, MLA_HEADS, MLA_NOPE + MLA_V)
    wk = jnp.zeros((MLA_KV_RANK, MLA_HEADS, LANES), F32).at[:, :, :MLA_NOPE].set(
        wkv[:, :, :MLA_NOPE]).reshape(MLA_KV_RANK, MLA_HEADS * LANES).astype(BF16)
    wv_pairs = wkv[:, :, MLA_NOPE:].reshape(MLA_KV_RANK, MLA_HEADS // 2, 2, MLA_V)
    gap = ((0, 0), (0, 0), (0, LANES - MLA_V))
    wv = jnp.stack([jnp.pad(wv_pairs[:, :, 0], gap), jnp.pad(wv_pairs[:, :, 1], gap[:2] + (gap[2][::-1],))],
                   axis=2).reshape(MLA_KV_RANK, MLA_HEADS * LANES).astype(BF16)
    lane_id = np.arange(MLA_HEADS * LANES) % (2 * LANES)
    vone = jnp.asarray((lane_id == MLA_ONE_LANE[0]) | (lane_id == LANES + MLA_ONE_LANE[1]), F32)[None, :]
    full = lambda shape: pl.BlockSpec(shape, lambda bi, i: (0,) * len(shape))
    tab = pl.BlockSpec((TM, LANES), lambda bi, i: (i, 0))
    return pl.pallas_call(
        functools.partial(_mla_prep_kernel, h=h),
        grid=(b, s // TM),
        in_specs=[pl.BlockSpec((1, TM, MLA_Q_RANK), lambda bi, i: (bi, i, P_CQ // MLA_Q_RANK)),
                  pl.BlockSpec((1, TM, LANES), lambda bi, i: (bi, i, P_CKV // LANES)),
                  pl.BlockSpec((1, TM, LANES), lambda bi, i: (bi, i, P_KR // LANES)),
                  full((1, MLA_Q_RANK)), full((1, MLA_KV_RANK)),
                  full(wq.shape), full(wk.shape), full(wv.shape), full(vone.shape), tab, tab, tab],
        out_specs=[pl.BlockSpec((1, TM, MLA_HEADS * LANES), lambda bi, i: (bi, i, 0))] * 3,
        out_shape=[jax.ShapeDtypeStruct((b, s, MLA_HEADS * LANES), BF16)] * 3,
        compiler_params=_cparams(("parallel", "parallel")),
        name="mla_prep",
    )(p, p, p, norm_q.reshape(1, -1), norm_kv.reshape(1, -1), wq, wk, wv, vone, cos, sa, sb)


MLA_Q_TILES = 1


def _mla_attn_kernel(*refs, n_ctx, n_lat_steps):
    q_refs = refs[:MLA_Q_TILES]
    k_ref, v_ref, o_ref = refs[MLA_Q_TILES:]

    def attend(n_keys):
        outs = []
        for hd in range(2):
            sl = slice(hd * LANES, (hd + 1) * LANES)
            q = jnp.concatenate([q_ref[0, :, sl] for q_ref in q_refs], axis=0)
            s = lax.dot_general(q, k_ref[0, :n_keys, sl], (((1,), (1,)), ((), ())),
                                preferred_element_type=F32)
            e = jnp.exp2(s - jnp.max(s, axis=-1, keepdims=True)).astype(BF16)
            o = jnp.dot(e, v_ref[0, :n_keys, sl], preferred_element_type=F32)
            one = MLA_ONE_LANE[hd]
            outs.append(o / o[:, one:one + 1])
        lane = lax.broadcasted_iota(jnp.int32, outs[0].shape, 1)
        o_ref[0] = jnp.where(lane < MLA_V, outs[0], outs[1]).astype(BF16)

    @pl.when(pl.program_id(2) < n_lat_steps)
    def _():
        attend(k_ref.shape[1])

    @pl.when(pl.program_id(2) >= n_lat_steps)
    def _():
        attend(n_ctx)


def _mla_attention(q, k, v, *, n_ctx, with_ctx):
    b, s, _ = q.shape
    nct = n_ctx // TM
    n_lat = s - n_ctx
    n_lat_steps = n_lat // (MLA_Q_TILES * TM)
    assert n_lat % (MLA_Q_TILES * TM) == 0 and (nct == 1 or not with_ctx)

    def q_spec(t):
        return pl.BlockSpec((1, TM, 2 * LANES), lambda bi, hp, j: (
            bi, jnp.where(j < n_lat_steps, nct + j * MLA_Q_TILES + t, 0), hp))

    kv = pl.BlockSpec((1, s, 2 * LANES), lambda bi, hp, j: (bi, 0, hp))
    return pl.pallas_call(
        functools.partial(_mla_attn_kernel, n_ctx=n_ctx, n_lat_steps=n_lat_steps),
        grid=(b, MLA_HEADS // 2, n_lat_steps + (1 if with_ctx else 0)),
        in_specs=[*[q_spec(t) for t in range(MLA_Q_TILES)], kv, kv],
        out_specs=pl.BlockSpec((1, MLA_Q_TILES * TM, LANES), lambda bi, hp, j: (bi, j, hp)),
        out_shape=jax.ShapeDtypeStruct((b, n_lat + (n_ctx if with_ctx else 0), MLA_HEADS * MLA_V), BF16),
        compiler_params=_cparams(("parallel", "parallel", "parallel")),
        name="mla_attention",
    )(*([q] * MLA_Q_TILES), k, v)


SWA_SCALE = SWA_HEAD ** -0.5 * LOG2E
SWA_TQ = 128
SWA_ONE_LANE = (SWA_HEAD, 0)


def _swa_prep_kernel(q_ref, k_ref, v_ref, cos_ref, sa_ref, sb_ref, qo_ref, ko_ref, vo_ref, *, h):
    cos, sa, sb = cos_ref[...], sa_ref[...], sb_ref[...]
    lane = lax.broadcasted_iota(jnp.int32, cos.shape, 1)
    low = lane < SWA_HEAD
    for j in range(SWA_HEADS // 2):
        blk = _rope128(q_ref[0, :, j * LANES:(j + 1) * LANES].astype(F32), cos, sa, sb, h) * SWA_SCALE
        qo_ref[0, :, (2 * j) * LANES:(2 * j + 1) * LANES] = jnp.where(low, blk, 0.0).astype(BF16)
        qo_ref[0, :, (2 * j + 1) * LANES:(2 * j + 2) * LANES] = jnp.where(
            low, pltpu.roll(blk, SWA_HEAD, axis=1), 0.0).astype(BF16)
    kb = _rope128(k_ref[0].astype(F32), cos, sa, sb, h)
    ko_ref[0, :, :LANES] = jnp.where(low, kb, 0.0).astype(BF16)
    ko_ref[0, :, LANES:] = jnp.where(low, pltpu.roll(kb, SWA_HEAD, axis=1), 0.0).astype(BF16)
    vb = v_ref[0].astype(F32)
    vr = pltpu.roll(vb, SWA_HEAD, axis=1)
    one_lo = jnp.where(lane == SWA_ONE_LANE[0], 1.0, 0.0)
    one_hi = jnp.where(lane == SWA_ONE_LANE[1], 1.0, 0.0)
    vo_ref[0, :, 0 * LANES:1 * LANES] = jnp.where(low, vb, one_lo).astype(BF16)
    vo_ref[0, :, 1 * LANES:2 * LANES] = jnp.where(low, one_hi, vr).astype(BF16)
    vo_ref[0, :, 2 * LANES:3 * LANES] = jnp.where(low, vr, one_lo).astype(BF16)
    vo_ref[0, :, 3 * LANES:4 * LANES] = jnp.where(low, one_hi, vb).astype(BF16)


def _swa_prep(p, tabs):
    b, s, _ = p.shape
    cos, sa, sb, h = tabs
    tab = pl.BlockSpec((TM, LANES), lambda bi, i: (i, 0))
    nq = SWA_HEADS * SWA_HEAD
    return pl.pallas_call(
        functools.partial(_swa_prep_kernel, h=h),
        grid=(b, s // TM),
        in_specs=[pl.BlockSpec((1, TM, nq), lambda bi, i: (bi, i, P_SWAQ // nq)),
                  pl.BlockSpec((1, TM, LANES), lambda bi, i: (bi, i, P_SWAK // LANES)),
                  pl.BlockSpec((1, TM, LANES), lambda bi, i: (bi, i, P_SWAV // LANES)),
                  tab, tab, tab],
        out_specs=[pl.BlockSpec((1, TM, SWA_HEADS * LANES), lambda bi, i: (bi, i, 0)),
                   pl.BlockSpec((1, TM, SWA_KV_HEADS * LANES), lambda bi, i: (bi, i, 0)),
                   pl.BlockSpec((1, TM, 4 * LANES), lambda bi, i: (bi, i, 0))],
        out_shape=[jax.ShapeDtypeStruct((b, s, SWA_HEADS * LANES), BF16),
                   jax.ShapeDtypeStruct((b, s, SWA_KV_HEADS * LANES), BF16),
                   jax.ShapeDtypeStruct((b, s, 4 * LANES), BF16)],
        compiler_params=_cparams(("parallel", "parallel")),
        name="swa_prep",
    )(p, p, p, cos, sa, sb)


def _swa_attn_kernel(sink_ref, q_ref, k_ref, v_ref, o_ref, *, n_ctx, q_off):
    i = pl.program_id(1) + q_off
    s_len = k_ref.shape[1]
    tq = SWA_TQ
    n_loc = tq + 2 * WINDOW
    r0 = i * tq
    is_lat = r0 >= n_ctx
    start = pl.multiple_of(jnp.clip(r0 - WINDOW, 0, s_len - n_loc), LANES)
    rows_g = SWA_GROUP * tq
    row = lax.broadcasted_iota(jnp.int32, (rows_g, n_loc), 0)
    qpos = r0 - n_ctx + row % tq
    kpos = start - n_ctx + lax.broadcasted_iota(jnp.int32, (rows_g, n_loc), 1)
    loc_ok = (jnp.abs(kpos - qpos) <= WINDOW) & (kpos >= 0) & is_lat
    k_loc = k_ref[0, pl.ds(start, n_loc), :]
    v_loc = v_ref[0, pl.ds(start, n_loc), :]
    k_ctx = k_ref[0, 0:n_ctx, :]
    v_ctx = v_ref[0, 0:n_ctx, :]
    head_row = lax.broadcasted_iota(jnp.int32, (rows_g, 1), 0) // tq
    lane = lax.broadcasted_iota(jnp.int32, (tq, LANES), 1)
    stages = []
    for g in range(SWA_KV_HEADS):
        q = jnp.concatenate([q_ref[0, :, hd * LANES:(hd + 1) * LANES]
                             for hd in range(g * SWA_GROUP, (g + 1) * SWA_GROUP)], axis=0)
        kg = slice(g * LANES, (g + 1) * LANES)
        s_loc = lax.dot_general(q, k_loc[:, kg], (((1,), (1,)), ((), ())), preferred_element_type=F32)
        s_ctx = lax.dot_general(q, k_ctx[:, kg], (((1,), (1,)), ((), ())), preferred_element_type=F32)
        sink = jnp.zeros((rows_g, 1), F32)
        for hh in range(SWA_GROUP):
            sink = jnp.where(head_row == hh, sink_ref[g * SWA_GROUP + hh] * LOG2E, sink)
        stages.append((jnp.where(loc_ok, s_loc, NEG_INF), s_ctx, sink))
    for g, (s_loc, s_ctx, sink) in enumerate(stages):
        m = jnp.maximum(jnp.maximum(jnp.max(s_loc, axis=-1, keepdims=True),
                                    jnp.max(s_ctx, axis=-1, keepdims=True)), sink)
        e = jnp.concatenate([jnp.exp2(s_loc - m), jnp.exp2(s_ctx - m)], axis=1).astype(BF16)
        e_sink = jnp.exp2(sink - m)
        outs = []
        for par in range(2):
            vg = slice((2 * g + par) * LANES, (2 * g + par + 1) * LANES)
            o = jnp.dot(e, jnp.concatenate([v_loc[:, vg], v_ctx[:, vg]], axis=0), preferred_element_type=F32)
            one = SWA_ONE_LANE[par]
            outs.append(o / (o[:, one:one + 1] + e_sink))
        for pi in range(SWA_GROUP // 2):
            even = outs[0][(2 * pi) * tq:(2 * pi + 1) * tq]
            odd = outs[1][(2 * pi + 1) * tq:(2 * pi + 2) * tq]
            blk = g * (SWA_GROUP // 2) + pi
            o_ref[0, :, blk * LANES:(blk + 1) * LANES] = jnp.where(lane < SWA_HEAD, even, odd).astype(BF16)


def _swa_attention(q, k, v, sink, *, n_ctx, q_off):
    b, s, _ = q.shape
    nq = s // SWA_TQ - q_off
    return pl.pallas_call(
        functools.partial(_swa_attn_kernel, n_ctx=n_ctx, q_off=q_off),
        grid=(b, nq),
        in_specs=[pl.BlockSpec(memory_space=pltpu.SMEM),
                  pl.BlockSpec((1, SWA_TQ, SWA_HEADS * LANES), lambda bi, i: (bi, i + q_off, 0)),
                  pl.BlockSpec((1, s, SWA_KV_HEADS * LANES), lambda bi, i: (bi, 0, 0)),
                  pl.BlockSpec((1, s, 4 * LANES), lambda bi, i: (bi, 0, 0))],
        out_specs=pl.BlockSpec((1, SWA_TQ, SWA_HEADS * SWA_HEAD), lambda bi, i: (bi, i, 0)),
        out_shape=jax.ShapeDtypeStruct((b, nq * SWA_TQ, SWA_HEADS * SWA_HEAD), BF16),
        compiler_params=_cparams(("parallel", "parallel")),
        name="swa_attention",
    )(sink, q, k, v)


N_PAIR = RWKV_HEADS // 2
RWKV_NB = 4
N_DOUBLINGS = int(math.log2(CHUNK))


def _softplus(x):
    return jnp.maximum(x, 0.0) + jnp.log(1.0 + jnp.exp(-jnp.abs(x)))


def _headsum(x, bd):
    hi = x.astype(BF16)
    lo = (x - hi.astype(F32)).astype(BF16)
    return (jnp.dot(hi, bd, preferred_element_type=F32) + jnp.dot(lo, bd, preferred_element_type=F32))


def _chunk_cumsum(x, reverse):
    rows = lax.broadcasted_iota(jnp.int32, x.shape, 0)
    s = 1
    while s < CHUNK:
        if reverse:
            x = x + jnp.where(rows < CHUNK - s, pltpu.roll(x, CHUNK - s, axis=0), 0.0)
        else:
            x = x + jnp.where(rows >= s, pltpu.roll(x, s, axis=0), 0.0)
        s *= 2
    return x


def _head_rows(x):
    first = lax.broadcasted_iota(jnp.int32, x.shape, 1) < RWKV_HEAD
    return jnp.concatenate([jnp.where(first, x, 0.0), jnp.where(first, 0.0, x)], axis=0)


def _mm_x3(a, b):
    a_hi = a.astype(BF16)
    b_hi = b.astype(BF16)
    a_lo = (a - a_hi.astype(F32)).astype(BF16)
    b_lo = (b - b_hi.astype(F32)).astype(BF16)
    dot = functools.partial(jnp.dot, preferred_element_type=F32)
    return dot(a_hi, b_hi) + dot(a_hi, b_lo) + dot(a_lo, b_hi)


def _rwkv_chunk_kernel(rf_ref, kf_ref, vf_ref, lof_ref, rb_ref, kb_ref, vb_ref, lob_ref,
                       kvec_ref, w0_ref, a0_ref, wup_ref, aup_ref, gup_ref, rk_ref, bd_ref,
                       yf_ref, yb_ref, bonus_ref, gate_ref, s_ref):
    @pl.when(pl.program_id(1) == 0)
    def _():
        s_ref[...] = jnp.zeros_like(s_ref)

    bd = bd_ref[...]
    n_seq = rf_ref.shape[0]
    data = []
    for nb in range(n_seq):
        per_dir = []
        for refs in ((rf_ref, kf_ref, vf_ref, lof_ref), (rb_ref, kb_ref, vb_ref, lob_ref)):
            r, k, v, lora = (ref[nb].astype(F32) for ref in refs)
            kk = k * kvec_ref[0:1]
            kk = kk * lax.rsqrt(_headsum(kk * kk, bd) + 1e-12)
            per_dir.append((r, k, v, lora, kk))
        data.append(per_dir)
        r, k, v, lora, _ = per_dir[0]
        gate_ref[nb] = _mm(_sigmoid(lora), gup_ref[...])
        k_both = sum(k * (1.0 + (_sigmoid(a0_ref[d] + _mm(lora, aup_ref[d])) - 1.0) * kvec_ref[1:2])
                     for d in range(2))
        bonus_ref[nb] = _headsum(r * (0.5 * k_both) * rk_ref[...], bd) * v

    trow = lax.broadcasted_iota(jnp.int32, (2 * CHUNK, 4 * CHUNK), 0) % CHUNK
    tcol = lax.broadcasted_iota(jnp.int32, (2 * CHUNK, 4 * CHUNK), 1) % CHUNK
    sq_r = lax.broadcasted_iota(jnp.int32, (LANES, LANES), 0)
    sq_c = lax.broadcasted_iota(jnp.int32, (LANES, LANES), 1)
    same_head = (sq_r // RWKV_HEAD) == (sq_c // RWKV_HEAD)
    eye = sq_r == sq_c

    chains = []
    for nb, d in ((nb, d) for nb in range(n_seq) for d in range(2)):
        reverse = d == 1
        r, k, v, lora, kk = data[nb][d]
        w_log = -_softplus(-(w0_ref[d] + _mm(jnp.tanh(lora), wup_ref[d]))) - 0.5
        ld = -jnp.exp(w_log)
        a = _sigmoid(a0_ref[d] + _mm(lora, aup_ref[d]))
        k_d = k * (1.0 + (a - 1.0) * kvec_ref[1:2])
        b_d = kk * a
        lg = _chunk_cumsum(ld, reverse)
        last = 0 if reverse else CHUNK - 1
        tot = lg[last:last + 1]
        e_neg = jnp.exp(-lg)
        e_end = jnp.exp(tot - lg)
        z_t = -kk * jnp.exp(lg - ld)
        r_t = r * jnp.exp(lg)
        b_t = b_d * e_neg
        k_t = k_d * e_neg
        b_e = b_d * e_end
        k_e = k_d * e_end
        e_tot = jnp.exp(tot)
        before = (tcol > trow) if reverse else (tcol < trow)
        before_eq = (tcol >= trow) if reverse else (tcol <= trow)
        for pr in range(N_PAIR):
            sl = slice(pr * LANES, (pr + 1) * LANES)
            ch = {"nb": nb, "d": d, "sl": sl, "rp": r_t[:, sl], "vp": v[:, sl], "e_tot": e_tot[:, sl],
                  "be_ke": jnp.concatenate([b_e[:, sl], k_e[:, sl]], axis=0)}
            zst, rst, vst = _head_rows(z_t[:, sl]), _head_rows(r_t[:, sl]), _head_rows(v[:, sl])
            bkst = jnp.concatenate([_head_rows(b_t[:, sl]), _head_rows(k_t[:, sl])], axis=0)
            ch["lz"] = jnp.where(before, _mm_nt(zst, bkst), 0.0)
            ch["lr"] = jnp.where(before_eq, _mm_nt(rst, bkst), 0.0)
            ch["zst"], ch["vst"] = zst, vst
            chains.append(ch)

    unit = jnp.where(eye, 1.0, 0.0)
    for ch in chains:
        ch["pw"] = ch["lz"][:, :LANES]
        ch["t"] = unit + ch["pw"]
        ch["x"] = jnp.concatenate([ch["zst"], _mm(ch["lz"][:, LANES:], ch["vst"])], axis=1)
    for it in range(1, N_DOUBLINGS):
        for ch in chains:
            ch["pw"] = _mm(ch["pw"], ch["pw"])
        for ch in chains:
            ch["t"] = ch["t"] + _mm(ch["pw"], ch["t"])
    for ch in chains:
        ch["x"] = _mm(ch["t"], ch["x"])
    for ch in chains:
        low = jnp.concatenate([jnp.zeros_like(ch["vst"]), ch["vst"]], axis=1)
        op = _mm(ch["lr"], jnp.concatenate([ch["x"], low], axis=0))
        ch["op"] = op[:CHUNK] + op[CHUNK:]
        ch["xp"] = ch["x"][:CHUNK] + ch["x"][CHUNK:]
    for ch in chains:
        rhs = jnp.concatenate([ch["xp"], jnp.concatenate([jnp.zeros_like(ch["vp"]), ch["vp"]], axis=1)], axis=0)
        ag = _mm_tn(ch["be_ke"], rhs)
        a_full = ag[:, :LANES] + jnp.where(eye, jnp.broadcast_to(ch["e_tot"], (LANES, LANES)), 0.0)
        ch["a"] = jnp.where(same_head, a_full, 0.0)
        ch["g"] = jnp.where(same_head, ag[:, LANES:], 0.0)
    for idx, ch in enumerate(chains):
        st = s_ref[idx]
        y_ref = yf_ref if ch["d"] == 0 else yb_ref
        y_ref[ch["nb"], :, ch["sl"]] = _mm(ch["rp"] + ch["op"][:, :LANES], st) + ch["op"][:, LANES:]
        s_ref[idx] = _mm_x3(ch["a"], st) + ch["g"]


def _head_block_diag():
    idx = np.arange(RWKV_DIM) // RWKV_HEAD
    return jnp.asarray(idx[:, None] == idx[None, :], BF16)


def _rwkv_chunks(p, kvec, w0, a0, w_up, a_up, g_up, r_k, *, n_ctx):
    b, s, _ = p.shape
    nc = s // CHUNK
    ncc = n_ctx // CHUNK
    lora_w = DECAY_LORA + AAA_LORA + GATE_LORA
    wup = jnp.zeros((2, lora_w, RWKV_DIM), F32).at[:, :DECAY_LORA].set(w_up).astype(BF16)
    aup = jnp.zeros((2, lora_w, RWKV_DIM), F32).at[:, DECAY_LORA:DECAY_LORA + AAA_LORA].set(a_up).astype(BF16)
    gup = jnp.zeros((lora_w, RWKV_DIM), F32).at[DECAY_LORA + AAA_LORA:].set(g_up).astype(BF16)

    def fwd(c):
        return c

    def back(c):
        return jnp.where(c < ncc, ncc - 1 - c, nc - 1 - (c - ncc))

    full = lambda shape: pl.BlockSpec(shape, lambda bi, c: (0,) * len(shape))

    n_seq = _sub_tiles(b, RWKV_NB)

    def inputs(chunk):
        col = lambda off: pl.BlockSpec((n_seq, CHUNK, RWKV_DIM), lambda bi, c: (bi, chunk(c), off // RWKV_DIM))
        return [col(P_RKV), col(P_RKV + RWKV_DIM), col(P_RKV + 2 * RWKV_DIM),
                pl.BlockSpec((n_seq, CHUNK, lora_w), lambda bi, c: (bi, chunk(c), P_LORA // lora_w))]

    tok = lambda chunk: pl.BlockSpec((n_seq, CHUNK, RWKV_DIM), lambda bi, c: (bi, chunk(c), 0))
    return pl.pallas_call(
        _rwkv_chunk_kernel,
        grid=(b // n_seq, nc),
        in_specs=[*inputs(fwd), *inputs(back),
                  full((2, RWKV_DIM)), full((2, 1, RWKV_DIM)), full((2, 1, RWKV_DIM)),
                  full(wup.shape), full(aup.shape), full(gup.shape), full((1, RWKV_DIM)),
                  full((RWKV_DIM, RWKV_DIM))],
        out_specs=[tok(fwd), tok(back), tok(fwd), tok(fwd)],
        out_shape=[jax.ShapeDtypeStruct((b, s, RWKV_DIM), F32)] * 4,
        scratch_shapes=[pltpu.VMEM((n_seq * 2 * N_PAIR, LANES, LANES), F32)],
        compiler_params=_cparams(("parallel", "arbitrary")),
        name="rwkv_chunks",
    )(*([p] * 8), kvec, w0.reshape(2, 1, -1), a0.reshape(2, 1, -1), wup, aup, gup, r_k.reshape(1, -1),
      _head_block_diag())


DFT_N2 = LANES
HY_MIN_LEN = 1024
HY_CT = 128


def _hyena_mlp_kernel(feats_ref, w1_ref, b1_ref, w2_ref, b2_ref, freq_ref, h_ref):
    h = jnp.sin(freq_ref[0:1] * (_mm_f32(feats_ref[0], w1_ref[...]) + b1_ref[...]))
    h_ref[0] = jnp.sin(freq_ref[1:2] * (_mm_f32(h, w2_ref[...]) + b2_ref[...]))


def _hyena_filter_kernel(h_ref, w3f_ref, w3b_ref, t_ref, delta_ref, k_ref):
    hf = _mm_f32(h_ref[0], w3f_ref[...]) * jnp.exp(-t_ref[0] * delta_ref[...])
    hb = _mm_f32(h_ref[1], w3b_ref[...]) * jnp.exp(-t_ref[1] * delta_ref[...])
    norm = (jnp.sum(jnp.abs(hf), axis=0, keepdims=True) + jnp.sum(jnp.abs(hb), axis=0, keepdims=True))
    r = pl.program_id(2)
    rows = lax.broadcasted_iota(jnp.int32, hf.shape, 0)
    tail = jnp.where(rows == 0, 0.0, hb)
    blk = jnp.where(r == 0, hf, jnp.where(r == pl.num_programs(2) - 1, tail, 0.0))
    k_ref[0] = blk / norm


def _hyena_filter_buffer(n, nc, w1, b1, w2, b2, w3, freq):
    lag = jnp.stack([jnp.arange(n), jnp.where(jnp.arange(n) == 0, 0, n - jnp.arange(n))]).astype(F32)
    t = (lag / (n - 1))[:, :, None]
    bands = jnp.linspace(1e-4, HYENA_BANDS - 1, HYENA_BANDS, dtype=F32)
    ang = (2.0 * math.pi / n) * lag[:, :, None] * bands[None, None, :]
    feats = jnp.concatenate([t, jnp.cos(ang), -jnp.sin(ang),
                             jnp.zeros((2, n, HYENA_FW - HYENA_EMB), F32)], axis=-1)
    w1p = jnp.zeros((HYENA_FW, HYENA_FW), F32).at[:HYENA_EMB].set(w1)
    deltas = jnp.abs(jnp.linspace(math.log(HYENA_TARGET) / HYENA_SLOW,
                                  math.log(HYENA_TARGET) / HYENA_FAST, HYENA_DIM, dtype=F32))[None, :]
    fixed = lambda shape: pl.BlockSpec(shape, lambda d: (0,) * len(shape))
    hidden = pl.pallas_call(
        _hyena_mlp_kernel,
        grid=(2,),
        in_specs=[pl.BlockSpec((1, n, HYENA_FW), lambda d: (d, 0, 0)), fixed((HYENA_FW, HYENA_FW)),
                  fixed((1, HYENA_FW)), fixed((HYENA_FW, HYENA_FW)), fixed((1, HYENA_FW)),
                  fixed((2, HYENA_FW))],
        out_specs=pl.BlockSpec((1, n, HYENA_FW), lambda d: (d, 0, 0)),
        out_shape=jax.ShapeDtypeStruct((2, n, HYENA_FW), F32),
        compiler_params=_cparams(("parallel",)),
        name="hyena_filter_mlp",
    )(feats, w1p, b1.reshape(1, -1), w2, b2.reshape(1, -1), freq)
    tc = 256
    nj = HYENA_DIM // tc
    full = lambda shape: pl.BlockSpec(shape, lambda o, j, r: (0,) * len(shape))
    return pl.pallas_call(
        _hyena_filter_kernel,
        grid=(HYENA_ORDER, nj, nc // n),
        in_specs=[full((2, n, HYENA_FW)),
                  pl.BlockSpec((HYENA_FW, tc), lambda o, j, r: (0, o * 2 * nj + j)),
                  pl.BlockSpec((HYENA_FW, tc), lambda o, j, r: (0, o * 2 * nj + nj + j)),
                  full((2, n, 1)), pl.BlockSpec((1, tc), lambda o, j, r: (0, j))],
        out_specs=pl.BlockSpec((1, n, tc), lambda o, j, r: (o, r, j)),
        out_shape=jax.ShapeDtypeStruct((HYENA_ORDER, nc, HYENA_DIM), F32),
        compiler_params=_cparams(("parallel", "parallel", "parallel")),
        name="hyena_filters",
    )(hidden, w3, w3, t, deltas)


HY_J = 8
HY_GROUPS = DFT_N2 // HY_J


def _dft_tables(n1):
    nc = n1 * DFT_N2
    f1 = np.arange(n1)
    ang = 2.0 * np.pi * ((f1[:, None] * f1[None, :]) % n1) / n1
    eye = np.eye(HY_J)
    w1 = np.kron(np.concatenate([np.cos(ang), -np.sin(ang)], axis=0), eye)
    v3 = np.kron(np.concatenate([np.cos(ang), -np.sin(ang)], axis=1), eye) / nc
    t2 = np.arange(DFT_N2).reshape(HY_GROUPS, 1, HY_J)
    tw = 2.0 * np.pi * ((f1[None, :, None] * t2) % nc) / nc
    tw = np.broadcast_to(tw.reshape(HY_GROUPS, n1 * HY_J, 1), (HY_GROUPS, n1 * HY_J, LANES))
    k = np.arange(DFT_N2)
    phi = 2.0 * np.pi * ((k[:, None] * k[None, :]) % DFT_N2) / DFT_N2
    c, s = np.cos(phi), np.sin(phi)
    f_fwd = np.block([[c, s], [-s, c]])
    f_inv = np.block([[c, -s], [s, c]])
    names = ("w1", "v3", "ctw", "stw", "f_fwd", "f_inv")
    return {n: jnp.asarray(t, BF16) for n, t in zip(names, (w1, v3, np.cos(tw), np.sin(tw), f_fwd, f_inv))}


def _tiles(ref, rows, g, lead=()):
    return jnp.concatenate([ref[lead + (pl.ds(r * DFT_N2 + g * HY_J, HY_J), slice(None))] for r in rows], axis=0)


def _dft_rows_in(x_ref, w_ref, ctw_ref, stw_ref, a_ref, t1n, n1, t1_valid):
    half = n1 * HY_J
    for g in range(HY_GROUPS):
        xg = _tiles(x_ref, range(t1_valid), g, lead=(0,))
        pq = _mm(w_ref[:, :t1_valid * HY_J], xg)
        p, q = pq[:half], pq[half:]
        c, s = ctw_ref[g].astype(F32), stw_ref[g].astype(F32)
        re = c * p + s * q
        im = c * q - s * p
        for m in range(n1):
            rows = slice(m * HY_J, (m + 1) * HY_J)
            a_ref[pl.ds(m * DFT_N2 + g * HY_J, HY_J), :] = re[rows]
            a_ref[pl.ds((n1 + m) * DFT_N2 + g * HY_J, HY_J), :] = im[rows]


def _slab(f1):
    return slice(f1 * DFT_N2, (f1 + 1) * DFT_N2)


def _spectrum_kernel(x_ref, w_ref, ctw_ref, stw_ref, ff_ref, k_ref, a_ref, *, n1):
    _dft_rows_in(x_ref, w_ref, ctw_ref, stw_ref, a_ref, n1, n1, n1)
    for f1 in range(n1):
        a = jnp.concatenate([a_ref[_slab(f1)], a_ref[_slab(n1 + f1)]], axis=0)
        k_ref[0, f1] = _mm(ff_ref[...], a)


def _filter_spectrum(kbuf, tabs, n1):
    no, nc, _ = kbuf.shape
    nj = HYENA_DIM // HY_CT
    const = lambda a: pl.BlockSpec(a.shape, lambda o, j: (0,) * a.ndim)
    consts = [tabs[n] for n in ("w1", "ctw", "stw", "f_fwd")]
    return pl.pallas_call(
        functools.partial(_spectrum_kernel, n1=n1),
        grid=(no, nj),
        in_specs=[pl.BlockSpec((1, nc, HY_CT), lambda o, j: (o, 0, j)), *[const(a) for a in consts]],
        out_specs=pl.BlockSpec((1, n1, 2 * DFT_N2, HY_CT), lambda o, j: (o, 0, 0, j)),
        out_shape=jax.ShapeDtypeStruct((no, n1, 2 * DFT_N2, HYENA_DIM), F32),
        scratch_shapes=[pltpu.VMEM((2 * n1 * DFT_N2, HY_CT), F32)],
        compiler_params=_cparams(("parallel", "parallel")),
        name="hyena_filter_spectrum",
    )(kbuf, *consts)


def _hyena_conv_kernel(x_ref, gate_ref, k_ref, w_ref, ctw_ref, stw_ref, ff_ref, fi_ref, v_ref, bias_ref,
                       o_ref, a_ref, *, t1n, n1, t1_valid):
    _dft_rows_in(x_ref, w_ref, ctw_ref, stw_ref, a_ref, t1n, n1, t1_valid)
    for f1 in range(n1):
        re, im = _slab(f1), _slab(n1 + f1)
        x = _mm(ff_ref[...], jnp.concatenate([a_ref[re], a_ref[im]], axis=0))
        xre, xim = x[:DFT_N2], x[DFT_N2:]
        kre, kim = k_ref[0, f1, :DFT_N2], k_ref[0, f1, DFT_N2:]
        bm = _mm(fi_ref[...], jnp.concatenate([xre * kre - xim * kim, xre * kim + xim * kre], axis=0))
        a_ref[re] = bm[:DFT_N2]
        a_ref[im] = bm[DFT_N2:]
    bias = bias_ref[...]
    for g in range(HY_GROUPS):
        br, bi = _tiles(a_ref, range(n1), g), _tiles(a_ref, range(n1, 2 * n1), g)
        c, s = ctw_ref[g].astype(F32), stw_ref[g].astype(F32)
        y = _mm(v_ref[:t1n * HY_J], jnp.concatenate([c * br - s * bi, s * br + c * bi], axis=0))
        for t1 in range(t1n):
            rows = pl.ds(t1 * DFT_N2 + g * HY_J, HY_J)
            o_ref[0, rows, :] = gate_ref[0, rows, :] * (y[t1 * HY_J:(t1 + 1) * HY_J] + bias * x_ref[0, rows, :])


def _hyena_conv(z, z_spec, gate, gate_spec, kspec, order, bias, tabs, n1, n_pad, n):
    bx = z.shape[0]
    t1n = n_pad // DFT_N2
    consts = [tabs[name] for name in ("w1", "ctw", "stw", "f_fwd", "f_inv", "v3")]
    const = lambda a: pl.BlockSpec(a.shape, lambda j, bi: (0,) * a.ndim)
    return pl.pallas_call(
        functools.partial(_hyena_conv_kernel, t1n=t1n, n1=n1, t1_valid=-(-n // DFT_N2)),
        grid=(HYENA_DIM // HY_CT, bx),
        in_specs=[z_spec, gate_spec,
                  pl.BlockSpec((1, n1, 2 * DFT_N2, HY_CT), lambda j, bi: (order, 0, 0, j)),
                  *[const(a) for a in consts],
                  pl.BlockSpec((1, HY_CT), lambda j, bi: (0, j))],
        out_specs=pl.BlockSpec((1, n_pad, HY_CT), lambda j, bi: (bi, 0, j)),
        out_shape=jax.ShapeDtypeStruct((bx, n_pad, HYENA_DIM), F32),
        scratch_shapes=[pltpu.VMEM((2 * n1 * DFT_N2, HY_CT), F32)],
        compiler_params=_cparams(("parallel", "parallel")),
        name="hyena_conv",
    )(z, gate, kspec, *consts, bias.reshape(1, -1))


def _hyena_operator(p_hy, row0, n, filt_params, bias):
    n_pad = max(n, HY_MIN_LEN)
    nc = 2 * n_pad
    n1 = nc // DFT_N2
    tabs = _dft_tables(n1)
    kspec = _filter_spectrum(_hyena_filter_buffer(n, nc, *filt_params), tabs, n1)
    nj = HYENA_DIM // HY_CT
    window = lambda part: pl.BlockSpec((pl.Element(1), pl.Element(n_pad), pl.Element(HY_CT)),
                                       lambda j, bi: (bi, row0, (part * nj + j) * HY_CT))
    own = pl.BlockSpec((1, n_pad, HY_CT), lambda j, bi: (bi, 0, j))
    z = _hyena_conv(p_hy, window(0), p_hy, window(1), kspec, 0, bias[0], tabs, n1, n_pad, n)
    return _hyena_conv(z, own, p_hy, window(2), kspec, 1, bias[1], tabs, n1, n_pad, n)


def _rwkv_readout(y, bonus, gate, ln_g, ln_b, bd):
    inv_n = 1.0 / RWKV_HEAD
    dev = y - _headsum(y, bd) * inv_n
    var = _headsum(dev * dev, bd) * inv_n
    return (dev * lax.rsqrt(var + RWKV_LN_EPS) * ln_g + ln_b + bonus) * gate


def _merge_kernel(x_ref, mod_ref, g_ref, gates_ref, bg_ref, ya_ref, yf_ref, yr_ref, bonus_ref, rgate_ref,
                  lng_ref, lnb_ref, bd_ref, yh_ref, yhc_ref, yd_ref, wb_ref, wo_ref, o_ref, *, n_ctx_tiles, row_off):
    yb = _rwkv_readout(yf_ref[0] + yr_ref[0], bonus_ref[0], rgate_ref[0], lng_ref[...], lnb_ref[...], bd_ref[...])
    is_ctx = pl.program_id(1) + row_off < n_ctx_tiles
    yh = jnp.where(is_ctx, yhc_ref[0], yh_ref[0])
    merged = None
    for br, y in enumerate((ya_ref[0], yb, yh, yd_ref[0])):
        gate = _sigmoid(gates_ref[0, :, br * D_MODEL:(br + 1) * D_MODEL] + bg_ref[br:br + 1])
        term = gate * jnp.dot(y.astype(BF16), wb_ref[br], preferred_element_type=F32)
        merged = term if merged is None else merged + term
    y = jnp.dot(merged.astype(BF16), wo_ref[...], preferred_element_type=F32)
    o_ref[0] = x_ref[0] + mod_ref[0, 5:6] * _rms(y, g_ref[3:4])


def _merge(x, mods, norm_g, p, b_gate, ya, rwkv, yh, yh_ctx, yd, w_branch, w_out, *, n_ctx_tiles, row_off):
    b, s, _ = x.shape
    nt = s // TM - row_off
    n_lat = mods.shape[0] - 1
    br = lambda: pl.BlockSpec((1, TM, BRANCH_DIM), lambda bi, i: (bi, i, 0))
    tok = pl.BlockSpec((1, TM, BRANCH_DIM), lambda bi, i: (bi, i + row_off, 0))
    row = pl.BlockSpec((1, RWKV_DIM), lambda bi, i: (0, 0))
    y_fwd, y_bwd, bonus, rgate, ln_g, ln_b = rwkv
    n_lat_tiles = s // TM - n_ctx_tiles
    br_a = pl.BlockSpec((1, TM, BRANCH_DIM), lambda bi, i: (
        bi, jnp.where(i + row_off < n_ctx_tiles, n_lat_tiles + i, i + row_off - n_ctx_tiles), 0))
    br_h = pl.BlockSpec((1, TM, BRANCH_DIM), lambda bi, i: (bi, jnp.maximum(i + row_off - n_ctx_tiles, 0), 0))
    br_hc = pl.BlockSpec((1, TM, BRANCH_DIM), lambda bi, i: (
        bi, jnp.minimum(i + row_off, max(n_ctx_tiles - 1, 0)) if yh_ctx is not None else 0, 0))
    return pl.pallas_call(
        functools.partial(_merge_kernel, n_ctx_tiles=n_ctx_tiles if yh_ctx is not None else 0, row_off=row_off),
        grid=(b, nt),
        in_specs=[pl.BlockSpec((1, TM, D_MODEL), lambda bi, i: (bi, i + row_off, 0)),
                  pl.BlockSpec((1, N_MOD, D_MODEL),
                               lambda bi, i: (jnp.where(i + row_off < n_ctx_tiles, n_lat, bi), 0, 0)),
                  pl.BlockSpec((6, D_MODEL), lambda bi, i: (0, 0)),
                  pl.BlockSpec((1, TM, GATE_COLS), lambda bi, i: (bi, i + row_off, 0)),
                  pl.BlockSpec((N_BRANCH, D_MODEL), lambda bi, i: (0, 0)),
                  br_a, tok, tok, tok, tok, row, row,
                  pl.BlockSpec((RWKV_DIM, RWKV_DIM), lambda bi, i: (0, 0)),
                  br_h, br_hc, br(),
                  pl.BlockSpec((N_BRANCH, BRANCH_DIM, D_MODEL), lambda bi, i: (0, 0, 0)),
                  pl.BlockSpec((D_MODEL, D_MODEL), lambda bi, i: (0, 0))],
        out_specs=pl.BlockSpec((1, TM, D_MODEL), lambda bi, i: (bi, i, 0)),
        out_shape=jax.ShapeDtypeStruct((b, nt * TM, D_MODEL), F32),
        compiler_params=_cparams(("parallel", "parallel")),
        name="merge_branches",
    )(x, mods, norm_g, p, b_gate, ya, y_fwd, y_bwd, bonus, rgate, ln_g.reshape(1, -1), ln_b.reshape(1, -1),
      _head_block_diag(), yh, yh if yh_ctx is None else yh_ctx, yd, w_branch, w_out)


def kernel(x, c, ctx, c_ctx, w_mod, b_mod, norm_g, ffn_w13, ffn_w2, w_in, b_gate, mla_norm_q, mla_norm_kv, mla_w_uq, mla_w_ukv, rwkv_mu, rwkv_w0, rwkv_w_up, rwkv_a0, rwkv_a_up, rwkv_g_up, rwkv_kvec, rwkv_r_k, rwkv_ln_g, rwkv_ln_b, hyena_conv, hyena_conv_b, hyena_w1, hyena_b1, hyena_w2, hyena_b2, hyena_w3, hyena_freq, hyena_bias, swa_sink, w_branch, w_out):
    b, n, _ = x.shape
    n_ctx = ctx.shape[1]
    nct = n_ctx // TM
    xall = jnp.concatenate([ctx, x], axis=1)
    c_all = jnp.concatenate([c, c_ctx[None]], axis=0)
    tabs_mla = _rope_tables(n, n_ctx, MLA_ROPE, MLA_NOPE, LANES)
    tabs_swa = _rope_tables(n, n_ctx, SWA_HEAD, 0, SWA_HEAD)
    depth = w_mod.shape[0]
    for l in range(depth):
        with_ctx = l + 1 < depth
        row_off = 0 if with_ctx else nct
        mods = _modulation(c_all, w_mod[l], b_mod[l])
        xall = _ffn(xall, mods, norm_g[l], ffn_w13[l, 0].astype(BF16), ffn_w2[l, 0].astype(BF16),
                    mod0=0, g0=0, n_ctx_tiles=nct)
        coef = _shift_coefficients(rwkv_mu[l], hyena_conv[l], hyena_conv_b[l])
        p, p_hy = _inproj(xall, mods, norm_g[l], _permute_w_in(w_in[l]).astype(BF16), coef, n_ctx_tiles=nct)
        q, k, v = _mla_prep(p, mla_norm_q[l], mla_norm_kv[l], mla_w_uq[l], mla_w_ukv[l], tabs_mla)
        ya = _mla_attention(q, k, v, n_ctx=n_ctx, with_ctx=with_ctx)
        y_fwd, y_bwd, bonus, gate = _rwkv_chunks(p, rwkv_kvec[l], rwkv_w0[l], rwkv_a0[l], rwkv_w_up[l],
                                                 rwkv_a_up[l], rwkv_g_up[l], rwkv_r_k[l], n_ctx=n_ctx)
        rwkv = (y_fwd, y_bwd, bonus, gate, rwkv_ln_g[l], rwkv_ln_b[l])
        filt = (hyena_w1[l], hyena_b1[l], hyena_w2[l], hyena_b2[l], hyena_w3[l], hyena_freq[l])
        yh = _hyena_operator(p_hy, n_ctx, n, filt, hyena_bias[l])
        yh_ctx = _hyena_operator(p_hy, 0, n_ctx, filt, hyena_bias[l]) if with_ctx else None
        q, k, v = _swa_prep(p, tabs_swa)
        yd = _swa_attention(q, k, v, swa_sink[l], n_ctx=n_ctx, q_off=row_off * TM // SWA_TQ)
        xall = _merge(xall, mods, norm_g[l], p, b_gate[l], ya, rwkv, yh, yh_ctx, yd, w_branch[l].astype(BF16),
                      w_out[l].astype(BF16), n_ctx_tiles=nct, row_off=row_off)
        xall = _ffn(xall, mods, norm_g[l], ffn_w13[l, 1].astype(BF16), ffn_w2[l, 1].astype(BF16),
                    mod0=6, g0=4, n_ctx_tiles=nct - row_off)
    return xall
```

```python
import functools
import math

import numpy as np
import jax
import jax.numpy as jnp
from jax import lax
from jax.experimental import pallas as pl
from jax.experimental.pallas import tpu as pltpu

F32 = jnp.float32
BF16 = jnp.bfloat16

D_MODEL = 1024
GRID_W = 64
N_BRANCH = 4
N_MOD = 9
FF_DIM = 2816
EPS = 1e-6
ROPE_BASE = 10000.0
NEG_INF = -1e30
BRANCH_DIM = 512
MLA_HEADS = 8
MLA_NOPE = 64
MLA_ROPE = 32
MLA_V = 64
MLA_Q_RANK = 256
MLA_KV_RANK = 128
RWKV_HEADS = 8
RWKV_HEAD = 64
RWKV_DIM = RWKV_HEADS * RWKV_HEAD
DECAY_LORA = 64
AAA_LORA = 64
GATE_LORA = 128
RWKV_LN_EPS = 64e-5
HYENA_DIM = 512
HYENA_ORDER = 2
HYENA_EMB = 33
HYENA_BANDS = (HYENA_EMB - 1) // 2
HYENA_FW = 64
HYENA_TARGET = 1e-2
HYENA_FAST = 0.3
HYENA_SLOW = 1.5
SWA_HEADS = 8
SWA_KV_HEADS = 2
SWA_HEAD = 64
SWA_GROUP = SWA_HEADS // SWA_KV_HEADS
WINDOW = 128
GATE_COLS = N_BRANCH * D_MODEL
MLA_COLS = MLA_Q_RANK + MLA_KV_RANK + MLA_ROPE
RWKV_COLS = 3 * RWKV_DIM + DECAY_LORA + AAA_LORA + GATE_LORA
HYENA_COLS = 3 * HYENA_DIM
SWA_COLS = (SWA_HEADS + 2 * SWA_KV_HEADS) * SWA_HEAD

LANES = 128
V7X_VMEM_LIMIT = 56 * 1024 * 1024

TM = 256
FFN_SUB_TILES = 4
INPROJ_SUB_TILES = 2
FF_CHUNK = 256
IN_CHUNK = 512
CHUNK = 64

P_GATE = 0
P_RKV = 4096
P_SWAQ = 5632
P_LORA = 6144
P_CQ = 6400
P_CKV = 6656
P_KR = 6784
P_SWAK = 6912
P_SWAV = 7040
P_COLS = 7168
P_HY = 7168
W_COLS = P_HY + HYENA_COLS
_SHIFT_COLS = ((P_RKV, P_SWAQ), (P_LORA, P_CQ), (P_HY, W_COLS))
_SHIFT_CHUNKS = [any(lo < (j + 1) * IN_CHUNK and j * IN_CHUNK < hi for lo, hi in _SHIFT_COLS)
                 for j in range(W_COLS // IN_CHUNK)]


def _cparams(sem, vmem=V7X_VMEM_LIMIT):
    return pltpu.CompilerParams(dimension_semantics=sem, vmem_limit_bytes=vmem)


def _mm(a, b):
    return jnp.dot(a.astype(BF16), b.astype(BF16), preferred_element_type=F32)


def _mm_nt(a, b):
    return lax.dot_general(a.astype(BF16), b.astype(BF16), (((1,), (1,)), ((), ())),
                           preferred_element_type=F32)


def _mm_tn(a, b):
    return lax.dot_general(a.astype(BF16), b.astype(BF16), (((0,), (0,)), ((), ())),
                           preferred_element_type=F32)


def _mm_f32(a, b):
    return jnp.dot(a, b, preferred_element_type=F32, precision=lax.Precision.HIGHEST)


def _rms(x, g):
    return x * lax.rsqrt(jnp.mean(x * x, axis=-1, keepdims=True) + EPS) * g


def _sigmoid(x):
    return 1.0 / (1.0 + jnp.exp(-x))


def _mod_kernel(c_ref, w_ref, b_ref, o_ref):
    c = c_ref[...]
    o_ref[...] = _mm(c * _sigmoid(c), w_ref[...]) + b_ref[...]


def _modulation(c_all, w_mod, b_mod):
    r = c_all.shape[0]
    rp = -(-r // 8) * 8
    c_pad = jnp.zeros((rp, D_MODEL), F32).at[:r].set(c_all)
    tn = 1024
    out = pl.pallas_call(
        _mod_kernel,
        grid=(N_MOD * D_MODEL // tn,),
        in_specs=[pl.BlockSpec((rp, D_MODEL), lambda j: (0, 0)),
                  pl.BlockSpec((D_MODEL, tn), lambda j: (0, j)),
                  pl.BlockSpec((1, tn), lambda j: (0, j))],
        out_specs=pl.BlockSpec((rp, tn), lambda j: (0, j)),
        out_shape=jax.ShapeDtypeStruct((rp, N_MOD * D_MODEL), F32),
        compiler_params=_cparams(("arbitrary",)),
        name="modulation",
    )(c_pad, w_mod, b_mod.reshape(1, -1))
    return out[:r].reshape(r, N_MOD, D_MODEL)


def _sub_tiles(n_tiles, most):
    return max(g for g in range(1, most + 1) if n_tiles % g == 0)


def _mod_specs(n_sub, tiles_per_seq, n_ctx_tiles, ctx_row):
    def spec(k):
        def index(i):
            t = i * n_sub + k
            return (jnp.where(t % tiles_per_seq < n_ctx_tiles, ctx_row, t // tiles_per_seq), 0, 0)
        return pl.BlockSpec((1, N_MOD, D_MODEL), index)
    return [spec(k) for k in range(n_sub)]


def _ffn_kernel(x_ref, *refs, mod0, g0, n_sub):
    mod_refs = refs[:n_sub]
    g_ref, w13_ref, w2_ref, o_ref = refs[n_sub:]
    tiles = [slice(t * TM, (t + 1) * TM) for t in range(n_sub)]
    u = jnp.concatenate(
        [(_rms(x_ref[rows], g_ref[g0:g0 + 1]) * (1.0 + m[0, mod0 + 1:mod0 + 2]) + m[0, mod0:mod0 + 1]).astype(BF16)
         for rows, m in zip(tiles, mod_refs)], axis=0)
    acc = jnp.zeros(x_ref.shape, F32)
    for f in range(FF_DIM // FF_CHUNK):
        lo = f * FF_CHUNK
        a = jnp.dot(u, w13_ref[:, lo:lo + FF_CHUNK], preferred_element_type=F32)
        b = jnp.dot(u, w13_ref[:, FF_DIM + lo:FF_DIM + lo + FF_CHUNK], preferred_element_type=F32)
        h = (a * _sigmoid(a) * b).astype(BF16)
        acc = acc + jnp.dot(h, w2_ref[lo:lo + FF_CHUNK, :], preferred_element_type=F32)
    hn = _rms(acc, g_ref[g0 + 1:g0 + 2])
    for rows, m in zip(tiles, mod_refs):
        o_ref[rows] = x_ref[rows] + 0.5 * m[0, mod0 + 2:mod0 + 3] * hn[rows]


def _ffn(x, mods, norm_g, w13, w2, *, mod0, g0, n_ctx_tiles):
    b, s, _ = x.shape
    n_tiles = b * s // TM
    n_sub = _sub_tiles(n_tiles, FFN_SUB_TILES)
    rows = n_sub * TM
    out = pl.pallas_call(
        functools.partial(_ffn_kernel, mod0=mod0, g0=g0, n_sub=n_sub),
        grid=(n_tiles // n_sub,),
        in_specs=[pl.BlockSpec((rows, D_MODEL), lambda i: (i, 0)),
                  *_mod_specs(n_sub, s // TM, n_ctx_tiles, mods.shape[0] - 1),
                  pl.BlockSpec((6, D_MODEL), lambda i: (0, 0)),
                  pl.BlockSpec(memory_space=pltpu.VMEM),
                  pl.BlockSpec(memory_space=pltpu.VMEM)],
        out_specs=pl.BlockSpec((rows, D_MODEL), lambda i: (i, 0)),
        out_shape=jax.ShapeDtypeStruct((b * s, D_MODEL), F32),
        compiler_params=_cparams(("parallel",)),
        name="ffn_half_step",
    )(x.reshape(b * s, D_MODEL), *([mods] * n_sub), norm_g, w13, w2)
    return out.reshape(b, s, D_MODEL)


def _inproj_kernel(x_ref, xp_ref, xn_ref, *refs, n_sub, tiles_per_seq, n_ctx_tiles):
    mod_refs = refs[:n_sub]
    g_ref, w_ref, coef_ref, o_ref, hy_ref = refs[n_sub:]
    g = g_ref[2:3]

    def modulated(x, m):
        return _rms(x, g) * (1.0 + m[0, 4:5]) + m[0, 3:4]

    u_halo = jnp.concatenate([modulated(x_ref[t * TM:(t + 1) * TM], m) for t, m in enumerate(mod_refs)]
                             + [modulated(xp_ref[...], mod_refs[0]), modulated(xn_ref[...], mod_refs[-1])],
                             axis=0).astype(BF16)
    u = u_halo[:n_sub * TM]
    rows = lax.broadcasted_iota(jnp.int32, (n_sub * TM, 1), 0)
    keep_prev = jnp.ones((n_sub * TM, 1), F32)
    keep_next = jnp.ones((n_sub * TM, 1), F32)
    for t in range(n_sub):
        w = (pl.program_id(0) * n_sub + t) % tiles_per_seq
        seg_start = (w == 0) | (w == n_ctx_tiles)
        seg_end = (w == n_ctx_tiles - 1) | (w == tiles_per_seq - 1)
        keep_prev = jnp.where((rows == t * TM) & seg_start, 0.0, keep_prev)
        keep_next = jnp.where((rows == (t + 1) * TM - 1) & seg_end, 0.0, keep_next)
    n_rows = n_sub * TM
    for j in range(W_COLS // IN_CHUNK):
        cols = slice(j * IN_CHUNK, (j + 1) * IN_CHUNK)
        if not _SHIFT_CHUNKS[j]:
            p = jnp.dot(u, w_ref[:, cols], preferred_element_type=F32)
        else:
            p_halo = jnp.dot(u_halo, w_ref[:, cols], preferred_element_type=F32)
            p = p_halo[:n_rows]
            p_first = p_halo[n_rows + 7:n_rows + 8]
            p_last = p_halo[n_rows + 8:n_rows + 9]
            prev = jnp.where(rows == 0, p_first, pltpu.roll(p, 1, axis=0)) * keep_prev
            nxt = jnp.where(rows == n_sub * TM - 1, p_last, pltpu.roll(p, n_sub * TM - 1, axis=0)) * keep_next
            p = (coef_ref[0:1, cols] * p + coef_ref[1:2, cols] * prev + coef_ref[2:3, cols] * nxt
                 + coef_ref[3:4, cols])
        if j * IN_CHUNK < P_COLS:
            o_ref[:, cols] = p.astype(BF16)
        else:
            hy_ref[:, j * IN_CHUNK - P_HY:(j + 1) * IN_CHUNK - P_HY] = p


def _inproj(x, mods, norm_g, w_in_p, coef, *, n_ctx_tiles):
    b, s, _ = x.shape
    n_tiles = b * s // TM
    n_sub = _sub_tiles(n_tiles, INPROJ_SUB_TILES)
    rows = n_sub * TM
    r8 = rows // 8
    p, hy = pl.pallas_call(
        functools.partial(_inproj_kernel, n_sub=n_sub, tiles_per_seq=s // TM, n_ctx_tiles=n_ctx_tiles),
        grid=(n_tiles // n_sub,),
        in_specs=[pl.BlockSpec((rows, D_MODEL), lambda i: (i, 0)),
                  pl.BlockSpec((8, D_MODEL), lambda i: (jnp.maximum(i * r8 - 1, 0), 0)),
                  pl.BlockSpec((8, D_MODEL), lambda i: (jnp.minimum((i + 1) * r8, b * s // 8 - 1), 0)),
                  *_mod_specs(n_sub, s // TM, n_ctx_tiles, mods.shape[0] - 1),
                  pl.BlockSpec((6, D_MODEL), lambda i: (0, 0)),
                  pl.BlockSpec(memory_space=pltpu.VMEM),
                  pl.BlockSpec((4, W_COLS), lambda i: (0, 0))],
        out_specs=[pl.BlockSpec((rows, P_COLS), lambda i: (i, 0)),
                   pl.BlockSpec((rows, HYENA_COLS), lambda i: (i, 0))],
        out_shape=[jax.ShapeDtypeStruct((b * s, P_COLS), BF16),
                   jax.ShapeDtypeStruct((b * s, HYENA_COLS), F32)],
        compiler_params=_cparams(("parallel",)),
        name="in_projection",
    )(*([x.reshape(b * s, D_MODEL)] * 3), *([mods] * n_sub), norm_g, w_in_p, coef)
    return p.reshape(b, s, P_COLS), hy.reshape(b, s, HYENA_COLS)


def _shift_coefficients(rwkv_mu, hyena_conv, hyena_conv_b):
    mu = rwkv_mu.astype(F32)
    coef = jnp.zeros((4, W_COLS), F32).at[0].set(1.0)
    for off, sl in ((P_RKV, slice(0, 3 * RWKV_DIM)), (P_LORA, slice(3 * RWKV_DIM, RWKV_COLS))):
        width = sl.stop - sl.start
        coef = coef.at[0, off:off + width].set(1.0 - mu[0, sl] - mu[1, sl])
        coef = coef.at[1, off:off + width].set(mu[0, sl])
        coef = coef.at[2, off:off + width].set(mu[1, sl])
    hy = slice(P_HY, P_HY + HYENA_COLS)
    coef = coef.at[0, hy].set(hyena_conv[1]).at[1, hy].set(hyena_conv[0]).at[2, hy].set(hyena_conv[2])
    return coef.at[3, hy].set(hyena_conv_b)


def _permute_w_in(w_in):
    o_mla = GATE_COLS
    o_rwkv = o_mla + MLA_COLS
    o_hy = o_rwkv + RWKV_COLS
    o_swa = o_hy + HYENA_COLS
    z = lambda n: jnp.zeros((D_MODEL, n), w_in.dtype)
    parts = [
        w_in[:, :GATE_COLS],
        w_in[:, o_rwkv:o_rwkv + 3 * RWKV_DIM],
        w_in[:, o_swa:o_swa + SWA_HEADS * SWA_HEAD],
        w_in[:, o_rwkv + 3 * RWKV_DIM:o_rwkv + RWKV_COLS],
        w_in[:, o_mla:o_mla + MLA_Q_RANK],
        w_in[:, o_mla + MLA_Q_RANK:o_mla + MLA_Q_RANK + MLA_KV_RANK],
        z(MLA_NOPE), w_in[:, o_mla + MLA_Q_RANK + MLA_KV_RANK:o_mla + MLA_COLS],
        z(LANES - MLA_NOPE - MLA_ROPE),
        w_in[:, o_swa + SWA_HEADS * SWA_HEAD:o_swa + SWA_COLS],
        w_in[:, o_hy:o_hy + HYENA_COLS],
    ]
    out = jnp.concatenate(parts, axis=1)
    assert out.shape[1] == W_COLS
    return out


def _rope_tables(n_lat, n_ctx, rot_dim, lane0, period):
    rows = n_lat // GRID_W
    row = jnp.repeat(jnp.arange(rows, dtype=F32), GRID_W)
    col = jnp.tile(jnp.arange(GRID_W, dtype=F32), rows)
    axis_dim = rot_dim // 2
    h = axis_dim // 2
    inv_freq = ROPE_BASE ** (-jnp.arange(0, axis_dim, 2, dtype=F32) / axis_dim)
    ang_r = row[:, None] * inv_freq
    ang_c = col[:, None] * inv_freq
    cos_rot = jnp.concatenate([jnp.cos(ang_r)] * 2 + [jnp.cos(ang_c)] * 2, axis=1)
    zeros = jnp.zeros_like(ang_r)
    sin_a = jnp.concatenate([-jnp.sin(ang_r), zeros, -jnp.sin(ang_c), zeros], axis=1)
    sin_b = jnp.concatenate([zeros, jnp.sin(ang_r), zeros, jnp.sin(ang_c)], axis=1)

    def widen(t, fill):
        g = jnp.full((n_lat, period), fill, F32).at[:, lane0:lane0 + rot_dim].set(t)
        g = jnp.tile(g, (1, LANES // period))
        ctx = jnp.full((n_ctx, LANES), fill, F32)
        return jnp.concatenate([ctx, g], axis=0)

    return widen(cos_rot, 1.0), widen(sin_a, 0.0), widen(sin_b, 0.0), h


def _rope128(x, cos, sin_a, sin_b, h):
    return x * cos + pltpu.roll(x, LANES - h, axis=1) * sin_a + pltpu.roll(x, h, axis=1) * sin_b


LOG2E = math.log2(math.e)
MLA_SCALE = (MLA_NOPE + MLA_ROPE) ** -0.5 * LOG2E
MLA_ONE_LANE = (MLA_V, 0)


def _mla_prep_kernel(cq_ref, ckv_ref, kr_ref, gq_ref, gkv_ref, wq_ref, wk_ref, wv_ref, vone_ref,
                     cos_ref, sa_ref, sb_ref, q_ref, k_ref, v_ref, *, h):
    cos, sa, sb = cos_ref[...], sa_ref[...], sb_ref[...]
    cq = _rms(cq_ref[0].astype(F32), gq_ref[...]).astype(BF16)
    ckv = _rms(ckv_ref[0].astype(F32), gkv_ref[...]).astype(BF16)
    q = jnp.dot(cq, wq_ref[...], preferred_element_type=F32)
    k = jnp.dot(ckv, wk_ref[...], preferred_element_type=F32)
    kr = _rope128(kr_ref[0].astype(F32), cos, sa, sb, h)
    for hd in range(MLA_HEADS):
        sl = slice(hd * LANES, (hd + 1) * LANES)
        q_ref[0, :, sl] = (_rope128(q[:, sl], cos, sa, sb, h) * MLA_SCALE).astype(BF16)
        k_ref[0, :, sl] = (k[:, sl] + kr).astype(BF16)
    v_ref[0] = (jnp.dot(ckv, wv_ref[...], preferred_element_type=F32) + vone_ref[...]).astype(BF16)


def _mla_prep(p, norm_q, norm_kv, w_uq, w_ukv, tabs):
    b, s, _ = p.shape
    cos, sa, sb, h = tabs
    hq = MLA_NOPE + MLA_ROPE
    wq = jnp.zeros((MLA_Q_RANK, MLA_HEADS, LANES), F32).at[:, :, :hq].set(
        w_uq.reshape(MLA_Q_RANK, MLA_HEADS, hq)).reshape(MLA_Q_RANK, MLA_HEADS * LANES).astype(BF16)
    wkv = w_ukv.reshape(MLA_KV_RANK, MLA_HEADS, MLA_NOPE + MLA_V)
    wk = jnp.zeros((MLA_KV_RANK, MLA_HEADS, LANES), F32).at[:, :, :MLA_NOPE].set(
        wkv[:, :, :MLA_NOPE]).reshape(MLA_KV_RANK, MLA_HEADS * LANES).astype(BF16)
    wv_pairs = wkv[:, :, MLA_NOPE:].reshape(MLA_KV_RANK, MLA_HEADS // 2, 2, MLA_V)
    gap = ((0, 0), (0, 0), (0, LANES - MLA_V))
    wv = jnp.stack([jnp.pad(wv_pairs[:, :, 0], gap), jnp.pad(wv_pairs[:, :, 1], gap[:2] + (gap[2][::-1],))],
                   axis=2).reshape(MLA_KV_RANK, MLA_HEADS * LANES).astype(BF16)
    lane_id = np.arange(MLA_HEADS * LANES) % (2 * LANES)
    vone = jnp.asarray((lane_id == MLA_ONE_LANE[0]) | (lane_id == LANES + MLA_ONE_LANE[1]), F32)[None, :]
    full = lambda shape: pl.BlockSpec(shape, lambda bi, i: (0,) * len(shape))
    tab = pl.BlockSpec((TM, LANES), lambda bi, i: (i, 0))
    return pl.pallas_call(
        functools.partial(_mla_prep_kernel, h=h),
        grid=(b, s // TM),
        in_specs=[pl.BlockSpec((1, TM, MLA_Q_RANK), lambda bi, i: (bi, i, P_CQ // MLA_Q_RANK)),
                  pl.BlockSpec((1, TM, LANES), lambda bi, i: (bi, i, P_CKV // LANES)),
                  pl.BlockSpec((1, TM, LANES), lambda bi, i: (bi, i, P_KR // LANES)),
                  full((1, MLA_Q_RANK)), full((1, MLA_KV_RANK)),
                  full(wq.shape), full(wk.shape), full(wv.shape), full(vone.shape), tab, tab, tab],
        out_specs=[pl.BlockSpec((1, TM, MLA_HEADS * LANES), lambda bi, i: (bi, i, 0))] * 3,
        out_shape=[jax.ShapeDtypeStruct((b, s, MLA_HEADS * LANES), BF16)] * 3,
        compiler_params=_cparams(("parallel", "parallel")),
        name="mla_prep",
    )(p, p, p, norm_q.reshape(1, -1), norm_kv.reshape(1, -1), wq, wk, wv, vone, cos, sa, sb)


MLA_Q_TILES = 1


def _mla_attn_kernel(*refs, n_ctx, n_lat_steps):
    q_refs = refs[:MLA_Q_TILES]
    k_ref, v_ref, o_ref = refs[MLA_Q_TILES:]

    def attend(n_keys):
        outs = []
        for hd in range(2):
            sl = slice(hd * LANES, (hd + 1) * LANES)
            q = jnp.concatenate([q_ref[0, :, sl] for q_ref in q_refs], axis=0)
            s = lax.dot_general(q, k_ref[0, :n_keys, sl], (((1,), (1,)), ((), ())),
                                preferred_element_type=F32)
            e = jnp.exp2(s - jnp.max(s, axis=-1, keepdims=True)).astype(BF16)
            o = jnp.dot(e, v_ref[0, :n_keys, sl], preferred_element_type=F32)
            one = MLA_ONE_LANE[hd]
            outs.append(o / o[:, one:one + 1])
        lane = lax.broadcasted_iota(jnp.int32, outs[0].shape, 1)
        o_ref[0] = jnp.where(lane < MLA_V, outs[0], outs[1]).astype(BF16)

    @pl.when(pl.program_id(2) < n_lat_steps)
    def _():
        attend(k_ref.shape[1])

    @pl.when(pl.program_id(2) >= n_lat_steps)
    def _():
        attend(n_ctx)


def _mla_attention(q, k, v, *, n_ctx, with_ctx):
    b, s, _ = q.shape
    nct = n_ctx // TM
    n_lat = s - n_ctx
    n_lat_steps = n_lat // (MLA_Q_TILES * TM)
    assert n_lat % (MLA_Q_TILES * TM) == 0 and (nct == 1 or not with_ctx)

    def q_spec(t):
        return pl.BlockSpec((1, TM, 2 * LANES), lambda bi, hp, j: (
            bi, jnp.where(j < n_lat_steps, nct + j * MLA_Q_TILES + t, 0), hp))

    kv = pl.BlockSpec((1, s, 2 * LANES), lambda bi, hp, j: (bi, 0, hp))
    return pl.pallas_call(
        functools.partial(_mla_attn_kernel, n_ctx=n_ctx, n_lat_steps=n_lat_steps),
        grid=(b, MLA_HEADS // 2, n_lat_steps + (1 if with_ctx else 0)),
        in_specs=[*[q_spec(t) for t in range(MLA_Q_TILES)], kv, kv],
        out_specs=pl.BlockSpec((1, MLA_Q_TILES * TM, LANES), lambda bi, hp, j: (bi, j, hp)),
        out_shape=jax.ShapeDtypeStruct((b, n_lat + (n_ctx if with_ctx else 0), MLA_HEADS * MLA_V), BF16),
        compiler_params=_cparams(("parallel", "parallel", "parallel")),
        name="mla_attention",
    )(*([q] * MLA_Q_TILES), k, v)


SWA_SCALE = SWA_HEAD ** -0.5 * LOG2E
SWA_TQ = 128
SWA_ONE_LANE = (SWA_HEAD, 0)


def _swa_prep_kernel(q_ref, k_ref, v_ref, cos_ref, sa_ref, sb_ref, qo_ref, ko_ref, vo_ref, *, h):
    cos, sa, sb = cos_ref[...], sa_ref[...], sb_ref[...]
    lane = lax.broadcasted_iota(jnp.int32, cos.shape, 1)
    low = lane < SWA_HEAD
    for j in range(SWA_HEADS // 2):
        blk = _rope128(q_ref[0, :, j * LANES:(j + 1) * LANES].astype(F32), cos, sa, sb, h) * SWA_SCALE
        qo_ref[0, :, (2 * j) * LANES:(2 * j + 1) * LANES] = jnp.where(low, blk, 0.0).astype(BF16)
        qo_ref[0, :, (2 * j + 1) * LANES:(2 * j + 2) * LANES] = jnp.where(
            low, pltpu.roll(blk, SWA_HEAD, axis=1), 0.0).astype(BF16)
    kb = _rope128(k_ref[0].astype(F32), cos, sa, sb, h)
    ko_ref[0, :, :LANES] = jnp.where(low, kb, 0.0).astype(BF16)
    ko_ref[0, :, LANES:] = jnp.where(low, pltpu.roll(kb, SWA_HEAD, axis=1), 0.0).astype(BF16)
    vb = v_ref[0].astype(F32)
    vr = pltpu.roll(vb, SWA_HEAD, axis=1)
    one_lo = jnp.where(lane == SWA_ONE_LANE[0], 1.0, 0.0)
    one_hi = jnp.where(lane == SWA_ONE_LANE[1], 1.0, 0.0)
    vo_ref[0, :, 0 * LANES:1 * LANES] = jnp.where(low, vb, one_lo).astype(BF16)
    vo_ref[0, :, 1 * LANES:2 * LANES] = jnp.where(low, one_hi, vr).astype(BF16)
    vo_ref[0, :, 2 * LANES:3 * LANES] = jnp.where(low, vr, one_lo).astype(BF16)
    vo_ref[0, :, 3 * LANES:4 * LANES] = jnp.where(low, one_hi, vb).astype(BF16)


def _swa_prep(p, tabs):
    b, s, _ = p.shape
    cos, sa, sb, h = tabs
    tab = pl.BlockSpec((TM, LANES), lambda bi, i: (i, 0))
    nq = SWA_HEADS * SWA_HEAD
    return pl.pallas_call(
        functools.partial(_swa_prep_kernel, h=h),
        grid=(b, s // TM),
        in_specs=[pl.BlockSpec((1, TM, nq), lambda bi, i: (bi, i, P_SWAQ // nq)),
                  pl.BlockSpec((1, TM, LANES), lambda bi, i: (bi, i, P_SWAK // LANES)),
                  pl.BlockSpec((1, TM, LANES), lambda bi, i: (bi, i, P_SWAV // LANES)),
                  tab, tab, tab],
        out_specs=[pl.BlockSpec((1, TM, SWA_HEADS * LANES), lambda bi, i: (bi, i, 0)),
                   pl.BlockSpec((1, TM, SWA_KV_HEADS * LANES), lambda bi, i: (bi, i, 0)),
                   pl.BlockSpec((1, TM, 4 * LANES), lambda bi, i: (bi, i, 0))],
        out_shape=[jax.ShapeDtypeStruct((b, s, SWA_HEADS * LANES), BF16),
                   jax.ShapeDtypeStruct((b, s, SWA_KV_HEADS * LANES), BF16),
                   jax.ShapeDtypeStruct((b, s, 4 * LANES), BF16)],
        compiler_params=_cparams(("parallel", "parallel")),
        name="swa_prep",
    )(p, p, p, cos, sa, sb)


def _swa_attn_kernel(sink_ref, q_ref, k_ref, v_ref, o_ref, *, n_ctx, q_off):
    i = pl.program_id(1) + q_off
    s_len = k_ref.shape[1]
    tq = SWA_TQ
    n_loc = tq + 2 * WINDOW
    r0 = i * tq
    is_lat = r0 >= n_ctx
    start = pl.multiple_of(jnp.clip(r0 - WINDOW, 0, s_len - n_loc), LANES)
    rows_g = SWA_GROUP * tq
    row = lax.broadcasted_iota(jnp.int32, (rows_g, n_loc), 0)
    qpos = r0 - n_ctx + row % tq
    kpos = start - n_ctx + lax.broadcasted_iota(jnp.int32, (rows_g, n_loc), 1)
    loc_ok = (jnp.abs(kpos - qpos) <= WINDOW) & (kpos >= 0) & is_lat
    k_loc = k_ref[0, pl.ds(start, n_loc), :]
    v_loc = v_ref[0, pl.ds(start, n_loc), :]
    k_ctx = k_ref[0, 0:n_ctx, :]
    v_ctx = v_ref[0, 0:n_ctx, :]
    head_row = lax.broadcasted_iota(jnp.int32, (rows_g, 1), 0) // tq
    lane = lax.broadcasted_iota(jnp.int32, (tq, LANES), 1)
    stages = []
    for g in range(SWA_KV_HEADS):
        q = jnp.concatenate([q_ref[0, :, hd * LANES:(hd + 1) * LANES]
                             for hd in range(g * SWA_GROUP, (g + 1) * SWA_GROUP)], axis=0)
        kg = slice(g * LANES, (g + 1) * LANES)
        s_loc = lax.dot_general(q, k_loc[:, kg], (((1,), (1,)), ((), ())), preferred_element_type=F32)
        s_ctx = lax.dot_general(q, k_ctx[:, kg], (((1,), (1,)), ((), ())), preferred_element_type=F32)
        sink = jnp.zeros((rows_g, 1), F32)
        for hh in range(SWA_GROUP):
            sink = jnp.where(head_row == hh, sink_ref[g * SWA_GROUP + hh] * LOG2E, sink)
        stages.append((jnp.where(loc_ok, s_loc, NEG_INF), s_ctx, sink))
    for g, (s_loc, s_ctx, sink) in enumerate(stages):
        m = jnp.maximum(jnp.maximum(jnp.max(s_loc, axis=-1, keepdims=True),
                                    jnp.max(s_ctx, axis=-1, keepdims=True)), sink)
        e = jnp.concatenate([jnp.exp2(s_loc - m), jnp.exp2(s_ctx - m)], axis=1).astype(BF16)
        e_sink = jnp.exp2(sink - m)
        outs = []
        for par in range(2):
            vg = slice((2 * g + par) * LANES, (2 * g + par + 1) * LANES)
            o = jnp.dot(e, jnp.concatenate([v_loc[:, vg], v_ctx[:, vg]], axis=0), preferred_element_type=F32)
            one = SWA_ONE_LANE[par]
            outs.append(o / (o[:, one:one + 1] + e_sink))
        for pi in range(SWA_GROUP // 2):
            even = outs[0][(2 * pi) * tq:(2 * pi + 1) * tq]
            odd = outs[1][(2 * pi + 1) * tq:(2 * pi + 2) * tq]
            blk = g * (SWA_GROUP // 2) + pi
            o_ref[0, :, blk * LANES:(blk + 1) * LANES] = jnp.where(lane < SWA_HEAD, even, odd).astype(BF16)


def _swa_attention(q, k, v, sink, *, n_ctx, q_off):
    b, s, _ = q.shape
    nq = s // SWA_TQ - q_off
    return pl.pallas_call(
        functools.partial(_swa_attn_kernel, n_ctx=n_ctx, q_off=q_off),
        grid=(b, nq),
        in_specs=[pl.BlockSpec(memory_space=pltpu.SMEM),
                  pl.BlockSpec((1, SWA_TQ, SWA_HEADS * LANES), lambda bi, i: (bi, i + q_off, 0)),
                  pl.BlockSpec((1, s, SWA_KV_HEADS * LANES), lambda bi, i: (bi, 0, 0)),
                  pl.BlockSpec((1, s, 4 * LANES), lambda bi, i: (bi, 0, 0))],
        out_specs=pl.BlockSpec((1, SWA_TQ, SWA_HEADS * SWA_HEAD), lambda bi, i: (bi, i, 0)),
        out_shape=jax.ShapeDtypeStruct((b, nq * SWA_TQ, SWA_HEADS * SWA_HEAD), BF16),
        compiler_params=_cparams(("parallel", "parallel")),
        name="swa_attention",
    )(sink, q, k, v)


N_PAIR = RWKV_HEADS // 2
RWKV_NB = 4
N_DOUBLINGS = int(math.log2(CHUNK))


def _softplus(x):
    return jnp.maximum(x, 0.0) + jnp.log(1.0 + jnp.exp(-jnp.abs(x)))


def _headsum(x, bd):
    hi = x.astype(BF16)
    lo = (x - hi.astype(F32)).astype(BF16)
    return (jnp.dot(hi, bd, preferred_element_type=F32) + jnp.dot(lo, bd, preferred_element_type=F32))


def _chunk_cumsum(x, reverse):
    rows = lax.broadcasted_iota(jnp.int32, x.shape, 0)
    s = 1
    while s < CHUNK:
        if reverse:
            x = x + jnp.where(rows < CHUNK - s, pltpu.roll(x, CHUNK - s, axis=0), 0.0)
        else:
            x = x + jnp.where(rows >= s, pltpu.roll(x, s, axis=0), 0.0)
        s *= 2
    return x


def _head_rows(x):
    first = lax.broadcasted_iota(jnp.int32, x.shape, 1) < RWKV_HEAD
    return jnp.concatenate([jnp.where(first, x, 0.0), jnp.where(first, 0.0, x)], axis=0)


def _mm_x3(a, b):
    a_hi = a.astype(BF16)
    b_hi = b.astype(BF16)
    a_lo = (a - a_hi.astype(F32)).astype(BF16)
    b_lo = (b - b_hi.astype(F32)).astype(BF16)
    dot = functools.partial(jnp.dot, preferred_element_type=F32)
    return dot(a_hi, b_hi) + dot(a_hi, b_lo) + dot(a_lo, b_hi)


def _rwkv_chunk_kernel(rf_ref, kf_ref, vf_ref, lof_ref, rb_ref, kb_ref, vb_ref, lob_ref,
                       kvec_ref, w0_ref, a0_ref, wup_ref, aup_ref, gup_ref, rk_ref, bd_ref,
                       yf_ref, yb_ref, bonus_ref, gate_ref, s_ref):
    @pl.when(pl.program_id(1) == 0)
    def _():
        s_ref[...] = jnp.zeros_like(s_ref)

    bd = bd_ref[...]
    n_seq = rf_ref.shape[0]
    data = []
    for nb in range(n_seq):
        per_dir = []
        for refs in ((rf_ref, kf_ref, vf_ref, lof_ref), (rb_ref, kb_ref, vb_ref, lob_ref)):
            r, k, v, lora = (ref[nb].astype(F32) for ref in refs)
            kk = k * kvec_ref[0:1]
            kk = kk * lax.rsqrt(_headsum(kk * kk, bd) + 1e-12)
            per_dir.append((r, k, v, lora, kk))
        data.append(per_dir)
        r, k, v, lora, _ = per_dir[0]
        gate_ref[nb] = _mm(_sigmoid(lora), gup_ref[...])
        k_both = sum(k * (1.0 + (_sigmoid(a0_ref[d] + _mm(lora, aup_ref[d])) - 1.0) * kvec_ref[1:2])
                     for d in range(2))
        bonus_ref[nb] = _headsum(r * (0.5 * k_both) * rk_ref[...], bd) * v

    trow = lax.broadcasted_iota(jnp.int32, (2 * CHUNK, 4 * CHUNK), 0) % CHUNK
    tcol = lax.broadcasted_iota(jnp.int32, (2 * CHUNK, 4 * CHUNK), 1) % CHUNK
    sq_r = lax.broadcasted_iota(jnp.int32, (LANES, LANES), 0)
    sq_c = lax.broadcasted_iota(jnp.int32, (LANES, LANES), 1)
    same_head = (sq_r // RWKV_HEAD) == (sq_c // RWKV_HEAD)
    eye = sq_r == sq_c

    chains = []
    for nb, d in ((nb, d) for nb in range(n_seq) for d in range(2)):
        reverse = d == 1
        r, k, v, lora, kk = data[nb][d]
        w_log = -_softplus(-(w0_ref[d] + _mm(jnp.tanh(lora), wup_ref[d]))) - 0.5
        ld = -jnp.exp(w_log)
        a = _sigmoid(a0_ref[d] + _mm(lora, aup_ref[d]))
        k_d = k * (1.0 + (a - 1.0) * kvec_ref[1:2])
        b_d = kk * a
        lg = _chunk_cumsum(ld, reverse)
        last = 0 if reverse else CHUNK - 1
        tot = lg[last:last + 1]
        e_neg = jnp.exp(-lg)
        e_end = jnp.exp(tot - lg)
        z_t = -kk * jnp.exp(lg - ld)
        r_t = r * jnp.exp(lg)
        b_t = b_d * e_neg
        k_t = k_d * e_neg
        b_e = b_d * e_end
        k_e = k_d * e_end
        e_tot = jnp.exp(tot)
        before = (tcol > trow) if reverse else (tcol < trow)
        before_eq = (tcol >= trow) if reverse else (tcol <= trow)
        for pr in range(N_PAIR):
            sl = slice(pr * LANES, (pr + 1) * LANES)
            ch = {"nb": nb, "d": d, "sl": sl, "rp": r_t[:, sl], "vp": v[:, sl], "e_tot": e_tot[:, sl],
                  "be_ke": jnp.concatenate([b_e[:, sl], k_e[:, sl]], axis=0)}
            zst, rst, vst = _head_rows(z_t[:, sl]), _head_rows(r_t[:, sl]), _head_rows(v[:, sl])
            bkst = jnp.concatenate([_head_rows(b_t[:, sl]), _head_rows(k_t[:, sl])], axis=0)
            ch["lz"] = jnp.where(before, _mm_nt(zst, bkst), 0.0)
            ch["lr"] = jnp.where(before_eq, _mm_nt(rst, bkst), 0.0)
            ch["zst"], ch["vst"] = zst, vst
            chains.append(ch)

    unit = jnp.where(eye, 1.0, 0.0)
    for ch in chains:
        ch["pw"] = ch["lz"][:, :LANES]
        ch["t"] = unit + ch["pw"]
        ch["x"] = jnp.concatenate([ch["zst"], _mm(ch["lz"][:, LANES:], ch["vst"])], axis=1)
    for it in range(1, N_DOUBLINGS):
        for ch in chains:
            ch["pw"] = _mm(ch["pw"], ch["pw"])
        for ch in chains:
            ch["t"] = ch["t"] + _mm(ch["pw"], ch["t"])
    for ch in chains:
        ch["x"] = _mm(ch["t"], ch["x"])
    for ch in chains:
        low = jnp.concatenate([jnp.zeros_like(ch["vst"]), ch["vst"]], axis=1)
        op = _mm(ch["lr"], jnp.concatenate([ch["x"], low], axis=0))
        ch["op"] = op[:CHUNK] + op[CHUNK:]
        ch["xp"] = ch["x"][:CHUNK] + ch["x"][CHUNK:]
    for ch in chains:
        rhs = jnp.concatenate([ch["xp"], jnp.concatenate([jnp.zeros_like(ch["vp"]), ch["vp"]], axis=1)], axis=0)
        ag = _mm_tn(ch["be_ke"], rhs)
        a_full = ag[:, :LANES] + jnp.where(eye, jnp.broadcast_to(ch["e_tot"], (LANES, LANES)), 0.0)
        ch["a"] = jnp.where(same_head, a_full, 0.0)
        ch["g"] = jnp.where(same_head, ag[:, LANES:], 0.0)
    for idx, ch in enumerate(chains):
        st = s_ref[idx]
        y_ref = yf_ref if ch["d"] == 0 else yb_ref
        y_ref[ch["nb"], :, ch["sl"]] = _mm(ch["rp"] + ch["op"][:, :LANES], st) + ch["op"][:, LANES:]
        s_ref[idx] = _mm_x3(ch["a"], st) + ch["g"]


def _head_block_diag():
    idx = np.arange(RWKV_DIM) // RWKV_HEAD
    return jnp.asarray(idx[:, None] == idx[None, :], BF16)


def _rwkv_chunks(p, kvec, w0, a0, w_up, a_up, g_up, r_k, *, n_ctx):
    b, s, _ = p.shape
    nc = s // CHUNK
    ncc = n_ctx // CHUNK
    lora_w = DECAY_LORA + AAA_LORA + GATE_LORA
    wup = jnp.zeros((2, lora_w, RWKV_DIM), F32).at[:, :DECAY_LORA].set(w_up).astype(BF16)
    aup = jnp.zeros((2, lora_w, RWKV_DIM), F32).at[:, DECAY_LORA:DECAY_LORA + AAA_LORA].set(a_up).astype(BF16)
    gup = jnp.zeros((lora_w, RWKV_DIM), F32).at[DECAY_LORA + AAA_LORA:].set(g_up).astype(BF16)

    def fwd(c):
        return c

    def back(c):
        return jnp.where(c < ncc, ncc - 1 - c, nc - 1 - (c - ncc))

    full = lambda shape: pl.BlockSpec(shape, lambda bi, c: (0,) * len(shape))

    n_seq = _sub_tiles(b, RWKV_NB)

    def inputs(chunk):
        col = lambda off: pl.BlockSpec((n_seq, CHUNK, RWKV_DIM), lambda bi, c: (bi, chunk(c), off // RWKV_DIM))
        return [col(P_RKV), col(P_RKV + RWKV_DIM), col(P_RKV + 2 * RWKV_DIM),
                pl.BlockSpec((n_seq, CHUNK, lora_w), lambda bi, c: (bi, chunk(c), P_LORA // lora_w))]

    tok = lambda chunk: pl.BlockSpec((n_seq, CHUNK, RWKV_DIM), lambda bi, c: (bi, chunk(c), 0))
    return pl.pallas_call(
        _rwkv_chunk_kernel,
        grid=(b // n_seq, nc),
        in_specs=[*inputs(fwd), *inputs(back),
                  full((2, RWKV_DIM)), full((2, 1, RWKV_DIM)), full((2, 1, RWKV_DIM)),
                  full(wup.shape), full(aup.shape), full(gup.shape), full((1, RWKV_DIM)),
                  full((RWKV_DIM, RWKV_DIM))],
        out_specs=[tok(fwd), tok(back), tok(fwd), tok(fwd)],
        out_shape=[jax.ShapeDtypeStruct((b, s, RWKV_DIM), F32)] * 4,
        scratch_shapes=[pltpu.VMEM((n_seq * 2 * N_PAIR, LANES, LANES), F32)],
        compiler_params=_cparams(("parallel", "arbitrary")),
        name="rwkv_chunks",
    )(*([p] * 8), kvec, w0.reshape(2, 1, -1), a0.reshape(2, 1, -1), wup, aup, gup, r_k.reshape(1, -1),
      _head_block_diag())


DFT_N2 = LANES
HY_MIN_LEN = 1024
HY_CT = 128


def _hyena_mlp_kernel(feats_ref, w1_ref, b1_ref, w2_ref, b2_ref, freq_ref, h_ref):
    h = jnp.sin(freq_ref[0:1] * (_mm_f32(feats_ref[0], w1_ref[...]) + b1_ref[...]))
    h_ref[0] = jnp.sin(freq_ref[1:2] * (_mm_f32(h, w2_ref[...]) + b2_ref[...]))


def _hyena_filter_kernel(h_ref, w3f_ref, w3b_ref, t_ref, delta_ref, k_ref):
    hf = _mm_f32(h_ref[0], w3f_ref[...]) * jnp.exp(-t_ref[0] * delta_ref[...])
    hb = _mm_f32(h_ref[1], w3b_ref[...]) * jnp.exp(-t_ref[1] * delta_ref[...])
    norm = (jnp.sum(jnp.abs(hf), axis=0, keepdims=True) + jnp.sum(jnp.abs(hb), axis=0, keepdims=True))
    r = pl.program_id(2)
    rows = lax.broadcasted_iota(jnp.int32, hf.shape, 0)
    tail = jnp.where(rows == 0, 0.0, hb)
    blk = jnp.where(r == 0, hf, jnp.where(r == pl.num_programs(2) - 1, tail, 0.0))
    k_ref[0] = blk / norm


def _hyena_filter_buffer(n, nc, w1, b1, w2, b2, w3, freq):
    lag = jnp.stack([jnp.arange(n), jnp.where(jnp.arange(n) == 0, 0, n - jnp.arange(n))]).astype(F32)
    t = (lag / (n - 1))[:, :, None]
    bands = jnp.linspace(1e-4, HYENA_BANDS - 1, HYENA_BANDS, dtype=F32)
    ang = (2.0 * math.pi / n) * lag[:, :, None] * bands[None, None, :]
    feats = jnp.concatenate([t, jnp.cos(ang), -jnp.sin(ang),
                             jnp.zeros((2, n, HYENA_FW - HYENA_EMB), F32)], axis=-1)
    w1p = jnp.zeros((HYENA_FW, HYENA_FW), F32).at[:HYENA_EMB].set(w1)
    deltas = jnp.abs(jnp.linspace(math.log(HYENA_TARGET) / HYENA_SLOW,
                                  math.log(HYENA_TARGET) / HYENA_FAST, HYENA_DIM, dtype=F32))[None, :]
    fixed = lambda shape: pl.BlockSpec(shape, lambda d: (0,) * len(shape))
    hidden = pl.pallas_call(
        _hyena_mlp_kernel,
        grid=(2,),
        in_specs=[pl.BlockSpec((1, n, HYENA_FW), lambda d: (d, 0, 0)), fixed((HYENA_FW, HYENA_FW)),
                  fixed((1, HYENA_FW)), fixed((HYENA_FW, HYENA_FW)), fixed((1, HYENA_FW)),
                  fixed((2, HYENA_FW))],
        out_specs=pl.BlockSpec((1, n, HYENA_FW), lambda d: (d, 0, 0)),
        out_shape=jax.ShapeDtypeStruct((2, n, HYENA_FW), F32),
        compiler_params=_cparams(("parallel",)),
        name="hyena_filter_mlp",
    )(feats, w1p, b1.reshape(1, -1), w2, b2.reshape(1, -1), freq)
    tc = 256
    nj = HYENA_DIM // tc
    full = lambda shape: pl.BlockSpec(shape, lambda o, j, r: (0,) * len(shape))
    return pl.pallas_call(
        _hyena_filter_kernel,
        grid=(HYENA_ORDER, nj, nc // n),
        in_specs=[full((2, n, HYENA_FW)),
                  pl.BlockSpec((HYENA_FW, tc), lambda o, j, r: (0, o * 2 * nj + j)),
                  pl.BlockSpec((HYENA_FW, tc), lambda o, j, r: (0, o * 2 * nj + nj + j)),
                  full((2, n, 1)), pl.BlockSpec((1, tc), lambda o, j, r: (0, j))],
        out_specs=pl.BlockSpec((1, n, tc), lambda o, j, r: (o, r, j)),
        out_shape=jax.ShapeDtypeStruct((HYENA_ORDER, nc, HYENA_DIM), F32),
        compiler_params=_cparams(("parallel", "parallel", "parallel")),
        name="hyena_filters",
    )(hidden, w3, w3, t, deltas)


HY_J = 8
HY_GROUPS = DFT_N2 // HY_J


def _dft_tables(n1):
    nc = n1 * DFT_N2
    f1 = np.arange(n1)
    ang = 2.0 * np.pi * ((f1[:, None] * f1[None, :]) % n1) / n1
    eye = np.eye(HY_J)
    w1 = np.kron(np.concatenate([np.cos(ang), -np.sin(ang)], axis=0), eye)
    v3 = np.kron(np.concatenate([np.cos(ang), -np.sin(ang)], axis=1), eye) / nc
    t2 = np.arange(DFT_N2).reshape(HY_GROUPS, 1, HY_J)
    tw = 2.0 * np.pi * ((f1[None, :, None] * t2) % nc) / nc
    tw = np.broadcast_to(tw.reshape(HY_GROUPS, n1 * HY_J, 1), (HY_GROUPS, n1 * HY_J, LANES))
    k = np.arange(DFT_N2)
    phi = 2.0 * np.pi * ((k[:, None] * k[None, :]) % DFT_N2) / DFT_N2
    c, s = np.cos(phi), np.sin(phi)
    f_fwd = np.block([[c, s], [-s, c]])
    f_inv = np.block([[c, -s], [s, c]])
    names = ("w1", "v3", "ctw", "stw", "f_fwd", "f_inv")
    return {n: jnp.asarray(t, BF16) for n, t in zip(names, (w1, v3, np.cos(tw), np.sin(tw), f_fwd, f_inv))}


def _tiles(ref, rows, g, lead=()):
    return jnp.concatenate([ref[lead + (pl.ds(r * DFT_N2 + g * HY_J, HY_J), slice(None))] for r in rows], axis=0)


def _dft_rows_in(x_ref, w_ref, ctw_ref, stw_ref, a_ref, t1n, n1, t1_valid):
    half = n1 * HY_J
    for g in range(HY_GROUPS):
        xg = _tiles(x_ref, range(t1_valid), g, lead=(0,))
        pq = _mm(w_ref[:, :t1_valid * HY_J], xg)
        p, q = pq[:half], pq[half:]
        c, s = ctw_ref[g].astype(F32), stw_ref[g].astype(F32)
        re = c * p + s * q
        im = c * q - s * p
        for m in range(n1):
            rows = slice(m * HY_J, (m + 1) * HY_J)
            a_ref[pl.ds(m * DFT_N2 + g * HY_J, HY_J), :] = re[rows]
            a_ref[pl.ds((n1 + m) * DFT_N2 + g * HY_J, HY_J), :] = im[rows]


def _slab(f1):
    return slice(f1 * DFT_N2, (f1 + 1) * DFT_N2)


def _spectrum_kernel(x_ref, w_ref, ctw_ref, stw_ref, ff_ref, k_ref, a_ref, *, n1):
    _dft_rows_in(x_ref, w_ref, ctw_ref, stw_ref, a_ref, n1, n1, n1)
    for f1 in range(n1):
        a = jnp.concatenate([a_ref[_slab(f1)], a_ref[_slab(n1 + f1)]], axis=0)
        k_ref[0, f1] = _mm(ff_ref[...], a)


def _filter_spectrum(kbuf, tabs, n1):
    no, nc, _ = kbuf.shape
    nj = HYENA_DIM // HY_CT
    const = lambda a: pl.BlockSpec(a.shape, lambda o, j: (0,) * a.ndim)
    consts = [tabs[n] for n in ("w1", "ctw", "stw", "f_fwd")]
    return pl.pallas_call(
        functools.partial(_spectrum_kernel, n1=n1),
        grid=(no, nj),
        in_specs=[pl.BlockSpec((1, nc, HY_CT), lambda o, j: (o, 0, j)), *[const(a) for a in consts]],
        out_specs=pl.BlockSpec((1, n1, 2 * DFT_N2, HY_CT), lambda o, j: (o, 0, 0, j)),
        out_shape=jax.ShapeDtypeStruct((no, n1, 2 * DFT_N2, HYENA_DIM), F32),
        scratch_shapes=[pltpu.VMEM((2 * n1 * DFT_N2, HY_CT), F32)],
        compiler_params=_cparams(("parallel", "parallel")),
        name="hyena_filter_spectrum",
    )(kbuf, *consts)


def _hyena_conv_kernel(x_ref, gate_ref, k_ref, w_ref, ctw_ref, stw_ref, ff_ref, fi_ref, v_ref, bias_ref,
                       o_ref, a_ref, *, t1n, n1, t1_valid):
    _dft_rows_in(x_ref, w_ref, ctw_ref, stw_ref, a_ref, t1n, n1, t1_valid)
    for f1 in range(n1):
        re, im = _slab(f1), _slab(n1 + f1)
        x = _mm(ff_ref[...], jnp.concatenate([a_ref[re], a_ref[im]], axis=0))
        xre, xim = x[:DFT_N2], x[DFT_N2:]
        kre, kim = k_ref[0, f1, :DFT_N2], k_ref[0, f1, DFT_N2:]
        bm = _mm(fi_ref[...], jnp.concatenate([xre * kre - xim * kim, xre * kim + xim * kre], axis=0))
        a_ref[re] = bm[:DFT_N2]
        a_ref[im] = bm[DFT_N2:]
    bias = bias_ref[...]
    for g in range(HY_GROUPS):
        br, bi = _tiles(a_ref, range(n1), g), _tiles(a_ref, range(n1, 2 * n1), g)
        c, s = ctw_ref[g].astype(F32), stw_ref[g].astype(F32)
        y = _mm(v_ref[:t1n * HY_J], jnp.concatenate([c * br - s * bi, s * br + c * bi], axis=0))
        for t1 in range(t1n):
            rows = pl.ds(t1 * DFT_N2 + g * HY_J, HY_J)
            o_ref[0, rows, :] = gate_ref[0, rows, :] * (y[t1 * HY_J:(t1 + 1) * HY_J] + bias * x_ref[0, rows, :])


def _hyena_conv(z, z_spec, gate, gate_spec, kspec, order, bias, tabs, n1, n_pad, n):
    bx = z.shape[0]
    t1n = n_pad // DFT_N2
    consts = [tabs[name] for name in ("w1", "ctw", "stw", "f_fwd", "f_inv", "v3")]
    const = lambda a: pl.BlockSpec(a.shape, lambda j, bi: (0,) * a.ndim)
    return pl.pallas_call(
        functools.partial(_hyena_conv_kernel, t1n=t1n, n1=n1, t1_valid=-(-n // DFT_N2)),
        grid=(HYENA_DIM // HY_CT, bx),
        in_specs=[z_spec, gate_spec,
                  pl.BlockSpec((1, n1, 2 * DFT_N2, HY_CT), lambda j, bi: (order, 0, 0, j)),
                  *[const(a) for a in consts],
                  pl.BlockSpec((1, HY_CT), lambda j, bi: (0, j))],
        out_specs=pl.BlockSpec((1, n_pad, HY_CT), lambda j, bi: (bi, 0, j)),
        out_shape=jax.ShapeDtypeStruct((bx, n_pad, HYENA_DIM), F32),
        scratch_shapes=[pltpu.VMEM((2 * n1 * DFT_N2, HY_CT), F32)],
        compiler_params=_cparams(("parallel", "parallel")),
        name="hyena_conv",
    )(z, gate, kspec, *consts, bias.reshape(1, -1))


def _hyena_operator(p_hy, row0, n, filt_params, bias):
    n_pad = max(n, HY_MIN_LEN)
    nc = 2 * n_pad
    n1 = nc // DFT_N2
    tabs = _dft_tables(n1)
    kspec = _filter_spectrum(_hyena_filter_buffer(n, nc, *filt_params), tabs, n1)
    nj = HYENA_DIM // HY_CT
    window = lambda part: pl.BlockSpec((pl.Element(1), pl.Element(n_pad), pl.Element(HY_CT)),
                                       lambda j, bi: (bi, row0, (part * nj + j) * HY_CT))
    own = pl.BlockSpec((1, n_pad, HY_CT), lambda j, bi: (bi, 0, j))
    z = _hyena_conv(p_hy, window(0), p_hy, window(1), kspec, 0, bias[0], tabs, n1, n_pad, n)
    return _hyena_conv(z, own, p_hy, window(2), kspec, 1, bias[1], tabs, n1, n_pad, n)


def _rwkv_readout(y, bonus, gate, ln_g, ln_b, bd):
    inv_n = 1.0 / RWKV_HEAD
    dev = y - _headsum(y, bd) * inv_n
    var = _headsum(dev * dev, bd) * inv_n
    return (dev * lax.rsqrt(var + RWKV_LN_EPS) * ln_g + ln_b + bonus) * gate


def _merge_kernel(x_ref, mod_ref, g_ref, gates_ref, bg_ref, ya_ref, yf_ref, yr_ref, bonus_ref, rgate_ref,
                  lng_ref, lnb_ref, bd_ref, yh_ref, yhc_ref, yd_ref, wb_ref, wo_ref, o_ref, *, n_ctx_tiles, row_off):
    yb = _rwkv_readout(yf_ref[0] + yr_ref[0], bonus_ref[0], rgate_ref[0], lng_ref[...], lnb_ref[...], bd_ref[...])
    is_ctx = pl.program_id(1) + row_off < n_ctx_tiles
    yh = jnp.where(is_ctx, yhc_ref[0], yh_ref[0])
    merged = None
    for br, y in enumerate((ya_ref[0], yb, yh, yd_ref[0])):
        gate = _sigmoid(gates_ref[0, :, br * D_MODEL:(br + 1) * D_MODEL] + bg_ref[br:br + 1])
        term = gate * jnp.dot(y.astype(BF16), wb_ref[br], preferred_element_type=F32)
        merged = term if merged is None else merged + term
    y = jnp.dot(merged.astype(BF16), wo_ref[...], preferred_element_type=F32)
    o_ref[0] = x_ref[0] + mod_ref[0, 5:6] * _rms(y, g_ref[3:4])


def _merge(x, mods, norm_g, p, b_gate, ya, rwkv, yh, yh_ctx, yd, w_branch, w_out, *, n_ctx_tiles, row_off):
    b, s, _ = x.shape
    nt = s // TM - row_off
    n_lat = mods.shape[0] - 1
    br = lambda: pl.BlockSpec((1, TM, BRANCH_DIM), lambda bi, i: (bi, i, 0))
    tok = pl.BlockSpec((1, TM, BRANCH_DIM), lambda bi, i: (bi, i + row_off, 0))
    row = pl.BlockSpec((1, RWKV_DIM), lambda bi, i: (0, 0))
    y_fwd, y_bwd, bonus, rgate, ln_g, ln_b = rwkv
    n_lat_tiles = s // TM - n_ctx_tiles
    br_a = pl.BlockSpec((1, TM, BRANCH_DIM), lambda bi, i: (
        bi, jnp.where(i + row_off < n_ctx_tiles, n_lat_tiles + i, i + row_off - n_ctx_tiles), 0))
    br_h = pl.BlockSpec((1, TM, BRANCH_DIM), lambda bi, i: (bi, jnp.maximum(i + row_off - n_ctx_tiles, 0), 0))
    br_hc = pl.BlockSpec((1, TM, BRANCH_DIM), lambda bi, i: (
        bi, jnp.minimum(i + row_off, max(n_ctx_tiles - 1, 0)) if yh_ctx is not None else 0, 0))
    return pl.pallas_call(
        functools.partial(_merge_kernel, n_ctx_tiles=n_ctx_tiles if yh_ctx is not None else 0, row_off=row_off),
        grid=(b, nt),
        in_specs=[pl.BlockSpec((1, TM, D_MODEL), lambda bi, i: (bi, i + row_off, 0)),
                  pl.BlockSpec((1, N_MOD, D_MODEL),
                               lambda bi, i: (jnp.where(i + row_off < n_ctx_tiles, n_lat, bi), 0, 0)),
                  pl.BlockSpec((6, D_MODEL), lambda bi, i: (0, 0)),
                  pl.BlockSpec((1, TM, GATE_COLS), lambda bi, i: (bi, i + row_off, 0)),
                  pl.BlockSpec((N_BRANCH, D_MODEL), lambda bi, i: (0, 0)),
                  br_a, tok, tok, tok, tok, row, row,
                  pl.BlockSpec((RWKV_DIM, RWKV_DIM), lambda bi, i: (0, 0)),
                  br_h, br_hc, br(),
                  pl.BlockSpec((N_BRANCH, BRANCH_DIM, D_MODEL), lambda bi, i: (0, 0, 0)),
                  pl.BlockSpec((D_MODEL, D_MODEL), lambda bi, i: (0, 0))],
        out_specs=pl.BlockSpec((1, TM, D_MODEL), lambda bi, i: (bi, i, 0)),
        out_shape=jax.ShapeDtypeStruct((b, nt * TM, D_MODEL), F32),
        compiler_params=_cparams(("parallel", "parallel")),
        name="merge_branches",
    )(x, mods, norm_g, p, b_gate, ya, y_fwd, y_bwd, bonus, rgate, ln_g.reshape(1, -1), ln_b.reshape(1, -1),
      _head_block_diag(), yh, yh if yh_ctx is None else yh_ctx, yd, w_branch, w_out)


def kernel(x, c, ctx, c_ctx, w_mod, b_mod, norm_g, ffn_w13, ffn_w2, w_in, b_gate, mla_norm_q, mla_norm_kv, mla_w_uq, mla_w_ukv, rwkv_mu, rwkv_w0, rwkv_w_up, rwkv_a0, rwkv_a_up, rwkv_g_up, rwkv_kvec, rwkv_r_k, rwkv_ln_g, rwkv_ln_b, hyena_conv, hyena_conv_b, hyena_w1, hyena_b1, hyena_w2, hyena_b2, hyena_w3, hyena_freq, hyena_bias, swa_sink, w_branch, w_out):
    b, n, _ = x.shape
    n_ctx = ctx.shape[1]
    nct = n_ctx // TM
    xall = jnp.concatenate([ctx, x], axis=1)
    c_all = jnp.concatenate([c, c_ctx[None]], axis=0)
    tabs_mla = _rope_tables(n, n_ctx, MLA_ROPE, MLA_NOPE, LANES)
    tabs_swa = _rope_tables(n, n_ctx, SWA_HEAD, 0, SWA_HEAD)
    depth = w_mod.shape[0]
    for l in range(depth):
        with_ctx = l + 1 < depth
        row_off = 0 if with_ctx else nct
        mods = _modulation(c_all, w_mod[l], b_mod[l])
        xall = _ffn(xall, mods, norm_g[l], ffn_w13[l, 0].astype(BF16), ffn_w2[l, 0].astype(BF16),
                    mod0=0, g0=0, n_ctx_tiles=nct)
        coef = _shift_coefficients(rwkv_mu[l], hyena_conv[l], hyena_conv_b[l])
        p, p_hy = _inproj(xall, mods, norm_g[l], _permute_w_in(w_in[l]).astype(BF16), coef, n_ctx_tiles=nct)
        q, k, v = _mla_prep(p, mla_norm_q[l], mla_norm_kv[l], mla_w_uq[l], mla_w_ukv[l], tabs_mla)
        ya = _mla_attention(q, k, v, n_ctx=n_ctx, with_ctx=with_ctx)
        y_fwd, y_bwd, bonus, gate = _rwkv_chunks(p, rwkv_kvec[l], rwkv_w0[l], rwkv_a0[l], rwkv_w_up[l],
                                                 rwkv_a_up[l], rwkv_g_up[l], rwkv_r_k[l], n_ctx=n_ctx)
        rwkv = (y_fwd, y_bwd, bonus, gate, rwkv_ln_g[l], rwkv_ln_b[l])
        filt = (hyena_w1[l], hyena_b1[l], hyena_w2[l], hyena_b2[l], hyena_w3[l], hyena_freq[l])
        yh = _hyena_operator(p_hy, n_ctx, n, filt, hyena_bias[l])
        yh_ctx = _hyena_operator(p_hy, 0, n_ctx, filt, hyena_bias[l]) if with_ctx else None
        q, k, v = _swa_prep(p, tabs_swa)
        yd = _swa_attention(q, k, v, swa_sink[l], n_ctx=n_ctx, q_off=row_off * TM // SWA_TQ)
        xall = _merge(xall, mods, norm_g[l], p, b_gate[l], ya, rwkv, yh, yh_ctx, yd, w_branch[l].astype(BF16),
                      w_out[l].astype(BF16), n_ctx_tiles=nct, row_off=row_off)
        xall = _ffn(xall, mods, norm_g[l], ffn_w13[l, 1].astype(BF16), ffn_w2[l, 1].astype(BF16),
                    mod0=6, g0=4, n_ctx_tiles=nct - row_off)
    return xall
```

```python
import functools
import math

import numpy as np
import jax
import jax.numpy as jnp
from jax import lax
from jax.experimental import pallas as pl
from jax.experimental.pallas import tpu as pltpu

F32 = jnp.float32
BF16 = jnp.bfloat16

D_MODEL = 1024
GRID_W = 64
N_BRANCH = 4
N_MOD = 9
FF_DIM = 2816
EPS = 1e-6
ROPE_BASE = 10000.0
NEG_INF = -1e30
BRANCH_DIM = 512
MLA_HEADS = 8
MLA_NOPE = 64
MLA_ROPE = 32
MLA_V = 64
MLA_Q_RANK = 256
MLA_KV_RANK = 128
RWKV_HEADS = 8
RWKV_HEAD = 64
RWKV_DIM = RWKV_HEADS * RWKV_HEAD
DECAY_LORA = 64
AAA_LORA = 64
GATE_LORA = 128
RWKV_LN_EPS = 64e-5
HYENA_DIM = 512
HYENA_ORDER = 2
HYENA_EMB = 33
HYENA_BANDS = (HYENA_EMB - 1) // 2
HYENA_FW = 64
HYENA_TARGET = 1e-2
HYENA_FAST = 0.3
HYENA_SLOW = 1.5
SWA_HEADS = 8
SWA_KV_HEADS = 2
SWA_HEAD = 64
SWA_GROUP = SWA_HEADS // SWA_KV_HEADS
WINDOW = 128
GATE_COLS = N_BRANCH * D_MODEL
MLA_COLS = MLA_Q_RANK + MLA_KV_RANK + MLA_ROPE
RWKV_COLS = 3 * RWKV_DIM + DECAY_LORA + AAA_LORA + GATE_LORA
HYENA_COLS = 3 * HYENA_DIM
SWA_COLS = (SWA_HEADS + 2 * SWA_KV_HEADS) * SWA_HEAD

LANES = 128
SUBLANES = 8
V7X_VMEM_LIMIT = 56 * 1024 * 1024

TM = 256
FFN_SUB_TILES = 4
INPROJ_SUB_TILES = 2
FF_CHUNK = 256
IN_CHUNK = 512
CHUNK = 64

P_RKV = 4096
P_SWAQ = 5632
P_LORA = 6144
P_CQ = 6400
P_CKV = 6656
P_KR = 6784
P_SWAK = 6912
P_SWAV = 7040
P_COLS = 7168
P_HY = 7168
W_COLS = P_HY + HYENA_COLS
_SHIFT_COLS = ((P_RKV, P_SWAQ), (P_LORA, P_CQ), (P_HY, W_COLS))
_SHIFT_CHUNKS = [any(lo < (j + 1) * IN_CHUNK and j * IN_CHUNK < hi for lo, hi in _SHIFT_COLS)
                 for j in range(W_COLS // IN_CHUNK)]


def _cparams(sem, vmem=V7X_VMEM_LIMIT):
    return pltpu.CompilerParams(dimension_semantics=sem, vmem_limit_bytes=vmem)


def _mm(a, b):
    return jnp.dot(a.astype(BF16), b.astype(BF16), preferred_element_type=F32)


def _mm_nt(a, b):
    return lax.dot_general(a.astype(BF16), b.astype(BF16), (((1,), (1,)), ((), ())),
                           preferred_element_type=F32)


def _mm_tn(a, b):
    return lax.dot_general(a.astype(BF16), b.astype(BF16), (((0,), (0,)), ((), ())),
                           preferred_element_type=F32)


def _mm_f32(a, b):
    return jnp.dot(a, b, preferred_element_type=F32, precision=lax.Precision.HIGHEST)


def _rms(x, g):
    return x * lax.rsqrt(jnp.mean(x * x, axis=-1, keepdims=True) + EPS) * g


def _sigmoid(x):
    return 1.0 / (1.0 + jnp.exp(-x))


def _mod_kernel(c_ref, w_ref, b_ref, o_ref):
    c = c_ref[...]
    o_ref[...] = _mm(c * _sigmoid(c), w_ref[...]) + b_ref[...]


def _modulation(c_all, w_mod, b_mod):
    r = c_all.shape[0]
    rp = -(-r // 8) * 8
    c_pad = jnp.zeros((rp, D_MODEL), F32).at[:r].set(c_all)
    tn = 1024
    out = pl.pallas_call(
        _mod_kernel,
        grid=(N_MOD * D_MODEL // tn,),
        in_specs=[pl.BlockSpec((rp, D_MODEL), lambda j: (0, 0)),
                  pl.BlockSpec((D_MODEL, tn), lambda j: (0, j)),
                  pl.BlockSpec((1, tn), lambda j: (0, j))],
        out_specs=pl.BlockSpec((rp, tn), lambda j: (0, j)),
        out_shape=jax.ShapeDtypeStruct((rp, N_MOD * D_MODEL), F32),
        compiler_params=_cparams(("arbitrary",)),
        name="modulation",
    )(c_pad, w_mod, b_mod.reshape(1, -1))
    return out[:r].reshape(r, N_MOD, D_MODEL)


def _sub_tiles(n_tiles, most):
    return max(g for g in range(1, most + 1) if n_tiles % g == 0)


def _mod_specs(n_sub, tiles_per_seq, n_ctx_tiles, ctx_row):
    def spec(k):
        def index(i):
            t = i * n_sub + k
            return (jnp.where(t % tiles_per_seq < n_ctx_tiles, ctx_row, t // tiles_per_seq), 0, 0)
        return pl.BlockSpec((1, N_MOD, D_MODEL), index)
    return [spec(k) for k in range(n_sub)]


def _ffn_kernel(x_ref, *refs, mod0, g0, n_sub):
    mod_refs = refs[:n_sub]
    g_ref, w13_ref, w2_ref, o_ref = refs[n_sub:]
    tiles = [slice(t * TM, (t + 1) * TM) for t in range(n_sub)]
    u = jnp.concatenate(
        [(_rms(x_ref[rows], g_ref[g0:g0 + 1]) * (1.0 + m[0, mod0 + 1:mod0 + 2]) + m[0, mod0:mod0 + 1]).astype(BF16)
         for rows, m in zip(tiles, mod_refs)], axis=0)
    acc = jnp.zeros(x_ref.shape, F32)
    for f in range(FF_DIM // FF_CHUNK):
        lo = f * FF_CHUNK
        a = jnp.dot(u, w13_ref[:, lo:lo + FF_CHUNK], preferred_element_type=F32)
        b = jnp.dot(u, w13_ref[:, FF_DIM + lo:FF_DIM + lo + FF_CHUNK], preferred_element_type=F32)
        h = (a * _sigmoid(a) * b).astype(BF16)
        acc = acc + jnp.dot(h, w2_ref[lo:lo + FF_CHUNK, :], preferred_element_type=F32)
    hn = _rms(acc, g_ref[g0 + 1:g0 + 2])
    for rows, m in zip(tiles, mod_refs):
        o_ref[rows] = x_ref[rows] + 0.5 * m[0, mod0 + 2:mod0 + 3] * hn[rows]


def _ffn(x, mods, norm_g, w13, w2, *, mod0, g0, n_ctx_tiles):
    b, s, _ = x.shape
    n_tiles = b * s // TM
    n_sub = _sub_tiles(n_tiles, FFN_SUB_TILES)
    rows = n_sub * TM
    out = pl.pallas_call(
        functools.partial(_ffn_kernel, mod0=mod0, g0=g0, n_sub=n_sub),
        grid=(n_tiles // n_sub,),
        in_specs=[pl.BlockSpec((rows, D_MODEL), lambda i: (i, 0)),
                  *_mod_specs(n_sub, s // TM, n_ctx_tiles, mods.shape[0] - 1),
                  pl.BlockSpec((6, D_MODEL), lambda i: (0, 0)),
                  pl.BlockSpec(memory_space=pltpu.VMEM),
                  pl.BlockSpec(memory_space=pltpu.VMEM)],
        out_specs=pl.BlockSpec((rows, D_MODEL), lambda i: (i, 0)),
        out_shape=jax.ShapeDtypeStruct((b * s, D_MODEL), F32),
        compiler_params=_cparams(("parallel",)),
        name="ffn_half_step",
    )(x.reshape(b * s, D_MODEL), *([mods] * n_sub), norm_g, w13, w2)
    return out.reshape(b, s, D_MODEL)


def _inproj_kernel(x_ref, xp_ref, xn_ref, *refs, n_sub, tiles_per_seq, n_ctx_tiles):
    mod_refs = refs[:n_sub]
    g_ref, w_ref, coef_ref, o_ref, hy_ref = refs[n_sub:]
    g = g_ref[2:3]

    def modulated(x, m):
        return _rms(x, g) * (1.0 + m[0, 4:5]) + m[0, 3:4]

    u_halo = jnp.concatenate([modulated(x_ref[t * TM:(t + 1) * TM], m) for t, m in enumerate(mod_refs)]
                             + [modulated(xp_ref[...], mod_refs[0]), modulated(xn_ref[...], mod_refs[-1])],
                             axis=0).astype(BF16)
    u = u_halo[:n_sub * TM]
    rows = lax.broadcasted_iota(jnp.int32, (n_sub * TM, 1), 0)
    keep_prev = jnp.ones((n_sub * TM, 1), F32)
    keep_next = jnp.ones((n_sub * TM, 1), F32)
    for t in range(n_sub):
        w = (pl.program_id(0) * n_sub + t) % tiles_per_seq
        seg_start = (w == 0) | (w == n_ctx_tiles)
        seg_end = (w == n_ctx_tiles - 1) | (w == tiles_per_seq - 1)
        keep_prev = jnp.where((rows == t * TM) & seg_start, 0.0, keep_prev)
        keep_next = jnp.where((rows == (t + 1) * TM - 1) & seg_end, 0.0, keep_next)
    n_rows = n_sub * TM
    for j in range(W_COLS // IN_CHUNK):
        cols = slice(j * IN_CHUNK, (j + 1) * IN_CHUNK)
        if not _SHIFT_CHUNKS[j]:
            p = jnp.dot(u, w_ref[:, cols], preferred_element_type=F32)
        else:
            p_halo = jnp.dot(u_halo, w_ref[:, cols], preferred_element_type=F32)
            p = p_halo[:n_rows]
            p_first = p_halo[n_rows + SUBLANES - 1:n_rows + SUBLANES]
            p_last = p_halo[n_rows + SUBLANES:n_rows + SUBLANES + 1]
            prev = jnp.where(rows == 0, p_first, pltpu.roll(p, 1, axis=0)) * keep_prev
            nxt = jnp.where(rows == n_sub * TM - 1, p_last, pltpu.roll(p, n_sub * TM - 1, axis=0)) * keep_next
            p = (coef_ref[0:1, cols] * p + coef_ref[1:2, cols] * prev + coef_ref[2:3, cols] * nxt
                 + coef_ref[3:4, cols])
        if j * IN_CHUNK < P_COLS:
            o_ref[:, cols] = p.astype(BF16)
        else:
            hy_ref[:, j * IN_CHUNK - P_HY:(j + 1) * IN_CHUNK - P_HY] = p


def _inproj(x, mods, norm_g, w_in_p, coef, *, n_ctx_tiles):
    b, s, _ = x.shape
    n_tiles = b * s // TM
    n_sub = _sub_tiles(n_tiles, INPROJ_SUB_TILES)
    rows = n_sub * TM
    r8 = rows // SUBLANES
    p, hy = pl.pallas_call(
        functools.partial(_inproj_kernel, n_sub=n_sub, tiles_per_seq=s // TM, n_ctx_tiles=n_ctx_tiles),
        grid=(n_tiles // n_sub,),
        in_specs=[pl.BlockSpec((rows, D_MODEL), lambda i: (i, 0)),
                  pl.BlockSpec((SUBLANES, D_MODEL), lambda i: (jnp.maximum(i * r8 - 1, 0), 0)),
                  pl.BlockSpec((SUBLANES, D_MODEL),
                               lambda i: (jnp.minimum((i + 1) * r8, b * s // SUBLANES - 1), 0)),
                  *_mod_specs(n_sub, s // TM, n_ctx_tiles, mods.shape[0] - 1),
                  pl.BlockSpec((6, D_MODEL), lambda i: (0, 0)),
                  pl.BlockSpec(memory_space=pltpu.VMEM),
                  pl.BlockSpec((4, W_COLS), lambda i: (0, 0))],
        out_specs=[pl.BlockSpec((rows, P_COLS), lambda i: (i, 0)),
                   pl.BlockSpec((rows, HYENA_COLS), lambda i: (i, 0))],
        out_shape=[jax.ShapeDtypeStruct((b * s, P_COLS), BF16),
                   jax.ShapeDtypeStruct((b * s, HYENA_COLS), F32)],
        compiler_params=_cparams(("parallel",)),
        name="in_projection",
    )(*([x.reshape(b * s, D_MODEL)] * 3), *([mods] * n_sub), norm_g, w_in_p, coef)
    return p.reshape(b, s, P_COLS), hy.reshape(b, s, HYENA_COLS)


def _shift_coefficients(rwkv_mu, hyena_conv, hyena_conv_b):
    mu = rwkv_mu.astype(F32)
    coef = jnp.zeros((4, W_COLS), F32).at[0].set(1.0)
    for off, sl in ((P_RKV, slice(0, 3 * RWKV_DIM)), (P_LORA, slice(3 * RWKV_DIM, RWKV_COLS))):
        width = sl.stop - sl.start
        coef = coef.at[0, off:off + width].set(1.0 - mu[0, sl] - mu[1, sl])
        coef = coef.at[1, off:off + width].set(mu[0, sl])
        coef = coef.at[2, off:off + width].set(mu[1, sl])
    hy = slice(P_HY, P_HY + HYENA_COLS)
    coef = coef.at[0, hy].set(hyena_conv[1]).at[1, hy].set(hyena_conv[0]).at[2, hy].set(hyena_conv[2])
    return coef.at[3, hy].set(hyena_conv_b)


def _permute_w_in(w_in):
    o_mla = GATE_COLS
    o_rwkv = o_mla + MLA_COLS
    o_hy = o_rwkv + RWKV_COLS
    o_swa = o_hy + HYENA_COLS
    z = lambda n: jnp.zeros((D_MODEL, n), w_in.dtype)
    parts = [
        w_in[:, :GATE_COLS],
        w_in[:, o_rwkv:o_rwkv + 3 * RWKV_DIM],
        w_in[:, o_swa:o_swa + SWA_HEADS * SWA_HEAD],
        w_in[:, o_rwkv + 3 * RWKV_DIM:o_rwkv + RWKV_COLS],
        w_in[:, o_mla:o_mla + MLA_Q_RANK],
        w_in[:, o_mla + MLA_Q_RANK:o_mla + MLA_Q_RANK + MLA_KV_RANK],
        z(MLA_NOPE), w_in[:, o_mla + MLA_Q_RANK + MLA_KV_RANK:o_mla + MLA_COLS],
        z(LANES - MLA_NOPE - MLA_ROPE),
        w_in[:, o_swa + SWA_HEADS * SWA_HEAD:o_swa + SWA_COLS],
        w_in[:, o_hy:o_hy + HYENA_COLS],
    ]
    out = jnp.concatenate(parts, axis=1)
    assert out.shape[1] == W_COLS
    return out


def _rope_tables(n_lat, n_ctx, rot_dim, lane0, period):
    rows = n_lat // GRID_W
    row = jnp.repeat(jnp.arange(rows, dtype=F32), GRID_W)
    col = jnp.tile(jnp.arange(GRID_W, dtype=F32), rows)
    axis_dim = rot_dim // 2
    h = axis_dim // 2
    inv_freq = ROPE_BASE ** (-jnp.arange(0, axis_dim, 2, dtype=F32) / axis_dim)
    ang_r = row[:, None] * inv_freq
    ang_c = col[:, None] * inv_freq
    cos_rot = jnp.concatenate([jnp.cos(ang_r)] * 2 + [jnp.cos(ang_c)] * 2, axis=1)
    zeros = jnp.zeros_like(ang_r)
    sin_a = jnp.concatenate([-jnp.sin(ang_r), zeros, -jnp.sin(ang_c), zeros], axis=1)
    sin_b = jnp.concatenate([zeros, jnp.sin(ang_r), zeros, jnp.sin(ang_c)], axis=1)

    def widen(t, fill):
        g = jnp.full((n_lat, period), fill, F32).at[:, lane0:lane0 + rot_dim].set(t)
        g = jnp.tile(g, (1, LANES // period))
        ctx = jnp.full((n_ctx, LANES), fill, F32)
        return jnp.concatenate([ctx, g], axis=0)

    return widen(cos_rot, 1.0), widen(sin_a, 0.0), widen(sin_b, 0.0), h


def _rope128(x, cos, sin_a, sin_b, h):
    return x * cos + pltpu.roll(x, LANES - h, axis=1) * sin_a + pltpu.roll(x, h, axis=1) * sin_b


LOG2E = math.log2(math.e)
MLA_SCALE = (MLA_NOPE + MLA_ROPE) ** -0.5 * LOG2E
MLA_ONE_LANE = (MLA_V, 0)


def _mla_prep_kernel(cq_ref, ckv_ref, kr_ref, gq_ref, gkv_ref, wq_ref, wk_ref, wv_ref, vone_ref,
                     cos_ref, sa_ref, sb_ref, q_ref, k_ref, v_ref, *, h):
    cos, sa, sb = cos_ref[...], sa_ref[...], sb_ref[...]
    cq = _rms(cq_ref[0].astype(F32), gq_ref[...]).astype(BF16)
    ckv = _rms(ckv_ref[0].astype(F32), gkv_ref[...]).astype(BF16)
    q = jnp.dot(cq, wq_ref[...], preferred_element_type=F32)
    k = jnp.dot(ckv, wk_ref[...], preferred_element_type=F32)
    kr = _rope128(kr_ref[0].astype(F32), cos, sa, sb, h)
    for hd in range(MLA_HEADS):
        sl = slice(hd * LANES, (hd + 1) * LANES)
        q_ref[0, :, sl] = (_rope128(q[:, sl], cos, sa, sb, h) * MLA_SCALE).astype(BF16)
        k_ref[0, :, sl] = (k[:, sl] + kr).astype(BF16)
    v_ref[0] = (jnp.dot(ckv, wv_ref[...], preferred_element_type=F32) + vone_ref[...]).astype(BF16)


def _mla_prep(p, norm_q, norm_kv, w_uq, w_ukv, tabs):
    b, s, _ = p.shape
    cos, sa, sb, h = tabs
    hq = MLA_NOPE + MLA_ROPE
    wq = jnp.zeros((MLA_Q_RANK, MLA_HEADS, LANES), F32).at[:, :, :hq].set(
        w_uq.reshape(MLA_Q_RANK, MLA_HEADS, hq)).reshape(MLA_Q_RANK, MLA_HEADS * LANES).astype(BF16)
    wkv = w_ukv.reshape(MLA_KV_RANK, MLA_HEADS, MLA_NOPE + MLA_V)
    wk = jnp.zeros((MLA_KV_RANK, MLA_HEADS, LANES), F32).at[:, :, :MLA_NOPE].set(
        wkv[:, :, :MLA_NOPE]).reshape(MLA_KV_RANK, MLA_HEADS * LANES).astype(BF16)
    wv_pairs = wkv[:, :, MLA_NOPE:].reshape(MLA_KV_RANK, MLA_HEADS // 2, 2, MLA_V)
    gap = ((0, 0), (0, 0), (0, LANES - MLA_V))
    wv = jnp.stack([jnp.pad(wv_pairs[:, :, 0], gap), jnp.pad(wv_pairs[:, :, 1], gap[:2] + (gap[2][::-1],))],
                   axis=2).reshape(MLA_KV_RANK, MLA_HEADS * LANES).astype(BF16)
    lane_id = np.arange(MLA_HEADS * LANES) % (2 * LANES)
    vone = jnp.asarray((lane_id == MLA_ONE_LANE[0]) | (lane_id == LANES + MLA_ONE_LANE[1]), F32)[None, :]
    full = lambda shape: pl.BlockSpec(shape, lambda bi, i: (0,) * len(shape))
    tab = pl.BlockSpec((TM, LANES), lambda bi, i: (i, 0))
    return pl.pallas_call(
        functools.partial(_mla_prep_kernel, h=h),
        grid=(b, s // TM),
        in_specs=[pl.BlockSpec((1, TM, MLA_Q_RANK), lambda bi, i: (bi, i, P_CQ // MLA_Q_RANK)),
                  pl.BlockSpec((1, TM, LANES), lambda bi, i: (bi, i, P_CKV // LANES)),
                  pl.BlockSpec((1, TM, LANES), lambda bi, i: (bi, i, P_KR // LANES)),
                  full((1, MLA_Q_RANK)), full((1, MLA_KV_RANK)),
                  full(wq.shape), full(wk.shape), full(wv.shape), full(vone.shape), tab, tab, tab],
        out_specs=[pl.BlockSpec((1, TM, MLA_HEADS * LANES), lambda bi, i: (bi, i, 0))] * 3,
        out_shape=[jax.ShapeDtypeStruct((b, s, MLA_HEADS * LANES), BF16)] * 3,
        compiler_params=_cparams(("parallel", "parallel")),
        name="mla_prep",
    )(p, p, p, norm_q.reshape(1, -1), norm_kv.reshape(1, -1), wq, wk, wv, vone, cos, sa, sb)


MLA_Q_TILES = 1


def _mla_attn_kernel(*refs, n_ctx, n_lat_steps):
    q_refs = refs[:MLA_Q_TILES]
    k_ref, v_ref, o_ref = refs[MLA_Q_TILES:]

    def attend(n_keys):
        outs = []
        for hd in range(2):
            sl = slice(hd * LANES, (hd + 1) * LANES)
            q = jnp.concatenate([q_ref[0, :, sl] for q_ref in q_refs], axis=0)
            s = lax.dot_general(q, k_ref[0, :n_keys, sl], (((1,), (1,)), ((), ())),
                                preferred_element_type=F32)
            e = jnp.exp2(s - jnp.max(s, axis=-1, keepdims=True)).astype(BF16)
            o = jnp.dot(e, v_ref[0, :n_keys, sl], preferred_element_type=F32)
            one = MLA_ONE_LANE[hd]
            outs.append(o / o[:, one:one + 1])
        lane = lax.broadcasted_iota(jnp.int32, outs[0].shape, 1)
        o_ref[0] = jnp.where(lane < MLA_V, outs[0], outs[1]).astype(BF16)

    @pl.when(pl.program_id(2) < n_lat_steps)
    def _():
        attend(k_ref.shape[1])

    @pl.when(pl.program_id(2) >= n_lat_steps)
    def _():
        attend(n_ctx)


def _mla_attention(q, k, v, *, n_ctx, with_ctx):
    b, s, _ = q.shape
    nct = n_ctx // TM
    n_lat = s - n_ctx
    n_lat_steps = n_lat // (MLA_Q_TILES * TM)
    assert n_lat % (MLA_Q_TILES * TM) == 0 and (nct == 1 or not with_ctx)

    def q_spec(t):
        return pl.BlockSpec((1, TM, 2 * LANES), lambda bi, hp, j: (
            bi, jnp.where(j < n_lat_steps, nct + j * MLA_Q_TILES + t, 0), hp))

    kv = pl.BlockSpec((1, s, 2 * LANES), lambda bi, hp, j: (bi, 0, hp))
    return pl.pallas_call(
        functools.partial(_mla_attn_kernel, n_ctx=n_ctx, n_lat_steps=n_lat_steps),
        grid=(b, MLA_HEADS // 2, n_lat_steps + (1 if with_ctx else 0)),
        in_specs=[*[q_spec(t) for t in range(MLA_Q_TILES)], kv, kv],
        out_specs=pl.BlockSpec((1, MLA_Q_TILES * TM, LANES), lambda bi, hp, j: (bi, j, hp)),
        out_shape=jax.ShapeDtypeStruct((b, n_lat + (n_ctx if with_ctx else 0), MLA_HEADS * MLA_V), BF16),
        compiler_params=_cparams(("parallel", "parallel", "parallel")),
        name="mla_attention",
    )(*([q] * MLA_Q_TILES), k, v)


SWA_SCALE = SWA_HEAD ** -0.5 * LOG2E
SWA_TQ = 128
SWA_ONE_LANE = (SWA_HEAD, 0)


def _swa_prep_kernel(q_ref, k_ref, v_ref, cos_ref, sa_ref, sb_ref, qo_ref, ko_ref, vo_ref, *, h):
    cos, sa, sb = cos_ref[...], sa_ref[...], sb_ref[...]
    lane = lax.broadcasted_iota(jnp.int32, cos.shape, 1)
    low = lane < SWA_HEAD
    for j in range(SWA_HEADS // 2):
        blk = _rope128(q_ref[0, :, j * LANES:(j + 1) * LANES].astype(F32), cos, sa, sb, h) * SWA_SCALE
        qo_ref[0, :, (2 * j) * LANES:(2 * j + 1) * LANES] = jnp.where(low, blk, 0.0).astype(BF16)
        qo_ref[0, :, (2 * j + 1) * LANES:(2 * j + 2) * LANES] = jnp.where(
            low, pltpu.roll(blk, SWA_HEAD, axis=1), 0.0).astype(BF16)
    kb = _rope128(k_ref[0].astype(F32), cos, sa, sb, h)
    ko_ref[0, :, :LANES] = jnp.where(low, kb, 0.0).astype(BF16)
    ko_ref[0, :, LANES:] = jnp.where(low, pltpu.roll(kb, SWA_HEAD, axis=1), 0.0).astype(BF16)
    vb = v_ref[0].astype(F32)
    vr = pltpu.roll(vb, SWA_HEAD, axis=1)
    one_lo = jnp.where(lane == SWA_ONE_LANE[0], 1.0, 0.0)
    one_hi = jnp.where(lane == SWA_ONE_LANE[1], 1.0, 0.0)
    vo_ref[0, :, 0 * LANES:1 * LANES] = jnp.where(low, vb, one_lo).astype(BF16)
    vo_ref[0, :, 1 * LANES:2 * LANES] = jnp.where(low, one_hi, vr).astype(BF16)
    vo_ref[0, :, 2 * LANES:3 * LANES] = jnp.where(low, vr, one_lo).astype(BF16)
    vo_ref[0, :, 3 * LANES:4 * LANES] = jnp.where(low, one_hi, vb).astype(BF16)


def _swa_prep(p, tabs):
    b, s, _ = p.shape
    cos, sa, sb, h = tabs
    tab = pl.BlockSpec((TM, LANES), lambda bi, i: (i, 0))
    nq = SWA_HEADS * SWA_HEAD
    return pl.pallas_call(
        functools.partial(_swa_prep_kernel, h=h),
        grid=(b, s // TM),
        in_specs=[pl.BlockSpec((1, TM, nq), lambda bi, i: (bi, i, P_SWAQ // nq)),
                  pl.BlockSpec((1, TM, LANES), lambda bi, i: (bi, i, P_SWAK // LANES)),
                  pl.BlockSpec((1, TM, LANES), lambda bi, i: (bi, i, P_SWAV // LANES)),
                  tab, tab, tab],
        out_specs=[pl.BlockSpec((1, TM, SWA_HEADS * LANES), lambda bi, i: (bi, i, 0)),
                   pl.BlockSpec((1, TM, SWA_KV_HEADS * LANES), lambda bi, i: (bi, i, 0)),
                   pl.BlockSpec((1, TM, 4 * LANES), lambda bi, i: (bi, i, 0))],
        out_shape=[jax.ShapeDtypeStruct((b, s, SWA_HEADS * LANES), BF16),
                   jax.ShapeDtypeStruct((b, s, SWA_KV_HEADS * LANES), BF16),
                   jax.ShapeDtypeStruct((b, s, 4 * LANES), BF16)],
        compiler_params=_cparams(("parallel", "parallel")),
        name="swa_prep",
    )(p, p, p, cos, sa, sb)


def _swa_attn_kernel(sink_ref, q_ref, k_ref, v_ref, o_ref, *, n_ctx, q_off):
    i = pl.program_id(1) + q_off
    s_len = k_ref.shape[1]
    tq = SWA_TQ
    n_loc = tq + 2 * WINDOW
    r0 = i * tq
    is_lat = r0 >= n_ctx
    start = pl.multiple_of(jnp.clip(r0 - WINDOW, 0, s_len - n_loc), LANES)
    rows_g = SWA_GROUP * tq
    row = lax.broadcasted_iota(jnp.int32, (rows_g, n_loc), 0)
    qpos = r0 - n_ctx + row % tq
    kpos = start - n_ctx + lax.broadcasted_iota(jnp.int32, (rows_g, n_loc), 1)
    loc_ok = (jnp.abs(kpos - qpos) <= WINDOW) & (kpos >= 0) & is_lat
    k_loc = k_ref[0, pl.ds(start, n_loc), :]
    v_loc = v_ref[0, pl.ds(start, n_loc), :]
    k_ctx = k_ref[0, 0:n_ctx, :]
    v_ctx = v_ref[0, 0:n_ctx, :]
    head_row = lax.broadcasted_iota(jnp.int32, (rows_g, 1), 0) // tq
    lane = lax.broadcasted_iota(jnp.int32, (tq, LANES), 1)
    stages = []
    for g in range(SWA_KV_HEADS):
        q = jnp.concatenate([q_ref[0, :, hd * LANES:(hd + 1) * LANES]
                             for hd in range(g * SWA_GROUP, (g + 1) * SWA_GROUP)], axis=0)
        kg = slice(g * LANES, (g + 1) * LANES)
        s_loc = lax.dot_general(q, k_loc[:, kg], (((1,), (1,)), ((), ())), preferred_element_type=F32)
        s_ctx = lax.dot_general(q, k_ctx[:, kg], (((1,), (1,)), ((), ())), preferred_element_type=F32)
        sink = jnp.zeros((rows_g, 1), F32)
        for hh in range(SWA_GROUP):
            sink = jnp.where(head_row == hh, sink_ref[g * SWA_GROUP + hh] * LOG2E, sink)
        stages.append((jnp.where(loc_ok, s_loc, NEG_INF), s_ctx, sink))
    for g, (s_loc, s_ctx, sink) in enumerate(stages):
        m = jnp.maximum(jnp.maximum(jnp.max(s_loc, axis=-1, keepdims=True),
                                    jnp.max(s_ctx, axis=-1, keepdims=True)), sink)
        e = jnp.concatenate([jnp.exp2(s_loc - m), jnp.exp2(s_ctx - m)], axis=1).astype(BF16)
        e_sink = jnp.exp2(sink - m)
        outs = []
        for par in range(2):
            vg = slice((2 * g + par) * LANES, (2 * g + par + 1) * LANES)
            o = jnp.dot(e, jnp.concatenate([v_loc[:, vg], v_ctx[:, vg]], axis=0), preferred_element_type=F32)
            one = SWA_ONE_LANE[par]
            outs.append(o / (o[:, one:one + 1] + e_sink))
        for pi in range(SWA_GROUP // 2):
            even = outs[0][(2 * pi) * tq:(2 * pi + 1) * tq]
            odd = outs[1][(2 * pi + 1) * tq:(2 * pi + 2) * tq]
            blk = g * (SWA_GROUP // 2) + pi
            o_ref[0, :, blk * LANES:(blk + 1) * LANES] = jnp.where(lane < SWA_HEAD, even, odd).astype(BF16)


def _swa_attention(q, k, v, sink, *, n_ctx, q_off):
    b, s, _ = q.shape
    nq = s // SWA_TQ - q_off
    return pl.pallas_call(
        functools.partial(_swa_attn_kernel, n_ctx=n_ctx, q_off=q_off),
        grid=(b, nq),
        in_specs=[pl.BlockSpec(memory_space=pltpu.SMEM),
                  pl.BlockSpec((1, SWA_TQ, SWA_HEADS * LANES), lambda bi, i: (bi, i + q_off, 0)),
                  pl.BlockSpec((1, s, SWA_KV_HEADS * LANES), lambda bi, i: (bi, 0, 0)),
                  pl.BlockSpec((1, s, 4 * LANES), lambda bi, i: (bi, 0, 0))],
        out_specs=pl.BlockSpec((1, SWA_TQ, SWA_HEADS * SWA_HEAD), lambda bi, i: (bi, i, 0)),
        out_shape=jax.ShapeDtypeStruct((b, nq * SWA_TQ, SWA_HEADS * SWA_HEAD), BF16),
        compiler_params=_cparams(("parallel", "parallel")),
        name="swa_attention",
    )(sink, q, k, v)


N_PAIR = RWKV_HEADS // 2
RWKV_NB = 4
N_DOUBLINGS = int(math.log2(CHUNK))


def _softplus(x):
    return jnp.maximum(x, 0.0) + jnp.log(1.0 + jnp.exp(-jnp.abs(x)))


def _headsum(x, bd):
    hi = x.astype(BF16)
    lo = (x - hi.astype(F32)).astype(BF16)
    return (jnp.dot(hi, bd, preferred_element_type=F32) + jnp.dot(lo, bd, preferred_element_type=F32))


def _chunk_cumsum(x, reverse):
    rows = lax.broadcasted_iota(jnp.int32, x.shape, 0)
    s = 1
    while s < CHUNK:
        if reverse:
            x = x + jnp.where(rows < CHUNK - s, pltpu.roll(x, CHUNK - s, axis=0), 0.0)
        else:
            x = x + jnp.where(rows >= s, pltpu.roll(x, s, axis=0), 0.0)
        s *= 2
    return x


def _head_rows(x):
    first = lax.broadcasted_iota(jnp.int32, x.shape, 1) < RWKV_HEAD
    return jnp.concatenate([jnp.where(first, x, 0.0), jnp.where(first, 0.0, x)], axis=0)


def _mm_x3(a, b):
    a_hi = a.astype(BF16)
    b_hi = b.astype(BF16)
    a_lo = (a - a_hi.astype(F32)).astype(BF16)
    b_lo = (b - b_hi.astype(F32)).astype(BF16)
    dot = functools.partial(jnp.dot, preferred_element_type=F32)
    return dot(a_hi, b_hi) + dot(a_hi, b_lo) + dot(a_lo, b_hi)


def _rwkv_chunk_kernel(rf_ref, kf_ref, vf_ref, lof_ref, rb_ref, kb_ref, vb_ref, lob_ref,
                       kvec_ref, w0_ref, a0_ref, wup_ref, aup_ref, gup_ref, rk_ref, bd_ref,
                       yf_ref, yb_ref, bonus_ref, gate_ref, s_ref):
    @pl.when(pl.program_id(1) == 0)
    def _():
        s_ref[...] = jnp.zeros_like(s_ref)

    bd = bd_ref[...]
    n_seq = rf_ref.shape[0]
    data = []
    for nb in range(n_seq):
        per_dir = []
        for refs in ((rf_ref, kf_ref, vf_ref, lof_ref), (rb_ref, kb_ref, vb_ref, lob_ref)):
            r, k, v, lora = (ref[nb].astype(F32) for ref in refs)
            kk = k * kvec_ref[0:1]
            kk = kk * lax.rsqrt(_headsum(kk * kk, bd) + 1e-12)
            per_dir.append((r, k, v, lora, kk))
        data.append(per_dir)
        r, k, v, lora, _ = per_dir[0]
        gate_ref[nb] = _mm(_sigmoid(lora), gup_ref[...])
        k_both = sum(k * (1.0 + (_sigmoid(a0_ref[d] + _mm(lora, aup_ref[d])) - 1.0) * kvec_ref[1:2])
                     for d in range(2))
        bonus_ref[nb] = _headsum(r * (0.5 * k_both) * rk_ref[...], bd) * v

    trow = lax.broadcasted_iota(jnp.int32, (2 * CHUNK, 4 * CHUNK), 0) % CHUNK
    tcol = lax.broadcasted_iota(jnp.int32, (2 * CHUNK, 4 * CHUNK), 1) % CHUNK
    sq_r = lax.broadcasted_iota(jnp.int32, (LANES, LANES), 0)
    sq_c = lax.broadcasted_iota(jnp.int32, (LANES, LANES), 1)
    same_head = (sq_r // RWKV_HEAD) == (sq_c // RWKV_HEAD)
    eye = sq_r == sq_c

    chains = []
    for nb, d in ((nb, d) for nb in range(n_seq) for d in range(2)):
        reverse = d == 1
        r, k, v, lora, kk = data[nb][d]
        w_log = -_softplus(-(w0_ref[d] + _mm(jnp.tanh(lora), wup_ref[d]))) - 0.5
        ld = -jnp.exp(w_log)
        a = _sigmoid(a0_ref[d] + _mm(lora, aup_ref[d]))
        k_d = k * (1.0 + (a - 1.0) * kvec_ref[1:2])
        b_d = kk * a
        lg = _chunk_cumsum(ld, reverse)
        last = 0 if reverse else CHUNK - 1
        tot = lg[last:last + 1]
        e_neg = jnp.exp(-lg)
        e_end = jnp.exp(tot - lg)
        z_t = -kk * jnp.exp(lg - ld)
        r_t = r * jnp.exp(lg)
        b_t = b_d * e_neg
        k_t = k_d * e_neg
        b_e = b_d * e_end
        k_e = k_d * e_end
        e_tot = jnp.exp(tot)
        before = (tcol > trow) if reverse else (tcol < trow)
        before_eq = (tcol >= trow) if reverse else (tcol <= trow)
        for pr in range(N_PAIR):
            sl = slice(pr * LANES, (pr + 1) * LANES)
            ch = {"nb": nb, "d": d, "sl": sl, "rp": r_t[:, sl], "vp": v[:, sl], "e_tot": e_tot[:, sl],
                  "be_ke": jnp.concatenate([b_e[:, sl], k_e[:, sl]], axis=0)}
            zst, rst, vst = _head_rows(z_t[:, sl]), _head_rows(r_t[:, sl]), _head_rows(v[:, sl])
            bkst = jnp.concatenate([_head_rows(b_t[:, sl]), _head_rows(k_t[:, sl])], axis=0)
            ch["lz"] = jnp.where(before, _mm_nt(zst, bkst), 0.0)
            ch["lr"] = jnp.where(before_eq, _mm_nt(rst, bkst), 0.0)
            ch["zst"], ch["vst"] = zst, vst
            chains.append(ch)

    unit = jnp.where(eye, 1.0, 0.0)
    for ch in chains:
        ch["pw"] = ch["lz"][:, :LANES]
        ch["t"] = unit + ch["pw"]
        ch["x"] = jnp.concatenate([ch["zst"], _mm(ch["lz"][:, LANES:], ch["vst"])], axis=1)
    for it in range(1, N_DOUBLINGS):
        for ch in chains:
            ch["pw"] = _mm(ch["pw"], ch["pw"])
        for ch in chains:
            ch["t"] = ch["t"] + _mm(ch["pw"], ch["t"])
    for ch in chains:
        ch["x"] = _mm(ch["t"], ch["x"])
    for ch in chains:
        low = jnp.concatenate([jnp.zeros_like(ch["vst"]), ch["vst"]], axis=1)
        op = _mm(ch["lr"], jnp.concatenate([ch["x"], low], axis=0))
        ch["op"] = op[:CHUNK] + op[CHUNK:]
        ch["xp"] = ch["x"][:CHUNK] + ch["x"][CHUNK:]
    for ch in chains:
        rhs = jnp.concatenate([ch["xp"], jnp.concatenate([jnp.zeros_like(ch["vp"]), ch["vp"]], axis=1)], axis=0)
        ag = _mm_tn(ch["be_ke"], rhs)
        a_full = ag[:, :LANES] + jnp.where(eye, jnp.broadcast_to(ch["e_tot"], (LANES, LANES)), 0.0)
        ch["a"] = jnp.where(same_head, a_full, 0.0)
        ch["g"] = jnp.where(same_head, ag[:, LANES:], 0.0)
    for idx, ch in enumerate(chains):
        st = s_ref[idx]
        y_ref = yf_ref if ch["d"] == 0 else yb_ref
        y_ref[ch["nb"], :, ch["sl"]] = _mm(ch["rp"] + ch["op"][:, :LANES], st) + ch["op"][:, LANES:]
        s_ref[idx] = _mm_x3(ch["a"], st) + ch["g"]


def _head_block_diag():
    idx = np.arange(RWKV_DIM) // RWKV_HEAD
    return jnp.asarray(idx[:, None] == idx[None, :], BF16)


def _rwkv_chunks(p, kvec, w0, a0, w_up, a_up, g_up, r_k, *, n_ctx):
    b, s, _ = p.shape
    nc = s // CHUNK
    ncc = n_ctx // CHUNK
    lora_w = DECAY_LORA + AAA_LORA + GATE_LORA
    wup = jnp.zeros((2, lora_w, RWKV_DIM), F32).at[:, :DECAY_LORA].set(w_up).astype(BF16)
    aup = jnp.zeros((2, lora_w, RWKV_DIM), F32).at[:, DECAY_LORA:DECAY_LORA + AAA_LORA].set(a_up).astype(BF16)
    gup = jnp.zeros((lora_w, RWKV_DIM), F32).at[DECAY_LORA + AAA_LORA:].set(g_up).astype(BF16)

    def fwd(c):
        return c

    def back(c):
        return jnp.where(c < ncc, ncc - 1 - c, nc - 1 - (c - ncc))

    full = lambda shape: pl.BlockSpec(shape, lambda bi, c: (0,) * len(shape))

    n_seq = _sub_tiles(b, RWKV_NB)

    def inputs(chunk):
        col = lambda off: pl.BlockSpec((n_seq, CHUNK, RWKV_DIM), lambda bi, c: (bi, chunk(c), off // RWKV_DIM))
        return [col(P_RKV), col(P_RKV + RWKV_DIM), col(P_RKV + 2 * RWKV_DIM),
                pl.BlockSpec((n_seq, CHUNK, lora_w), lambda bi, c: (bi, chunk(c), P_LORA // lora_w))]

    tok = lambda chunk: pl.BlockSpec((n_seq, CHUNK, RWKV_DIM), lambda bi, c: (bi, chunk(c), 0))
    return pl.pallas_call(
        _rwkv_chunk_kernel,
        grid=(b // n_seq, nc),
        in_specs=[*inputs(fwd), *inputs(back),
                  full((2, RWKV_DIM)), full((2, 1, RWKV_DIM)), full((2, 1, RWKV_DIM)),
                  full(wup.shape), full(aup.shape), full(gup.shape), full((1, RWKV_DIM)),
                  full((RWKV_DIM, RWKV_DIM))],
        out_specs=[tok(fwd), tok(back), tok(fwd), tok(fwd)],
        out_shape=[jax.ShapeDtypeStruct((b, s, RWKV_DIM), F32)] * 4,
        scratch_shapes=[pltpu.VMEM((n_seq * 2 * N_PAIR, LANES, LANES), F32)],
        compiler_params=_cparams(("parallel", "arbitrary")),
        name="rwkv_chunks",
    )(*([p] * 8), kvec, w0.reshape(2, 1, -1), a0.reshape(2, 1, -1), wup, aup, gup, r_k.reshape(1, -1),
      _head_block_diag())


DFT_N2 = LANES
HY_MIN_LEN = 1024
HY_CT = 128


def _hyena_mlp_kernel(feats_ref, w1_ref, b1_ref, w2_ref, b2_ref, freq_ref, h_ref):
    h = jnp.sin(freq_ref[0:1] * (_mm_f32(feats_ref[0], w1_ref[...]) + b1_ref[...]))
    h_ref[0] = jnp.sin(freq_ref[1:2] * (_mm_f32(h, w2_ref[...]) + b2_ref[...]))


def _hyena_filter_kernel(h_ref, w3f_ref, w3b_ref, t_ref, delta_ref, k_ref):
    hf = _mm_f32(h_ref[0], w3f_ref[...]) * jnp.exp(-t_ref[0] * delta_ref[...])
    hb = _mm_f32(h_ref[1], w3b_ref[...]) * jnp.exp(-t_ref[1] * delta_ref[...])
    norm = (jnp.sum(jnp.abs(hf), axis=0, keepdims=True) + jnp.sum(jnp.abs(hb), axis=0, keepdims=True))
    r = pl.program_id(2)
    rows = lax.broadcasted_iota(jnp.int32, hf.shape, 0)
    tail = jnp.where(rows == 0, 0.0, hb)
    blk = jnp.where(r == 0, hf, jnp.where(r == pl.num_programs(2) - 1, tail, 0.0))
    k_ref[0] = blk / norm


def _hyena_filter_buffer(n, nc, w1, b1, w2, b2, w3, freq):
    lag = jnp.stack([jnp.arange(n), jnp.where(jnp.arange(n) == 0, 0, n - jnp.arange(n))]).astype(F32)
    t = (lag / (n - 1))[:, :, None]
    bands = jnp.linspace(1e-4, HYENA_BANDS - 1, HYENA_BANDS, dtype=F32)
    ang = (2.0 * math.pi / n) * lag[:, :, None] * bands[None, None, :]
    feats = jnp.concatenate([t, jnp.cos(ang), -jnp.sin(ang),
                             jnp.zeros((2, n, HYENA_FW - HYENA_EMB), F32)], axis=-1)
    w1p = jnp.zeros((HYENA_FW, HYENA_FW), F32).at[:HYENA_EMB].set(w1)
    deltas = jnp.abs(jnp.linspace(math.log(HYENA_TARGET) / HYENA_SLOW,
                                  math.log(HYENA_TARGET) / HYENA_FAST, HYENA_DIM, dtype=F32))[None, :]
    fixed = lambda shape: pl.BlockSpec(shape, lambda d: (0,) * len(shape))
    hidden = pl.pallas_call(
        _hyena_mlp_kernel,
        grid=(2,),
        in_specs=[pl.BlockSpec((1, n, HYENA_FW), lambda d: (d, 0, 0)), fixed((HYENA_FW, HYENA_FW)),
                  fixed((1, HYENA_FW)), fixed((HYENA_FW, HYENA_FW)), fixed((1, HYENA_FW)),
                  fixed((2, HYENA_FW))],
        out_specs=pl.BlockSpec((1, n, HYENA_FW), lambda d: (d, 0, 0)),
        out_shape=jax.ShapeDtypeStruct((2, n, HYENA_FW), F32),
        compiler_params=_cparams(("parallel",)),
        name="hyena_filter_mlp",
    )(feats, w1p, b1.reshape(1, -1), w2, b2.reshape(1, -1), freq)
    tc = 256
    nj = HYENA_DIM // tc
    full = lambda shape: pl.BlockSpec(shape, lambda o, j, r: (0,) * len(shape))
    return pl.pallas_call(
        _hyena_filter_kernel,
        grid=(HYENA_ORDER, nj, nc // n),
        in_specs=[full((2, n, HYENA_FW)),
                  pl.BlockSpec((HYENA_FW, tc), lambda o, j, r: (0, o * 2 * nj + j)),
                  pl.BlockSpec((HYENA_FW, tc), lambda o, j, r: (0, o * 2 * nj + nj + j)),
                  full((2, n, 1)), pl.BlockSpec((1, tc), lambda o, j, r: (0, j))],
        out_specs=pl.BlockSpec((1, n, tc), lambda o, j, r: (o, r, j)),
        out_shape=jax.ShapeDtypeStruct((HYENA_ORDER, nc, HYENA_DIM), F32),
        compiler_params=_cparams(("parallel", "parallel", "parallel")),
        name="hyena_filters",
    )(hidden, w3, w3, t, deltas)


HY_J = SUBLANES
HY_GROUPS = DFT_N2 // HY_J


def _dft_tables(n1):
    nc = n1 * DFT_N2
    f1 = np.arange(n1)
    ang = 2.0 * np.pi * ((f1[:, None] * f1[None, :]) % n1) / n1
    eye = np.eye(HY_J)
    w1 = np.kron(np.concatenate([np.cos(ang), -np.sin(ang)], axis=0), eye)
    v3 = np.kron(np.concatenate([np.cos(ang), -np.sin(ang)], axis=1), eye) / nc
    t2 = np.arange(DFT_N2).reshape(HY_GROUPS, 1, HY_J)
    tw = 2.0 * np.pi * ((f1[None, :, None] * t2) % nc) / nc
    tw = np.broadcast_to(tw.reshape(HY_GROUPS, n1 * HY_J, 1), (HY_GROUPS, n1 * HY_J, LANES))
    k = np.arange(DFT_N2)
    phi = 2.0 * np.pi * ((k[:, None] * k[None, :]) % DFT_N2) / DFT_N2
    c, s = np.cos(phi), np.sin(phi)
    f_fwd = np.block([[c, s], [-s, c]])
    f_inv = np.block([[c, -s], [s, c]])
    names = ("w1", "v3", "ctw", "stw", "f_fwd", "f_inv")
    return {n: jnp.asarray(t, BF16) for n, t in zip(names, (w1, v3, np.cos(tw), np.sin(tw), f_fwd, f_inv))}


def _tiles(ref, rows, g, lead=()):
    return jnp.concatenate([ref[lead + (pl.ds(r * DFT_N2 + g * HY_J, HY_J), slice(None))] for r in rows], axis=0)


def _dft_rows_in(x_ref, w_ref, ctw_ref, stw_ref, a_ref, t1n, n1, t1_valid):
    half = n1 * HY_J
    for g in range(HY_GROUPS):
        xg = _tiles(x_ref, range(t1_valid), g, lead=(0,))
        pq = _mm(w_ref[:, :t1_valid * HY_J], xg)
        p, q = pq[:half], pq[half:]
        c, s = ctw_ref[g].astype(F32), stw_ref[g].astype(F32)
        re = c * p + s * q
        im = c * q - s * p
        for m in range(n1):
            rows = slice(m * HY_J, (m + 1) * HY_J)
            a_ref[pl.ds(m * DFT_N2 + g * HY_J, HY_J), :] = re[rows]
            a_ref[pl.ds((n1 + m) * DFT_N2 + g * HY_J, HY_J), :] = im[rows]


def _slab(f1):
    return slice(f1 * DFT_N2, (f1 + 1) * DFT_N2)


def _spectrum_kernel(x_ref, w_ref, ctw_ref, stw_ref, ff_ref, k_ref, a_ref, *, n1):
    _dft_rows_in(x_ref, w_ref, ctw_ref, stw_ref, a_ref, n1, n1, n1)
    for f1 in range(n1):
        a = jnp.concatenate([a_ref[_slab(f1)], a_ref[_slab(n1 + f1)]], axis=0)
        k_ref[0, f1] = _mm(ff_ref[...], a)


def _filter_spectrum(kbuf, tabs, n1):
    no, nc, _ = kbuf.shape
    nj = HYENA_DIM // HY_CT
    const = lambda a: pl.BlockSpec(a.shape, lambda o, j: (0,) * a.ndim)
    consts = [tabs[n] for n in ("w1", "ctw", "stw", "f_fwd")]
    return pl.pallas_call(
        functools.partial(_spectrum_kernel, n1=n1),
        grid=(no, nj),
        in_specs=[pl.BlockSpec((1, nc, HY_CT), lambda o, j: (o, 0, j)), *[const(a) for a in consts]],
        out_specs=pl.BlockSpec((1, n1, 2 * DFT_N2, HY_CT), lambda o, j: (o, 0, 0, j)),
        out_shape=jax.ShapeDtypeStruct((no, n1, 2 * DFT_N2, HYENA_DIM), F32),
        scratch_shapes=[pltpu.VMEM((2 * n1 * DFT_N2, HY_CT), F32)],
        compiler_params=_cparams(("parallel", "parallel")),
        name="hyena_filter_spectrum",
    )(kbuf, *consts)


def _hyena_conv_kernel(x_ref, gate_ref, k_ref, w_ref, ctw_ref, stw_ref, ff_ref, fi_ref, v_ref, bias_ref,
                       o_ref, a_ref, *, t1n, n1, t1_valid):
    _dft_rows_in(x_ref, w_ref, ctw_ref, stw_ref, a_ref, t1n, n1, t1_valid)
    for f1 in range(n1):
        re, im = _slab(f1), _slab(n1 + f1)
        x = _mm(ff_ref[...], jnp.concatenate([a_ref[re], a_ref[im]], axis=0))
        xre, xim = x[:DFT_N2], x[DFT_N2:]
        kre, kim = k_ref[0, f1, :DFT_N2], k_ref[0, f1, DFT_N2:]
        bm = _mm(fi_ref[...], jnp.concatenate([xre * kre - xim * kim, xre * kim + xim * kre], axis=0))
        a_ref[re] = bm[:DFT_N2]
        a_ref[im] = bm[DFT_N2:]
    bias = bias_ref[...]
    for g in range(HY_GROUPS):
        br, bi = _tiles(a_ref, range(n1), g), _tiles(a_ref, range(n1, 2 * n1), g)
        c, s = ctw_ref[g].astype(F32), stw_ref[g].astype(F32)
        y = _mm(v_ref[:t1n * HY_J], jnp.concatenate([c * br - s * bi, s * br + c * bi], axis=0))
        for t1 in range(t1n):
            rows = pl.ds(t1 * DFT_N2 + g * HY_J, HY_J)
            o_ref[0, rows, :] = gate_ref[0, rows, :] * (y[t1 * HY_J:(t1 + 1) * HY_J] + bias * x_ref[0, rows, :])


def _hyena_conv(z, z_spec, gate, gate_spec, kspec, order, bias, tabs, n1, n_pad, n):
    bx = z.shape[0]
    t1n = n_pad // DFT_N2
    consts = [tabs[name] for name in ("w1", "ctw", "stw", "f_fwd", "f_inv", "v3")]
    const = lambda a: pl.BlockSpec(a.shape, lambda j, bi: (0,) * a.ndim)
    return pl.pallas_call(
        functools.partial(_hyena_conv_kernel, t1n=t1n, n1=n1, t1_valid=-(-n // DFT_N2)),
        grid=(HYENA_DIM // HY_CT, bx),
        in_specs=[z_spec, gate_spec,
                  pl.BlockSpec((1, n1, 2 * DFT_N2, HY_CT), lambda j, bi: (order, 0, 0, j)),
                  *[const(a) for a in consts],
                  pl.BlockSpec((1, HY_CT), lambda j, bi: (0, j))],
        out_specs=pl.BlockSpec((1, n_pad, HY_CT), lambda j, bi: (bi, 0, j)),
        out_shape=jax.ShapeDtypeStruct((bx, n_pad, HYENA_DIM), F32),
        scratch_shapes=[pltpu.VMEM((2 * n1 * DFT_N2, HY_CT), F32)],
        compiler_params=_cparams(("parallel", "parallel")),
        name="hyena_conv",
    )(z, gate, kspec, *consts, bias.reshape(1, -1))


def _hyena_operator(p_hy, row0, n, filt_params, bias):
    n_pad = max(n, HY_MIN_LEN)
    nc = 2 * n_pad
    n1 = nc // DFT_N2
    tabs = _dft_tables(n1)
    kspec = _filter_spectrum(_hyena_filter_buffer(n, nc, *filt_params), tabs, n1)
    nj = HYENA_DIM // HY_CT
    window = lambda part: pl.BlockSpec((pl.Element(1), pl.Element(n_pad), pl.Element(HY_CT)),
                                       lambda j, bi: (bi, row0, (part * nj + j) * HY_CT))
    own = pl.BlockSpec((1, n_pad, HY_CT), lambda j, bi: (bi, 0, j))
    z = _hyena_conv(p_hy, window(0), p_hy, window(1), kspec, 0, bias[0], tabs, n1, n_pad, n)
    return _hyena_conv(z, own, p_hy, window(2), kspec, 1, bias[1], tabs, n1, n_pad, n)


def _rwkv_readout(y, bonus, gate, ln_g, ln_b, bd):
    inv_n = 1.0 / RWKV_HEAD
    dev = y - _headsum(y, bd) * inv_n
    var = _headsum(dev * dev, bd) * inv_n
    return (dev * lax.rsqrt(var + RWKV_LN_EPS) * ln_g + ln_b + bonus) * gate


def _merge_kernel(x_ref, mod_ref, g_ref, gates_ref, bg_ref, ya_ref, yf_ref, yr_ref, bonus_ref, rgate_ref,
                  lng_ref, lnb_ref, bd_ref, yh_ref, yhc_ref, yd_ref, wb_ref, wo_ref, o_ref, *, n_ctx_tiles, row_off):
    yb = _rwkv_readout(yf_ref[0] + yr_ref[0], bonus_ref[0], rgate_ref[0], lng_ref[...], lnb_ref[...], bd_ref[...])
    is_ctx = pl.program_id(1) + row_off < n_ctx_tiles
    yh = jnp.where(is_ctx, yhc_ref[0], yh_ref[0])
    merged = None
    for br, y in enumerate((ya_ref[0], yb, yh, yd_ref[0])):
        gate = _sigmoid(gates_ref[0, :, br * D_MODEL:(br + 1) * D_MODEL] + bg_ref[br:br + 1])
        term = gate * jnp.dot(y.astype(BF16), wb_ref[br], preferred_element_type=F32)
        merged = term if merged is None else merged + term
    y = jnp.dot(merged.astype(BF16), wo_ref[...], preferred_element_type=F32)
    o_ref[0] = x_ref[0] + mod_ref[0, 5:6] * _rms(y, g_ref[3:4])


def _merge(x, mods, norm_g, p, b_gate, ya, rwkv, yh, yh_ctx, yd, w_branch, w_out, *, n_ctx_tiles, row_off):
    b, s, _ = x.shape
    nt = s // TM - row_off
    n_lat = mods.shape[0] - 1
    br = lambda: pl.BlockSpec((1, TM, BRANCH_DIM), lambda bi, i: (bi, i, 0))
    tok = pl.BlockSpec((1, TM, BRANCH_DIM), lambda bi, i: (bi, i + row_off, 0))
    row = pl.BlockSpec((1, RWKV_DIM), lambda bi, i: (0, 0))
    y_fwd, y_bwd, bonus, rgate, ln_g, ln_b = rwkv
    n_lat_tiles = s // TM - n_ctx_tiles
    br_a = pl.BlockSpec((1, TM, BRANCH_DIM), lambda bi, i: (
        bi, jnp.where(i + row_off < n_ctx_tiles, n_lat_tiles + i, i + row_off - n_ctx_tiles), 0))
    br_h = pl.BlockSpec((1, TM, BRANCH_DIM), lambda bi, i: (bi, jnp.maximum(i + row_off - n_ctx_tiles, 0), 0))
    if yh_ctx is None:
        br_hc = pl.BlockSpec((1, TM, BRANCH_DIM), lambda bi, i: (bi, 0, 0))
    else:
        br_hc = pl.BlockSpec((1, TM, BRANCH_DIM), lambda bi, i: (bi, jnp.minimum(i + row_off, n_ctx_tiles - 1), 0))
    return pl.pallas_call(
        functools.partial(_merge_kernel, n_ctx_tiles=n_ctx_tiles if yh_ctx is not None else 0, row_off=row_off),
        grid=(b, nt),
        in_specs=[pl.BlockSpec((1, TM, D_MODEL), lambda bi, i: (bi, i + row_off, 0)),
                  pl.BlockSpec((1, N_MOD, D_MODEL),
                               lambda bi, i: (jnp.where(i + row_off < n_ctx_tiles, n_lat, bi), 0, 0)),
                  pl.BlockSpec((6, D_MODEL), lambda bi, i: (0, 0)),
                  pl.BlockSpec((1, TM, GATE_COLS), lambda bi, i: (bi, i + row_off, 0)),
                  pl.BlockSpec((N_BRANCH, D_MODEL), lambda bi, i: (0, 0)),
                  br_a, tok, tok, tok, tok, row, row,
                  pl.BlockSpec((RWKV_DIM, RWKV_DIM), lambda bi, i: (0, 0)),
                  br_h, br_hc, br(),
                  pl.BlockSpec((N_BRANCH, BRANCH_DIM, D_MODEL), lambda bi, i: (0, 0, 0)),
                  pl.BlockSpec((D_MODEL, D_MODEL), lambda bi, i: (0, 0))],
        out_specs=pl.BlockSpec((1, TM, D_MODEL), lambda bi, i: (bi, i, 0)),
        out_shape=jax.ShapeDtypeStruct((b, nt * TM, D_MODEL), F32),
        compiler_params=_cparams(("parallel", "parallel")),
        name="merge_branches",
    )(x, mods, norm_g, p, b_gate, ya, y_fwd, y_bwd, bonus, rgate, ln_g.reshape(1, -1), ln_b.reshape(1, -1),
      _head_block_diag(), yh, yh if yh_ctx is None else yh_ctx, yd, w_branch, w_out)


def kernel(x, c, ctx, c_ctx, w_mod, b_mod, norm_g, ffn_w13, ffn_w2, w_in, b_gate, mla_norm_q, mla_norm_kv, mla_w_uq, mla_w_ukv, rwkv_mu, rwkv_w0, rwkv_w_up, rwkv_a0, rwkv_a_up, rwkv_g_up, rwkv_kvec, rwkv_r_k, rwkv_ln_g, rwkv_ln_b, hyena_conv, hyena_conv_b, hyena_w1, hyena_b1, hyena_w2, hyena_b2, hyena_w3, hyena_freq, hyena_bias, swa_sink, w_branch, w_out):
    b, n, _ = x.shape
    n_ctx = ctx.shape[1]
    nct = n_ctx // TM
    xall = jnp.concatenate([ctx, x], axis=1)
    c_all = jnp.concatenate([c, c_ctx[None]], axis=0)
    tabs_mla = _rope_tables(n, n_ctx, MLA_ROPE, MLA_NOPE, LANES)
    tabs_swa = _rope_tables(n, n_ctx, SWA_HEAD, 0, SWA_HEAD)
    depth = w_mod.shape[0]
    for l in range(depth):
        with_ctx = l + 1 < depth
        row_off = 0 if with_ctx else nct
        mods = _modulation(c_all, w_mod[l], b_mod[l])
        xall = _ffn(xall, mods, norm_g[l], ffn_w13[l, 0].astype(BF16), ffn_w2[l, 0].astype(BF16),
                    mod0=0, g0=0, n_ctx_tiles=nct)
        coef = _shift_coefficients(rwkv_mu[l], hyena_conv[l], hyena_conv_b[l])
        p, p_hy = _inproj(xall, mods, norm_g[l], _permute_w_in(w_in[l]).astype(BF16), coef, n_ctx_tiles=nct)
        q, k, v = _mla_prep(p, mla_norm_q[l], mla_norm_kv[l], mla_w_uq[l], mla_w_ukv[l], tabs_mla)
        ya = _mla_attention(q, k, v, n_ctx=n_ctx, with_ctx=with_ctx)
        y_fwd, y_bwd, bonus, gate = _rwkv_chunks(p, rwkv_kvec[l], rwkv_w0[l], rwkv_a0[l], rwkv_w_up[l],
                                                 rwkv_a_up[l], rwkv_g_up[l], rwkv_r_k[l], n_ctx=n_ctx)
        rwkv = (y_fwd, y_bwd, bonus, gate, rwkv_ln_g[l], rwkv_ln_b[l])
        filt = (hyena_w1[l], hyena_b1[l], hyena_w2[l], hyena_b2[l], hyena_w3[l], hyena_freq[l])
        yh = _hyena_operator(p_hy, n_ctx, n, filt, hyena_bias[l])
        yh_ctx = _hyena_operator(p_hy, 0, n_ctx, filt, hyena_bias[l]) if with_ctx else None
        q, k, v = _swa_prep(p, tabs_swa)
        yd = _swa_attention(q, k, v, swa_sink[l], n_ctx=n_ctx, q_off=row_off * TM // SWA_TQ)
        xall = _merge(xall, mods, norm_g[l], p, b_gate[l], ya, rwkv, yh, yh_ctx, yd, w_branch[l].astype(BF16),
                      w_out[l].astype(BF16), n_ctx_tiles=nct, row_off=row_off)
        xall = _ffn(xall, mods, norm_g[l], ffn_w13[l, 1].astype(BF16), ffn_w2[l, 1].astype(BF16),
                    mod0=6, g0=4, n_ctx_tiles=nct - row_off)
    return xall
```

```python
import functools
import math

import numpy as np
import jax
import jax.numpy as jnp
from jax import lax
from jax.experimental import pallas as pl
from jax.experimental.pallas import tpu as pltpu

F32 = jnp.float32
BF16 = jnp.bfloat16

D_MODEL = 1024
GRID_W = 64
N_BRANCH = 4
N_MOD = 9
FF_DIM = 2816
EPS = 1e-6
ROPE_BASE = 10000.0
NEG_INF = -1e30
BRANCH_DIM = 512
MLA_HEADS = 8
MLA_NOPE = 64
MLA_ROPE = 32
MLA_V = 64
MLA_Q_RANK = 256
MLA_KV_RANK = 128
RWKV_HEADS = 8
RWKV_HEAD = 64
RWKV_DIM = RWKV_HEADS * RWKV_HEAD
DECAY_LORA = 64
AAA_LORA = 64
GATE_LORA = 128
RWKV_LN_EPS = 64e-5
HYENA_DIM = 512
HYENA_ORDER = 2
HYENA_EMB = 33
HYENA_BANDS = (HYENA_EMB - 1) // 2
HYENA_FW = 64
HYENA_TARGET = 1e-2
HYENA_FAST = 0.3
HYENA_SLOW = 1.5
SWA_HEADS = 8
SWA_KV_HEADS = 2
SWA_HEAD = 64
SWA_GROUP = SWA_HEADS // SWA_KV_HEADS
WINDOW = 128
GATE_COLS = N_BRANCH * D_MODEL
MLA_COLS = MLA_Q_RANK + MLA_KV_RANK + MLA_ROPE
RWKV_COLS = 3 * RWKV_DIM + DECAY_LORA + AAA_LORA + GATE_LORA
HYENA_COLS = 3 * HYENA_DIM
SWA_COLS = (SWA_HEADS + 2 * SWA_KV_HEADS) * SWA_HEAD

LANES = 128
SUBLANES = 8
V7X_VMEM_LIMIT = 56 * 1024 * 1024

TM = 256
FFN_SUB_TILES = 4
INPROJ_SUB_TILES = 2
FF_CHUNK = 256
IN_CHUNK = 512
CHUNK = 64

P_RKV = 4096
P_SWAQ = 5632
P_LORA = 6144
P_CQ = 6400
P_CKV = 6656
P_KR = 6784
P_SWAK = 6912
P_SWAV = 7040
P_COLS = 7168
P_HY = 7168
W_COLS = P_HY + HYENA_COLS
_SHIFT_COLS = ((P_RKV, P_SWAQ), (P_LORA, P_CQ), (P_HY, W_COLS))
_SHIFT_CHUNKS = [any(lo < (j + 1) * IN_CHUNK and j * IN_CHUNK < hi for lo, hi in _SHIFT_COLS)
                 for j in range(W_COLS // IN_CHUNK)]


def _cparams(sem, vmem=V7X_VMEM_LIMIT):
    return pltpu.CompilerParams(dimension_semantics=sem, vmem_limit_bytes=vmem)


def _mm(a, b):
    return jnp.dot(a.astype(BF16), b.astype(BF16), preferred_element_type=F32)


def _mm_nt(a, b):
    return lax.dot_general(a.astype(BF16), b.astype(BF16), (((1,), (1,)), ((), ())),
                           preferred_element_type=F32)


def _mm_tn(a, b):
    return lax.dot_general(a.astype(BF16), b.astype(BF16), (((0,), (0,)), ((), ())),
                           preferred_element_type=F32)


def _mm_f32(a, b):
    return jnp.dot(a, b, preferred_element_type=F32, precision=lax.Precision.HIGHEST)


def _rms(x, g):
    return x * lax.rsqrt(jnp.mean(x * x, axis=-1, keepdims=True) + EPS) * g


def _sigmoid(x):
    return 1.0 / (1.0 + jnp.exp(-x))


def _mod_kernel(c_ref, w_ref, b_ref, o_ref):
    c = c_ref[...]
    o_ref[...] = _mm(c * _sigmoid(c), w_ref[...]) + b_ref[...]


def _modulation(c_all, w_mod, b_mod):
    r = c_all.shape[0]
    rp = -(-r // 8) * 8
    c_pad = jnp.zeros((rp, D_MODEL), F32).at[:r].set(c_all)
    tn = 1024
    out = pl.pallas_call(
        _mod_kernel,
        grid=(N_MOD * D_MODEL // tn,),
        in_specs=[pl.BlockSpec((rp, D_MODEL), lambda j: (0, 0)),
                  pl.BlockSpec((D_MODEL, tn), lambda j: (0, j)),
                  pl.BlockSpec((1, tn), lambda j: (0, j))],
        out_specs=pl.BlockSpec((rp, tn), lambda j: (0, j)),
        out_shape=jax.ShapeDtypeStruct((rp, N_MOD * D_MODEL), F32),
        compiler_params=_cparams(("arbitrary",)),
        name="modulation",
    )(c_pad, w_mod, b_mod.reshape(1, -1))
    return out[:r].reshape(r, N_MOD, D_MODEL)


def _sub_tiles(n_tiles, most):
    return max(g for g in range(1, most + 1) if n_tiles % g == 0)


def _mod_specs(n_sub, tiles_per_seq, n_ctx_tiles, ctx_row):
    def spec(k):
        def index(i):
            t = i * n_sub + k
            return (jnp.where(t % tiles_per_seq < n_ctx_tiles, ctx_row, t // tiles_per_seq), 0, 0)
        return pl.BlockSpec((1, N_MOD, D_MODEL), index)
    return [spec(k) for k in range(n_sub)]


def _ffn_kernel(x_ref, *refs, mod0, g0, n_sub):
    mod_refs = refs[:n_sub]
    g_ref, w13_ref, w2_ref, o_ref = refs[n_sub:]
    tiles = [slice(t * TM, (t + 1) * TM) for t in range(n_sub)]
    u = jnp.concatenate(
        [(_rms(x_ref[rows], g_ref[g0:g0 + 1]) * (1.0 + m[0, mod0 + 1:mod0 + 2]) + m[0, mod0:mod0 + 1]).astype(BF16)
         for rows, m in zip(tiles, mod_refs)], axis=0)
    acc = jnp.zeros(x_ref.shape, F32)
    for f in range(FF_DIM // FF_CHUNK):
        lo = f * FF_CHUNK
        a = jnp.dot(u, w13_ref[:, lo:lo + FF_CHUNK], preferred_element_type=F32)
        b = jnp.dot(u, w13_ref[:, FF_DIM + lo:FF_DIM + lo + FF_CHUNK], preferred_element_type=F32)
        h = (a * _sigmoid(a) * b).astype(BF16)
        acc = acc + jnp.dot(h, w2_ref[lo:lo + FF_CHUNK, :], preferred_element_type=F32)
    hn = _rms(acc, g_ref[g0 + 1:g0 + 2])
    for rows, m in zip(tiles, mod_refs):
        o_ref[rows] = x_ref[rows] + 0.5 * m[0, mod0 + 2:mod0 + 3] * hn[rows]


def _ffn(x, mods, norm_g, w13, w2, *, mod0, g0, n_ctx_tiles):
    b, s, _ = x.shape
    n_tiles = b * s // TM
    n_sub = _sub_tiles(n_tiles, FFN_SUB_TILES)
    rows = n_sub * TM
    out = pl.pallas_call(
        functools.partial(_ffn_kernel, mod0=mod0, g0=g0, n_sub=n_sub),
        grid=(n_tiles // n_sub,),
        in_specs=[pl.BlockSpec((rows, D_MODEL), lambda i: (i, 0)),
                  *_mod_specs(n_sub, s // TM, n_ctx_tiles, mods.shape[0] - 1),
                  pl.BlockSpec((6, D_MODEL), lambda i: (0, 0)),
                  pl.BlockSpec(memory_space=pltpu.VMEM),
                  pl.BlockSpec(memory_space=pltpu.VMEM)],
        out_specs=pl.BlockSpec((rows, D_MODEL), lambda i: (i, 0)),
        out_shape=jax.ShapeDtypeStruct((b * s, D_MODEL), F32),
        compiler_params=_cparams(("parallel",)),
        name="ffn_half_step",
    )(x.reshape(b * s, D_MODEL), *([mods] * n_sub), norm_g, w13, w2)
    return out.reshape(b, s, D_MODEL)


def _inproj_kernel(x_ref, xp_ref, xn_ref, *refs, n_sub, tiles_per_seq, n_ctx_tiles):
    mod_refs = refs[:n_sub]
    g_ref, w_ref, coef_ref, o_ref, hy_ref = refs[n_sub:]
    g = g_ref[2:3]

    def modulated(x, m):
        return _rms(x, g) * (1.0 + m[0, 4:5]) + m[0, 3:4]

    u_halo = jnp.concatenate([modulated(x_ref[t * TM:(t + 1) * TM], m) for t, m in enumerate(mod_refs)]
                             + [modulated(xp_ref[...], mod_refs[0]), modulated(xn_ref[...], mod_refs[-1])],
                             axis=0).astype(BF16)
    u = u_halo[:n_sub * TM]
    rows = lax.broadcasted_iota(jnp.int32, (n_sub * TM, 1), 0)
    keep_prev = jnp.ones((n_sub * TM, 1), F32)
    keep_next = jnp.ones((n_sub * TM, 1), F32)
    for t in range(n_sub):
        w = (pl.program_id(0) * n_sub + t) % tiles_per_seq
        seg_start = (w == 0) | (w == n_ctx_tiles)
        seg_end = (w == n_ctx_tiles - 1) | (w == tiles_per_seq - 1)
        keep_prev = jnp.where((rows == t * TM) & seg_start, 0.0, keep_prev)
        keep_next = jnp.where((rows == (t + 1) * TM - 1) & seg_end, 0.0, keep_next)
    n_rows = n_sub * TM
    for j in range(W_COLS // IN_CHUNK):
        cols = slice(j * IN_CHUNK, (j + 1) * IN_CHUNK)
        if not _SHIFT_CHUNKS[j]:
            p = jnp.dot(u, w_ref[:, cols], preferred_element_type=F32)
        else:
            p_halo = jnp.dot(u_halo, w_ref[:, cols], preferred_element_type=F32)
            p = p_halo[:n_rows]
            p_first = p_halo[n_rows + SUBLANES - 1:n_rows + SUBLANES]
            p_last = p_halo[n_rows + SUBLANES:n_rows + SUBLANES + 1]
            prev = jnp.where(rows == 0, p_first, pltpu.roll(p, 1, axis=0)) * keep_prev
            nxt = jnp.where(rows == n_sub * TM - 1, p_last, pltpu.roll(p, n_sub * TM - 1, axis=0)) * keep_next
            p = (coef_ref[0:1, cols] * p + coef_ref[1:2, cols] * prev + coef_ref[2:3, cols] * nxt
                 + coef_ref[3:4, cols])
        if j * IN_CHUNK < P_COLS:
            o_ref[:, cols] = p.astype(BF16)
        else:
            hy_ref[:, j * IN_CHUNK - P_HY:(j + 1) * IN_CHUNK - P_HY] = p


def _inproj(x, mods, norm_g, w_in_p, coef, *, n_ctx_tiles):
    b, s, _ = x.shape
    n_tiles = b * s // TM
    n_sub = _sub_tiles(n_tiles, INPROJ_SUB_TILES)
    rows = n_sub * TM
    r8 = rows // SUBLANES
    p, hy = pl.pallas_call(
        functools.partial(_inproj_kernel, n_sub=n_sub, tiles_per_seq=s // TM, n_ctx_tiles=n_ctx_tiles),
        grid=(n_tiles // n_sub,),
        in_specs=[pl.BlockSpec((rows, D_MODEL), lambda i: (i, 0)),
                  pl.BlockSpec((SUBLANES, D_MODEL), lambda i: (jnp.maximum(i * r8 - 1, 0), 0)),
                  pl.BlockSpec((SUBLANES, D_MODEL),
                               lambda i: (jnp.minimum((i + 1) * r8, b * s // SUBLANES - 1), 0)),
                  *_mod_specs(n_sub, s // TM, n_ctx_tiles, mods.shape[0] - 1),
                  pl.BlockSpec((6, D_MODEL), lambda i: (0, 0)),
                  pl.BlockSpec(memory_space=pltpu.VMEM),
                  pl.BlockSpec((4, W_COLS), lambda i: (0, 0))],
        out_specs=[pl.BlockSpec((rows, P_COLS), lambda i: (i, 0)),
                   pl.BlockSpec((rows, HYENA_COLS), lambda i: (i, 0))],
        out_shape=[jax.ShapeDtypeStruct((b * s, P_COLS), BF16),
                   jax.ShapeDtypeStruct((b * s, HYENA_COLS), F32)],
        compiler_params=_cparams(("parallel",)),
        name="in_projection",
    )(*([x.reshape(b * s, D_MODEL)] * 3), *([mods] * n_sub), norm_g, w_in_p, coef)
    return p.reshape(b, s, P_COLS), hy.reshape(b, s, HYENA_COLS)


def _shift_coefficients(rwkv_mu, hyena_conv, hyena_conv_b):
    mu = rwkv_mu.astype(F32)
    coef = jnp.zeros((4, W_COLS), F32).at[0].set(1.0)
    for off, sl in ((P_RKV, slice(0, 3 * RWKV_DIM)), (P_LORA, slice(3 * RWKV_DIM, RWKV_COLS))):
        width = sl.stop - sl.start
        coef = coef.at[0, off:off + width].set(1.0 - mu[0, sl] - mu[1, sl])
        coef = coef.at[1, off:off + width].set(mu[0, sl])
        coef = coef.at[2, off:off + width].set(mu[1, sl])
    hy = slice(P_HY, P_HY + HYENA_COLS)
    coef = coef.at[0, hy].set(hyena_conv[1]).at[1, hy].set(hyena_conv[0]).at[2, hy].set(hyena_conv[2])
    return coef.at[3, hy].set(hyena_conv_b)


def _permute_w_in(w_in):
    o_mla = GATE_COLS
    o_rwkv = o_mla + MLA_COLS
    o_hy = o_rwkv + RWKV_COLS
    o_swa = o_hy + HYENA_COLS
    z = lambda n: jnp.zeros((D_MODEL, n), w_in.dtype)
    parts = [
        w_in[:, :GATE_COLS],
        w_in[:, o_rwkv:o_rwkv + 3 * RWKV_DIM],
        w_in[:, o_swa:o_swa + SWA_HEADS * SWA_HEAD],
        w_in[:, o_rwkv + 3 * RWKV_DIM:o_rwkv + RWKV_COLS],
        w_in[:, o_mla:o_mla + MLA_Q_RANK],
        w_in[:, o_mla + MLA_Q_RANK:o_mla + MLA_Q_RANK + MLA_KV_RANK],
        z(MLA_NOPE), w_in[:, o_mla + MLA_Q_RANK + MLA_KV_RANK:o_mla + MLA_COLS],
        z(LANES - MLA_NOPE - MLA_ROPE),
        w_in[:, o_swa + SWA_HEADS * SWA_HEAD:o_swa + SWA_COLS],
        w_in[:, o_hy:o_hy + HYENA_COLS],
    ]
    out = jnp.concatenate(parts, axis=1)
    assert out.shape[1] == W_COLS
    return out


def _rope_tables(n_lat, n_ctx, rot_dim, lane0, period):
    rows = n_lat // GRID_W
    row = jnp.repeat(jnp.arange(rows, dtype=F32), GRID_W)
    col = jnp.tile(jnp.arange(GRID_W, dtype=F32), rows)
    axis_dim = rot_dim // 2
    h = axis_dim // 2
    inv_freq = ROPE_BASE ** (-jnp.arange(0, axis_dim, 2, dtype=F32) / axis_dim)
    ang_r = row[:, None] * inv_freq
    ang_c = col[:, None] * inv_freq
    cos_rot = jnp.concatenate([jnp.cos(ang_r)] * 2 + [jnp.cos(ang_c)] * 2, axis=1)
    zeros = jnp.zeros_like(ang_r)
    sin_a = jnp.concatenate([-jnp.sin(ang_r), zeros, -jnp.sin(ang_c), zeros], axis=1)
    sin_b = jnp.concatenate([zeros, jnp.sin(ang_r), zeros, jnp.sin(ang_c)], axis=1)

    def widen(t, fill):
        g = jnp.full((n_lat, period), fill, F32).at[:, lane0:lane0 + rot_dim].set(t)
        g = jnp.tile(g, (1, LANES // period))
        ctx = jnp.full((n_ctx, LANES), fill, F32)
        return jnp.concatenate([ctx, g], axis=0)

    return widen(cos_rot, 1.0), widen(sin_a, 0.0), widen(sin_b, 0.0), h


def _rope128(x, cos, sin_a, sin_b, h):
    return x * cos + pltpu.roll(x, LANES - h, axis=1) * sin_a + pltpu.roll(x, h, axis=1) * sin_b


LOG2E = math.log2(math.e)
MLA_SCALE = (MLA_NOPE + MLA_ROPE) ** -0.5 * LOG2E
MLA_ONE_LANE = (MLA_V, 0)


def _mla_prep_kernel(cq_ref, ckv_ref, kr_ref, gq_ref, gkv_ref, wq_ref, wk_ref, wv_ref, vone_ref,
                     cos_ref, sa_ref, sb_ref, q_ref, k_ref, v_ref, *, h):
    cos, sa, sb = cos_ref[...], sa_ref[...], sb_ref[...]
    cq = _rms(cq_ref[0].astype(F32), gq_ref[...]).astype(BF16)
    ckv = _rms(ckv_ref[0].astype(F32), gkv_ref[...]).astype(BF16)
    q = jnp.dot(cq, wq_ref[...], preferred_element_type=F32)
    k = jnp.dot(ckv, wk_ref[...], preferred_element_type=F32)
    kr = _rope128(kr_ref[0].astype(F32), cos, sa, sb, h)
    for hd in range(MLA_HEADS):
        sl = slice(hd * LANES, (hd + 1) * LANES)
        q_ref[0, :, sl] = (_rope128(q[:, sl], cos, sa, sb, h) * MLA_SCALE).astype(BF16)
        k_ref[0, :, sl] = (k[:, sl] + kr).astype(BF16)
    v_ref[0] = (jnp.dot(ckv, wv_ref[...], preferred_element_type=F32) + vone_ref[...]).astype(BF16)


def _mla_prep(p, norm_q, norm_kv, w_uq, w_ukv, tabs):
    b, s, _ = p.shape
    cos, sa, sb, h = tabs
    hq = MLA_NOPE + MLA_ROPE
    wq = jnp.zeros((MLA_Q_RANK, MLA_HEADS, LANES), F32).at[:, :, :hq].set(
        w_uq.reshape(MLA_Q_RANK, MLA_HEADS, hq)).reshape(MLA_Q_RANK, MLA_HEADS * LANES).astype(BF16)
    wkv = w_ukv.reshape(MLA_KV_RANK, MLA_HEADS, MLA_NOPE + MLA_V)
    wk = jnp.zeros((MLA_KV_RANK, MLA_HEADS, LANES), F32).at[:, :, :MLA_NOPE].set(
        wkv[:, :, :MLA_NOPE]).reshape(MLA_KV_RANK, MLA_HEADS * LANES).astype(BF16)
    wv_pairs = wkv[:, :, MLA_NOPE:].reshape(MLA_KV_RANK, MLA_HEADS // 2, 2, MLA_V)
    gap = ((0, 0), (0, 0), (0, LANES - MLA_V))
    wv = jnp.stack([jnp.pad(wv_pairs[:, :, 0], gap), jnp.pad(wv_pairs[:, :, 1], gap[:2] + (gap[2][::-1],))],
                   axis=2).reshape(MLA_KV_RANK, MLA_HEADS * LANES).astype(BF16)
    lane_id = np.arange(MLA_HEADS * LANES) % (2 * LANES)
    vone = jnp.asarray((lane_id == MLA_ONE_LANE[0]) | (lane_id == LANES + MLA_ONE_LANE[1]), F32)[None, :]
    full = lambda shape: pl.BlockSpec(shape, lambda bi, i: (0,) * len(shape))
    tab = pl.BlockSpec((TM, LANES), lambda bi, i: (i, 0))
    return pl.pallas_call(
        functools.partial(_mla_prep_kernel, h=h),
        grid=(b, s // TM),
        in_specs=[pl.BlockSpec((1, TM, MLA_Q_RANK), lambda bi, i: (bi, i, P_CQ // MLA_Q_RANK)),
                  pl.BlockSpec((1, TM, LANES), lambda bi, i: (bi, i, P_CKV // LANES)),
                  pl.BlockSpec((1, TM, LANES), lambda bi, i: (bi, i, P_KR // LANES)),
                  full((1, MLA_Q_RANK)), full((1, MLA_KV_RANK)),
                  full(wq.shape), full(wk.shape), full(wv.shape), full(vone.shape), tab, tab, tab],
        out_specs=[pl.BlockSpec((1, TM, MLA_HEADS * LANES), lambda bi, i: (bi, i, 0))] * 3,
        out_shape=[jax.ShapeDtypeStruct((b, s, MLA_HEADS * LANES), BF16)] * 3,
        compiler_params=_cparams(("parallel", "parallel")),
        name="mla_prep",
    )(p, p, p, norm_q.reshape(1, -1), norm_kv.reshape(1, -1), wq, wk, wv, vone, cos, sa, sb)


MLA_KEY_BLOCK = 2176
MLA_Q_TILES = 1


def _mla_attn_kernel(*refs, n_ctx, n_lat_steps):
    q_refs = refs[:MLA_Q_TILES]
    k_ref, v_ref, o_ref = refs[MLA_Q_TILES:]

    def attend(n_keys):
        outs = []
        for hd in range(2):
            sl = slice(hd * LANES, (hd + 1) * LANES)
            q = jnp.concatenate([q_ref[0, :, sl] for q_ref in q_refs], axis=0)
            n_blocks = -(-n_keys // MLA_KEY_BLOCK)
            width = n_keys // n_blocks
            m = o = None
            for kb in range(n_blocks):
                keys = slice(kb * width, (kb + 1) * width)
                s = lax.dot_general(q, k_ref[0, keys, sl], (((1,), (1,)), ((), ())),
                                    preferred_element_type=F32)
                m_blk = jnp.max(s, axis=-1, keepdims=True)
                m_new = m_blk if m is None else jnp.maximum(m, m_blk)
                pv = jnp.dot(jnp.exp2(s - m_new).astype(BF16), v_ref[0, keys, sl], preferred_element_type=F32)
                o = pv if o is None else o * jnp.exp2(m - m_new) + pv
                m = m_new
            one = MLA_ONE_LANE[hd]
            outs.append(o / o[:, one:one + 1])
        lane = lax.broadcasted_iota(jnp.int32, outs[0].shape, 1)
        o_ref[0] = jnp.where(lane < MLA_V, outs[0], outs[1]).astype(BF16)

    @pl.when(pl.program_id(2) < n_lat_steps)
    def _():
        attend(k_ref.shape[1])

    @pl.when(pl.program_id(2) >= n_lat_steps)
    def _():
        attend(n_ctx)


def _mla_attention(q, k, v, *, n_ctx, with_ctx):
    b, s, _ = q.shape
    nct = n_ctx // TM
    n_lat = s - n_ctx
    n_lat_steps = n_lat // (MLA_Q_TILES * TM)
    assert n_lat % (MLA_Q_TILES * TM) == 0 and (nct == 1 or not with_ctx)

    def q_spec(t):
        return pl.BlockSpec((1, TM, 2 * LANES), lambda bi, hp, j: (
            bi, jnp.where(j < n_lat_steps, nct + j * MLA_Q_TILES + t, 0), hp))

    kv = pl.BlockSpec((1, s, 2 * LANES), lambda bi, hp, j: (bi, 0, hp))
    return pl.pallas_call(
        functools.partial(_mla_attn_kernel, n_ctx=n_ctx, n_lat_steps=n_lat_steps),
        grid=(b, MLA_HEADS // 2, n_lat_steps + (1 if with_ctx else 0)),
        in_specs=[*[q_spec(t) for t in range(MLA_Q_TILES)], kv, kv],
        out_specs=pl.BlockSpec((1, MLA_Q_TILES * TM, LANES), lambda bi, hp, j: (bi, j, hp)),
        out_shape=jax.ShapeDtypeStruct((b, n_lat + (n_ctx if with_ctx else 0), MLA_HEADS * MLA_V), BF16),
        compiler_params=_cparams(("parallel", "parallel", "parallel")),
        name="mla_attention",
    )(*([q] * MLA_Q_TILES), k, v)


SWA_SCALE = SWA_HEAD ** -0.5 * LOG2E
SWA_TQ = 128
SWA_ONE_LANE = (SWA_HEAD, 0)


def _swa_prep_kernel(q_ref, k_ref, v_ref, cos_ref, sa_ref, sb_ref, qo_ref, ko_ref, vo_ref, *, h):
    cos, sa, sb = cos_ref[...], sa_ref[...], sb_ref[...]
    lane = lax.broadcasted_iota(jnp.int32, cos.shape, 1)
    low = lane < SWA_HEAD
    for j in range(SWA_HEADS // 2):
        blk = _rope128(q_ref[0, :, j * LANES:(j + 1) * LANES].astype(F32), cos, sa, sb, h) * SWA_SCALE
        qo_ref[0, :, (2 * j) * LANES:(2 * j + 1) * LANES] = jnp.where(low, blk, 0.0).astype(BF16)
        qo_ref[0, :, (2 * j + 1) * LANES:(2 * j + 2) * LANES] = jnp.where(
            low, pltpu.roll(blk, SWA_HEAD, axis=1), 0.0).astype(BF16)
    kb = _rope128(k_ref[0].astype(F32), cos, sa, sb, h)
    ko_ref[0, :, :LANES] = jnp.where(low, kb, 0.0).astype(BF16)
    ko_ref[0, :, LANES:] = jnp.where(low, pltpu.roll(kb, SWA_HEAD, axis=1), 0.0).astype(BF16)
    vb = v_ref[0].astype(F32)
    vr = pltpu.roll(vb, SWA_HEAD, axis=1)
    one_lo = jnp.where(lane == SWA_ONE_LANE[0], 1.0, 0.0)
    one_hi = jnp.where(lane == SWA_ONE_LANE[1], 1.0, 0.0)
    vo_ref[0, :, 0 * LANES:1 * LANES] = jnp.where(low, vb, one_lo).astype(BF16)
    vo_ref[0, :, 1 * LANES:2 * LANES] = jnp.where(low, one_hi, vr).astype(BF16)
    vo_ref[0, :, 2 * LANES:3 * LANES] = jnp.where(low, vr, one_lo).astype(BF16)
    vo_ref[0, :, 3 * LANES:4 * LANES] = jnp.where(low, one_hi, vb).astype(BF16)


def _swa_prep(p, tabs):
    b, s, _ = p.shape
    cos, sa, sb, h = tabs
    tab = pl.BlockSpec((TM, LANES), lambda bi, i: (i, 0))
    nq = SWA_HEADS * SWA_HEAD
    return pl.pallas_call(
        functools.partial(_swa_prep_kernel, h=h),
        grid=(b, s // TM),
        in_specs=[pl.BlockSpec((1, TM, nq), lambda bi, i: (bi, i, P_SWAQ // nq)),
                  pl.BlockSpec((1, TM, LANES), lambda bi, i: (bi, i, P_SWAK // LANES)),
                  pl.BlockSpec((1, TM, LANES), lambda bi, i: (bi, i, P_SWAV // LANES)),
                  tab, tab, tab],
        out_specs=[pl.BlockSpec((1, TM, SWA_HEADS * LANES), lambda bi, i: (bi, i, 0)),
                   pl.BlockSpec((1, TM, SWA_KV_HEADS * LANES), lambda bi, i: (bi, i, 0)),
                   pl.BlockSpec((1, TM, 4 * LANES), lambda bi, i: (bi, i, 0))],
        out_shape=[jax.ShapeDtypeStruct((b, s, SWA_HEADS * LANES), BF16),
                   jax.ShapeDtypeStruct((b, s, SWA_KV_HEADS * LANES), BF16),
                   jax.ShapeDtypeStruct((b, s, 4 * LANES), BF16)],
        compiler_params=_cparams(("parallel", "parallel")),
        name="swa_prep",
    )(p, p, p, cos, sa, sb)


def _swa_attn_kernel(sink_ref, q_ref, k_ref, v_ref, o_ref, *, n_ctx, q_off):
    i = pl.program_id(1) + q_off
    s_len = k_ref.shape[1]
    tq = SWA_TQ
    n_loc = tq + 2 * WINDOW
    r0 = i * tq
    is_lat = r0 >= n_ctx
    start = pl.multiple_of(jnp.clip(r0 - WINDOW, 0, s_len - n_loc), LANES)
    rows_g = SWA_GROUP * tq
    row = lax.broadcasted_iota(jnp.int32, (rows_g, n_loc), 0)
    qpos = r0 - n_ctx + row % tq
    kpos = start - n_ctx + lax.broadcasted_iota(jnp.int32, (rows_g, n_loc), 1)
    loc_ok = (jnp.abs(kpos - qpos) <= WINDOW) & (kpos >= 0) & is_lat
    k_loc = k_ref[0, pl.ds(start, n_loc), :]
    v_loc = v_ref[0, pl.ds(start, n_loc), :]
    k_ctx = k_ref[0, 0:n_ctx, :]
    v_ctx = v_ref[0, 0:n_ctx, :]
    head_row = lax.broadcasted_iota(jnp.int32, (rows_g, 1), 0) // tq
    lane = lax.broadcasted_iota(jnp.int32, (tq, LANES), 1)
    stages = []
    for g in range(SWA_KV_HEADS):
        q = jnp.concatenate([q_ref[0, :, hd * LANES:(hd + 1) * LANES]
                             for hd in range(g * SWA_GROUP, (g + 1) * SWA_GROUP)], axis=0)
        kg = slice(g * LANES, (g + 1) * LANES)
        s_loc = lax.dot_general(q, k_loc[:, kg], (((1,), (1,)), ((), ())), preferred_element_type=F32)
        s_ctx = lax.dot_general(q, k_ctx[:, kg], (((1,), (1,)), ((), ())), preferred_element_type=F32)
        sink = jnp.zeros((rows_g, 1), F32)
        for hh in range(SWA_GROUP):
            sink = jnp.where(head_row == hh, sink_ref[g * SWA_GROUP + hh] * LOG2E, sink)
        stages.append((jnp.where(loc_ok, s_loc, NEG_INF), s_ctx, sink))
    for g, (s_loc, s_ctx, sink) in enumerate(stages):
        m = jnp.maximum(jnp.maximum(jnp.max(s_loc, axis=-1, keepdims=True),
                                    jnp.max(s_ctx, axis=-1, keepdims=True)), sink)
        e = jnp.concatenate([jnp.exp2(s_loc - m), jnp.exp2(s_ctx - m)], axis=1).astype(BF16)
        e_sink = jnp.exp2(sink - m)
        outs = []
        for par in range(2):
            vg = slice((2 * g + par) * LANES, (2 * g + par + 1) * LANES)
            o = jnp.dot(e, jnp.concatenate([v_loc[:, vg], v_ctx[:, vg]], axis=0), preferred_element_type=F32)
            one = SWA_ONE_LANE[par]
            outs.append(o / (o[:, one:one + 1] + e_sink))
        for pi in range(SWA_GROUP // 2):
            even = outs[0][(2 * pi) * tq:(2 * pi + 1) * tq]
            odd = outs[1][(2 * pi + 1) * tq:(2 * pi + 2) * tq]
            blk = g * (SWA_GROUP // 2) + pi
            o_ref[0, :, blk * LANES:(blk + 1) * LANES] = jnp.where(lane < SWA_HEAD, even, odd).astype(BF16)


def _swa_attention(q, k, v, sink, *, n_ctx, q_off):
    b, s, _ = q.shape
    nq = s // SWA_TQ - q_off
    return pl.pallas_call(
        functools.partial(_swa_attn_kernel, n_ctx=n_ctx, q_off=q_off),
        grid=(b, nq),
        in_specs=[pl.BlockSpec(memory_space=pltpu.SMEM),
                  pl.BlockSpec((1, SWA_TQ, SWA_HEADS * LANES), lambda bi, i: (bi, i + q_off, 0)),
                  pl.BlockSpec((1, s, SWA_KV_HEADS * LANES), lambda bi, i: (bi, 0, 0)),
                  pl.BlockSpec((1, s, 4 * LANES), lambda bi, i: (bi, 0, 0))],
        out_specs=pl.BlockSpec((1, SWA_TQ, SWA_HEADS * SWA_HEAD), lambda bi, i: (bi, i, 0)),
        out_shape=jax.ShapeDtypeStruct((b, nq * SWA_TQ, SWA_HEADS * SWA_HEAD), BF16),
        compiler_params=_cparams(("parallel", "parallel")),
        name="swa_attention",
    )(sink, q, k, v)


N_PAIR = RWKV_HEADS // 2
RWKV_NB = 4
N_DOUBLINGS = int(math.log2(CHUNK))


def _softplus(x):
    return jnp.maximum(x, 0.0) + jnp.log(1.0 + jnp.exp(-jnp.abs(x)))


def _headsum(x, bd):
    hi = x.astype(BF16)
    lo = (x - hi.astype(F32)).astype(BF16)
    return (jnp.dot(hi, bd, preferred_element_type=F32) + jnp.dot(lo, bd, preferred_element_type=F32))


def _chunk_cumsum(x, reverse):
    rows = lax.broadcasted_iota(jnp.int32, x.shape, 0)
    s = 1
    while s < CHUNK:
        if reverse:
            x = x + jnp.where(rows < CHUNK - s, pltpu.roll(x, CHUNK - s, axis=0), 0.0)
        else:
            x = x + jnp.where(rows >= s, pltpu.roll(x, s, axis=0), 0.0)
        s *= 2
    return x


def _head_rows(x):
    first = lax.broadcasted_iota(jnp.int32, x.shape, 1) < RWKV_HEAD
    return jnp.concatenate([jnp.where(first, x, 0.0), jnp.where(first, 0.0, x)], axis=0)


def _mm_x3(a, b):
    a_hi = a.astype(BF16)
    b_hi = b.astype(BF16)
    a_lo = (a - a_hi.astype(F32)).astype(BF16)
    b_lo = (b - b_hi.astype(F32)).astype(BF16)
    dot = functools.partial(jnp.dot, preferred_element_type=F32)
    return dot(a_hi, b_hi) + dot(a_hi, b_lo) + dot(a_lo, b_hi)


def _rwkv_chunk_kernel(rf_ref, kf_ref, vf_ref, lof_ref, rb_ref, kb_ref, vb_ref, lob_ref,
                       kvec_ref, w0_ref, a0_ref, wup_ref, aup_ref, gup_ref, rk_ref, bd_ref,
                       yf_ref, yb_ref, bonus_ref, gate_ref, s_ref):
    @pl.when(pl.program_id(1) == 0)
    def _():
        s_ref[...] = jnp.zeros_like(s_ref)

    bd = bd_ref[...]
    n_seq = rf_ref.shape[0]
    data = []
    for nb in range(n_seq):
        per_dir = []
        for refs in ((rf_ref, kf_ref, vf_ref, lof_ref), (rb_ref, kb_ref, vb_ref, lob_ref)):
            r, k, v, lora = (ref[nb].astype(F32) for ref in refs)
            kk = k * kvec_ref[0:1]
            kk = kk * lax.rsqrt(_headsum(kk * kk, bd) + 1e-12)
            per_dir.append((r, k, v, lora, kk))
        data.append(per_dir)
        r, k, v, lora, _ = per_dir[0]
        gate_ref[nb] = _mm(_sigmoid(lora), gup_ref[...])
        k_both = sum(k * (1.0 + (_sigmoid(a0_ref[d] + _mm(lora, aup_ref[d])) - 1.0) * kvec_ref[1:2])
                     for d in range(2))
        bonus_ref[nb] = _headsum(r * (0.5 * k_both) * rk_ref[...], bd) * v

    trow = lax.broadcasted_iota(jnp.int32, (2 * CHUNK, 4 * CHUNK), 0) % CHUNK
    tcol = lax.broadcasted_iota(jnp.int32, (2 * CHUNK, 4 * CHUNK), 1) % CHUNK
    sq_r = lax.broadcasted_iota(jnp.int32, (LANES, LANES), 0)
    sq_c = lax.broadcasted_iota(jnp.int32, (LANES, LANES), 1)
    same_head = (sq_r // RWKV_HEAD) == (sq_c // RWKV_HEAD)
    eye = sq_r == sq_c

    chains = []
    for nb, d in ((nb, d) for nb in range(n_seq) for d in range(2)):
        reverse = d == 1
        r, k, v, lora, kk = data[nb][d]
        w_log = -_softplus(-(w0_ref[d] + _mm(jnp.tanh(lora), wup_ref[d]))) - 0.5
        ld = -jnp.exp(w_log)
        a = _sigmoid(a0_ref[d] + _mm(lora, aup_ref[d]))
        k_d = k * (1.0 + (a - 1.0) * kvec_ref[1:2])
        b_d = kk * a
        lg = _chunk_cumsum(ld, reverse)
        last = 0 if reverse else CHUNK - 1
        tot = lg[last:last + 1]
        e_neg = jnp.exp(-lg)
        e_end = jnp.exp(tot - lg)
        z_t = -kk * jnp.exp(lg - ld)
        r_t = r * jnp.exp(lg)
        b_t = b_d * e_neg
        k_t = k_d * e_neg
        b_e = b_d * e_end
        k_e = k_d * e_end
        e_tot = jnp.exp(tot)
        before = (tcol > trow) if reverse else (tcol < trow)
        before_eq = (tcol >= trow) if reverse else (tcol <= trow)
        for pr in range(N_PAIR):
            sl = slice(pr * LANES, (pr + 1) * LANES)
            ch = {"nb": nb, "d": d, "sl": sl, "rp": r_t[:, sl], "vp": v[:, sl], "e_tot": e_tot[:, sl],
                  "be_ke": jnp.concatenate([b_e[:, sl], k_e[:, sl]], axis=0)}
            zst, rst, vst = _head_rows(z_t[:, sl]), _head_rows(r_t[:, sl]), _head_rows(v[:, sl])
            bkst = jnp.concatenate([_head_rows(b_t[:, sl]), _head_rows(k_t[:, sl])], axis=0)
            ch["lz"] = jnp.where(before, _mm_nt(zst, bkst), 0.0)
            ch["lr"] = jnp.where(before_eq, _mm_nt(rst, bkst), 0.0)
            ch["zst"], ch["vst"] = zst, vst
            chains.append(ch)

    unit = jnp.where(eye, 1.0, 0.0)
    for ch in chains:
        ch["pw"] = ch["lz"][:, :LANES]
        ch["t"] = unit + ch["pw"]
        ch["x"] = jnp.concatenate([ch["zst"], _mm(ch["lz"][:, LANES:], ch["vst"])], axis=1)
    for it in range(1, N_DOUBLINGS):
        for ch in chains:
            ch["pw"] = _mm(ch["pw"], ch["pw"])
        for ch in chains:
            ch["t"] = ch["t"] + _mm(ch["pw"], ch["t"])
    for ch in chains:
        ch["x"] = _mm(ch["t"], ch["x"])
    for ch in chains:
        low = jnp.concatenate([jnp.zeros_like(ch["vst"]), ch["vst"]], axis=1)
        op = _mm(ch["lr"], jnp.concatenate([ch["x"], low], axis=0))
        ch["op"] = op[:CHUNK] + op[CHUNK:]
        ch["xp"] = ch["x"][:CHUNK] + ch["x"][CHUNK:]
    for ch in chains:
        rhs = jnp.concatenate([ch["xp"], jnp.concatenate([jnp.zeros_like(ch["vp"]), ch["vp"]], axis=1)], axis=0)
        ag = _mm_tn(ch["be_ke"], rhs)
        a_full = ag[:, :LANES] + jnp.where(eye, jnp.broadcast_to(ch["e_tot"], (LANES, LANES)), 0.0)
        ch["a"] = jnp.where(same_head, a_full, 0.0)
        ch["g"] = jnp.where(same_head, ag[:, LANES:], 0.0)
    for idx, ch in enumerate(chains):
        st = s_ref[idx]
        y_ref = yf_ref if ch["d"] == 0 else yb_ref
        y_ref[ch["nb"], :, ch["sl"]] = _mm(ch["rp"] + ch["op"][:, :LANES], st) + ch["op"][:, LANES:]
        s_ref[idx] = _mm_x3(ch["a"], st) + ch["g"]


def _head_block_diag():
    idx = np.arange(RWKV_DIM) // RWKV_HEAD
    return jnp.asarray(idx[:, None] == idx[None, :], BF16)


def _rwkv_chunks(p, kvec, w0, a0, w_up, a_up, g_up, r_k, *, n_ctx):
    b, s, _ = p.shape
    nc = s // CHUNK
    ncc = n_ctx // CHUNK
    lora_w = DECAY_LORA + AAA_LORA + GATE_LORA
    wup = jnp.zeros((2, lora_w, RWKV_DIM), F32).at[:, :DECAY_LORA].set(w_up).astype(BF16)
    aup = jnp.zeros((2, lora_w, RWKV_DIM), F32).at[:, DECAY_LORA:DECAY_LORA + AAA_LORA].set(a_up).astype(BF16)
    gup = jnp.zeros((lora_w, RWKV_DIM), F32).at[DECAY_LORA + AAA_LORA:].set(g_up).astype(BF16)

    def fwd(c):
        return c

    def back(c):
        return jnp.where(c < ncc, ncc - 1 - c, nc - 1 - (c - ncc))

    full = lambda shape: pl.BlockSpec(shape, lambda bi, c: (0,) * len(shape))

    n_seq = _sub_tiles(b, RWKV_NB)

    def inputs(chunk):
        col = lambda off: pl.BlockSpec((n_seq, CHUNK, RWKV_DIM), lambda bi, c: (bi, chunk(c), off // RWKV_DIM))
        return [col(P_RKV), col(P_RKV + RWKV_DIM), col(P_RKV + 2 * RWKV_DIM),
                pl.BlockSpec((n_seq, CHUNK, lora_w), lambda bi, c: (bi, chunk(c), P_LORA // lora_w))]

    tok = lambda chunk: pl.BlockSpec((n_seq, CHUNK, RWKV_DIM), lambda bi, c: (bi, chunk(c), 0))
    return pl.pallas_call(
        _rwkv_chunk_kernel,
        grid=(b // n_seq, nc),
        in_specs=[*inputs(fwd), *inputs(back),
                  full((2, RWKV_DIM)), full((2, 1, RWKV_DIM)), full((2, 1, RWKV_DIM)),
                  full(wup.shape), full(aup.shape), full(gup.shape), full((1, RWKV_DIM)),
                  full((RWKV_DIM, RWKV_DIM))],
        out_specs=[tok(fwd), tok(back), tok(fwd), tok(fwd)],
        out_shape=[jax.ShapeDtypeStruct((b, s, RWKV_DIM), F32)] * 4,
        scratch_shapes=[pltpu.VMEM((n_seq * 2 * N_PAIR, LANES, LANES), F32)],
        compiler_params=_cparams(("parallel", "arbitrary")),
        name="rwkv_chunks",
    )(*([p] * 8), kvec, w0.reshape(2, 1, -1), a0.reshape(2, 1, -1), wup, aup, gup, r_k.reshape(1, -1),
      _head_block_diag())


DFT_N2 = LANES
HY_MIN_LEN = 1024
HY_CT = 128


def _hyena_mlp_kernel(feats_ref, w1_ref, b1_ref, w2_ref, b2_ref, freq_ref, h_ref):
    h = jnp.sin(freq_ref[0:1] * (_mm_f32(feats_ref[0], w1_ref[...]) + b1_ref[...]))
    h_ref[0] = jnp.sin(freq_ref[1:2] * (_mm_f32(h, w2_ref[...]) + b2_ref[...]))


def _hyena_filter_kernel(h_ref, w3f_ref, w3b_ref, t_ref, delta_ref, k_ref):
    hf = _mm_f32(h_ref[0], w3f_ref[...]) * jnp.exp(-t_ref[0] * delta_ref[...])
    hb = _mm_f32(h_ref[1], w3b_ref[...]) * jnp.exp(-t_ref[1] * delta_ref[...])
    norm = (jnp.sum(jnp.abs(hf), axis=0, keepdims=True) + jnp.sum(jnp.abs(hb), axis=0, keepdims=True))
    r = pl.program_id(2)
    rows = lax.broadcasted_iota(jnp.int32, hf.shape, 0)
    tail = jnp.where(rows == 0, 0.0, hb)
    blk = jnp.where(r == 0, hf, jnp.where(r == pl.num_programs(2) - 1, tail, 0.0))
    k_ref[0] = blk / norm


def _hyena_filter_buffer(n, nc, w1, b1, w2, b2, w3, freq):
    lag = jnp.stack([jnp.arange(n), jnp.where(jnp.arange(n) == 0, 0, n - jnp.arange(n))]).astype(F32)
    t = (lag / (n - 1))[:, :, None]
    bands = jnp.linspace(1e-4, HYENA_BANDS - 1, HYENA_BANDS, dtype=F32)
    ang = (2.0 * math.pi / n) * lag[:, :, None] * bands[None, None, :]
    feats = jnp.concatenate([t, jnp.cos(ang), -jnp.sin(ang),
                             jnp.zeros((2, n, HYENA_FW - HYENA_EMB), F32)], axis=-1)
    w1p = jnp.zeros((HYENA_FW, HYENA_FW), F32).at[:HYENA_EMB].set(w1)
    deltas = jnp.abs(jnp.linspace(math.log(HYENA_TARGET) / HYENA_SLOW,
                                  math.log(HYENA_TARGET) / HYENA_FAST, HYENA_DIM, dtype=F32))[None, :]
    fixed = lambda shape: pl.BlockSpec(shape, lambda d: (0,) * len(shape))
    hidden = pl.pallas_call(
        _hyena_mlp_kernel,
        grid=(2,),
        in_specs=[pl.BlockSpec((1, n, HYENA_FW), lambda d: (d, 0, 0)), fixed((HYENA_FW, HYENA_FW)),
                  fixed((1, HYENA_FW)), fixed((HYENA_FW, HYENA_FW)), fixed((1, HYENA_FW)),
                  fixed((2, HYENA_FW))],
        out_specs=pl.BlockSpec((1, n, HYENA_FW), lambda d: (d, 0, 0)),
        out_shape=jax.ShapeDtypeStruct((2, n, HYENA_FW), F32),
        compiler_params=_cparams(("parallel",)),
        name="hyena_filter_mlp",
    )(feats, w1p, b1.reshape(1, -1), w2, b2.reshape(1, -1), freq)
    tc = 256
    nj = HYENA_DIM // tc
    full = lambda shape: pl.BlockSpec(shape, lambda o, j, r: (0,) * len(shape))
    return pl.pallas_call(
        _hyena_filter_kernel,
        grid=(HYENA_ORDER, nj, nc // n),
        in_specs=[full((2, n, HYENA_FW)),
                  pl.BlockSpec((HYENA_FW, tc), lambda o, j, r: (0, o * 2 * nj + j)),
                  pl.BlockSpec((HYENA_FW, tc), lambda o, j, r: (0, o * 2 * nj + nj + j)),
                  full((2, n, 1)), pl.BlockSpec((1, tc), lambda o, j, r: (0, j))],
        out_specs=pl.BlockSpec((1, n, tc), lambda o, j, r: (o, r, j)),
        out_shape=jax.ShapeDtypeStruct((HYENA_ORDER, nc, HYENA_DIM), F32),
        compiler_params=_cparams(("parallel", "parallel", "parallel")),
        name="hyena_filters",
    )(hidden, w3, w3, t, deltas)


HY_J = SUBLANES
HY_GROUPS = DFT_N2 // HY_J


def _dft_tables(n1):
    nc = n1 * DFT_N2
    f1 = np.arange(n1)
    ang = 2.0 * np.pi * ((f1[:, None] * f1[None, :]) % n1) / n1
    eye = np.eye(HY_J)
    w1 = np.kron(np.concatenate([np.cos(ang), -np.sin(ang)], axis=0), eye)
    v3 = np.kron(np.concatenate([np.cos(ang), -np.sin(ang)], axis=1), eye) / nc
    t2 = np.arange(DFT_N2).reshape(HY_GROUPS, 1, HY_J)
    tw = 2.0 * np.pi * ((f1[None, :, None] * t2) % nc) / nc
    tw = np.broadcast_to(tw.reshape(HY_GROUPS, n1 * HY_J, 1), (HY_GROUPS, n1 * HY_J, LANES))
    k = np.arange(DFT_N2)
    phi = 2.0 * np.pi * ((k[:, None] * k[None, :]) % DFT_N2) / DFT_N2
    c, s = np.cos(phi), np.sin(phi)
    f_fwd = np.block([[c, s], [-s, c]])
    f_inv = np.block([[c, -s], [s, c]])
    names = ("w1", "v3", "ctw", "stw", "f_fwd", "f_inv")
    return {n: jnp.asarray(t, BF16) for n, t in zip(names, (w1, v3, np.cos(tw), np.sin(tw), f_fwd, f_inv))}


def _tiles(ref, rows, g, lead=()):
    return jnp.concatenate([ref[lead + (pl.ds(r * DFT_N2 + g * HY_J, HY_J), slice(None))] for r in rows], axis=0)


def _dft_rows_in(x_ref, w_ref, ctw_ref, stw_ref, a_ref, t1n, n1, t1_valid):
    half = n1 * HY_J
    for g in range(HY_GROUPS):
        xg = _tiles(x_ref, range(t1_valid), g, lead=(0,))
        pq = _mm(w_ref[:, :t1_valid * HY_J], xg)
        p, q = pq[:half], pq[half:]
        c, s = ctw_ref[g].astype(F32), stw_ref[g].astype(F32)
        re = c * p + s * q
        im = c * q - s * p
        for m in range(n1):
            rows = slice(m * HY_J, (m + 1) * HY_J)
            a_ref[pl.ds(m * DFT_N2 + g * HY_J, HY_J), :] = re[rows]
            a_ref[pl.ds((n1 + m) * DFT_N2 + g * HY_J, HY_J), :] = im[rows]


def _slab(f1):
    return slice(f1 * DFT_N2, (f1 + 1) * DFT_N2)


def _spectrum_kernel(x_ref, w_ref, ctw_ref, stw_ref, ff_ref, k_ref, a_ref, *, n1):
    _dft_rows_in(x_ref, w_ref, ctw_ref, stw_ref, a_ref, n1, n1, n1)
    for f1 in range(n1):
        a = jnp.concatenate([a_ref[_slab(f1)], a_ref[_slab(n1 + f1)]], axis=0)
        k_ref[0, f1] = _mm(ff_ref[...], a)


def _filter_spectrum(kbuf, tabs, n1):
    no, nc, _ = kbuf.shape
    nj = HYENA_DIM // HY_CT
    const = lambda a: pl.BlockSpec(a.shape, lambda o, j: (0,) * a.ndim)
    consts = [tabs[n] for n in ("w1", "ctw", "stw", "f_fwd")]
    return pl.pallas_call(
        functools.partial(_spectrum_kernel, n1=n1),
        grid=(no, nj),
        in_specs=[pl.BlockSpec((1, nc, HY_CT), lambda o, j: (o, 0, j)), *[const(a) for a in consts]],
        out_specs=pl.BlockSpec((1, n1, 2 * DFT_N2, HY_CT), lambda o, j: (o, 0, 0, j)),
        out_shape=jax.ShapeDtypeStruct((no, n1, 2 * DFT_N2, HYENA_DIM), F32),
        scratch_shapes=[pltpu.VMEM((2 * n1 * DFT_N2, HY_CT), F32)],
        compiler_params=_cparams(("parallel", "parallel")),
        name="hyena_filter_spectrum",
    )(kbuf, *consts)


def _hyena_conv_kernel(x_ref, gate_ref, k_ref, w_ref, ctw_ref, stw_ref, ff_ref, fi_ref, v_ref, bias_ref,
                       o_ref, a_ref, *, t1n, n1, t1_valid):
    _dft_rows_in(x_ref, w_ref, ctw_ref, stw_ref, a_ref, t1n, n1, t1_valid)
    for f1 in range(n1):
        re, im = _slab(f1), _slab(n1 + f1)
        x = _mm(ff_ref[...], jnp.concatenate([a_ref[re], a_ref[im]], axis=0))
        xre, xim = x[:DFT_N2], x[DFT_N2:]
        kre, kim = k_ref[0, f1, :DFT_N2], k_ref[0, f1, DFT_N2:]
        bm = _mm(fi_ref[...], jnp.concatenate([xre * kre - xim * kim, xre * kim + xim * kre], axis=0))
        a_ref[re] = bm[:DFT_N2]
        a_ref[im] = bm[DFT_N2:]
    bias = bias_ref[...]
    for g in range(HY_GROUPS):
        br, bi = _tiles(a_ref, range(n1), g), _tiles(a_ref, range(n1, 2 * n1), g)
        c, s = ctw_ref[g].astype(F32), stw_ref[g].astype(F32)
        y = _mm(v_ref[:t1n * HY_J], jnp.concatenate([c * br - s * bi, s * br + c * bi], axis=0))
        for t1 in range(t1n):
            rows = pl.ds(t1 * DFT_N2 + g * HY_J, HY_J)
            o_ref[0, rows, :] = gate_ref[0, rows, :] * (y[t1 * HY_J:(t1 + 1) * HY_J] + bias * x_ref[0, rows, :])


def _hyena_conv(z, z_spec, gate, gate_spec, kspec, order, bias, tabs, n1, n_pad, n):
    bx = z.shape[0]
    t1n = n_pad // DFT_N2
    consts = [tabs[name] for name in ("w1", "ctw", "stw", "f_fwd", "f_inv", "v3")]
    const = lambda a: pl.BlockSpec(a.shape, lambda j, bi: (0,) * a.ndim)
    return pl.pallas_call(
        functools.partial(_hyena_conv_kernel, t1n=t1n, n1=n1, t1_valid=-(-n // DFT_N2)),
        grid=(HYENA_DIM // HY_CT, bx),
        in_specs=[z_spec, gate_spec,
                  pl.BlockSpec((1, n1, 2 * DFT_N2, HY_CT), lambda j, bi: (order, 0, 0, j)),
                  *[const(a) for a in consts],
                  pl.BlockSpec((1, HY_CT), lambda j, bi: (0, j))],
        out_specs=pl.BlockSpec((1, n_pad, HY_CT), lambda j, bi: (bi, 0, j)),
        out_shape=jax.ShapeDtypeStruct((bx, n_pad, HYENA_DIM), F32),
        scratch_shapes=[pltpu.VMEM((2 * n1 * DFT_N2, HY_CT), F32)],
        compiler_params=_cparams(("parallel", "parallel")),
        name="hyena_conv",
    )(z, gate, kspec, *consts, bias.reshape(1, -1))


def _hyena_operator(p_hy, row0, n, filt_params, bias):
    n_pad = max(n, HY_MIN_LEN)
    nc = 2 * n_pad
    n1 = nc // DFT_N2
    tabs = _dft_tables(n1)
    kspec = _filter_spectrum(_hyena_filter_buffer(n, nc, *filt_params), tabs, n1)
    nj = HYENA_DIM // HY_CT
    window = lambda part: pl.BlockSpec((pl.Element(1), pl.Element(n_pad), pl.Element(HY_CT)),
                                       lambda j, bi: (bi, row0, (part * nj + j) * HY_CT))
    own = pl.BlockSpec((1, n_pad, HY_CT), lambda j, bi: (bi, 0, j))
    z = _hyena_conv(p_hy, window(0), p_hy, window(1), kspec, 0, bias[0], tabs, n1, n_pad, n)
    return _hyena_conv(z, own, p_hy, window(2), kspec, 1, bias[1], tabs, n1, n_pad, n)


def _rwkv_readout(y, bonus, gate, ln_g, ln_b, bd):
    inv_n = 1.0 / RWKV_HEAD
    dev = y - _headsum(y, bd) * inv_n
    var = _headsum(dev * dev, bd) * inv_n
    return (dev * lax.rsqrt(var + RWKV_LN_EPS) * ln_g + ln_b + bonus) * gate


def _merge_kernel(x_ref, mod_ref, g_ref, gates_ref, bg_ref, ya_ref, yf_ref, yr_ref, bonus_ref, rgate_ref,
                  lng_ref, lnb_ref, bd_ref, yh_ref, yhc_ref, yd_ref, wb_ref, wo_ref, o_ref, *, n_ctx_tiles, row_off):
    yb = _rwkv_readout(yf_ref[0] + yr_ref[0], bonus_ref[0], rgate_ref[0], lng_ref[...], lnb_ref[...], bd_ref[...])
    is_ctx = pl.program_id(1) + row_off < n_ctx_tiles
    yh = jnp.where(is_ctx, yhc_ref[0], yh_ref[0])
    merged = None
    for br, y in enumerate((ya_ref[0], yb, yh, yd_ref[0])):
        gate = _sigmoid(gates_ref[0, :, br * D_MODEL:(br + 1) * D_MODEL] + bg_ref[br:br + 1])
        term = gate * jnp.dot(y.astype(BF16), wb_ref[br], preferred_element_type=F32)
        merged = term if merged is None else merged + term
    y = jnp.dot(merged.astype(BF16), wo_ref[...], preferred_element_type=F32)
    o_ref[0] = x_ref[0] + mod_ref[0, 5:6] * _rms(y, g_ref[3:4])


def _merge(x, mods, norm_g, p, b_gate, ya, rwkv, yh, yh_ctx, yd, w_branch, w_out, *, n_ctx_tiles, row_off):
    b, s, _ = x.shape
    nt = s // TM - row_off
    n_lat = mods.shape[0] - 1
    br = lambda: pl.BlockSpec((1, TM, BRANCH_DIM), lambda bi, i: (bi, i, 0))
    tok = pl.BlockSpec((1, TM, BRANCH_DIM), lambda bi, i: (bi, i + row_off, 0))
    row = pl.BlockSpec((1, RWKV_DIM), lambda bi, i: (0, 0))
    y_fwd, y_bwd, bonus, rgate, ln_g, ln_b = rwkv
    n_lat_tiles = s // TM - n_ctx_tiles
    br_a = pl.BlockSpec((1, TM, BRANCH_DIM), lambda bi, i: (
        bi, jnp.where(i + row_off < n_ctx_tiles, n_lat_tiles + i, i + row_off - n_ctx_tiles), 0))
    br_h = pl.BlockSpec((1, TM, BRANCH_DIM), lambda bi, i: (bi, jnp.maximum(i + row_off - n_ctx_tiles, 0), 0))
    if yh_ctx is None:
        br_hc = pl.BlockSpec((1, TM, BRANCH_DIM), lambda bi, i: (bi, 0, 0))
    else:
        br_hc = pl.BlockSpec((1, TM, BRANCH_DIM), lambda bi, i: (bi, jnp.minimum(i + row_off, n_ctx_tiles - 1), 0))
    return pl.pallas_call(
        functools.partial(_merge_kernel, n_ctx_tiles=n_ctx_tiles if yh_ctx is not None else 0, row_off=row_off),
        grid=(b, nt),
        in_specs=[pl.BlockSpec((1, TM, D_MODEL), lambda bi, i: (bi, i + row_off, 0)),
                  pl.BlockSpec((1, N_MOD, D_MODEL),
                               lambda bi, i: (jnp.where(i + row_off < n_ctx_tiles, n_lat, bi), 0, 0)),
                  pl.BlockSpec((6, D_MODEL), lambda bi, i: (0, 0)),
                  pl.BlockSpec((1, TM, GATE_COLS), lambda bi, i: (bi, i + row_off, 0)),
                  pl.BlockSpec((N_BRANCH, D_MODEL), lambda bi, i: (0, 0)),
                  br_a, tok, tok, tok, tok, row, row,
                  pl.BlockSpec((RWKV_DIM, RWKV_DIM), lambda bi, i: (0, 0)),
                  br_h, br_hc, br(),
                  pl.BlockSpec((N_BRANCH, BRANCH_DIM, D_MODEL), lambda bi, i: (0, 0, 0)),
                  pl.BlockSpec((D_MODEL, D_MODEL), lambda bi, i: (0, 0))],
        out_specs=pl.BlockSpec((1, TM, D_MODEL), lambda bi, i: (bi, i, 0)),
        out_shape=jax.ShapeDtypeStruct((b, nt * TM, D_MODEL), F32),
        compiler_params=_cparams(("parallel", "parallel")),
        name="merge_branches",
    )(x, mods, norm_g, p, b_gate, ya, y_fwd, y_bwd, bonus, rgate, ln_g.reshape(1, -1), ln_b.reshape(1, -1),
      _head_block_diag(), yh, yh if yh_ctx is None else yh_ctx, yd, w_branch, w_out)


def kernel(x, c, ctx, c_ctx, w_mod, b_mod, norm_g, ffn_w13, ffn_w2, w_in, b_gate, mla_norm_q, mla_norm_kv, mla_w_uq, mla_w_ukv, rwkv_mu, rwkv_w0, rwkv_w_up, rwkv_a0, rwkv_a_up, rwkv_g_up, rwkv_kvec, rwkv_r_k, rwkv_ln_g, rwkv_ln_b, hyena_conv, hyena_conv_b, hyena_w1, hyena_b1, hyena_w2, hyena_b2, hyena_w3, hyena_freq, hyena_bias, swa_sink, w_branch, w_out):
    b, n, _ = x.shape
    n_ctx = ctx.shape[1]
    nct = n_ctx // TM
    xall = jnp.concatenate([ctx, x], axis=1)
    c_all = jnp.concatenate([c, c_ctx[None]], axis=0)
    tabs_mla = _rope_tables(n, n_ctx, MLA_ROPE, MLA_NOPE, LANES)
    tabs_swa = _rope_tables(n, n_ctx, SWA_HEAD, 0, SWA_HEAD)
    depth = w_mod.shape[0]
    for l in range(depth):
        with_ctx = l + 1 < depth
        row_off = 0 if with_ctx else nct
        mods = _modulation(c_all, w_mod[l], b_mod[l])
        xall = _ffn(xall, mods, norm_g[l], ffn_w13[l, 0].astype(BF16), ffn_w2[l, 0].astype(BF16),
                    mod0=0, g0=0, n_ctx_tiles=nct)
        coef = _shift_coefficients(rwkv_mu[l], hyena_conv[l], hyena_conv_b[l])
        p, p_hy = _inproj(xall, mods, norm_g[l], _permute_w_in(w_in[l]).astype(BF16), coef, n_ctx_tiles=nct)
        q, k, v = _mla_prep(p, mla_norm_q[l], mla_norm_kv[l], mla_w_uq[l], mla_w_ukv[l], tabs_mla)
        ya = _mla_attention(q, k, v, n_ctx=n_ctx, with_ctx=with_ctx)
        y_fwd, y_bwd, bonus, gate = _rwkv_chunks(p, rwkv_kvec[l], rwkv_w0[l], rwkv_a0[l], rwkv_w_up[l],
                                                 rwkv_a_up[l], rwkv_g_up[l], rwkv_r_k[l], n_ctx=n_ctx)
        rwkv = (y_fwd, y_bwd, bonus, gate, rwkv_ln_g[l], rwkv_ln_b[l])
        filt = (hyena_w1[l], hyena_b1[l], hyena_w2[l], hyena_b2[l], hyena_w3[l], hyena_freq[l])
        yh = _hyena_operator(p_hy, n_ctx, n, filt, hyena_bias[l])
        yh_ctx = _hyena_operator(p_hy, 0, n_ctx, filt, hyena_bias[l]) if with_ctx else None
        q, k, v = _swa_prep(p, tabs_swa)
        yd = _swa_attention(q, k, v, swa_sink[l], n_ctx=n_ctx, q_off=row_off * TM // SWA_TQ)
        xall = _merge(xall, mods, norm_g[l], p, b_gate[l], ya, rwkv, yh, yh_ctx, yd, w_branch[l].astype(BF16),
                      w_out[l].astype(BF16), n_ctx_tiles=nct, row_off=row_off)
        xall = _ffn(xall, mods, norm_g[l], ffn_w13[l, 1].astype(BF16), ffn_w2[l, 1].astype(BF16),
                    mod0=6, g0=4, n_ctx_tiles=nct - row_off)
    return xall
```
